```python
import jax, jax.numpy as jnp
from jax import lax
import numpy as np

D_MODEL = 1024
BATCH = 2
SEQ = 8192
DEPTH = 4

GRID_W = 64
CTX_LEN = 256
EXPAND = 2
D_INNER = EXPAND * D_MODEL
RWKV_WIDTH = D_INNER // 2
RWKV_HEAD = 64
RWKV_HEADS = RWKV_WIDTH // RWKV_HEAD
RWKV_W_LORA = 64
RWKV_A_LORA = 64
RWKV_GN_EPS = 64e-5
SHIFT_WIDTH = 3 * RWKV_WIDTH + 2 * RWKV_W_LORA + 2 * RWKV_A_LORA
HGRN_WIDTH = D_INNER // 2
HGRN_HEAD = 128
HGRN_HEADS = HGRN_WIDTH // HGRN_HEAD
HGRN_CHUNK = 64
EVEN_IN = SHIFT_WIDTH + 4 * HGRN_WIDTH + D_INNER
DIFF_HEAD = 64
DIFF_HEADS = D_INNER // (2 * DIFF_HEAD)
ODD_IN = 4 * D_INNER
Q_BLOCK = 128
ROPE_BASE = 10000.0
N_EVEN = (DEPTH + 1) // 2
N_ODD = DEPTH // 2
DEEPNORM_ALPHA = (2.0 * DEPTH) ** 0.25
DEEPNORM_BETA = (8.0 * DEPTH) ** -0.25
LN_EPS = 1e-5

kernel_name = 'hybrid_rwkv7_hgrn2_diffattn_deepnorm_prefix'


def _split(z, sizes):
    idx = [int(i) for i in np.cumsum(sizes)[:-1]]
    return jnp.split(z, idx, axis=-1)


def _layer_norm(x, g, b):
    xf = x.astype(jnp.float32)
    mu = jnp.mean(xf, -1, keepdims=True)
    var = jnp.mean(jnp.square(xf - mu), -1, keepdims=True)
    return ((xf - mu) * lax.rsqrt(var + LN_EPS)).astype(x.dtype) * g + b


def _head_norm(y, g, b, eps):
    yf = y.astype(jnp.float32)
    mu = jnp.mean(yf, -1, keepdims=True)
    var = jnp.mean(jnp.square(yf - mu), -1, keepdims=True)
    return ((yf - mu) * lax.rsqrt(var + eps)).astype(y.dtype) * g + b


def _rms_norm(y, g):
    yf = y.astype(jnp.float32)
    return (yf * lax.rsqrt(jnp.mean(jnp.square(yf), -1, keepdims=True) + LN_EPS)).astype(y.dtype) * g


def _centred_shift(p, mu_prev, mu_next):
    zero = jnp.zeros_like(p[:, :1])
    prev = jnp.concatenate([zero, p[:, :-1]], 1)
    nxt = jnp.concatenate([p[:, 1:], zero], 1)
    return p + mu_prev * (prev - p) + mu_next * (nxt - p)


def _rwkv7_inputs(ps, w0, w2, a0, a2, k_k, k_a):
    b, t, _ = ps.shape
    r, k, v, wlo, alo = _split(ps, [RWKV_WIDTH] * 3 + [2 * RWKV_W_LORA, 2 * RWKV_A_LORA])
    wlo = wlo.reshape(b, t, 2, RWKV_W_LORA)
    alo = alo.reshape(b, t, 2, RWKV_A_LORA)
    w_log = -jax.nn.softplus(-(w0 + jnp.einsum('btdr,drc->btdc', jnp.tanh(wlo), w2))) - 0.5
    decay = jnp.exp(-jnp.exp(w_log))
    a = jax.nn.sigmoid(a0 + jnp.einsum('btdr,drc->btdc', alo, a2))
    k_dir = k[:, :, None] * (1 + (a - 1) * k_a)
    heads = lambda u: u.reshape(u.shape[:-1] + (RWKV_HEADS, RWKV_HEAD))
    kk = heads(k * k_k).astype(jnp.float32)
    kk = (kk * lax.rsqrt(jnp.sum(jnp.square(kk), -1, keepdims=True) + 1e-12)).astype(ps.dtype)
    return heads(r), heads(k), heads(v), kk, heads(decay), heads(a), heads(k_dir)


def _rwkv7_scan(s0, inputs, d, reverse):
    r, _, v, kk, decay, a, k_dir = inputs

    def step(s, inp):
        r_t, w_t, k_t, v_t, kk_t, a_t = inp
        s = (s * w_t[:, :, None, :]
             - jnp.einsum('bhvk,bhk->bhv', s, kk_t)[..., None] * (kk_t * a_t)[:, :, None, :]
             + v_t[..., None] * k_t[:, :, None, :])
        return s, jnp.einsum('bhvk,bhk->bhv', s, r_t)

    xs = tuple(jnp.moveaxis(z, 1, 0) for z in (r, decay[:, :, d], k_dir[:, :, d], v, kk, a[:, :, d]))
    s, y = lax.scan(step, s0, xs, reverse=reverse)
    return s, jnp.moveaxis(y, 0, 1)


def _rwkv7_readout(y, inputs, r_k, g, b):
    r, k, v = inputs[:3]
    shp = (RWKV_HEADS, RWKV_HEAD)
    yn = _head_norm(y, g.reshape(shp), b.reshape(shp), RWKV_GN_EPS)
    bonus = jnp.sum(r * k * r_k.reshape(shp), -1, keepdims=True) * v
    return (yn + bonus).reshape(y.shape[:2] + (RWKV_WIDTH,))


def _rwkv7_branch(p, pc, mu_prev, mu_next, w0, w2, a0, a2, k_k, k_a, r_k, lnx_g, lnx_b, need_ctx):
    lat = _rwkv7_inputs(_centred_shift(p, mu_prev, mu_next), w0, w2, a0, a2, k_k, k_a)
    cx = _rwkv7_inputs(_centred_shift(pc, mu_prev, mu_next), w0, w2, a0, a2, k_k, k_a)
    s0 = jnp.zeros((p.shape[0], RWKV_HEADS, RWKV_HEAD, RWKV_HEAD), p.dtype)
    y = 0.0
    yc = 0.0
    for d, rev in enumerate((False, True)):
        s_ctx, yc_d = _rwkv7_scan(s0, cx, d, rev)
        _, y_d = _rwkv7_scan(s_ctx, lat, d, rev)
        y = y + y_d
        yc = yc + yc_d
    out = _rwkv7_readout(y, lat, r_k, lnx_g, lnx_b)
    out_c = _rwkv7_readout(yc, cx, r_k, lnx_g, lnx_b) if need_ctx else None
    return out, out_c


def _hgrn2_gates(z, lb_d):
    k = (1 - lb_d) * jax.nn.sigmoid(-z)
    log_f = jnp.log1p(-k)
    return log_f, k


def _hgrn2_scan(s0, q, log_f, k, v, reverse):
    if reverse:
        q, log_f, k, v = (jnp.flip(z, 1) for z in (q, log_f, k, v))
    b, t, h, _ = q.shape
    n = t // HGRN_CHUNK
    to_chunks = lambda z: z.reshape(b, n, HGRN_CHUNK, h, -1).transpose(1, 0, 3, 2, 4)
    mask = jnp.tril(jnp.ones((HGRN_CHUNK, HGRN_CHUNK), bool))[:, :, None]

    def step(s, inp):
        q_c, g_c, k_c, v_c = inp
        cum = jnp.cumsum(g_c, axis=2)
        diff = cum[:, :, :, None, :] - cum[:, :, None, :, :]
        dec = jnp.where(mask, jnp.exp(jnp.where(mask, diff, 0.0)), 0.0)
        scores = jnp.einsum('bhtsk,bhsk->bhts', q_c[:, :, :, None, :] * dec, k_c)
        o = (jnp.einsum('bhts,bhsv->bhtv', scores, v_c)
             + jnp.einsum('bhtk,bhkv->bhtv', q_c * jnp.exp(cum), s))
        last = cum[:, :, -1:, :]
        s = jnp.exp(last[:, :, 0])[..., None] * s + jnp.einsum('bhsk,bhsv->bhkv', k_c * jnp.exp(last - cum), v_c)
        return s, o

    s, o = lax.scan(step, s0, tuple(to_chunks(z) for z in (q, log_f, k, v)))
    o = o.transpose(1, 0, 3, 2, 4).reshape(b, t, h, -1)
    if reverse:
        o = jnp.flip(o, 1)
    return s, o


def _hgrn2_branch(p, pc, lb, norm_g, need_ctx):
    def prep(z):
        b, t, _ = z.shape
        heads = lambda u: u.reshape(b, t, HGRN_HEADS, HGRN_HEAD)
        q, f_fwd, f_bwd, i = jnp.split(z, 4, axis=-1)
        gates = [tuple(heads(u) for u in _hgrn2_gates(f, lb[d])) for d, f in enumerate((f_fwd, f_bwd))]
        return heads(jax.nn.silu(q)), gates, heads(i)

    q, g, i = prep(p)
    qc, gc, ic = prep(pc)
    s0 = jnp.zeros((p.shape[0], HGRN_HEADS, HGRN_HEAD, HGRN_HEAD), p.dtype)
    o = 0.0
    oc = 0.0
    for d, rev in enumerate((False, True)):
        s_ctx, oc_d = _hgrn2_scan(s0, qc, gc[d][0], gc[d][1], ic, rev)
        _, o_d = _hgrn2_scan(s_ctx, q, g[d][0], g[d][1], i, rev)
        o = o + o_d
        oc = oc + oc_d
    y = _rms_norm(o, norm_g).reshape(p.shape[:2] + (HGRN_WIDTH,))
    yc = _rms_norm(oc, norm_g).reshape(pc.shape[:2] + (HGRN_WIDTH,)) if need_ctx else None
    return y, yc


def _even_mixer(h, hc, w_in, mu_prev, mu_next, w0, w2, a0, a2, k_k, k_a, r_k, lnx_g, lnx_b, lb, norm_g, need_ctx):
    p = h @ w_in
    pc = hc @ (w_in if need_ctx else w_in[:, :EVEN_IN - D_INNER])
    hs = SHIFT_WIDTH
    he = SHIFT_WIDTH + 4 * HGRN_WIDTH
    ya, yac = _rwkv7_branch(p[..., :hs], pc[..., :hs], mu_prev, mu_next, w0, w2, a0, a2,
                            k_k, k_a, r_k, lnx_g, lnx_b, need_ctx)
    yb, ybc = _hgrn2_branch(p[..., hs:he], pc[..., hs:he], lb, norm_g, need_ctx)
    y = jnp.concatenate([ya, yb], -1) * jax.nn.silu(p[..., he:])
    yc = jnp.concatenate([yac, ybc], -1) * jax.nn.silu(pc[..., he:]) if need_ctx else None
    return y, yc


def _rope_2d(x, rows, cols):
    half = x.shape[-1] // 2
    quarter = half // 2
    inv = ROPE_BASE ** (-jnp.arange(quarter, dtype=jnp.float32) / quarter)

    def rot(xa, pos):
        ang = pos.astype(jnp.float32)[:, None] * inv
        cos = jnp.cos(ang)[None, :, None, None, :].astype(x.dtype)
        sin = jnp.sin(ang)[None, :, None, None, :].astype(x.dtype)
        x1, x2 = xa[..., :quarter], xa[..., quarter:]
        return jnp.concatenate([x1 * cos - x2 * sin, x1 * sin + x2 * cos], -1)

    return jnp.concatenate([rot(x[..., :half], rows), rot(x[..., half:], cols)], -1)


def _diff_attend(q, k, v, lam):
    s = jnp.einsum('bhmqd,bhmkd->bhmqk', q, k).astype(jnp.float32) * (DIFF_HEAD ** -0.5)
    p = jax.nn.softmax(s, axis=-1)
    attn = (p[:, :, 0] - lam * p[:, :, 1]).astype(v.dtype)
    return jnp.einsum('bhqk,bhkv->bhqv', attn, v)


def _odd_mixer(h, hc, w_in, lam_p, subln_g, lam_init, rows, cols, need_ctx):
    b, t, _ = h.shape
    q, k, v, g = jnp.split(h @ w_in, 4, axis=-1)
    if need_ctx:
        qc, kc, vc, gc = jnp.split(hc @ w_in, 4, axis=-1)
    else:
        kc, vc = jnp.split(hc @ w_in[:, D_INNER:3 * D_INNER], 2, axis=-1)
    qk_heads = lambda u: u.reshape(u.shape[0], u.shape[1], DIFF_HEADS, 2, DIFF_HEAD)
    v_heads = lambda u: u.reshape(u.shape[0], u.shape[1], DIFF_HEADS, 2 * DIFF_HEAD)
    to_bhm = lambda u: u.transpose(0, 2, 3, 1, 4)
    q = _rope_2d(qk_heads(q), rows, cols)
    k = _rope_2d(qk_heads(k), rows, cols)
    kc = qk_heads(kc)
    vc = v_heads(vc)
    lam = jnp.exp(jnp.sum(lam_p[0] * lam_p[1])) - jnp.exp(jnp.sum(lam_p[2] * lam_p[3])) + lam_init
    k_all = to_bhm(jnp.concatenate([kc, k], 1))
    v_all = jnp.concatenate([vc, v_heads(v)], 1).transpose(0, 2, 1, 3)
    n_blk = t // Q_BLOCK
    q_blocks = q.reshape(b, n_blk, Q_BLOCK, DIFF_HEADS, 2, DIFF_HEAD).transpose(1, 0, 3, 4, 2, 5)
    o = lax.map(lambda qb: _diff_attend(qb, k_all, v_all, lam), q_blocks)
    o = o.transpose(1, 0, 3, 2, 4).reshape(b, t, DIFF_HEADS, 2 * DIFF_HEAD)
    y = (_rms_norm(o, subln_g) * (1 - lam_init)).reshape(b, t, D_INNER) * jax.nn.silu(g)
    if need_ctx:
        oc = _diff_attend(to_bhm(qk_heads(qc)), to_bhm(kc), vc.transpose(0, 2, 1, 3), lam).transpose(0, 2, 1, 3)
        yc = (_rms_norm(oc, subln_g) * (1 - lam_init)).reshape(b, -1, D_INNER) * jax.nn.silu(gc)
    else:
        yc = None
    return y, yc


def setup_inputs(seed: int = 0) -> dict:
    key = jax.random.key(seed)
    ks = iter(jax.random.split(key, 32))

    def nrm(shape, std):
        return std * jax.random.normal(next(ks), shape, jnp.float32)

    def unif(shape, lo, hi):
        return jax.random.uniform(next(ks), shape, jnp.float32, lo, hi)

    d, di = D_MODEL, D_INNER
    ramp = jnp.linspace(-6.0, 1.0, RWKV_WIDTH, dtype=jnp.float32)
    return {
        'x': nrm((BATCH, SEQ, d), 1.0),
        'c': nrm((BATCH, d), 1.0),
        'ctx': nrm((BATCH, CTX_LEN, d), 1.0),
        'c_ctx': nrm((d,), 1.0),
        'ada_w': nrm((DEPTH, d, 3 * d), 0.5 * d ** -0.5),
        'ada_b': nrm((DEPTH, 3 * d), 0.02),
        'ln_g': 1.0 + nrm((DEPTH, d), 0.02),
        'ln_b': nrm((DEPTH, d), 0.02),
        'even_w_in': nrm((N_EVEN, d, EVEN_IN), d ** -0.5),
        'even_w_out': nrm((N_EVEN, di, d), DEEPNORM_BETA * di ** -0.5),
        'rwkv_mu_prev': unif((N_EVEN, SHIFT_WIDTH), 0.1, 0.45),
        'rwkv_mu_next': unif((N_EVEN, SHIFT_WIDTH), 0.1, 0.45),
        'rwkv_w0': ramp + nrm((N_EVEN, 2, RWKV_WIDTH), 0.1),
        'rwkv_w2': nrm((N_EVEN, 2, RWKV_W_LORA, RWKV_WIDTH), 0.1),
        'rwkv_a0': nrm((N_EVEN, 2, RWKV_WIDTH), 0.1),
        'rwkv_a2': nrm((N_EVEN, 2, RWKV_A_LORA, RWKV_WIDTH), 0.1),
        'rwkv_k_k': 0.85 + nrm((N_EVEN, RWKV_WIDTH), 0.02),
        'rwkv_k_a': 1.0 + nrm((N_EVEN, RWKV_WIDTH), 0.02),
        'rwkv_r_k': nrm((N_EVEN, RWKV_WIDTH), 0.1),
        'rwkv_lnx_g': 1.0 + nrm((N_EVEN, RWKV_WIDTH), 0.02),
        'rwkv_lnx_b': nrm((N_EVEN, RWKV_WIDTH), 0.02),
        'hgrn_lb_logits': nrm((N_EVEN, 2, HGRN_WIDTH), 0.5),
        'hgrn_norm_g': 1.0 + nrm((N_EVEN, HGRN_HEAD), 0.02),
        'odd_w_in': nrm((N_ODD, d, ODD_IN), d ** -0.5),
        'odd_w_out': nrm((N_ODD, di, d), DEEPNORM_BETA * di ** -0.5),
        'diff_lambda': nrm((N_ODD, 4, DIFF_HEAD), 0.1),
        'diff_subln_g': 1.0 + nrm((N_ODD, 2 * DIFF_HEAD), 0.02),
    }


def reference(x, c, ctx, c_ctx, ada_w, ada_b, ln_g, ln_b, even_w_in, even_w_out,
              rwkv_mu_prev, rwkv_mu_next, rwkv_w0, rwkv_w2, rwkv_a0, rwkv_a2,
              rwkv_k_k, rwkv_k_a, rwkv_r_k, rwkv_lnx_g, rwkv_lnx_b,
              hgrn_lb_logits, hgrn_norm_g, odd_w_in, odd_w_out, diff_lambda, diff_subln_g):
    n_tok = x.shape[1]
    n_rows = n_tok // GRID_W
    rows = jnp.repeat(jnp.arange(n_rows), GRID_W)
    cols = jnp.tile(jnp.arange(GRID_W), n_rows)
    lb_all = jax.nn.softmax(hgrn_lb_logits.astype(jnp.float32), axis=0)
    lb_all = (jnp.cumsum(lb_all, axis=0) - lb_all[0]).astype(x.dtype)
    cond = jax.nn.silu(c)
    cond_ctx = jax.nn.silu(c_ctx)
    for layer in range(DEPTH):
        need_ctx = layer < DEPTH - 1
        shift, scale, gate = jnp.split(cond @ ada_w[layer] + ada_b[layer], 3, axis=-1)
        shift_c, scale_c, gate_c = jnp.split(cond_ctx @ ada_w[layer] + ada_b[layer], 3, axis=-1)
        h = x * (1 + scale[:, None]) + shift[:, None]
        hc = ctx * (1 + scale_c) + shift_c
        j = layer // 2
        if layer % 2 == 0:
            y, yc = _even_mixer(h, hc, even_w_in[j], rwkv_mu_prev[j], rwkv_mu_next[j], rwkv_w0[j], rwkv_w2[j],
                                rwkv_a0[j], rwkv_a2[j], rwkv_k_k[j], rwkv_k_a[j], rwkv_r_k[j],
                                rwkv_lnx_g[j], rwkv_lnx_b[j], lb_all[j], hgrn_norm_g[j], need_ctx)
            w_out = even_w_out[j]
        else:
            lam_init = 0.8 - 0.6 * float(np.exp(-0.3 * layer))
            y, yc = _odd_mixer(h, hc, odd_w_in[j], diff_lambda[j], diff_subln_g[j], lam_init, rows, cols, need_ctx)
            w_out = odd_w_out[j]
        x = _layer_norm(DEEPNORM_ALPHA * x + gate[:, None] * (y @ w_out), ln_g[layer], ln_b[layer])
        if need_ctx:
            ctx = _layer_norm(DEEPNORM_ALPHA * ctx + gate_c * (yc @ w_out), ln_g[layer], ln_b[layer])
    return x
```

```python
import functools

import numpy as np
import jax
import jax.numpy as jnp
from jax import lax
from jax.experimental import pallas as pl
from jax.experimental.pallas import tpu as pltpu

F32 = jnp.float32
BF16 = jnp.bfloat16
HI = lax.Precision.HIGHEST

GRID_W = 64
RWKV_HEAD = 64
RWKV_LORA = 64
RWKV_GN_EPS = 64e-5
HGRN_HEAD = 128
DIFF_HEAD = 64
ROPE_BASE = 10000.0
LN_EPS = 1e-5

LANES = 128
MXU_DIM = 256
VMEM_LIMIT = 56 * 1024 * 1024

CHUNK = 64
SUB = 16
TOK_BLOCK = 256
GROUP_HEADS = MXU_DIM // RWKV_HEAD


def _cparams(*sem):
    return pltpu.CompilerParams(dimension_semantics=sem, vmem_limit_bytes=VMEM_LIMIT)


def _nt(a, b):
    return lax.dot_general(a, b, (((1,), (1,)), ((), ())), preferred_element_type=F32)


def _tn(a, b):
    return lax.dot_general(a, b, (((0,), (0,)), ((), ())), preferred_element_type=F32)


def _mm(a, b):
    return jnp.dot(a, b, preferred_element_type=F32)


def _mm_exact(a, b):
    return jnp.dot(a, b, preferred_element_type=F32, precision=HI)


def _sigmoid(x):
    return 1.0 / (1.0 + jnp.exp(-x))


def _silu(x):
    return x * _sigmoid(x)


def _lane_tiles(x):
    return [x[:, j * LANES:(j + 1) * LANES] for j in range(x.shape[-1] // LANES)]


def _group64_sum(x):
    lane = lax.broadcasted_iota(jnp.int32, (1, LANES), 1)
    low = lane < RWKV_HEAD
    out = []
    for xt in _lane_tiles(x):
        s_all = jnp.sum(xt, axis=-1, keepdims=True)
        s_lo = jnp.sum(jnp.where(low, xt, 0.0), axis=-1, keepdims=True)
        out.append(jnp.where(low, s_lo, s_all - s_lo))
    return jnp.concatenate(out, axis=-1)


def _group128_mean(x):
    out = []
    for xt in _lane_tiles(x):
        out.append(jnp.broadcast_to(jnp.mean(xt, axis=-1, keepdims=True), xt.shape))
    return jnp.concatenate(out, axis=-1)


def _ada_kernel(c_ref, w_ref, b_ref, o_ref):
    cond = _silu(c_ref[...])
    o_ref[0] = _mm_exact(cond, w_ref[0]) + b_ref[0]


def _ada_mods(cvec, ada_w, ada_b):
    depth, d, d3 = ada_w.shape
    tn = 1024
    return pl.pallas_call(
        _ada_kernel,
        grid=(depth, d3 // tn),
        in_specs=[
            pl.BlockSpec((8, d), lambda l, j: (0, 0)),
            pl.BlockSpec((1, d, tn), lambda l, j: (l, 0, j)),
            pl.BlockSpec((1, 1, tn), lambda l, j: (l, 0, j)),
        ],
        out_specs=pl.BlockSpec((1, 8, tn), lambda l, j: (l, 0, j)),
        out_shape=jax.ShapeDtypeStruct((depth, 8, d3), F32),
        compiler_params=_cparams("arbitrary", "arbitrary"),
        name="ada_mods",
    )(cvec, ada_w, ada_b.reshape(depth, 1, d3))


def _proj_kernel(x_ref, sc_ref, sh_ref, w_ref, o_ref):
    h = x_ref[0] * (1.0 + sc_ref[0, 0]) + sh_ref[0, 0]
    o_ref[0] = _mm(h.astype(BF16), w_ref[...]).astype(o_ref.dtype)


def _project(xs, scale, shift, w, tn, out_dtype):
    b, lt, d = xs.shape
    n = w.shape[1]
    seg = lambda i: jnp.minimum(i, 1)
    return pl.pallas_call(
        _proj_kernel,
        grid=(n // tn, b, lt // TOK_BLOCK),
        in_specs=[
            pl.BlockSpec((1, TOK_BLOCK, d), lambda j, bb, i: (bb, i, 0)),
            pl.BlockSpec((1, 1, 1, d), lambda j, bb, i: (bb, seg(i), 0, 0)),
            pl.BlockSpec((1, 1, 1, d), lambda j, bb, i: (bb, seg(i), 0, 0)),
            pl.BlockSpec((d, tn), lambda j, bb, i: (0, j)),
        ],
        out_specs=pl.BlockSpec((1, TOK_BLOCK, tn), lambda j, bb, i: (bb, i, j)),
        out_shape=jax.ShapeDtypeStruct((b, lt, n), out_dtype),
        compiler_params=_cparams("arbitrary", "arbitrary", "arbitrary"),
        name="mod_proj",
    )(xs, scale, shift, w)


def _rwkv_prep_kernel(nblk, p_ref, prev_ref, next_ref, mup_ref, mun_ref, w0_ref, w2_ref, a0_ref, a2_ref,
                      kk_ref, ka_ref, rk_ref, tri_ref, ones_ref,
                      v_out, bonus_out, *dir_outs):
    i = pl.program_id(1)
    w = kk_ref.shape[-1]
    p = p_ref[0]
    row = lax.broadcasted_iota(jnp.int32, (TOK_BLOCK, 1), 0)
    prev_row = jnp.where(i >= 2, prev_ref[0][7:8, :], 0.0)
    next_row = jnp.where(jnp.logical_and(i >= 1, i < nblk - 1), next_ref[0][0:1, :], 0.0)
    prev = jnp.where(row == 0, prev_row, pltpu.roll(p, 1, 0))
    nxt = jnp.where(row == TOK_BLOCK - 1, next_row, pltpu.roll(p, TOK_BLOCK - 1, 0))
    ps = p + mup_ref[...] * (prev - p) + mun_ref[...] * (nxt - p)

    r = ps[:, 0:w]
    k = ps[:, w:2 * w]
    v = ps[:, 2 * w:3 * w]
    wlo = ps[:, 3 * w:3 * w + 2 * RWKV_LORA]
    alo = ps[:, 3 * w + 2 * RWKV_LORA:3 * w + 4 * RWKV_LORA]

    z = _mm_exact(jnp.tanh(wlo), w2_ref[...]) + w0_ref[...]
    w_log = -(jnp.maximum(-z, 0.0) + jnp.log(1.0 + jnp.exp(-jnp.abs(z)))) - 0.5
    lw = -jnp.exp(w_log)
    a = _sigmoid(_mm_exact(alo, a2_ref[...]) + a0_ref[...])

    kk = k * kk_ref[...]
    kk = kk * lax.rsqrt(_group64_sum(kk * kk) + 1e-12)
    bonus_out[0] = _group64_sum(r * k * rk_ref[...]) * v
    v_out[0] = v.astype(BF16)

    for d in range(2):
        rh_o, kh_o, bh_o, kap_o, kg_o, bg_o, gl_o = dir_outs[7 * d:7 * d + 7]
        lw_d = lw[:, d * w:(d + 1) * w]
        a_d = a[:, d * w:(d + 1) * w]
        kd = k * (1.0 + (a_d - 1.0) * ka_ref[...])
        b_d = kk * a_d
        g = _mm_exact(tri_ref[d], lw_d)
        g_all = _mm_exact(ones_ref[...], lw_d)
        e_neg = jnp.exp(-g)
        e_rest = jnp.exp(g_all - g)
        rh_o[0] = (r * jnp.exp(g)).astype(BF16)
        kh_o[0] = (kd * e_neg).astype(BF16)
        bh_o[0] = (b_d * e_neg).astype(BF16)
        kap_o[0] = (kk * jnp.exp(g - lw_d)).astype(BF16)
        kg_o[0] = (kd * e_rest).astype(BF16)
        bg_o[0] = (b_d * e_rest).astype(BF16)
        e_all = jnp.exp(g_all)
        for c in range(TOK_BLOCK // CHUNK):
            gl_o[0, c] = e_all[c * CHUNK:c * CHUNK + 1, :]


def _chunk_tri(reverse):
    t = np.arange(TOK_BLOCK)[:, None]
    s = np.arange(TOK_BLOCK)[None, :]
    same = (t // CHUNK) == (s // CHUNK)
    order = (s >= t) if reverse else (s <= t)
    return (same & order).astype(np.float32)


def _rwkv_prep(p_rwkv, mu_prev, mu_next, w0, w2, a0, a2, k_k, k_a, r_k):
    b, lt, sw = p_rwkv.shape
    w = k_k.shape[-1]
    nblk = lt // TOK_BLOCK
    nch = lt // CHUNK
    halo = TOK_BLOCK // 8
    zl = jnp.zeros((RWKV_LORA, w), F32)
    w2cat = jnp.concatenate([jnp.concatenate([w2[0], zl], 1), jnp.concatenate([zl, w2[1]], 1)], 0)
    a2cat = jnp.concatenate([jnp.concatenate([a2[0], zl], 1), jnp.concatenate([zl, a2[1]], 1)], 0)
    tri = jnp.asarray(np.stack([_chunk_tri(False), _chunk_tri(True)]))
    ones = jnp.asarray(np.maximum(_chunk_tri(False), _chunk_tri(True)))
    row = lambda u: u.reshape(1, -1)
    full = lambda shape: pl.BlockSpec(shape, lambda bb, i: (0,) * len(shape))
    tok = lambda n, dt: (pl.BlockSpec((1, TOK_BLOCK, n), lambda bb, i: (bb, i, 0)), jax.ShapeDtypeStruct((b, lt, n), dt))
    gl = (pl.BlockSpec((1, TOK_BLOCK // CHUNK, 1, w), lambda bb, i: (bb, i, 0, 0)),
          jax.ShapeDtypeStruct((b, nch, 1, w), F32))
    outs = [tok(w, BF16), tok(w, F32)] + 2 * ([tok(w, BF16)] * 6 + [gl])
    res = pl.pallas_call(
        functools.partial(_rwkv_prep_kernel, nblk),
        grid=(b, nblk),
        in_specs=[
            pl.BlockSpec((1, TOK_BLOCK, sw), lambda bb, i: (bb, i, 0)),
            pl.BlockSpec((1, 8, sw), lambda bb, i: (bb, jnp.maximum(i * halo - 1, 0), 0)),
            pl.BlockSpec((1, 8, sw), lambda bb, i: (bb, jnp.minimum((i + 1) * halo, lt // 8 - 1), 0)),
            full((1, sw)), full((1, sw)), full((1, 2 * w)), full((2 * RWKV_LORA, 2 * w)),
            full((1, 2 * w)), full((2 * RWKV_LORA, 2 * w)), full((1, w)), full((1, w)), full((1, w)),
            full((2, TOK_BLOCK, TOK_BLOCK)), full((TOK_BLOCK, TOK_BLOCK)),
        ],
        out_specs=[o[0] for o in outs],
        out_shape=[o[1] for o in outs],
        compiler_params=_cparams("arbitrary", "arbitrary"),
        name="rwkv_prep",
    )(p_rwkv, p_rwkv, p_rwkv, row(mu_prev), row(mu_next), row(w0), w2cat, row(a0), a2cat,
      row(k_k), row(k_a), row(r_k), tri, ones)
    v, bonus = res[0], res[1]
    return v, bonus, res[2:9], res[9:16]


def _rwkv_masks(reverse):
    n = GROUP_HEADS * CHUNK
    i = np.arange(n)[:, None]
    j = np.arange(n)[None, :]
    same_head = (i // CHUNK) == (j // CHUNK)
    before = (j > i) if reverse else (j < i)
    strict = same_head & before
    incl = same_head & (before | (i == j))
    levels = []
    size = 1
    while size < CHUNK:
        levels.append(same_head & ((i // (2 * size)) == (j // (2 * size))) & ((i // size) != (j // size)))
        size *= 2
    head_lane = (i // CHUNK) == (j // RWKV_HEAD)
    tri = np.stack([strict, incl]).astype(np.float32)
    return tri, np.stack(levels).astype(np.float32), head_lane.astype(np.float32)


def _rwkv_scan_kernel(rh_ref, kh_ref, bh_ref, kap_ref, kg_ref, bg_ref, v_ref, gl_ref,
                      tri_ref, lvl_ref, hm_ref, y_ref, s_ref):
    @pl.when(pl.program_id(1) == 0)
    def _():
        s_ref[...] = jnp.zeros_like(s_ref)

    n = GROUP_HEADS * CHUNK
    hm = hm_ref[...]
    strict = tri_ref[0]
    incl = tri_ref[1]
    eye = incl - strict
    n_levels = lvl_ref.shape[0]

    def stack(ref, g):
        x = ref[0, :, g * MXU_DIM:(g + 1) * MXU_DIM]
        return jnp.concatenate([x] * GROUP_HEADS, axis=0) * hm

    for g in range(y_ref.shape[-1] // MXU_DIM):
        kap = stack(kap_ref, g)
        bh = stack(bh_ref, g)
        kh = stack(kh_ref, g)
        rh = stack(rh_ref, g)
        vs = stack(v_ref, g)
        a_ab = _nt(kap, bh) * strict
        a_ak = _nt(kap, kh) * strict
        p_rb = _nt(rh, bh) * incl
        p_rk = _nt(rh, kh) * incl
        t = eye - a_ab * lvl_ref[0]
        for lv in range(1, n_levels):
            tb = t.astype(BF16)
            t = t - _mm(_mm(tb, (a_ab * lvl_ref[lv]).astype(BF16)).astype(BF16), tb)
        s0 = s_ref[g]
        s0b = s0.astype(BF16)
        wmat = _nt(kap, s0b) + _mm(a_ak.astype(BF16), vs)
        u = -_mm(t.astype(BF16), wmat.astype(BF16))
        ub = u.astype(BF16)
        ys = _nt(rh, s0b) + _mm(p_rb.astype(BF16), ub) + _mm(p_rk.astype(BF16), vs)
        y = ys[0:CHUNK]
        for h in range(1, GROUP_HEADS):
            y = y + ys[h * CHUNK:(h + 1) * CHUNK]
        y_ref[0, :, g * MXU_DIM:(g + 1) * MXU_DIM] = y
        gam = gl_ref[0, 0][:, g * MXU_DIM:(g + 1) * MXU_DIM]
        s_ref[g] = s0 * gam + _tn(ub, stack(bg_ref, g)) + _tn(vs, stack(kg_ref, g))


def _scan_chunk_index(c, nch, reverse):
    if not reverse:
        return c
    n_ctx = TOK_BLOCK // CHUNK
    return jnp.where(c < n_ctx, n_ctx - 1 - c, nch + n_ctx - 1 - c)


def _rwkv_scan(dir_ops, v, reverse):
    rh, kh, bh, kap, kg, bg, gl = dir_ops
    b, lt, w = v.shape
    nch = lt // CHUNK
    n = GROUP_HEADS * CHUNK
    tri, lvl, hm = _rwkv_masks(reverse)
    cidx = lambda c: _scan_chunk_index(c, nch, reverse)
    tok = pl.BlockSpec((1, CHUNK, w), lambda bb, c: (bb, cidx(c), 0))
    full = lambda shape: pl.BlockSpec(shape, lambda bb, c: (0,) * len(shape))
    return pl.pallas_call(
        _rwkv_scan_kernel,
        grid=(b, nch),
        in_specs=[tok] * 7 + [
            pl.BlockSpec((1, 1, 1, w), lambda bb, c: (bb, cidx(c), 0, 0)),
            full(tri.shape), full(lvl.shape), full(hm.shape),
        ],
        out_specs=tok,
        out_shape=jax.ShapeDtypeStruct((b, lt, w), F32),
        scratch_shapes=[pltpu.VMEM((w // MXU_DIM, n, n), F32)],
        compiler_params=_cparams("arbitrary", "arbitrary"),
        name="rwkv_scan_rev" if reverse else "rwkv_scan_fwd",
    )(rh, kh, bh, kap, kg, bg, v, gl, jnp.asarray(tri), jnp.asarray(lvl), jnp.asarray(hm, dtype=BF16))


def _hgrn_consts(reverse):
    t = np.arange(CHUNK)[:, None]
    s = np.arange(CHUNK)[None, :]
    order = (s >= t) if reverse else (s <= t)
    tri = order.astype(np.float32)
    earlier_sub = ((s // SUB) > (t // SUB)) if reverse else ((s // SUB) < (t // SUB))
    sub_before = earlier_sub.astype(np.float32)
    return np.stack([tri, sub_before])


def _hgrn_scan_kernel(reverse, q_ref, f_ref, i_ref, lb_ref, cm_ref, o_ref, s_ref, g_scr, gb_scr, q_scr, k_scr):
    @pl.when(pl.program_id(1) == 0)
    def _():
        s_ref[...] = jnp.zeros_like(s_ref)

    n_sub = CHUNK // SUB
    n_heads = q_ref.shape[-1] // HGRN_HEAD
    n_chunks = TOK_BLOCK // CHUNK
    sub_order = list(range(n_sub))[::-1] if reverse else list(range(n_sub))
    row = lax.broadcasted_iota(jnp.int32, (CHUNK, 1), 0)
    row_sub = row // SUB
    srow = lax.broadcasted_iota(jnp.int32, (SUB, 1), 0)
    lane = lax.broadcasted_iota(jnp.int32, (1, LANES), 1)
    last_row = 0 if reverse else CHUNK - 1

    for ci in range(n_chunks):
        c = n_chunks - 1 - ci if reverse else ci
        rows = pl.ds(c * CHUNK, CHUNK)
        kgate = (1.0 - lb_ref[...]) * _sigmoid(-f_ref[0, rows, :])
        lf = jnp.log1p(-kgate)
        g_scr[...] = _mm_exact(cm_ref[0], lf)
        gb_scr[...] = _mm_exact(cm_ref[1], lf)
        q_scr[...] = _silu(q_ref[0, rows, :])
        k_scr[...] = kgate

        def head(h, carry):
            cols = pl.ds(pl.multiple_of(h * HGRN_HEAD, HGRN_HEAD), HGRN_HEAD)
            g = g_scr[:, cols]
            gb = gb_scr[:, cols]
            q = q_scr[:, cols]
            k = k_scr[:, cols]
            v = i_ref[0, rows, cols]
            vb = v.astype(BF16)
            s0 = s_ref[h]
            q_in = q * jnp.exp(g - gb)
            kts, qts = [], []
            for si in sub_order[1:]:
                first = si * SUB + (SUB - 1 if reverse else 0)
                gb_i = gb[first:first + 1, :]
                earlier = (row_sub > si) if reverse else (row_sub < si)
                kts.append(jnp.where(earlier, k * jnp.exp(jnp.minimum(gb_i - g, 0.0)), 0.0).astype(BF16))
                qts.append(jnp.where(row_sub == si, q_in, 0.0).astype(BF16))
            at = _nt(jnp.concatenate(kts, axis=-1), jnp.concatenate(qts, axis=-1))
            diag = []
            for si in range(n_sub):
                blk = slice(si * SUB, (si + 1) * SUB)
                g_b, k_b = g[blk], k[blk]
                acc = jnp.zeros((SUB, LANES), F32)
                for tt in range(SUB):
                    t = si * SUB + tt
                    e = jnp.exp(jnp.minimum(g[t:t + 1, :] - g_b, 0.0))
                    col = jnp.sum(k_b * e * q[t:t + 1, :], axis=-1, keepdims=True)
                    valid = (srow >= tt) if reverse else (srow <= tt)
                    acc = jnp.where(jnp.logical_and(lane == t, valid), col, acc)
                diag.append(acc)
            at = at + jnp.concatenate(diag, axis=0)[:, :CHUNK]
            o = _tn(at.astype(BF16), vb) + _nt((q * jnp.exp(g)).astype(BF16), s0.astype(BF16))
            o_ref[0, rows, cols] = o
            g_last = g[last_row:last_row + 1, :]
            k_out = (k * jnp.exp(g_last - g)).astype(BF16)
            s_ref[h] = s0 * jnp.exp(g_last) + _tn(vb, k_out)
            return carry

        lax.fori_loop(0, n_heads, head, 0)


def _hgrn_scan(p_hgrn, lb_d, d, reverse):
    b, lt, w4 = p_hgrn.shape
    w = w4 // 4
    nblk = lt // TOK_BLOCK
    bidx = (lambda i: jnp.where(i == 0, 0, nblk - i)) if reverse else (lambda i: i)
    col = lambda j: pl.BlockSpec((1, TOK_BLOCK, w), lambda bb, i: (bb, bidx(i), j))
    cm = jnp.asarray(_hgrn_consts(reverse))
    return pl.pallas_call(
        functools.partial(_hgrn_scan_kernel, reverse),
        grid=(b, nblk),
        in_specs=[col(0), col(1 + d), col(3),
                  pl.BlockSpec((1, w), lambda bb, i: (0, 0)),
                  pl.BlockSpec(cm.shape, lambda bb, i: (0, 0, 0))],
        out_specs=col(0),
        out_shape=jax.ShapeDtypeStruct((b, lt, w), F32),
        scratch_shapes=[pltpu.VMEM((w // HGRN_HEAD, HGRN_HEAD, HGRN_HEAD), F32)]
        + [pltpu.VMEM((CHUNK, w), F32)] * 4,
        compiler_params=_cparams("arbitrary", "arbitrary"),
        name="hgrn_scan_rev" if reverse else "hgrn_scan_fwd",
    )(p_hgrn, p_hgrn, p_hgrn, lb_d.reshape(1, w), cm)


def _residual_ln(alpha, x, gm, proj, lng, lnb):
    z = alpha * x + gm * proj
    mu = jnp.mean(z, axis=-1, keepdims=True)
    zc = z - mu
    var = jnp.mean(zc * zc, axis=-1, keepdims=True)
    return zc * lax.rsqrt(var + LN_EPS) * lng + lnb


def _even_out_kernel(alpha, ya0_ref, ya1_ref, bonus_ref, yb0_ref, yb1_ref, gate_ref, x_ref, gm_ref, w_ref,
                     lnxg_ref, lnxb_ref, ng_ref, lng_ref, lnb_ref, o_ref):
    y = ya0_ref[0] + ya1_ref[0]
    mu = _group64_sum(y) * (1.0 / RWKV_HEAD)
    yc = y - mu
    var = _group64_sum(yc * yc) * (1.0 / RWKV_HEAD)
    ya = yc * lax.rsqrt(var + RWKV_GN_EPS) * lnxg_ref[...] + lnxb_ref[...] + bonus_ref[0]
    o = yb0_ref[0] + yb1_ref[0]
    yb = o * lax.rsqrt(_group128_mean(o * o) + LN_EPS) * ng_ref[...]
    ycat = jnp.concatenate([ya, yb], axis=-1) * _silu(gate_ref[0])
    proj = _mm(ycat.astype(BF16), w_ref[...])
    o_ref[0] = _residual_ln(alpha, x_ref[0], gm_ref[0, 0], proj, lng_ref[...], lnb_ref[...])


def _odd_out_kernel(alpha, y_ref, x_ref, gm_ref, w_ref, lng_ref, lnb_ref, o_ref):
    proj = _mm(y_ref[0].astype(BF16), w_ref[...])
    o_ref[0] = _residual_ln(alpha, x_ref[0], gm_ref[0, 0], proj, lng_ref[...], lnb_ref[...])


def _tok_spec(n):
    return pl.BlockSpec((1, TOK_BLOCK, n), lambda bb, i: (bb, i, 0))


def _full2(shape):
    return pl.BlockSpec(shape, lambda bb, i: (0,) * len(shape))


def _seg_spec(d):
    return pl.BlockSpec((1, 1, 1, d), lambda bb, i: (bb, jnp.minimum(i, 1), 0, 0))


def _even_out(alpha, ya0, ya1, bonus, yb0, yb1, gate, xs, gmod, w_out, lnx_g, lnx_b, norm_g, ln_g, ln_b):
    b, lt, d = xs.shape
    w = ya0.shape[-1]
    di = gate.shape[-1]
    row = lambda u: u.reshape(1, -1)
    return pl.pallas_call(
        functools.partial(_even_out_kernel, alpha),
        grid=(b, lt // TOK_BLOCK),
        in_specs=[_tok_spec(w)] * 5 + [_tok_spec(di), _tok_spec(d), _seg_spec(d), _full2((di, d)),
                                       _full2((1, w)), _full2((1, w)), _full2((1, w)), _full2((1, d)), _full2((1, d))],
        out_specs=_tok_spec(d),
        out_shape=jax.ShapeDtypeStruct((b, lt, d), F32),
        compiler_params=_cparams("arbitrary", "arbitrary"),
        name="even_out",
    )(ya0, ya1, bonus, yb0, yb1, gate, xs, gmod, w_out, row(lnx_g), row(lnx_b),
      row(jnp.tile(norm_g, w // HGRN_HEAD)), row(ln_g), row(ln_b))


def _odd_out(alpha, y, xs, gmod, w_out, ln_g, ln_b):
    b, lt, d = xs.shape
    di = y.shape[-1]
    row = lambda u: u.reshape(1, -1)
    return pl.pallas_call(
        functools.partial(_odd_out_kernel, alpha),
        grid=(b, lt // TOK_BLOCK),
        in_specs=[_tok_spec(di), _tok_spec(d), _seg_spec(d), _full2((di, d)), _full2((1, d)), _full2((1, d))],
        out_specs=_tok_spec(d),
        out_shape=jax.ShapeDtypeStruct((b, lt, d), F32),
        compiler_params=_cparams("arbitrary", "arbitrary"),
        name="odd_out",
    )(y, xs, gmod, w_out, row(ln_g), row(ln_b))


def _rope(x, cos, sin):
    lane = lax.broadcasted_iota(jnp.int32, (1, LANES), 1)
    first = (lane % 32) < 16
    partner = jnp.where(first, pltpu.roll(x, LANES - 16, 1), pltpu.roll(x, 16, 1))
    return x * cos + partner * sin


def _attn_kernel(lam_init, tk, lam_ref, q_ref, k_ref, v_ref, g_ref, cos_ref, sin_ref, sg_ref, y_ref, kr_scr):
    qi = pl.program_id(2)
    n_kv = k_ref.shape[1] // tk

    @pl.when(qi == 0)
    def _():
        def body(j, carry):
            rows = pl.ds(pl.multiple_of(j * tk, tk), tk)
            kr_scr[rows, :] = _rope(k_ref[0, rows, :].astype(F32), cos_ref[rows, :], sin_ref[rows, :]).astype(BF16)
            return carry
        lax.fori_loop(0, n_kv, body, 0)

    tq = q_ref.shape[1]
    qrows = pl.ds(pl.multiple_of(qi * tq, tq), tq)
    q = _rope(q_ref[0].astype(F32), cos_ref[qrows, :], sin_ref[qrows, :]) * (DIFF_HEAD ** -0.5)
    lane = lax.broadcasted_iota(jnp.int32, (1, LANES), 1)
    qm = [jnp.where(lane < DIFF_HEAD, q, 0.0).astype(BF16), jnp.where(lane >= DIFF_HEAD, q, 0.0).astype(BF16)]

    def body(j, carry):
        rows = pl.ds(pl.multiple_of(j * tk, tk), tk)
        kb = kr_scr[rows, :]
        vb = v_ref[0, rows, :]
        out = []
        for m in range(2):
            mx, l, acc = carry[3 * m:3 * m + 3]
            s = _nt(qm[m], kb)
            mx_new = jnp.maximum(mx, jnp.max(s, axis=-1, keepdims=True))
            corr = jnp.exp(mx - mx_new)
            p = jnp.exp(s - mx_new)
            out += [mx_new, corr * l + jnp.sum(p, axis=-1, keepdims=True), corr * acc + _mm(p.astype(BF16), vb)]
        return tuple(out)

    init = (jnp.full((tq, 1), -jnp.inf, F32), jnp.zeros((tq, 1), F32), jnp.zeros((tq, LANES), F32)) * 2
    m0, l0, acc0, m1, l1, acc1 = lax.fori_loop(0, jnp.where(qi == 0, 1, n_kv), body, init)
    o = acc0 / l0 - lam_ref[0] * (acc1 / l1)
    y = o * lax.rsqrt(jnp.mean(o * o, axis=-1, keepdims=True) + LN_EPS) * sg_ref[...] * (1.0 - lam_init)
    y_ref[0] = y * _silu(g_ref[0].astype(F32))


def _diff_attention(p_odd, lam, subln_g, lam_init, cos, sin):
    b, lt, n4 = p_odd.shape
    di = n4 // 4
    nh = di // LANES
    tq = tk = TOK_BLOCK
    head = lambda off: pl.BlockSpec((1, tq, LANES), lambda bb, h, i: (bb, i, off * nh + h))
    whole = lambda off: pl.BlockSpec((1, lt, LANES), lambda bb, h, i: (bb, 0, off * nh + h))
    tab = pl.BlockSpec((lt, LANES), lambda bb, h, i: (0, 0))
    return pl.pallas_call(
        functools.partial(_attn_kernel, lam_init, tk),
        grid=(b, nh, lt // tq),
        in_specs=[pl.BlockSpec(memory_space=pltpu.SMEM), head(0), whole(1), whole(2), head(3), tab, tab,
                  pl.BlockSpec((1, LANES), lambda bb, h, i: (0, 0))],
        out_specs=pl.BlockSpec((1, tq, LANES), lambda bb, h, i: (bb, i, h)),
        out_shape=jax.ShapeDtypeStruct((b, lt, di), F32),
        scratch_shapes=[pltpu.VMEM((lt, LANES), BF16)],
        compiler_params=_cparams("arbitrary", "arbitrary", "arbitrary"),
        name="diff_attn",
    )(lam.reshape(1), p_odd, p_odd, p_odd, p_odd, cos, sin, subln_g.reshape(1, LANES))


def _rope_tables(n_ctx, t):
    quarter = DIFF_HEAD // 4
    inv = ROPE_BASE ** (-jnp.arange(quarter, dtype=F32) / quarter)
    pos = jnp.arange(t)
    rows = (pos // GRID_W).astype(F32)[:, None] * inv
    cols = (pos % GRID_W).astype(F32)[:, None] * inv
    cos64 = jnp.concatenate([jnp.cos(rows), jnp.cos(rows), jnp.cos(cols), jnp.cos(cols)], -1)
    sin64 = jnp.concatenate([-jnp.sin(rows), jnp.sin(rows), -jnp.sin(cols), jnp.sin(cols)], -1)
    cos = jnp.concatenate([jnp.ones((n_ctx, LANES), F32), jnp.tile(cos64, (1, 2))], 0)
    sin = jnp.concatenate([jnp.zeros((n_ctx, LANES), F32), jnp.tile(sin64, (1, 2))], 0)
    return cos, sin


def _even_layer(alpha, xs, scale, shift, gmod, w_in, w_out, mu_prev, mu_next, w0, w2, a0, a2, k_k, k_a, r_k,
                lnx_g, lnx_b, lb, norm_g, ln_g, ln_b):
    w = k_k.shape[-1]
    sw = mu_prev.shape[-1]
    hw = 4 * lb.shape[-1]
    w_in = w_in.astype(BF16)
    p_rwkv = _project(xs, scale, shift, w_in[:, :sw], sw, F32)
    p_hgrn = _project(xs, scale, shift, w_in[:, sw:sw + hw], hw // 2, F32)
    gate = _project(xs, scale, shift, w_in[:, sw + hw:], w_in.shape[1] - sw - hw, F32)
    v, bonus, ops_f, ops_r = _rwkv_prep(p_rwkv, mu_prev, mu_next, w0, w2, a0, a2, k_k, k_a, r_k)
    ya0 = _rwkv_scan(ops_f, v, False)
    ya1 = _rwkv_scan(ops_r, v, True)
    yb0 = _hgrn_scan(p_hgrn, lb[0], 0, False)
    yb1 = _hgrn_scan(p_hgrn, lb[1], 1, True)
    return _even_out(alpha, ya0, ya1, bonus, yb0, yb1, gate, xs, gmod, w_out.astype(BF16),
                     lnx_g, lnx_b, norm_g, ln_g, ln_b)


def _odd_layer(alpha, xs, scale, shift, gmod, w_in, w_out, lam_p, subln_g, lam_init, cos, sin, ln_g, ln_b):
    p_odd = _project(xs, scale, shift, w_in.astype(BF16), 2048, BF16)
    lam = jnp.exp(jnp.sum(lam_p[0] * lam_p[1])) - jnp.exp(jnp.sum(lam_p[2] * lam_p[3])) + lam_init
    y = _diff_attention(p_odd, lam, subln_g, lam_init, cos, sin)
    return _odd_out(alpha, y, xs, gmod, w_out.astype(BF16), ln_g, ln_b)


def kernel(x, c, ctx, c_ctx, ada_w, ada_b, ln_g, ln_b, even_w_in, even_w_out, rwkv_mu_prev, rwkv_mu_next, rwkv_w0, rwkv_w2, rwkv_a0, rwkv_a2, rwkv_k_k, rwkv_k_a, rwkv_r_k, rwkv_lnx_g, rwkv_lnx_b, hgrn_lb_logits, hgrn_norm_g, odd_w_in, odd_w_out, diff_lambda, diff_subln_g):
    b, t, d = x.shape
    n_ctx = ctx.shape[1]
    depth = ada_w.shape[0]
    assert n_ctx == TOK_BLOCK and t % TOK_BLOCK == 0 and b + 1 <= 8
    alpha = (2.0 * depth) ** 0.25

    xs = jnp.concatenate([ctx, x], axis=1)
    cvec = jnp.concatenate([c, c_ctx[None], jnp.zeros((8 - b - 1, d), F32)], axis=0)
    mods = _ada_mods(cvec, ada_w, ada_b)
    lb_all = jax.nn.softmax(hgrn_lb_logits.astype(F32), axis=0)
    lb_all = jnp.cumsum(lb_all, axis=0) - lb_all[0]
    cos, sin = _rope_tables(n_ctx, t)

    for layer in range(depth):
        m = mods[layer]
        per_seg = lambda u: jnp.stack([jnp.broadcast_to(u[b], (b, d)), u[:b]], axis=1)[:, :, None, :]
        shift, scale, gmod = (per_seg(m[:, j * d:(j + 1) * d]) for j in range(3))
        j = layer // 2
        if layer % 2 == 0:
            xs = _even_layer(alpha, xs, scale, shift, gmod, even_w_in[j], even_w_out[j], rwkv_mu_prev[j],
                             rwkv_mu_next[j], rwkv_w0[j], rwkv_w2[j], rwkv_a0[j], rwkv_a2[j], rwkv_k_k[j],
                             rwkv_k_a[j], rwkv_r_k[j], rwkv_lnx_g[j], rwkv_lnx_b[j], lb_all[j], hgrn_norm_g[j],
                             ln_g[layer], ln_b[layer])
        else:
            lam_init = 0.8 - 0.6 * float(np.exp(-0.3 * layer))
            xs = _odd_layer(alpha, xs, scale, shift, gmod, odd_w_in[j], odd_w_out[j], diff_lambda[j],
                            diff_subln_g[j], lam_init, cos, sin, ln_g[layer], ln_b[layer])
    return xs[:, n_ctx:]
```

```python
import functools

import numpy as np
import jax
import jax.numpy as jnp
from jax import lax
from jax.experimental import pallas as pl
from jax.experimental.pallas import tpu as pltpu

F32 = jnp.float32
BF16 = jnp.bfloat16
HI = lax.Precision.HIGHEST

GRID_W = 64
RWKV_HEAD = 64
RWKV_LORA = 64
RWKV_GN_EPS = 64e-5
HGRN_HEAD = 128
DIFF_HEAD = 64
ROPE_BASE = 10000.0
LN_EPS = 1e-5

LANES = 128
MXU_DIM = 256
VMEM_LIMIT = 56 * 1024 * 1024

CHUNK = 64
SUB = 16
TOK_BLOCK = 256
GROUP_HEADS = MXU_DIM // RWKV_HEAD


def _cparams(*sem):
    return pltpu.CompilerParams(dimension_semantics=sem, vmem_limit_bytes=VMEM_LIMIT)


def _nt(a, b):
    return lax.dot_general(a, b, (((1,), (1,)), ((), ())), preferred_element_type=F32)


def _tn(a, b):
    return lax.dot_general(a, b, (((0,), (0,)), ((), ())), preferred_element_type=F32)


def _mm(a, b):
    return jnp.dot(a, b, preferred_element_type=F32)


def _mm_exact(a, b):
    return jnp.dot(a, b, preferred_element_type=F32, precision=HI)


def _sigmoid(x):
    return 1.0 / (1.0 + jnp.exp(-x))


def _silu(x):
    return x * _sigmoid(x)


def _lane_tiles(x):
    return [x[:, j * LANES:(j + 1) * LANES] for j in range(x.shape[-1] // LANES)]


def _group64_sum(x):
    lane = lax.broadcasted_iota(jnp.int32, (1, LANES), 1)
    low = lane < RWKV_HEAD
    out = []
    for xt in _lane_tiles(x):
        s_all = jnp.sum(xt, axis=-1, keepdims=True)
        s_lo = jnp.sum(jnp.where(low, xt, 0.0), axis=-1, keepdims=True)
        out.append(jnp.where(low, s_lo, s_all - s_lo))
    return jnp.concatenate(out, axis=-1)


def _group128_mean(x):
    out = []
    for xt in _lane_tiles(x):
        out.append(jnp.broadcast_to(jnp.mean(xt, axis=-1, keepdims=True), xt.shape))
    return jnp.concatenate(out, axis=-1)


def _ada_kernel(c_ref, w_ref, b_ref, o_ref):
    cond = _silu(c_ref[...])
    o_ref[0] = _mm_exact(cond, w_ref[0]) + b_ref[0]


def _ada_mods(cvec, ada_w, ada_b):
    depth, d, d3 = ada_w.shape
    tn = 1024
    return pl.pallas_call(
        _ada_kernel,
        grid=(depth, d3 // tn),
        in_specs=[
            pl.BlockSpec((8, d), lambda l, j: (0, 0)),
            pl.BlockSpec((1, d, tn), lambda l, j: (l, 0, j)),
            pl.BlockSpec((1, 1, tn), lambda l, j: (l, 0, j)),
        ],
        out_specs=pl.BlockSpec((1, 8, tn), lambda l, j: (l, 0, j)),
        out_shape=jax.ShapeDtypeStruct((depth, 8, d3), F32),
        compiler_params=_cparams("arbitrary", "arbitrary"),
        name="ada_mods",
    )(cvec, ada_w, ada_b.reshape(depth, 1, d3))


def _seg(i, nblk):
    return jnp.where(i == nblk - 1, 0, 1)


def _modulate(x_ref, sc_ref, sh_ref):
    return (x_ref[0] * (1.0 + sc_ref[0, 0]) + sh_ref[0, 0]).astype(BF16)


def _proj_kernel(x_ref, sc_ref, sh_ref, w_ref, o_ref):
    o_ref[0] = _mm(_modulate(x_ref, sc_ref, sh_ref), w_ref[...]).astype(o_ref.dtype)


def _proj_rope_kernel(x_ref, sc_ref, sh_ref, w_ref, cos_ref, sin_ref, o_ref):
    p = _mm(_modulate(x_ref, sc_ref, sh_ref), w_ref[...])
    reps = p.shape[-1] // LANES
    cos = jnp.concatenate([cos_ref[...]] * reps, axis=-1)
    sin = jnp.concatenate([sin_ref[...]] * reps, axis=-1)
    o_ref[0] = _rope(p, cos, sin).astype(o_ref.dtype)


def _proj_t_kernel(x_ref, sc_ref, sh_ref, wt_ref, o_ref):
    o_ref[0] = _nt(wt_ref[...], _modulate(x_ref, sc_ref, sh_ref)).astype(o_ref.dtype)


def _project(xs, scale, shift, w, tn, out_dtype, rope=None):
    b, lt, d = xs.shape
    n = w.shape[1]
    nblk = lt // TOK_BLOCK
    in_specs = [
        pl.BlockSpec((1, TOK_BLOCK, d), lambda j, bb, i: (bb, i, 0)),
        pl.BlockSpec((1, 1, 1, d), lambda j, bb, i: (bb, _seg(i, nblk), 0, 0)),
        pl.BlockSpec((1, 1, 1, d), lambda j, bb, i: (bb, _seg(i, nblk), 0, 0)),
        pl.BlockSpec((d, tn), lambda j, bb, i: (0, j)),
    ]
    args = (xs, scale, shift, w)
    if rope is not None:
        in_specs += [pl.BlockSpec((TOK_BLOCK, LANES), lambda j, bb, i: (i, 0))] * 2
        args += tuple(rope)
    return pl.pallas_call(
        _proj_kernel if rope is None else _proj_rope_kernel,
        grid=(n // tn, b, nblk),
        in_specs=in_specs,
        out_specs=pl.BlockSpec((1, TOK_BLOCK, tn), lambda j, bb, i: (bb, i, j)),
        out_shape=jax.ShapeDtypeStruct((b, lt, n), out_dtype),
        compiler_params=_cparams("arbitrary", "arbitrary", "arbitrary"),
        name="mod_proj" if rope is None else "mod_proj_rope",
    )(*args)


def _project_t(xs, scale, shift, wt, out_dtype):
    b, lt, d = xs.shape
    n = wt.shape[0]
    nblk = lt // TOK_BLOCK
    return pl.pallas_call(
        _proj_t_kernel,
        grid=(b, nblk),
        in_specs=[
            pl.BlockSpec((1, TOK_BLOCK, d), lambda bb, i: (bb, i, 0)),
            pl.BlockSpec((1, 1, 1, d), lambda bb, i: (bb, _seg(i, nblk), 0, 0)),
            pl.BlockSpec((1, 1, 1, d), lambda bb, i: (bb, _seg(i, nblk), 0, 0)),
            pl.BlockSpec((n, d), lambda bb, i: (0, 0)),
        ],
        out_specs=pl.BlockSpec((1, n, TOK_BLOCK), lambda bb, i: (bb, 0, i)),
        out_shape=jax.ShapeDtypeStruct((b, n, lt), out_dtype),
        compiler_params=_cparams("arbitrary", "arbitrary"),
        name="mod_proj_t",
    )(xs, scale, shift, wt)


def _rwkv_prep_kernel(nblk, p_ref, prev_ref, next_ref, mup_ref, mun_ref, w0_ref, w2_ref, a0_ref, a2_ref,
                      kk_ref, ka_ref, rk_ref, tri_ref, ones_ref,
                      v_out, bonus_out, *dir_outs):
    i = pl.program_id(1)
    w = kk_ref.shape[-1]
    p = p_ref[0]
    row = lax.broadcasted_iota(jnp.int32, (TOK_BLOCK, 1), 0)
    prev_row = jnp.where(jnp.logical_and(i >= 1, i <= nblk - 2), prev_ref[0][7:8, :], 0.0)
    next_row = jnp.where(i <= nblk - 3, next_ref[0][0:1, :], 0.0)
    prev = jnp.where(row == 0, prev_row, pltpu.roll(p, 1, 0))
    nxt = jnp.where(row == TOK_BLOCK - 1, next_row, pltpu.roll(p, TOK_BLOCK - 1, 0))
    ps = p + mup_ref[...] * (prev - p) + mun_ref[...] * (nxt - p)

    r = ps[:, 0:w]
    k = ps[:, w:2 * w]
    v = ps[:, 2 * w:3 * w]
    wlo = ps[:, 3 * w:3 * w + 2 * RWKV_LORA]
    alo = ps[:, 3 * w + 2 * RWKV_LORA:3 * w + 4 * RWKV_LORA]

    z = _mm_exact(jnp.tanh(wlo), w2_ref[...]) + w0_ref[...]
    w_log = -(jnp.maximum(-z, 0.0) + jnp.log(1.0 + jnp.exp(-jnp.abs(z)))) - 0.5
    lw = -jnp.exp(w_log)
    a = _sigmoid(_mm_exact(alo, a2_ref[...]) + a0_ref[...])

    kk = k * kk_ref[...]
    kk = kk * lax.rsqrt(_group64_sum(kk * kk) + 1e-12)
    bonus_out[0] = _group64_sum(r * k * rk_ref[...]) * v
    v_out[0] = v.astype(BF16)

    for d in range(2):
        rh_o, kh_o, bh_o, kap_o, kg_o, bg_o, gl_o = dir_outs[7 * d:7 * d + 7]
        lw_d = lw[:, d * w:(d + 1) * w]
        a_d = a[:, d * w:(d + 1) * w]
        kd = k * (1.0 + (a_d - 1.0) * ka_ref[...])
        b_d = kk * a_d
        g = _mm_exact(tri_ref[d], lw_d)
        g_all = _mm_exact(ones_ref[...], lw_d)
        e_neg = jnp.exp(-g)
        e_rest = jnp.exp(g_all - g)
        rh_o[0] = (r * jnp.exp(g)).astype(BF16)
        kh_o[0] = (kd * e_neg).astype(BF16)
        bh_o[0] = (b_d * e_neg).astype(BF16)
        kap_o[0] = (kk * jnp.exp(g - lw_d)).astype(BF16)
        kg_o[0] = (kd * e_rest).astype(BF16)
        bg_o[0] = (b_d * e_rest).astype(BF16)
        e_all = jnp.exp(g_all)
        for c in range(TOK_BLOCK // CHUNK):
            gl_o[0, c] = e_all[c * CHUNK:c * CHUNK + 1, :]


def _chunk_tri(reverse):
    t = np.arange(TOK_BLOCK)[:, None]
    s = np.arange(TOK_BLOCK)[None, :]
    same = (t // CHUNK) == (s // CHUNK)
    order = (s >= t) if reverse else (s <= t)
    return (same & order).astype(np.float32)


def _rwkv_prep(p_rwkv, mu_prev, mu_next, w0, w2, a0, a2, k_k, k_a, r_k):
    b, lt, sw = p_rwkv.shape
    w = k_k.shape[-1]
    nblk = lt // TOK_BLOCK
    nch = lt // CHUNK
    halo = TOK_BLOCK // 8
    zl = jnp.zeros((RWKV_LORA, w), F32)
    w2cat = jnp.concatenate([jnp.concatenate([w2[0], zl], 1), jnp.concatenate([zl, w2[1]], 1)], 0)
    a2cat = jnp.concatenate([jnp.concatenate([a2[0], zl], 1), jnp.concatenate([zl, a2[1]], 1)], 0)
    tri = jnp.asarray(np.stack([_chunk_tri(False), _chunk_tri(True)]))
    ones = jnp.asarray(np.maximum(_chunk_tri(False), _chunk_tri(True)))
    row = lambda u: u.reshape(1, -1)
    full = lambda shape: pl.BlockSpec(shape, lambda bb, i: (0,) * len(shape))
    tok = lambda n, dt: (pl.BlockSpec((1, TOK_BLOCK, n), lambda bb, i: (bb, i, 0)), jax.ShapeDtypeStruct((b, lt, n), dt))
    gl = (pl.BlockSpec((1, TOK_BLOCK // CHUNK, 1, w), lambda bb, i: (bb, i, 0, 0)),
          jax.ShapeDtypeStruct((b, nch, 1, w), F32))
    outs = [tok(w, BF16), tok(w, F32)] + 2 * ([tok(w, BF16)] * 6 + [gl])
    res = pl.pallas_call(
        functools.partial(_rwkv_prep_kernel, nblk),
        grid=(b, nblk),
        in_specs=[
            pl.BlockSpec((1, TOK_BLOCK, sw), lambda bb, i: (bb, i, 0)),
            pl.BlockSpec((1, 8, sw), lambda bb, i: (bb, jnp.maximum(i * halo - 1, 0), 0)),
            pl.BlockSpec((1, 8, sw), lambda bb, i: (bb, jnp.minimum((i + 1) * halo, lt // 8 - 1), 0)),
            full((1, sw)), full((1, sw)), full((1, 2 * w)), full((2 * RWKV_LORA, 2 * w)),
            full((1, 2 * w)), full((2 * RWKV_LORA, 2 * w)), full((1, w)), full((1, w)), full((1, w)),
            full((2, TOK_BLOCK, TOK_BLOCK)), full((TOK_BLOCK, TOK_BLOCK)),
        ],
        out_specs=[o[0] for o in outs],
        out_shape=[o[1] for o in outs],
        compiler_params=_cparams("arbitrary", "arbitrary"),
        name="rwkv_prep",
    )(p_rwkv, p_rwkv, p_rwkv, row(mu_prev), row(mu_next), row(w0), w2cat, row(a0), a2cat,
      row(k_k), row(k_a), row(r_k), tri, ones)
    v, bonus = res[0], res[1]
    return v, bonus, res[2:9], res[9:16]


def _rwkv_masks(reverse):
    n = GROUP_HEADS * CHUNK
    i = np.arange(n)[:, None]
    j = np.arange(n)[None, :]
    same_head = (i // CHUNK) == (j // CHUNK)
    before = (j > i) if reverse else (j < i)
    strict = same_head & before
    incl = same_head & (before | (i == j))
    levels = []
    size = 1
    while size < CHUNK:
        levels.append(same_head & ((i // (2 * size)) == (j // (2 * size))) & ((i // size) != (j // size)))
        size *= 2
    head_lane = (i // CHUNK) == (j // RWKV_HEAD)
    tri = np.stack([strict, incl]).astype(np.float32)
    return tri, np.stack(levels).astype(np.float32), head_lane.astype(np.float32)


def _rwkv_scan_kernel(rh_ref, kh_ref, bh_ref, kap_ref, kg_ref, bg_ref, v_ref, gl_ref,
                      tri_ref, lvl_ref, hm_ref, y_ref, s_ref):
    @pl.when(pl.program_id(1) == 0)
    def _():
        s_ref[...] = jnp.zeros_like(s_ref)

    n = GROUP_HEADS * CHUNK
    hm = hm_ref[...]
    strict = tri_ref[0]
    incl = tri_ref[1]
    eye = incl - strict
    n_levels = lvl_ref.shape[0]

    def stack(ref, g):
        x = ref[0, :, g * MXU_DIM:(g + 1) * MXU_DIM]
        return jnp.concatenate([x] * GROUP_HEADS, axis=0) * hm

    for g in range(y_ref.shape[-1] // MXU_DIM):
        kap = stack(kap_ref, g)
        bh = stack(bh_ref, g)
        kh = stack(kh_ref, g)
        rh = stack(rh_ref, g)
        vs = stack(v_ref, g)
        a_ab = _nt(kap, bh) * strict
        a_ak = _nt(kap, kh) * strict
        p_rb = _nt(rh, bh) * incl
        p_rk = _nt(rh, kh) * incl
        t = eye - a_ab * lvl_ref[0]
        for lv in range(1, n_levels):
            tb = t.astype(BF16)
            t = t - _mm(_mm(tb, (a_ab * lvl_ref[lv]).astype(BF16)).astype(BF16), tb)
        s0 = s_ref[g]
        s0b = s0.astype(BF16)
        wmat = _nt(kap, s0b) + _mm(a_ak.astype(BF16), vs)
        u = -_mm(t.astype(BF16), wmat.astype(BF16))
        ub = u.astype(BF16)
        ys = _nt(rh, s0b) + _mm(p_rb.astype(BF16), ub) + _mm(p_rk.astype(BF16), vs)
        y = ys[0:CHUNK]
        for h in range(1, GROUP_HEADS):
            y = y + ys[h * CHUNK:(h + 1) * CHUNK]
        y_ref[0, :, g * MXU_DIM:(g + 1) * MXU_DIM] = y
        gam = gl_ref[0, 0][:, g * MXU_DIM:(g + 1) * MXU_DIM]
        s_ref[g] = s0 * gam + _tn(ub, stack(bg_ref, g)) + _tn(vs, stack(kg_ref, g))


def _scan_chunk_index(c, nch, reverse):
    if reverse:
        return nch - 1 - c
    n_ctx = TOK_BLOCK // CHUNK
    return jnp.where(c < n_ctx, nch - n_ctx + c, c - n_ctx)


def _rwkv_scan(dir_ops, v, reverse):
    rh, kh, bh, kap, kg, bg, gl = dir_ops
    b, lt, w = v.shape
    nch = lt // CHUNK
    n = GROUP_HEADS * CHUNK
    tri, lvl, hm = _rwkv_masks(reverse)
    cidx = lambda c: _scan_chunk_index(c, nch, reverse)
    tok = pl.BlockSpec((1, CHUNK, w), lambda bb, c: (bb, cidx(c), 0))
    full = lambda shape: pl.BlockSpec(shape, lambda bb, c: (0,) * len(shape))
    return pl.pallas_call(
        _rwkv_scan_kernel,
        grid=(b, nch),
        in_specs=[tok] * 7 + [
            pl.BlockSpec((1, 1, 1, w), lambda bb, c: (bb, cidx(c), 0, 0)),
            full(tri.shape), full(lvl.shape), full(hm.shape),
        ],
        out_specs=tok,
        out_shape=jax.ShapeDtypeStruct((b, lt, w), F32),
        scratch_shapes=[pltpu.VMEM((w // MXU_DIM, n, n), F32)],
        compiler_params=_cparams("arbitrary", "arbitrary"),
        name="rwkv_scan_rev" if reverse else "rwkv_scan_fwd",
    )(rh, kh, bh, kap, kg, bg, v, gl, jnp.asarray(tri), jnp.asarray(lvl), jnp.asarray(hm, dtype=BF16))


def _hgrn_consts(reverse):
    t = np.arange(CHUNK)[:, None]
    s = np.arange(CHUNK)[None, :]
    order = (s >= t) if reverse else (s <= t)
    tri = order.astype(np.float32)
    earlier_sub = ((s // SUB) > (t // SUB)) if reverse else ((s // SUB) < (t // SUB))
    sub_before = earlier_sub.astype(np.float32)
    return np.stack([tri, sub_before])


def _hgrn_scan_kernel(reverse, q_ref, f_ref, i_ref, lb_ref, cm_ref, o_ref, s_ref, g_scr, gb_scr, q_scr, k_scr):
    @pl.when(pl.program_id(1) == 0)
    def _():
        s_ref[...] = jnp.zeros_like(s_ref)

    n_sub = CHUNK // SUB
    n_heads = q_ref.shape[-1] // HGRN_HEAD
    n_chunks = TOK_BLOCK // CHUNK
    sub_order = list(range(n_sub))[::-1] if reverse else list(range(n_sub))
    row = lax.broadcasted_iota(jnp.int32, (CHUNK, 1), 0)
    row_sub = row // SUB
    srow = lax.broadcasted_iota(jnp.int32, (SUB, 1), 0)
    lane = lax.broadcasted_iota(jnp.int32, (1, LANES), 1)
    last_row = 0 if reverse else CHUNK - 1

    for ci in range(n_chunks):
        c = n_chunks - 1 - ci if reverse else ci
        rows = pl.ds(c * CHUNK, CHUNK)
        kgate = (1.0 - lb_ref[...]) * _sigmoid(-f_ref[0, rows, :])
        lf = jnp.log1p(-kgate)
        g_scr[...] = _mm_exact(cm_ref[0], lf)
        gb_scr[...] = _mm_exact(cm_ref[1], lf)
        q_scr[...] = _silu(q_ref[0, rows, :])
        k_scr[...] = kgate

        def head(h, carry):
            cols = pl.ds(pl.multiple_of(h * HGRN_HEAD, HGRN_HEAD), HGRN_HEAD)
            g = g_scr[:, cols]
            gb = gb_scr[:, cols]
            q = q_scr[:, cols]
            k = k_scr[:, cols]
            v = i_ref[0, rows, cols]
            vb = v.astype(BF16)
            s0 = s_ref[h]
            q_in = q * jnp.exp(g - gb)
            kts, qts = [], []
            for si in sub_order[1:]:
                first = si * SUB + (SUB - 1 if reverse else 0)
                gb_i = gb[first:first + 1, :]
                earlier = (row_sub > si) if reverse else (row_sub < si)
                kts.append(jnp.where(earlier, k * jnp.exp(jnp.minimum(gb_i - g, 0.0)), 0.0).astype(BF16))
                qts.append(jnp.where(row_sub == si, q_in, 0.0).astype(BF16))
            at = _nt(jnp.concatenate(kts, axis=-1), jnp.concatenate(qts, axis=-1))
            diag = []
            for si in range(n_sub):
                blk = slice(si * SUB, (si + 1) * SUB)
                g_b, k_b = g[blk], k[blk]
                acc = jnp.zeros((SUB, LANES), F32)
                for tt in range(SUB):
                    t = si * SUB + tt
                    e = jnp.exp(jnp.minimum(g[t:t + 1, :] - g_b, 0.0))
                    col = jnp.sum(k_b * e * q[t:t + 1, :], axis=-1, keepdims=True)
                    valid = (srow >= tt) if reverse else (srow <= tt)
                    acc = jnp.where(jnp.logical_and(lane == t, valid), col, acc)
                diag.append(acc)
            at = at + jnp.concatenate(diag, axis=0)[:, :CHUNK]
            o = _tn(at.astype(BF16), vb) + _nt((q * jnp.exp(g)).astype(BF16), s0.astype(BF16))
            o_ref[0, rows, cols] = o
            g_last = g[last_row:last_row + 1, :]
            k_out = (k * jnp.exp(g_last - g)).astype(BF16)
            s_ref[h] = s0 * jnp.exp(g_last) + _tn(vb, k_out)
            return carry

        lax.fori_loop(0, n_heads, head, 0)


def _hgrn_scan(p_hgrn, lb_d, d, reverse):
    b, lt, w4 = p_hgrn.shape
    w = w4 // 4
    nblk = lt // TOK_BLOCK
    bidx = (lambda i: nblk - 1 - i) if reverse else (lambda i: jnp.where(i == 0, nblk - 1, i - 1))
    col = lambda j: pl.BlockSpec((1, TOK_BLOCK, w), lambda bb, i: (bb, bidx(i), j))
    cm = jnp.asarray(_hgrn_consts(reverse))
    return pl.pallas_call(
        functools.partial(_hgrn_scan_kernel, reverse),
        grid=(b, nblk),
        in_specs=[col(0), col(1 + d), col(3),
                  pl.BlockSpec((1, w), lambda bb, i: (0, 0)),
                  pl.BlockSpec(cm.shape, lambda bb, i: (0, 0, 0))],
        out_specs=col(0),
        out_shape=jax.ShapeDtypeStruct((b, lt, w), F32),
        scratch_shapes=[pltpu.VMEM((w // HGRN_HEAD, HGRN_HEAD, HGRN_HEAD), F32)]
        + [pltpu.VMEM((CHUNK, w), F32)] * 4,
        compiler_params=_cparams("arbitrary", "arbitrary"),
        name="hgrn_scan_rev" if reverse else "hgrn_scan_fwd",
    )(p_hgrn, p_hgrn, p_hgrn, lb_d.reshape(1, w), cm)


def _residual_ln(alpha, x, gm, proj, lng, lnb):
    z = alpha * x + gm * proj
    mu = jnp.mean(z, axis=-1, keepdims=True)
    zc = z - mu
    var = jnp.mean(zc * zc, axis=-1, keepdims=True)
    return zc * lax.rsqrt(var + LN_EPS) * lng + lnb


def _even_out_kernel(alpha, ya0_ref, ya1_ref, bonus_ref, yb0_ref, yb1_ref, gate_ref, x_ref, gm_ref, w_ref,
                     lnxg_ref, lnxb_ref, ng_ref, lng_ref, lnb_ref, o_ref):
    y = ya0_ref[0] + ya1_ref[0]
    mu = _group64_sum(y) * (1.0 / RWKV_HEAD)
    yc = y - mu
    var = _group64_sum(yc * yc) * (1.0 / RWKV_HEAD)
    ya = yc * lax.rsqrt(var + RWKV_GN_EPS) * lnxg_ref[...] + lnxb_ref[...] + bonus_ref[0]
    o = yb0_ref[0] + yb1_ref[0]
    yb = o * lax.rsqrt(_group128_mean(o * o) + LN_EPS) * ng_ref[...]
    ycat = jnp.concatenate([ya, yb], axis=-1) * _silu(gate_ref[0])
    proj = _mm(ycat.astype(BF16), w_ref[...])
    o_ref[0] = _residual_ln(alpha, x_ref[0], gm_ref[0, 0], proj, lng_ref[...], lnb_ref[...])


def _odd_out_kernel(alpha, y_ref, x_ref, gm_ref, w_ref, lng_ref, lnb_ref, o_ref):
    proj = _mm(y_ref[0].astype(BF16), w_ref[...])
    o_ref[0] = _residual_ln(alpha, x_ref[0], gm_ref[0, 0], proj, lng_ref[...], lnb_ref[...])


def _tok_spec(n):
    return pl.BlockSpec((1, TOK_BLOCK, n), lambda bb, i: (bb, i, 0))


def _full2(shape):
    return pl.BlockSpec(shape, lambda bb, i: (0,) * len(shape))


def _seg_spec(d, nblk):
    return pl.BlockSpec((1, 1, 1, d), lambda bb, i: (bb, _seg(i, nblk), 0, 0))


def _even_out(alpha, ya0, ya1, bonus, yb0, yb1, gate, xs, gmod, w_out, lnx_g, lnx_b, norm_g, ln_g, ln_b):
    b, lt, d = xs.shape
    w = ya0.shape[-1]
    di = gate.shape[-1]
    row = lambda u: u.reshape(1, -1)
    return pl.pallas_call(
        functools.partial(_even_out_kernel, alpha),
        grid=(b, lt // TOK_BLOCK),
        in_specs=[_tok_spec(w)] * 5 + [_tok_spec(di), _tok_spec(d), _seg_spec(d, lt // TOK_BLOCK), _full2((di, d)),
                                       _full2((1, w)), _full2((1, w)), _full2((1, w)), _full2((1, d)), _full2((1, d))],
        out_specs=_tok_spec(d),
        out_shape=jax.ShapeDtypeStruct((b, lt, d), F32),
        compiler_params=_cparams("arbitrary", "arbitrary"),
        name="even_out",
    )(ya0, ya1, bonus, yb0, yb1, gate, xs, gmod, w_out, row(lnx_g), row(lnx_b),
      row(jnp.tile(norm_g, w // HGRN_HEAD)), row(ln_g), row(ln_b))


def _odd_out(alpha, y, xs, gmod, w_out, ln_g, ln_b):
    b, lt, d = xs.shape
    di = y.shape[-1]
    row = lambda u: u.reshape(1, -1)
    return pl.pallas_call(
        functools.partial(_odd_out_kernel, alpha),
        grid=(b, lt // TOK_BLOCK),
        in_specs=[_tok_spec(di), _tok_spec(d), _seg_spec(d, lt // TOK_BLOCK), _full2((di, d)), _full2((1, d)),
                  _full2((1, d))],
        out_specs=_tok_spec(d),
        out_shape=jax.ShapeDtypeStruct((b, lt, d), F32),
        compiler_params=_cparams("arbitrary", "arbitrary"),
        name="odd_out",
    )(y, xs, gmod, w_out, row(ln_g), row(ln_b))


def _rope(x, cos, sin):
    width = x.shape[-1]
    lane = lax.broadcasted_iota(jnp.int32, (1, width), 1)
    first = (lane % 32) < 16
    partner = jnp.where(first, pltpu.roll(x, width - 16, 1), pltpu.roll(x, 16, 1))
    return x * cos + partner * sin


def _attn_kernel(lam_init, tk, n_full, tail, lam_ref, q_ref, k_ref, vt_ref, g_ref, sg_ref, *rest):
    y_ref, m_scr, acc_scr = rest[-3:]
    tq = q_ref.shape[1]
    ones_rows = 16
    q = q_ref[0].astype(F32) * (DIFF_HEAD ** -0.5 * float(np.log2(np.e)))
    lane = lax.broadcasted_iota(jnp.int32, (1, LANES), 1)
    q_cat = jnp.concatenate([jnp.where(lane < DIFF_HEAD, q, 0.0), jnp.where(lane >= DIFF_HEAD, q, 0.0)],
                            axis=0).astype(BF16)
    m_scr[...] = jnp.full(m_scr.shape, -jnp.inf, F32)
    acc_scr[...] = jnp.zeros(acc_scr.shape, F32)

    def block(start, size):
        kb = k_ref[0, pl.ds(start, size), :]
        vta = jnp.concatenate([vt_ref[0, :, pl.ds(start, size)], jnp.ones((ones_rows, size), BF16)], axis=0)
        st = _nt(kb, q_cat)
        m_old = m_scr[...]
        m_new = jnp.maximum(m_old, jnp.max(st, axis=0, keepdims=True))
        m_scr[...] = m_new
        corr = jnp.exp2(m_old - m_new)
        pt = jnp.exp2(st - m_new).astype(BF16)
        for m in range(2):
            cols = slice(m * tq, (m + 1) * tq)
            acc_scr[m] = acc_scr[m] * corr[:, cols] + _mm(vta, pt[:, cols])

    if n_full:
        def body(j, carry):
            block(pl.multiple_of(j * tk, tk), tk)
            return carry
        lax.fori_loop(0, n_full, body, 0)
    block(n_full * tk, tail)

    a0 = acc_scr[0]
    a1 = acc_scr[1]
    o_t = a0[:LANES] / a0[LANES:LANES + 1] - lam_ref[0] * (a1[:LANES] / a1[LANES:LANES + 1])
    o = jnp.transpose(o_t)
    y = o * lax.rsqrt(jnp.mean(o * o, axis=-1, keepdims=True) + LN_EPS) * sg_ref[...] * (1.0 - lam_init)
    y_ref[0] = y * _silu(g_ref[0].astype(F32))


def _diff_attention(p_qk, p_vt, p_g, lam, subln_g, lam_init):
    b, lt, di = p_g.shape
    nh = di // LANES
    t = lt - TOK_BLOCK
    tq = tk = 512
    assert t % tq == 0
    ctx_blk = t // TOK_BLOCK
    sm = pl.BlockSpec(memory_space=pltpu.SMEM)
    sg_spec = pl.BlockSpec((1, LANES), lambda bb, h, i: (0, 0))
    scratch = lambda n: [pltpu.VMEM((1, 2 * n), F32), pltpu.VMEM((2, LANES + 16, n), F32)]
    args = (lam.reshape(1), p_qk, p_qk, p_vt, p_g, subln_g.reshape(1, LANES))
    y = pl.pallas_call(
        functools.partial(_attn_kernel, lam_init, tk, t // tk, TOK_BLOCK),
        grid=(b, nh, t // tq),
        in_specs=[sm,
                  pl.BlockSpec((1, tq, LANES), lambda bb, h, i: (bb, i, h)),
                  pl.BlockSpec((1, lt, LANES), lambda bb, h, i: (bb, 0, nh + h)),
                  pl.BlockSpec((1, LANES, lt), lambda bb, h, i: (bb, h, 0)),
                  pl.BlockSpec((1, tq, LANES), lambda bb, h, i: (bb, i, h)),
                  sg_spec],
        out_specs=pl.BlockSpec((1, tq, LANES), lambda bb, h, i: (bb, i, h)),
        out_shape=jax.ShapeDtypeStruct((b, lt, di), F32),
        scratch_shapes=scratch(tq),
        compiler_params=_cparams("arbitrary", "arbitrary", "arbitrary"),
        name="diff_attn",
    )(*args)
    blk = lambda bb, h, i: (bb, ctx_blk, h)
    return pl.pallas_call(
        functools.partial(_attn_kernel, lam_init, tk, 0, TOK_BLOCK),
        grid=(b, nh, 1),
        in_specs=[sm,
                  pl.BlockSpec((1, TOK_BLOCK, LANES), blk),
                  pl.BlockSpec((1, TOK_BLOCK, LANES), lambda bb, h, i: (bb, ctx_blk, nh + h)),
                  pl.BlockSpec((1, LANES, TOK_BLOCK), lambda bb, h, i: (bb, h, ctx_blk)),
                  pl.BlockSpec((1, TOK_BLOCK, LANES), blk),
                  sg_spec,
                  pl.BlockSpec(memory_space=pl.ANY)],
        out_specs=pl.BlockSpec((1, TOK_BLOCK, LANES), blk),
        out_shape=jax.ShapeDtypeStruct((b, lt, di), F32),
        scratch_shapes=scratch(TOK_BLOCK),
        input_output_aliases={6: 0},
        compiler_params=_cparams("arbitrary", "arbitrary", "arbitrary"),
        name="diff_attn_ctx",
    )(*args, y)


def _rope_tables(n_ctx, t):
    quarter = DIFF_HEAD // 4
    inv = ROPE_BASE ** (-jnp.arange(quarter, dtype=F32) / quarter)
    pos = jnp.arange(t)
    rows = (pos // GRID_W).astype(F32)[:, None] * inv
    cols = (pos % GRID_W).astype(F32)[:, None] * inv
    cos64 = jnp.concatenate([jnp.cos(rows), jnp.cos(rows), jnp.cos(cols), jnp.cos(cols)], -1)
    sin64 = jnp.concatenate([-jnp.sin(rows), jnp.sin(rows), -jnp.sin(cols), jnp.sin(cols)], -1)
    cos = jnp.concatenate([jnp.tile(cos64, (1, 2)), jnp.ones((n_ctx, LANES), F32)], 0)
    sin = jnp.concatenate([jnp.tile(sin64, (1, 2)), jnp.zeros((n_ctx, LANES), F32)], 0)
    return cos, sin


def _even_layer(alpha, xs, scale, shift, gmod, w_in, w_out, mu_prev, mu_next, w0, w2, a0, a2, k_k, k_a, r_k,
                lnx_g, lnx_b, lb, norm_g, ln_g, ln_b):
    w = k_k.shape[-1]
    sw = mu_prev.shape[-1]
    hw = 4 * lb.shape[-1]
    w_in = w_in.astype(BF16)
    p_rwkv = _project(xs, scale, shift, w_in[:, :sw], sw, F32)
    p_hgrn = _project(xs, scale, shift, w_in[:, sw:sw + hw], hw // 2, F32)
    gate = _project(xs, scale, shift, w_in[:, sw + hw:], w_in.shape[1] - sw - hw, F32)
    v, bonus, ops_f, ops_r = _rwkv_prep(p_rwkv, mu_prev, mu_next, w0, w2, a0, a2, k_k, k_a, r_k)
    ya0 = _rwkv_scan(ops_f, v, False)
    ya1 = _rwkv_scan(ops_r, v, True)
    yb0 = _hgrn_scan(p_hgrn, lb[0], 0, False)
    yb1 = _hgrn_scan(p_hgrn, lb[1], 1, True)
    return _even_out(alpha, ya0, ya1, bonus, yb0, yb1, gate, xs, gmod, w_out.astype(BF16),
                     lnx_g, lnx_b, norm_g, ln_g, ln_b)


def _odd_layer(alpha, xs, scale, shift, gmod, w_in, w_out, lam_p, subln_g, lam_init, cos, sin, ln_g, ln_b):
    di = w_out.shape[0]
    w_in = w_in.astype(BF16)
    p_qk = _project(xs, scale, shift, w_in[:, :2 * di], di, BF16, rope=(cos, sin))
    p_vt = _project_t(xs, scale, shift, jnp.transpose(w_in[:, 2 * di:3 * di]), BF16)
    p_g = _project(xs, scale, shift, w_in[:, 3 * di:], di, BF16)
    lam = jnp.exp(jnp.sum(lam_p[0] * lam_p[1])) - jnp.exp(jnp.sum(lam_p[2] * lam_p[3])) + lam_init
    y = _diff_attention(p_qk, p_vt, p_g, lam, subln_g, lam_init)
    return _odd_out(alpha, y, xs, gmod, w_out.astype(BF16), ln_g, ln_b)


def kernel(x, c, ctx, c_ctx, ada_w, ada_b, ln_g, ln_b, even_w_in, even_w_out, rwkv_mu_prev, rwkv_mu_next, rwkv_w0, rwkv_w2, rwkv_a0, rwkv_a2, rwkv_k_k, rwkv_k_a, rwkv_r_k, rwkv_lnx_g, rwkv_lnx_b, hgrn_lb_logits, hgrn_norm_g, odd_w_in, odd_w_out, diff_lambda, diff_subln_g):
    b, t, d = x.shape
    n_ctx = ctx.shape[1]
    depth = ada_w.shape[0]
    assert n_ctx == TOK_BLOCK and t % TOK_BLOCK == 0 and b + 1 <= 8
    alpha = (2.0 * depth) ** 0.25

    xs = jnp.concatenate([x, ctx], axis=1)
    cvec = jnp.concatenate([c, c_ctx[None], jnp.zeros((8 - b - 1, d), F32)], axis=0)
    mods = _ada_mods(cvec, ada_w, ada_b)
    lb_all = jax.nn.softmax(hgrn_lb_logits.astype(F32), axis=0)
    lb_all = jnp.cumsum(lb_all, axis=0) - lb_all[0]
    cos, sin = _rope_tables(n_ctx, t)

    for layer in range(depth):
        m = mods[layer]
        per_seg = lambda u: jnp.stack([jnp.broadcast_to(u[b], (b, d)), u[:b]], axis=1)[:, :, None, :]
        shift, scale, gmod = (per_seg(m[:, j * d:(j + 1) * d]) for j in range(3))
        j = layer // 2
        if layer % 2 == 0:
            xs = _even_layer(alpha, xs, scale, shift, gmod, even_w_in[j], even_w_out[j], rwkv_mu_prev[j],
                             rwkv_mu_next[j], rwkv_w0[j], rwkv_w2[j], rwkv_a0[j], rwkv_a2[j], rwkv_k_k[j],
                             rwkv_k_a[j], rwkv_r_k[j], rwkv_lnx_g[j], rwkv_lnx_b[j], lb_all[j], hgrn_norm_g[j],
                             ln_g[layer], ln_b[layer])
        else:
            lam_init = 0.8 - 0.6 * float(np.exp(-0.3 * layer))
            xs = _odd_layer(alpha, xs, scale, shift, gmod, odd_w_in[j], odd_w_out[j], diff_lambda[j],
                            diff_subln_g[j], lam_init, cos, sin, ln_g[layer], ln_b[layer])
    return xs[:, :t]
```

```python
import functools

import numpy as np
import jax
import jax.numpy as jnp
from jax import lax
from jax.experimental import pallas as pl
from jax.experimental.pallas import tpu as pltpu

F32 = jnp.float32
BF16 = jnp.bfloat16
HI = lax.Precision.HIGHEST

GRID_W = 64
RWKV_HEAD = 64
RWKV_LORA = 64
RWKV_GN_EPS = 64e-5
HGRN_HEAD = 128
DIFF_HEAD = 64
ROPE_BASE = 10000.0
LN_EPS = 1e-5

LANES = 128
MXU_DIM = 256
VMEM_LIMIT = 56 * 1024 * 1024

CHUNK = 64
SUB = 16
TOK_BLOCK = 256
GROUP_HEADS = MXU_DIM // RWKV_HEAD
ATTN_UNROLL = 8
ATTN_TILE = 512


def _cparams(*sem):
    return pltpu.CompilerParams(dimension_semantics=sem, vmem_limit_bytes=VMEM_LIMIT)


def _nt(a, b):
    return lax.dot_general(a, b, (((1,), (1,)), ((), ())), preferred_element_type=F32)


def _tn(a, b):
    return lax.dot_general(a, b, (((0,), (0,)), ((), ())), preferred_element_type=F32)


def _mm(a, b):
    return jnp.dot(a, b, preferred_element_type=F32)


def _mm_exact(a, b):
    return jnp.dot(a, b, preferred_element_type=F32, precision=HI)


def _sigmoid(x):
    return 1.0 / (1.0 + jnp.exp(-x))


def _silu(x):
    return x * _sigmoid(x)


def _lane_tiles(x):
    return [x[:, j * LANES:(j + 1) * LANES] for j in range(x.shape[-1] // LANES)]


def _group64_sum(x):
    lane = lax.broadcasted_iota(jnp.int32, (1, LANES), 1)
    low = lane < RWKV_HEAD
    out = []
    for xt in _lane_tiles(x):
        s_all = jnp.sum(xt, axis=-1, keepdims=True)
        s_lo = jnp.sum(jnp.where(low, xt, 0.0), axis=-1, keepdims=True)
        out.append(jnp.where(low, s_lo, s_all - s_lo))
    return jnp.concatenate(out, axis=-1)


def _group128_mean(x):
    out = []
    for xt in _lane_tiles(x):
        out.append(jnp.broadcast_to(jnp.mean(xt, axis=-1, keepdims=True), xt.shape))
    return jnp.concatenate(out, axis=-1)


def _ada_kernel(c_ref, w_ref, b_ref, o_ref):
    cond = _silu(c_ref[...])
    o_ref[0] = _mm_exact(cond, w_ref[0]) + b_ref[0]


def _ada_mods(cvec, ada_w, ada_b):
    depth, d, d3 = ada_w.shape
    tn = 1024
    return pl.pallas_call(
        _ada_kernel,
        grid=(depth, d3 // tn),
        in_specs=[
            pl.BlockSpec((8, d), lambda l, j: (0, 0)),
            pl.BlockSpec((1, d, tn), lambda l, j: (l, 0, j)),
            pl.BlockSpec((1, 1, tn), lambda l, j: (l, 0, j)),
        ],
        out_specs=pl.BlockSpec((1, 8, tn), lambda l, j: (l, 0, j)),
        out_shape=jax.ShapeDtypeStruct((depth, 8, d3), F32),
        compiler_params=_cparams("arbitrary", "arbitrary"),
        name="ada_mods",
    )(cvec, ada_w, ada_b.reshape(depth, 1, d3))


def _seg(i, nblk):
    return jnp.where(i == nblk - 1, 0, 1)


def _modulate(x_ref, sc_ref, sh_ref):
    return (x_ref[0] * (1.0 + sc_ref[0, 0]) + sh_ref[0, 0]).astype(BF16)


def _proj_kernel(x_ref, sc_ref, sh_ref, w_ref, o_ref):
    o_ref[0] = _mm(_modulate(x_ref, sc_ref, sh_ref), w_ref[...]).astype(o_ref.dtype)


def _proj_rope_kernel(x_ref, sc_ref, sh_ref, w_ref, cos_ref, sin_ref, o_ref):
    p = _mm(_modulate(x_ref, sc_ref, sh_ref), w_ref[...])
    reps = p.shape[-1] // LANES
    cos = jnp.concatenate([cos_ref[...]] * reps, axis=-1)
    sin = jnp.concatenate([sin_ref[...]] * reps, axis=-1)
    o_ref[0] = _rope(p, cos, sin).astype(o_ref.dtype)


def _proj_t_kernel(x_ref, sc_ref, sh_ref, wt_ref, o_ref):
    o_ref[0] = _nt(wt_ref[...], _modulate(x_ref, sc_ref, sh_ref)).astype(o_ref.dtype)


def _project(xs, scale, shift, w, tn, out_dtype, rope=None):
    b, lt, d = xs.shape
    n = w.shape[1]
    nblk = lt // TOK_BLOCK
    in_specs = [
        pl.BlockSpec((1, TOK_BLOCK, d), lambda j, bb, i: (bb, i, 0)),
        pl.BlockSpec((1, 1, 1, d), lambda j, bb, i: (bb, _seg(i, nblk), 0, 0)),
        pl.BlockSpec((1, 1, 1, d), lambda j, bb, i: (bb, _seg(i, nblk), 0, 0)),
        pl.BlockSpec((d, tn), lambda j, bb, i: (0, j)),
    ]
    args = (xs, scale, shift, w)
    if rope is not None:
        in_specs += [pl.BlockSpec((TOK_BLOCK, LANES), lambda j, bb, i: (i, 0))] * 2
        args += tuple(rope)
    return pl.pallas_call(
        _proj_kernel if rope is None else _proj_rope_kernel,
        grid=(n // tn, b, nblk),
        in_specs=in_specs,
        out_specs=pl.BlockSpec((1, TOK_BLOCK, tn), lambda j, bb, i: (bb, i, j)),
        out_shape=jax.ShapeDtypeStruct((b, lt, n), out_dtype),
        compiler_params=_cparams("arbitrary", "arbitrary", "arbitrary"),
        name="mod_proj" if rope is None else "mod_proj_rope",
    )(*args)


def _project_t(xs, scale, shift, wt, out_dtype):
    b, lt, d = xs.shape
    n = wt.shape[0]
    nblk = lt // TOK_BLOCK
    return pl.pallas_call(
        _proj_t_kernel,
        grid=(b, nblk),
        in_specs=[
            pl.BlockSpec((1, TOK_BLOCK, d), lambda bb, i: (bb, i, 0)),
            pl.BlockSpec((1, 1, 1, d), lambda bb, i: (bb, _seg(i, nblk), 0, 0)),
            pl.BlockSpec((1, 1, 1, d), lambda bb, i: (bb, _seg(i, nblk), 0, 0)),
            pl.BlockSpec((n, d), lambda bb, i: (0, 0)),
        ],
        out_specs=pl.BlockSpec((1, n, TOK_BLOCK), lambda bb, i: (bb, 0, i)),
        out_shape=jax.ShapeDtypeStruct((b, n, lt), out_dtype),
        compiler_params=_cparams("arbitrary", "arbitrary"),
        name="mod_proj_t",
    )(xs, scale, shift, wt)


def _rwkv_prep_kernel(nblk, p_ref, prev_ref, next_ref, mup_ref, mun_ref, w0_ref, w2_ref, a0_ref, a2_ref,
                      kk_ref, ka_ref, rk_ref, tri_ref, ones_ref,
                      v_out, bonus_out, *dir_outs):
    i = pl.program_id(1)
    w = kk_ref.shape[-1]
    p = p_ref[0]
    row = lax.broadcasted_iota(jnp.int32, (TOK_BLOCK, 1), 0)
    prev_row = jnp.where(jnp.logical_and(i >= 1, i <= nblk - 2), prev_ref[0][7:8, :], 0.0)
    next_row = jnp.where(i <= nblk - 3, next_ref[0][0:1, :], 0.0)
    prev = jnp.where(row == 0, prev_row, pltpu.roll(p, 1, 0))
    nxt = jnp.where(row == TOK_BLOCK - 1, next_row, pltpu.roll(p, TOK_BLOCK - 1, 0))
    ps = p + mup_ref[...] * (prev - p) + mun_ref[...] * (nxt - p)

    r = ps[:, 0:w]
    k = ps[:, w:2 * w]
    v = ps[:, 2 * w:3 * w]
    wlo = ps[:, 3 * w:3 * w + 2 * RWKV_LORA]
    alo = ps[:, 3 * w + 2 * RWKV_LORA:3 * w + 4 * RWKV_LORA]

    z = _mm_exact(jnp.tanh(wlo), w2_ref[...]) + w0_ref[...]
    w_log = -(jnp.maximum(-z, 0.0) + jnp.log(1.0 + jnp.exp(-jnp.abs(z)))) - 0.5
    lw = -jnp.exp(w_log)
    a = _sigmoid(_mm_exact(alo, a2_ref[...]) + a0_ref[...])

    kk = k * kk_ref[...]
    kk = kk * lax.rsqrt(_group64_sum(kk * kk) + 1e-12)
    bonus_out[0] = _group64_sum(r * k * rk_ref[...]) * v
    v_out[0] = v.astype(BF16)

    for d in range(2):
        rh_o, kh_o, bh_o, kap_o, kg_o, bg_o, gl_o = dir_outs[7 * d:7 * d + 7]
        lw_d = lw[:, d * w:(d + 1) * w]
        a_d = a[:, d * w:(d + 1) * w]
        kd = k * (1.0 + (a_d - 1.0) * ka_ref[...])
        b_d = kk * a_d
        g = _mm_exact(tri_ref[d], lw_d)
        g_all = _mm_exact(ones_ref[...], lw_d)
        e_neg = jnp.exp(-g)
        e_rest = jnp.exp(g_all - g)
        rh_o[0] = (r * jnp.exp(g)).astype(BF16)
        kh_o[0] = (kd * e_neg).astype(BF16)
        bh_o[0] = (b_d * e_neg).astype(BF16)
        kap_o[0] = (kk * jnp.exp(g - lw_d)).astype(BF16)
        kg_o[0] = (kd * e_rest).astype(BF16)
        bg_o[0] = (b_d * e_rest).astype(BF16)
        e_all = jnp.exp(g_all)
        for c in range(TOK_BLOCK // CHUNK):
            gl_o[0, c] = e_all[c * CHUNK:c * CHUNK + 1, :]


def _chunk_tri(reverse):
    t = np.arange(TOK_BLOCK)[:, None]
    s = np.arange(TOK_BLOCK)[None, :]
    same = (t // CHUNK) == (s // CHUNK)
    order = (s >= t) if reverse else (s <= t)
    return (same & order).astype(np.float32)


def _rwkv_prep(p_rwkv, mu_prev, mu_next, w0, w2, a0, a2, k_k, k_a, r_k):
    b, lt, sw = p_rwkv.shape
    w = k_k.shape[-1]
    nblk = lt // TOK_BLOCK
    nch = lt // CHUNK
    halo = TOK_BLOCK // 8
    zl = jnp.zeros((RWKV_LORA, w), F32)
    w2cat = jnp.concatenate([jnp.concatenate([w2[0], zl], 1), jnp.concatenate([zl, w2[1]], 1)], 0)
    a2cat = jnp.concatenate([jnp.concatenate([a2[0], zl], 1), jnp.concatenate([zl, a2[1]], 1)], 0)
    tri = jnp.asarray(np.stack([_chunk_tri(False), _chunk_tri(True)]))
    ones = jnp.asarray(np.maximum(_chunk_tri(False), _chunk_tri(True)))
    row = lambda u: u.reshape(1, -1)
    full = lambda shape: pl.BlockSpec(shape, lambda bb, i: (0,) * len(shape))
    tok = lambda n, dt: (pl.BlockSpec((1, TOK_BLOCK, n), lambda bb, i: (bb, i, 0)), jax.ShapeDtypeStruct((b, lt, n), dt))
    gl = (pl.BlockSpec((1, TOK_BLOCK // CHUNK, 1, w), lambda bb, i: (bb, i, 0, 0)),
          jax.ShapeDtypeStruct((b, nch, 1, w), F32))
    outs = [tok(w, BF16), tok(w, F32)] + 2 * ([tok(w, BF16)] * 6 + [gl])
    res = pl.pallas_call(
        functools.partial(_rwkv_prep_kernel, nblk),
        grid=(b, nblk),
        in_specs=[
            pl.BlockSpec((1, TOK_BLOCK, sw), lambda bb, i: (bb, i, 0)),
            pl.BlockSpec((1, 8, sw), lambda bb, i: (bb, jnp.maximum(i * halo - 1, 0), 0)),
            pl.BlockSpec((1, 8, sw), lambda bb, i: (bb, jnp.minimum((i + 1) * halo, lt // 8 - 1), 0)),
            full((1, sw)), full((1, sw)), full((1, 2 * w)), full((2 * RWKV_LORA, 2 * w)),
            full((1, 2 * w)), full((2 * RWKV_LORA, 2 * w)), full((1, w)), full((1, w)), full((1, w)),
            full((2, TOK_BLOCK, TOK_BLOCK)), full((TOK_BLOCK, TOK_BLOCK)),
        ],
        out_specs=[o[0] for o in outs],
        out_shape=[o[1] for o in outs],
        compiler_params=_cparams("arbitrary", "arbitrary"),
        name="rwkv_prep",
    )(p_rwkv, p_rwkv, p_rwkv, row(mu_prev), row(mu_next), row(w0), w2cat, row(a0), a2cat,
      row(k_k), row(k_a), row(r_k), tri, ones)
    v, bonus = res[0], res[1]
    return v, bonus, res[2:9], res[9:16]


def _rwkv_masks(reverse):
    n = GROUP_HEADS * CHUNK
    i = np.arange(n)[:, None]
    j = np.arange(n)[None, :]
    same_head = (i // CHUNK) == (j // CHUNK)
    before = (j > i) if reverse else (j < i)
    strict = same_head & before
    incl = same_head & (before | (i == j))
    levels = []
    size = 1
    while size < CHUNK:
        levels.append(same_head & ((i // (2 * size)) == (j // (2 * size))) & ((i // size) != (j // size)))
        size *= 2
    head_lane = (i // CHUNK) == (j // RWKV_HEAD)
    tri = np.stack([strict, incl]).astype(np.float32)
    return tri, np.stack(levels).astype(np.float32), head_lane.astype(np.float32)


def _rwkv_scan_kernel(rh_ref, kh_ref, bh_ref, kap_ref, kg_ref, bg_ref, v_ref, gl_ref,
                      tri_ref, lvl_ref, hm_ref, y_ref, s_ref):
    @pl.when(pl.program_id(1) == 0)
    def _():
        s_ref[...] = jnp.zeros_like(s_ref)

    n = GROUP_HEADS * CHUNK
    hm = hm_ref[...]
    strict = tri_ref[0]
    incl = tri_ref[1]
    eye = incl - strict
    n_levels = lvl_ref.shape[0]

    def stack(ref, g):
        x = ref[0, :, g * MXU_DIM:(g + 1) * MXU_DIM]
        return jnp.concatenate([x] * GROUP_HEADS, axis=0) * hm

    for g in range(y_ref.shape[-1] // MXU_DIM):
        kap = stack(kap_ref, g)
        bh = stack(bh_ref, g)
        kh = stack(kh_ref, g)
        rh = stack(rh_ref, g)
        vs = stack(v_ref, g)
        a_ab = _nt(kap, bh) * strict
        a_ak = _nt(kap, kh) * strict
        p_rb = _nt(rh, bh) * incl
        p_rk = _nt(rh, kh) * incl
        t = eye - a_ab * lvl_ref[0]
        for lv in range(1, n_levels):
            tb = t.astype(BF16)
            t = t - _mm(_mm(tb, (a_ab * lvl_ref[lv]).astype(BF16)).astype(BF16), tb)
        s0 = s_ref[g]
        s0b = s0.astype(BF16)
        wmat = _nt(kap, s0b) + _mm(a_ak.astype(BF16), vs)
        u = -_mm(t.astype(BF16), wmat.astype(BF16))
        ub = u.astype(BF16)
        ys = _nt(rh, s0b) + _mm(p_rb.astype(BF16), ub) + _mm(p_rk.astype(BF16), vs)
        y = ys[0:CHUNK]
        for h in range(1, GROUP_HEADS):
            y = y + ys[h * CHUNK:(h + 1) * CHUNK]
        y_ref[0, :, g * MXU_DIM:(g + 1) * MXU_DIM] = y
        gam = gl_ref[0, 0][:, g * MXU_DIM:(g + 1) * MXU_DIM]
        s_ref[g] = s0 * gam + _tn(ub, stack(bg_ref, g)) + _tn(vs, stack(kg_ref, g))


def _scan_chunk_index(c, nch, reverse):
    if reverse:
        return nch - 1 - c
    n_ctx = TOK_BLOCK // CHUNK
    return jnp.where(c < n_ctx, nch - n_ctx + c, c - n_ctx)


def _rwkv_scan(dir_ops, v, reverse):
    rh, kh, bh, kap, kg, bg, gl = dir_ops
    b, lt, w = v.shape
    nch = lt // CHUNK
    n = GROUP_HEADS * CHUNK
    tri, lvl, hm = _rwkv_masks(reverse)
    cidx = lambda c: _scan_chunk_index(c, nch, reverse)
    tok = pl.BlockSpec((1, CHUNK, w), lambda bb, c: (bb, cidx(c), 0))
    full = lambda shape: pl.BlockSpec(shape, lambda bb, c: (0,) * len(shape))
    return pl.pallas_call(
        _rwkv_scan_kernel,
        grid=(b, nch),
        in_specs=[tok] * 7 + [
            pl.BlockSpec((1, 1, 1, w), lambda bb, c: (bb, cidx(c), 0, 0)),
            full(tri.shape), full(lvl.shape), full(hm.shape),
        ],
        out_specs=tok,
        out_shape=jax.ShapeDtypeStruct((b, lt, w), F32),
        scratch_shapes=[pltpu.VMEM((w // MXU_DIM, n, n), F32)],
        compiler_params=_cparams("arbitrary", "arbitrary"),
        name="rwkv_scan_rev" if reverse else "rwkv_scan_fwd",
    )(rh, kh, bh, kap, kg, bg, v, gl, jnp.asarray(tri), jnp.asarray(lvl), jnp.asarray(hm, dtype=BF16))


def _hgrn_consts(reverse):
    t = np.arange(CHUNK)[:, None]
    s = np.arange(CHUNK)[None, :]
    order = (s >= t) if reverse else (s <= t)
    tri = order.astype(np.float32)
    earlier_sub = ((s // SUB) > (t // SUB)) if reverse else ((s // SUB) < (t // SUB))
    sub_before = earlier_sub.astype(np.float32)
    return np.stack([tri, sub_before])


def _hgrn_scan_kernel(reverse, q_ref, f_ref, i_ref, lb_ref, cm_ref, o_ref, s_ref, g_scr, gb_scr, q_scr, k_scr):
    @pl.when(pl.program_id(1) == 0)
    def _():
        s_ref[...] = jnp.zeros_like(s_ref)

    n_sub = CHUNK // SUB
    n_heads = q_ref.shape[-1] // HGRN_HEAD
    n_chunks = TOK_BLOCK // CHUNK
    sub_order = list(range(n_sub))[::-1] if reverse else list(range(n_sub))
    row = lax.broadcasted_iota(jnp.int32, (CHUNK, 1), 0)
    row_sub = row // SUB
    srow = lax.broadcasted_iota(jnp.int32, (SUB, 1), 0)
    lane = lax.broadcasted_iota(jnp.int32, (1, LANES), 1)
    last_row = 0 if reverse else CHUNK - 1

    for ci in range(n_chunks):
        c = n_chunks - 1 - ci if reverse else ci
        rows = pl.ds(c * CHUNK, CHUNK)
        kgate = (1.0 - lb_ref[...]) * _sigmoid(-f_ref[0, rows, :])
        lf = jnp.log1p(-kgate)
        g_scr[...] = _mm_exact(cm_ref[0], lf)
        gb_scr[...] = _mm_exact(cm_ref[1], lf)
        q_scr[...] = _silu(q_ref[0, rows, :])
        k_scr[...] = kgate

        def head(h, carry):
            cols = pl.ds(pl.multiple_of(h * HGRN_HEAD, HGRN_HEAD), HGRN_HEAD)
            g = g_scr[:, cols]
            gb = gb_scr[:, cols]
            q = q_scr[:, cols]
            k = k_scr[:, cols]
            v = i_ref[0, rows, cols]
            vb = v.astype(BF16)
            s0 = s_ref[h]
            q_in = q * jnp.exp(g - gb)
            kts, qts = [], []
            for si in sub_order[1:]:
                first = si * SUB + (SUB - 1 if reverse else 0)
                gb_i = gb[first:first + 1, :]
                earlier = (row_sub > si) if reverse else (row_sub < si)
                kts.append(jnp.where(earlier, k * jnp.exp(jnp.minimum(gb_i - g, 0.0)), 0.0).astype(BF16))
                qts.append(jnp.where(row_sub == si, q_in, 0.0).astype(BF16))
            at = _nt(jnp.concatenate(kts, axis=-1), jnp.concatenate(qts, axis=-1))
            diag = []
            for si in range(n_sub):
                blk = slice(si * SUB, (si + 1) * SUB)
                g_b, k_b = g[blk], k[blk]
                acc = jnp.zeros((SUB, LANES), F32)
                for tt in range(SUB):
                    t = si * SUB + tt
                    e = jnp.exp(jnp.minimum(g[t:t + 1, :] - g_b, 0.0))
                    col = jnp.sum(k_b * e * q[t:t + 1, :], axis=-1, keepdims=True)
                    valid = (srow >= tt) if reverse else (srow <= tt)
                    acc = jnp.where(jnp.logical_and(lane == t, valid), col, acc)
                diag.append(acc)
            at = at + jnp.concatenate(diag, axis=0)[:, :CHUNK]
            o = _tn(at.astype(BF16), vb) + _nt((q * jnp.exp(g)).astype(BF16), s0.astype(BF16))
            o_ref[0, rows, cols] = o
            g_last = g[last_row:last_row + 1, :]
            k_out = (k * jnp.exp(g_last - g)).astype(BF16)
            s_ref[h] = s0 * jnp.exp(g_last) + _tn(vb, k_out)
            return carry

        lax.fori_loop(0, n_heads, head, 0)


def _hgrn_scan(p_hgrn, lb_d, d, reverse):
    b, lt, w4 = p_hgrn.shape
    w = w4 // 4
    nblk = lt // TOK_BLOCK
    bidx = (lambda i: nblk - 1 - i) if reverse else (lambda i: jnp.where(i == 0, nblk - 1, i - 1))
    col = lambda j: pl.BlockSpec((1, TOK_BLOCK, w), lambda bb, i: (bb, bidx(i), j))
    cm = jnp.asarray(_hgrn_consts(reverse))
    return pl.pallas_call(
        functools.partial(_hgrn_scan_kernel, reverse),
        grid=(b, nblk),
        in_specs=[col(0), col(1 + d), col(3),
                  pl.BlockSpec((1, w), lambda bb, i: (0, 0)),
                  pl.BlockSpec(cm.shape, lambda bb, i: (0, 0, 0))],
        out_specs=col(0),
        out_shape=jax.ShapeDtypeStruct((b, lt, w), F32),
        scratch_shapes=[pltpu.VMEM((w // HGRN_HEAD, HGRN_HEAD, HGRN_HEAD), F32)]
        + [pltpu.VMEM((CHUNK, w), F32)] * 4,
        compiler_params=_cparams("arbitrary", "arbitrary"),
        name="hgrn_scan_rev" if reverse else "hgrn_scan_fwd",
    )(p_hgrn, p_hgrn, p_hgrn, lb_d.reshape(1, w), cm)


def _residual_ln(alpha, x, gm, proj, lng, lnb):
    z = alpha * x + gm * proj
    mu = jnp.mean(z, axis=-1, keepdims=True)
    zc = z - mu
    var = jnp.mean(zc * zc, axis=-1, keepdims=True)
    return zc * lax.rsqrt(var + LN_EPS) * lng + lnb


def _even_out_kernel(alpha, ya0_ref, ya1_ref, bonus_ref, yb0_ref, yb1_ref, gate_ref, x_ref, gm_ref, w_ref,
                     lnxg_ref, lnxb_ref, ng_ref, lng_ref, lnb_ref, o_ref):
    y = ya0_ref[0] + ya1_ref[0]
    mu = _group64_sum(y) * (1.0 / RWKV_HEAD)
    yc = y - mu
    var = _group64_sum(yc * yc) * (1.0 / RWKV_HEAD)
    ya = yc * lax.rsqrt(var + RWKV_GN_EPS) * lnxg_ref[...] + lnxb_ref[...] + bonus_ref[0]
    o = yb0_ref[0] + yb1_ref[0]
    yb = o * lax.rsqrt(_group128_mean(o * o) + LN_EPS) * ng_ref[...]
    ycat = jnp.concatenate([ya, yb], axis=-1) * _silu(gate_ref[0])
    proj = _mm(ycat.astype(BF16), w_ref[...])
    o_ref[0] = _residual_ln(alpha, x_ref[0], gm_ref[0, 0], proj, lng_ref[...], lnb_ref[...])


def _odd_out_kernel(alpha, y_ref, x_ref, gm_ref, w_ref, lng_ref, lnb_ref, o_ref):
    proj = _mm(y_ref[0].astype(BF16), w_ref[...])
    o_ref[0] = _residual_ln(alpha, x_ref[0], gm_ref[0, 0], proj, lng_ref[...], lnb_ref[...])


def _tok_spec(n):
    return pl.BlockSpec((1, TOK_BLOCK, n), lambda bb, i: (bb, i, 0))


def _full2(shape):
    return pl.BlockSpec(shape, lambda bb, i: (0,) * len(shape))


def _seg_spec(d, nblk):
    return pl.BlockSpec((1, 1, 1, d), lambda bb, i: (bb, _seg(i, nblk), 0, 0))


def _even_out(alpha, ya0, ya1, bonus, yb0, yb1, gate, xs, gmod, w_out, lnx_g, lnx_b, norm_g, ln_g, ln_b):
    b, lt, d = xs.shape
    w = ya0.shape[-1]
    di = gate.shape[-1]
    row = lambda u: u.reshape(1, -1)
    return pl.pallas_call(
        functools.partial(_even_out_kernel, alpha),
        grid=(b, lt // TOK_BLOCK),
        in_specs=[_tok_spec(w)] * 5 + [_tok_spec(di), _tok_spec(d), _seg_spec(d, lt // TOK_BLOCK), _full2((di, d)),
                                       _full2((1, w)), _full2((1, w)), _full2((1, w)), _full2((1, d)), _full2((1, d))],
        out_specs=_tok_spec(d),
        out_shape=jax.ShapeDtypeStruct((b, lt, d), F32),
        compiler_params=_cparams("arbitrary", "arbitrary"),
        name="even_out",
    )(ya0, ya1, bonus, yb0, yb1, gate, xs, gmod, w_out, row(lnx_g), row(lnx_b),
      row(jnp.tile(norm_g, w // HGRN_HEAD)), row(ln_g), row(ln_b))


def _odd_out(alpha, y, xs, gmod, w_out, ln_g, ln_b):
    b, lt, d = xs.shape
    di = y.shape[-1]
    row = lambda u: u.reshape(1, -1)
    return pl.pallas_call(
        functools.partial(_odd_out_kernel, alpha),
        grid=(b, lt // TOK_BLOCK),
        in_specs=[_tok_spec(di), _tok_spec(d), _seg_spec(d, lt // TOK_BLOCK), _full2((di, d)), _full2((1, d)),
                  _full2((1, d))],
        out_specs=_tok_spec(d),
        out_shape=jax.ShapeDtypeStruct((b, lt, d), F32),
        compiler_params=_cparams("arbitrary", "arbitrary"),
        name="odd_out",
    )(y, xs, gmod, w_out, row(ln_g), row(ln_b))


def _rope(x, cos, sin):
    width = x.shape[-1]
    lane = lax.broadcasted_iota(jnp.int32, (1, width), 1)
    first = (lane % 32) < 16
    partner = jnp.where(first, pltpu.roll(x, width - 16, 1), pltpu.roll(x, 16, 1))
    return x * cos + partner * sin


def _attn_kernel(lam_init, tk, n_full, tail, lam_ref, q_ref, k_ref, vt_ref, g_ref, sg_ref, *rest):
    y_ref, m_scr, acc_scr = rest[-3:]
    tq = q_ref.shape[1]
    ones_rows = 16
    tile = min(ATTN_TILE, 2 * tq)
    n_tiles = 2 * tq // tile
    q = q_ref[0].astype(F32) * (DIFF_HEAD ** -0.5 * float(np.log2(np.e)))
    lane = lax.broadcasted_iota(jnp.int32, (1, LANES), 1)
    q_cat = jnp.concatenate([jnp.where(lane < DIFF_HEAD, q, 0.0), jnp.where(lane >= DIFF_HEAD, q, 0.0)],
                            axis=0).astype(BF16)
    m_scr[...] = jnp.full(m_scr.shape, -jnp.inf, F32)
    acc_scr[...] = jnp.zeros(acc_scr.shape, F32)

    def scores(item, size):
        start, c = item
        return _nt(k_ref[0, pl.ds(start, size), :], q_cat[c * tile:(c + 1) * tile])

    def run(starts, size):
        cols = [slice(c * tile, (c + 1) * tile) for c in range(n_tiles)]
        m_run = [m_scr[:, cs] for cs in cols]
        acc = [acc_scr[:, cs] for cs in cols]
        vta = None
        items =[(start, c) for start in starts for c in range(n_tiles)]
        st_next = scores(items[0], size)
        for i, (start, c) in enumerate(items):
            st = st_next
            if i + 1 < len(items):
                st_next = scores(items[i + 1], size)
            if c == 0:
                vta = jnp.concatenate([vt_ref[0, :, pl.ds(start, size)], jnp.ones((ones_rows, size), BF16)], axis=0)
            m_new = jnp.maximum(m_run[c], jnp.max(st, axis=0, keepdims=True))
            corr = jnp.exp2(m_run[c] - m_new)
            pt = jnp.exp2(st - m_new).astype(BF16)
            acc[c] = acc[c] * corr + _mm(vta, pt)
            m_run[c] = m_new
        for c, cs in enumerate(cols):
            m_scr[:, cs] = m_run[c]
            acc_scr[:, cs] = acc[c]

    if n_full:
        unroll = ATTN_UNROLL if n_full % ATTN_UNROLL == 0 else 1
        def body(j, carry):
            run([pl.multiple_of((j * unroll + u) * tk, tk) for u in range(unroll)], tk)
            return carry
        lax.fori_loop(0, n_full // unroll, body, 0)
    run([n_full * tk], tail)

    a0 = acc_scr[:, :tq]
    a1 = acc_scr[:, tq:]
    o_t = a0[:LANES] / a0[LANES:LANES + 1] - lam_ref[0] * (a1[:LANES] / a1[LANES:LANES + 1])
    o = jnp.transpose(o_t)
    y = o * lax.rsqrt(jnp.mean(o * o, axis=-1, keepdims=True) + LN_EPS) * sg_ref[...] * (1.0 - lam_init)
    y_ref[0] = y * _silu(g_ref[0].astype(F32))


def _diff_attention(p_qk, p_vt, p_g, lam, subln_g, lam_init):
    b, lt, di = p_g.shape
    nh = di // LANES
    t = lt - TOK_BLOCK
    tq = tk = 512
    assert t % tq == 0
    ctx_blk = t // TOK_BLOCK
    sm = pl.BlockSpec(memory_space=pltpu.SMEM)
    sg_spec = pl.BlockSpec((1, LANES), lambda bb, h, i: (0, 0))
    scratch = lambda n: [pltpu.VMEM((1, 2 * n), F32), pltpu.VMEM((LANES + 16, 2 * n), F32)]
    args = (lam.reshape(1), p_qk, p_qk, p_vt, p_g, subln_g.reshape(1, LANES))
    y = pl.pallas_call(
        functools.partial(_attn_kernel, lam_init, tk, t // tk, TOK_BLOCK),
        grid=(b, nh, t // tq),
        in_specs=[sm,
                  pl.BlockSpec((1, tq, LANES), lambda bb, h, i: (bb, i, h)),
                  pl.BlockSpec((1, lt, LANES), lambda bb, h, i: (bb, 0, nh + h)),
                  pl.BlockSpec((1, LANES, lt), lambda bb, h, i: (bb, h, 0)),
                  pl.BlockSpec((1, tq, LANES), lambda bb, h, i: (bb, i, h)),
                  sg_spec],
        out_specs=pl.BlockSpec((1, tq, LANES), lambda bb, h, i: (bb, i, h)),
        out_shape=jax.ShapeDtypeStruct((b, lt, di), F32),
        scratch_shapes=scratch(tq),
        compiler_params=_cparams("arbitrary", "arbitrary", "arbitrary"),
        name="diff_attn",
    )(*args)
    blk = lambda bb, h, i: (bb, ctx_blk, h)
    return pl.pallas_call(
        functools.partial(_attn_kernel, lam_init, tk, 0, TOK_BLOCK),
        grid=(b, nh, 1),
        in_specs=[sm,
                  pl.BlockSpec((1, TOK_BLOCK, LANES), blk),
                  pl.BlockSpec((1, TOK_BLOCK, LANES), lambda bb, h, i: (bb, ctx_blk, nh + h)),
                  pl.BlockSpec((1, LANES, TOK_BLOCK), lambda bb, h, i: (bb, h, ctx_blk)),
                  pl.BlockSpec((1, TOK_BLOCK, LANES), blk),
                  sg_spec,
                  pl.BlockSpec(memory_space=pl.ANY)],
        out_specs=pl.BlockSpec((1, TOK_BLOCK, LANES), blk),
        out_shape=jax.ShapeDtypeStruct((b, lt, di), F32),
        scratch_shapes=scratch(TOK_BLOCK),
        input_output_aliases={6: 0},
        compiler_params=_cparams("arbitrary", "arbitrary", "arbitrary"),
        name="diff_attn_ctx",
    )(*args, y)


def _rope_tables(n_ctx, t):
    quarter = DIFF_HEAD // 4
    inv = ROPE_BASE ** (-jnp.arange(quarter, dtype=F32) / quarter)
    pos = jnp.arange(t)
    rows = (pos // GRID_W).astype(F32)[:, None] * inv
    cols = (pos % GRID_W).astype(F32)[:, None] * inv
    cos64 = jnp.concatenate([jnp.cos(rows), jnp.cos(rows), jnp.cos(cols), jnp.cos(cols)], -1)
    sin64 = jnp.concatenate([-jnp.sin(rows), jnp.sin(rows), -jnp.sin(cols), jnp.sin(cols)], -1)
    cos = jnp.concatenate([jnp.tile(cos64, (1, 2)), jnp.ones((n_ctx, LANES), F32)], 0)
    sin = jnp.concatenate([jnp.tile(sin64, (1, 2)), jnp.zeros((n_ctx, LANES), F32)], 0)
    return cos, sin


def _even_layer(alpha, xs, scale, shift, gmod, w_in, w_out, mu_prev, mu_next, w0, w2, a0, a2, k_k, k_a, r_k,
                lnx_g, lnx_b, lb, norm_g, ln_g, ln_b):
    w = k_k.shape[-1]
    sw = mu_prev.shape[-1]
    hw = 4 * lb.shape[-1]
    w_in = w_in.astype(BF16)
    p_rwkv = _project(xs, scale, shift, w_in[:, :sw], sw, F32)
    p_hgrn = _project(xs, scale, shift, w_in[:, sw:sw + hw], hw // 2, F32)
    gate = _project(xs, scale, shift, w_in[:, sw + hw:], w_in.shape[1] - sw - hw, F32)
    v, bonus, ops_f, ops_r = _rwkv_prep(p_rwkv, mu_prev, mu_next, w0, w2, a0, a2, k_k, k_a, r_k)
    ya0 = _rwkv_scan(ops_f, v, False)
    ya1 = _rwkv_scan(ops_r, v, True)
    yb0 = _hgrn_scan(p_hgrn, lb[0], 0, False)
    yb1 = _hgrn_scan(p_hgrn, lb[1], 1, True)
    return _even_out(alpha, ya0, ya1, bonus, yb0, yb1, gate, xs, gmod, w_out.astype(BF16),
                     lnx_g, lnx_b, norm_g, ln_g, ln_b)


def _odd_layer(alpha, xs, scale, shift, gmod, w_in, w_out, lam_p, subln_g, lam_init, cos, sin, ln_g, ln_b):
    di = w_out.shape[0]
    w_in = w_in.astype(BF16)
    p_qk = _project(xs, scale, shift, w_in[:, :2 * di], di, BF16, rope=(cos, sin))
    p_vt = _project_t(xs, scale, shift, jnp.transpose(w_in[:, 2 * di:3 * di]), BF16)
    p_g = _project(xs, scale, shift, w_in[:, 3 * di:], di, BF16)
    lam = jnp.exp(jnp.sum(lam_p[0] * lam_p[1])) - jnp.exp(jnp.sum(lam_p[2] * lam_p[3])) + lam_init
    y = _diff_attention(p_qk, p_vt, p_g, lam, subln_g, lam_init)
    return _odd_out(alpha, y, xs, gmod, w_out.astype(BF16), ln_g, ln_b)


def kernel(x, c, ctx, c_ctx, ada_w, ada_b, ln_g, ln_b, even_w_in, even_w_out, rwkv_mu_prev, rwkv_mu_next, rwkv_w0, rwkv_w2, rwkv_a0, rwkv_a2, rwkv_k_k, rwkv_k_a, rwkv_r_k, rwkv_lnx_g, rwkv_lnx_b, hgrn_lb_logits, hgrn_norm_g, odd_w_in, odd_w_out, diff_lambda, diff_subln_g):
    b, t, d = x.shape
    n_ctx = ctx.shape[1]
    depth = ada_w.shape[0]
    assert n_ctx == TOK_BLOCK and t % TOK_BLOCK == 0 and b + 1 <= 8
    alpha = (2.0 * depth) ** 0.25

    xs = jnp.concatenate([x, ctx], axis=1)
    cvec = jnp.concatenate([c, c_ctx[None], jnp.zeros((8 - b - 1, d), F32)], axis=0)
    mods = _ada_mods(cvec, ada_w, ada_b)
    lb_all = jax.nn.softmax(hgrn_lb_logits.astype(F32), axis=0)
    lb_all = jnp.cumsum(lb_all, axis=0) - lb_all[0]
    cos, sin = _rope_tables(n_ctx, t)

    for layer in range(depth):
        m = mods[layer]
        per_seg = lambda u: jnp.stack([jnp.broadcast_to(u[b], (b, d)), u[:b]], axis=1)[:, :, None, :]
        shift, scale, gmod = (per_seg(m[:, j * d:(j + 1) * d]) for j in range(3))
        j = layer // 2
        if layer % 2 == 0:
            xs = _even_layer(alpha, xs, scale, shift, gmod, even_w_in[j], even_w_out[j], rwkv_mu_prev[j],
                             rwkv_mu_next[j], rwkv_w0[j], rwkv_w2[j], rwkv_a0[j], rwkv_a2[j], rwkv_k_k[j],
                             rwkv_k_a[j], rwkv_r_k[j], rwkv_lnx_g[j], rwkv_lnx_b[j], lb_all[j], hgrn_norm_g[j],
                             ln_g[layer], ln_b[layer])
        else:
            lam_init = 0.8 - 0.6 * float(np.exp(-0.3 * layer))
            xs = _odd_layer(alpha, xs, scale, shift, gmod, odd_w_in[j], odd_w_out[j], diff_lambda[j],
                            diff_subln_g[j], lam_init, cos, sin, ln_g[layer], ln_b[layer])
    return xs[:, :t]
```

```python
import functools

import numpy as np
import jax
import jax.numpy as jnp
from jax import lax
from jax.experimental import pallas as pl
from jax.experimental.pallas import tpu as pltpu

F32 = jnp.float32
BF16 = jnp.bfloat16
HI = lax.Precision.HIGHEST

GRID_W = 64
RWKV_HEAD = 64
RWKV_LORA = 64
RWKV_GN_EPS = 64e-5
HGRN_HEAD = 128
DIFF_HEAD = 64
ROPE_BASE = 10000.0
LN_EPS = 1e-5

LANES = 128
MXU_DIM = 256
VMEM_LIMIT = 56 * 1024 * 1024

CHUNK = 64
SUB = 16
TOK_BLOCK = 256
GROUP_HEADS = MXU_DIM // RWKV_HEAD
HGRN_PAR = 4
ATTN_UNROLL = 8
ATTN_TILE = 512


def _cparams(*sem):
    return pltpu.CompilerParams(dimension_semantics=sem, vmem_limit_bytes=VMEM_LIMIT)


def _nt(a, b):
    return lax.dot_general(a, b, (((1,), (1,)), ((), ())), preferred_element_type=F32)


def _tn(a, b):
    return lax.dot_general(a, b, (((0,), (0,)), ((), ())), preferred_element_type=F32)


def _mm(a, b):
    return jnp.dot(a, b, preferred_element_type=F32)


def _mm_exact(a, b):
    return jnp.dot(a, b, preferred_element_type=F32, precision=HI)


def _sigmoid(x):
    return 1.0 / (1.0 + jnp.exp(-x))


def _silu(x):
    return x * _sigmoid(x)


def _lane_tiles(x):
    return [x[:, j * LANES:(j + 1) * LANES] for j in range(x.shape[-1] // LANES)]


def _group64_sum(x):
    lane = lax.broadcasted_iota(jnp.int32, (1, LANES), 1)
    low = lane < RWKV_HEAD
    out = []
    for xt in _lane_tiles(x):
        s_all = jnp.sum(xt, axis=-1, keepdims=True)
        s_lo = jnp.sum(jnp.where(low, xt, 0.0), axis=-1, keepdims=True)
        out.append(jnp.where(low, s_lo, s_all - s_lo))
    return jnp.concatenate(out, axis=-1)


def _group128_mean(x):
    out = []
    for xt in _lane_tiles(x):
        out.append(jnp.broadcast_to(jnp.mean(xt, axis=-1, keepdims=True), xt.shape))
    return jnp.concatenate(out, axis=-1)


def _ada_kernel(c_ref, w_ref, b_ref, o_ref):
    cond = _silu(c_ref[...])
    o_ref[0] = _mm_exact(cond, w_ref[0]) + b_ref[0]


def _ada_mods(cvec, ada_w, ada_b):
    depth, d, d3 = ada_w.shape
    tn = 1024
    return pl.pallas_call(
        _ada_kernel,
        grid=(depth, d3 // tn),
        in_specs=[
            pl.BlockSpec((8, d), lambda l, j: (0, 0)),
            pl.BlockSpec((1, d, tn), lambda l, j: (l, 0, j)),
            pl.BlockSpec((1, 1, tn), lambda l, j: (l, 0, j)),
        ],
        out_specs=pl.BlockSpec((1, 8, tn), lambda l, j: (l, 0, j)),
        out_shape=jax.ShapeDtypeStruct((depth, 8, d3), F32),
        compiler_params=_cparams("arbitrary", "arbitrary"),
        name="ada_mods",
    )(cvec, ada_w, ada_b.reshape(depth, 1, d3))


def _seg(i, nblk):
    return jnp.where(i == nblk - 1, 0, 1)


def _modulate(x_ref, sc_ref, sh_ref):
    return (x_ref[0] * (1.0 + sc_ref[0, 0]) + sh_ref[0, 0]).astype(BF16)


def _proj_kernel(x_ref, sc_ref, sh_ref, w_ref, o_ref):
    o_ref[0] = _mm(_modulate(x_ref, sc_ref, sh_ref), w_ref[...]).astype(o_ref.dtype)


def _proj_rope_kernel(x_ref, sc_ref, sh_ref, w_ref, cos_ref, sin_ref, o_ref):
    p = _mm(_modulate(x_ref, sc_ref, sh_ref), w_ref[...])
    reps = p.shape[-1] // LANES
    cos = jnp.concatenate([cos_ref[...]] * reps, axis=-1)
    sin = jnp.concatenate([sin_ref[...]] * reps, axis=-1)
    o_ref[0] = _rope(p, cos, sin).astype(o_ref.dtype)


def _proj_t_kernel(x_ref, sc_ref, sh_ref, wt_ref, o_ref):
    o_ref[0] = _nt(wt_ref[...], _modulate(x_ref, sc_ref, sh_ref)).astype(o_ref.dtype)


def _project(xs, scale, shift, w, tn, out_dtype, rope=None):
    b, lt, d = xs.shape
    n = w.shape[1]
    nblk = lt // TOK_BLOCK
    in_specs = [
        pl.BlockSpec((1, TOK_BLOCK, d), lambda j, bb, i: (bb, i, 0)),
        pl.BlockSpec((1, 1, 1, d), lambda j, bb, i: (bb, _seg(i, nblk), 0, 0)),
        pl.BlockSpec((1, 1, 1, d), lambda j, bb, i: (bb, _seg(i, nblk), 0, 0)),
        pl.BlockSpec((d, tn), lambda j, bb, i: (0, j)),
    ]
    args = (xs, scale, shift, w)
    if rope is not None:
        in_specs += [pl.BlockSpec((TOK_BLOCK, LANES), lambda j, bb, i: (i, 0))] * 2
        args += tuple(rope)
    return pl.pallas_call(
        _proj_kernel if rope is None else _proj_rope_kernel,
        grid=(n // tn, b, nblk),
        in_specs=in_specs,
        out_specs=pl.BlockSpec((1, TOK_BLOCK, tn), lambda j, bb, i: (bb, i, j)),
        out_shape=jax.ShapeDtypeStruct((b, lt, n), out_dtype),
        compiler_params=_cparams("arbitrary", "arbitrary", "arbitrary"),
        name="mod_proj" if rope is None else "mod_proj_rope",
    )(*args)


def _project_t(xs, scale, shift, wt, out_dtype):
    b, lt, d = xs.shape
    n = wt.shape[0]
    nblk = lt // TOK_BLOCK
    return pl.pallas_call(
        _proj_t_kernel,
        grid=(b, nblk),
        in_specs=[
            pl.BlockSpec((1, TOK_BLOCK, d), lambda bb, i: (bb, i, 0)),
            pl.BlockSpec((1, 1, 1, d), lambda bb, i: (bb, _seg(i, nblk), 0, 0)),
            pl.BlockSpec((1, 1, 1, d), lambda bb, i: (bb, _seg(i, nblk), 0, 0)),
            pl.BlockSpec((n, d), lambda bb, i: (0, 0)),
        ],
        out_specs=pl.BlockSpec((1, n, TOK_BLOCK), lambda bb, i: (bb, 0, i)),
        out_shape=jax.ShapeDtypeStruct((b, n, lt), out_dtype),
        compiler_params=_cparams("arbitrary", "arbitrary"),
        name="mod_proj_t",
    )(xs, scale, shift, wt)


def _lora(x, w_ref):
    x1 = x.astype(BF16)
    x2 = (x - x1.astype(F32)).astype(BF16)
    return _mm(jnp.concatenate([x1, x1, x2], axis=-1), w_ref[...])


def _rwkv_prep_kernel(nblk, p_ref, prev_ref, next_ref, mup_ref, mun_ref, w0_ref, w2_ref, a0_ref, a2_ref,
                      kk_ref, ka_ref, rk_ref, cm_ref,
                      v_out, bonus_out, *dir_outs):
    i = pl.program_id(1)
    w = kk_ref.shape[-1]
    p = p_ref[0]
    row = lax.broadcasted_iota(jnp.int32, (TOK_BLOCK, 1), 0)
    prev_row = jnp.where(jnp.logical_and(i >= 1, i <= nblk - 2), prev_ref[0][7:8, :], 0.0)
    next_row = jnp.where(i <= nblk - 3, next_ref[0][0:1, :], 0.0)
    prev = jnp.where(row == 0, prev_row, pltpu.roll(p, 1, 0))
    nxt = jnp.where(row == TOK_BLOCK - 1, next_row, pltpu.roll(p, TOK_BLOCK - 1, 0))
    ps = p + mup_ref[...] * (prev - p) + mun_ref[...] * (nxt - p)

    r = ps[:, 0:w]
    k = ps[:, w:2 * w]
    v = ps[:, 2 * w:3 * w]
    wlo = ps[:, 3 * w:3 * w + 2 * RWKV_LORA]
    alo = ps[:, 3 * w + 2 * RWKV_LORA:3 * w + 4 * RWKV_LORA]

    z = _lora(jnp.tanh(wlo), w2_ref) + w0_ref[...]
    w_log = -(jnp.maximum(-z, 0.0) + jnp.log(1.0 + jnp.exp(-jnp.abs(z)))) - 0.5
    lw = -jnp.exp(w_log)
    a = _sigmoid(_lora(alo, a2_ref) + a0_ref[...])

    kk = k * kk_ref[...]
    kk = kk * lax.rsqrt(_group64_sum(kk * kk) + 1e-12)
    bonus_out[0] = _group64_sum(r * k * rk_ref[...]) * v
    v_out[0] = v.astype(BF16)

    for d in range(2):
        rh_o, kh_o, bh_o, kap_o, kg_o, bg_o, gl_o = dir_outs[7 * d:7 * d + 7]
        lw_d = lw[:, d * w:(d + 1) * w]
        a_d = a[:, d * w:(d + 1) * w]
        kd = k * (1.0 + (a_d - 1.0) * ka_ref[...])
        b_d = kk * a_d
        sums = _mm(cm_ref[d], _split3(lw_d))
        g = sums[:TOK_BLOCK]
        g_all = sums[TOK_BLOCK:]
        e_neg = jnp.exp(-g)
        e_rest = jnp.exp(g_all - g)
        rh_o[0] = (r * jnp.exp(g)).astype(BF16)
        kh_o[0] = (kd * e_neg).astype(BF16)
        bh_o[0] = (b_d * e_neg).astype(BF16)
        kap_o[0] = (kk * jnp.exp(g - lw_d)).astype(BF16)
        kg_o[0] = (kd * e_rest).astype(BF16)
        bg_o[0] = (b_d * e_rest).astype(BF16)
        e_all = jnp.exp(g_all)
        for c in range(TOK_BLOCK // CHUNK):
            gl_o[0, c] = e_all[c * CHUNK:c * CHUNK + 1, :]


def _rwkv_sum_consts(reverse):
    t = np.arange(TOK_BLOCK)[:, None]
    s = np.arange(TOK_BLOCK)[None, :]
    same = (t // CHUNK) == (s // CHUNK)
    order = (s >= t) if reverse else (s <= t)
    both = np.concatenate([same & order, same], axis=0).astype(np.float32)
    return np.concatenate([both] * 3, axis=1)


def _lora_pieces(m):
    hi = m.astype(BF16)
    lo = (m - hi.astype(F32)).astype(BF16)
    return jnp.concatenate([hi, lo, hi], axis=0)


def _rwkv_prep(p_rwkv, mu_prev, mu_next, w0, w2, a0, a2, k_k, k_a, r_k):
    b, lt, sw = p_rwkv.shape
    w = k_k.shape[-1]
    nblk = lt // TOK_BLOCK
    nch = lt // CHUNK
    halo = TOK_BLOCK // 8
    zl = jnp.zeros((RWKV_LORA, w), F32)
    w2cat = _lora_pieces(jnp.concatenate([jnp.concatenate([w2[0], zl], 1), jnp.concatenate([zl, w2[1]], 1)], 0))
    a2cat = _lora_pieces(jnp.concatenate([jnp.concatenate([a2[0], zl], 1), jnp.concatenate([zl, a2[1]], 1)], 0))
    cm = jnp.asarray(np.stack([_rwkv_sum_consts(False), _rwkv_sum_consts(True)]), dtype=BF16)
    row = lambda u: u.reshape(1, -1)
    full = lambda shape: pl.BlockSpec(shape, lambda bb, i: (0,) * len(shape))
    tok = lambda n, dt: (pl.BlockSpec((1, TOK_BLOCK, n), lambda bb, i: (bb, i, 0)), jax.ShapeDtypeStruct((b, lt, n), dt))
    gl = (pl.BlockSpec((1, TOK_BLOCK // CHUNK, 1, w), lambda bb, i: (bb, i, 0, 0)),
          jax.ShapeDtypeStruct((b, nch, 1, w), F32))
    outs = [tok(w, BF16), tok(w, F32)] + 2 * ([tok(w, BF16)] * 6 + [gl])
    res = pl.pallas_call(
        functools.partial(_rwkv_prep_kernel, nblk),
        grid=(b, nblk),
        in_specs=[
            pl.BlockSpec((1, TOK_BLOCK, sw), lambda bb, i: (bb, i, 0)),
            pl.BlockSpec((1, 8, sw), lambda bb, i: (bb, jnp.maximum(i * halo - 1, 0), 0)),
            pl.BlockSpec((1, 8, sw), lambda bb, i: (bb, jnp.minimum((i + 1) * halo, lt // 8 - 1), 0)),
            full((1, sw)), full((1, sw)), full((1, 2 * w)), full(w2cat.shape),
            full((1, 2 * w)), full(a2cat.shape), full((1, w)), full((1, w)), full((1, w)),
            full(cm.shape),
        ],
        out_specs=[o[0] for o in outs],
        out_shape=[o[1] for o in outs],
        compiler_params=_cparams("arbitrary", "arbitrary"),
        name="rwkv_prep",
    )(p_rwkv, p_rwkv, p_rwkv, row(mu_prev), row(mu_next), row(w0), w2cat, row(a0), a2cat,
      row(k_k), row(k_a), row(r_k), cm)
    v, bonus = res[0], res[1]
    return v, bonus, res[2:9], res[9:16]


def _rwkv_masks(reverse):
    n = GROUP_HEADS * CHUNK
    i = np.arange(n)[:, None]
    j = np.arange(n)[None, :]
    same_head = (i // CHUNK) == (j // CHUNK)
    before = (j > i) if reverse else (j < i)
    strict = same_head & before
    incl = same_head & (before | (i == j))
    levels = []
    size = 1
    while size < CHUNK:
        levels.append(same_head & ((i // (2 * size)) == (j // (2 * size))) & ((i // size) != (j // size)))
        size *= 2
    head_lane = (i // CHUNK) == (j // RWKV_HEAD)
    tri = np.stack([strict, incl]).astype(np.float32)
    return tri, np.stack(levels).astype(np.float32), head_lane.astype(np.float32)


def _rwkv_scan_kernel(rh_ref, kh_ref, bh_ref, kap_ref, kg_ref, bg_ref, v_ref, gl_ref,
                      tri_ref, lvl_ref, hm_ref, y_ref, s_ref):
    @pl.when(pl.program_id(1) == 0)
    def _():
        s_ref[...] = jnp.zeros_like(s_ref)

    n = GROUP_HEADS * CHUNK
    hm = hm_ref[...]
    strict = tri_ref[0]
    incl = tri_ref[1]
    eye = incl - strict
    n_levels = lvl_ref.shape[0]

    def stack(ref, g):
        x = ref[0, :, g * MXU_DIM:(g + 1) * MXU_DIM]
        return jnp.concatenate([x] * GROUP_HEADS, axis=0) * hm

    groups = range(y_ref.shape[-1] // MXU_DIM)
    kap = [stack(kap_ref, g) for g in groups]
    bh = [stack(bh_ref, g) for g in groups]
    kh = [stack(kh_ref, g) for g in groups]
    rh = [stack(rh_ref, g) for g in groups]
    vs = [stack(v_ref, g) for g in groups]
    a_ab = [_nt(kap[g], bh[g]) * strict for g in groups]
    a_ak = [(_nt(kap[g], kh[g]) * strict).astype(BF16) for g in groups]
    p_rb = [(_nt(rh[g], bh[g]) * incl).astype(BF16) for g in groups]
    p_rk = [(_nt(rh[g], kh[g]) * incl).astype(BF16) for g in groups]
    t = [eye - a_ab[g] * lvl_ref[0] for g in groups]
    for lv in range(1, n_levels):
        tb = [t[g].astype(BF16) for g in groups]
        x = [_mm(tb[g], (a_ab[g] * lvl_ref[lv]).astype(BF16)).astype(BF16) for g in groups]
        t = [t[g] - _mm(x[g], tb[g]) for g in groups]
    s0 = [s_ref[g] for g in groups]
    s0b = [s0[g].astype(BF16) for g in groups]
    wmat = [(_nt(kap[g], s0b[g]) + _mm(a_ak[g], vs[g])).astype(BF16) for g in groups]
    ub = [(-_mm(t[g].astype(BF16), wmat[g])).astype(BF16) for g in groups]
    ys = [_nt(rh[g], s0b[g]) + _mm(p_rb[g], ub[g]) + _mm(p_rk[g], vs[g]) for g in groups]
    for g in groups:
        y = ys[g][0:CHUNK]
        for h in range(1, GROUP_HEADS):
            y = y + ys[g][h * CHUNK:(h + 1) * CHUNK]
        y_ref[0, :, g * MXU_DIM:(g + 1) * MXU_DIM] = y
        gam = gl_ref[0, 0][:, g * MXU_DIM:(g + 1) * MXU_DIM]
        s_ref[g] = s0[g] * gam + _tn(ub[g], stack(bg_ref, g)) + _tn(vs[g], stack(kg_ref, g))


def _scan_chunk_index(c, nch, reverse):
    if reverse:
        return nch - 1 - c
    n_ctx = TOK_BLOCK // CHUNK
    return jnp.where(c < n_ctx, nch - n_ctx + c, c - n_ctx)


def _rwkv_scan(dir_ops, v, reverse):
    rh, kh, bh, kap, kg, bg, gl = dir_ops
    b, lt, w = v.shape
    nch = lt // CHUNK
    n = GROUP_HEADS * CHUNK
    tri, lvl, hm = _rwkv_masks(reverse)
    cidx = lambda c: _scan_chunk_index(c, nch, reverse)
    tok = pl.BlockSpec((1, CHUNK, w), lambda bb, c: (bb, cidx(c), 0))
    full = lambda shape: pl.BlockSpec(shape, lambda bb, c: (0,) * len(shape))
    return pl.pallas_call(
        _rwkv_scan_kernel,
        grid=(b, nch),
        in_specs=[tok] * 7 + [
            pl.BlockSpec((1, 1, 1, w), lambda bb, c: (bb, cidx(c), 0, 0)),
            full(tri.shape), full(lvl.shape), full(hm.shape),
        ],
        out_specs=tok,
        out_shape=jax.ShapeDtypeStruct((b, lt, w), F32),
        scratch_shapes=[pltpu.VMEM((w // MXU_DIM, n, n), F32)],
        compiler_params=_cparams("arbitrary", "arbitrary"),
        name="rwkv_scan_rev" if reverse else "rwkv_scan_fwd",
    )(rh, kh, bh, kap, kg, bg, v, gl, jnp.asarray(tri), jnp.asarray(lvl), jnp.asarray(hm, dtype=BF16))


def _hgrn_consts(reverse):
    t = np.arange(TOK_BLOCK)[:, None]
    s = np.arange(TOK_BLOCK)[None, :]
    same = (t // CHUNK) == (s // CHUNK)
    order = (s >= t) if reverse else (s <= t)
    earlier_sub = ((s // SUB) > (t // SUB)) if reverse else ((s // SUB) < (t // SUB))
    both = np.concatenate([same & order, same & earlier_sub], axis=0).astype(np.float32)
    return np.concatenate([both] * 3, axis=1)


def _split3(x):
    x1 = x.astype(BF16)
    r1 = x - x1.astype(F32)
    x2 = r1.astype(BF16)
    x3 = (r1 - x2.astype(F32)).astype(BF16)
    return jnp.concatenate([x1, x2, x3], axis=0)


def _hgrn_scan_kernel(reverse, q_ref, f_ref, i_ref, lb_ref, cm_ref, o_ref, s_ref, g_scr, gb_scr, q_scr, k_scr):
    @pl.when(pl.program_id(1) == 0)
    def _():
        s_ref[...] = jnp.zeros_like(s_ref)

    n_sub = CHUNK // SUB
    n_heads = q_ref.shape[-1] // HGRN_HEAD
    n_chunks = TOK_BLOCK // CHUNK
    sub_order = list(range(n_sub))[::-1] if reverse else list(range(n_sub))
    row = lax.broadcasted_iota(jnp.int32, (CHUNK, 1), 0)
    row_sub = row // SUB
    srow = lax.broadcasted_iota(jnp.int32, (SUB, 1), 0)
    lane = lax.broadcasted_iota(jnp.int32, (1, LANES), 1)
    last_row = 0 if reverse else CHUNK - 1

    kgate = (1.0 - lb_ref[...]) * _sigmoid(-f_ref[0])
    sums = _mm(cm_ref[...], _split3(jnp.log1p(-kgate)))
    g_scr[...] = sums[:TOK_BLOCK]
    gb_scr[...] = sums[TOK_BLOCK:]
    q_scr[...] = _silu(q_ref[0])
    k_scr[...] = kgate

    for ci in range(n_chunks):
        c = n_chunks - 1 - ci if reverse else ci
        rows = pl.ds(c * CHUNK, CHUNK)

        def heads(hp, carry):
            par = range(HGRN_PAR)
            hs = [hp * HGRN_PAR + j for j in par]
            cols = [pl.ds(pl.multiple_of(h * HGRN_HEAD, HGRN_HEAD), HGRN_HEAD) for h in hs]
            g = [g_scr[rows, cs] for cs in cols]
            gb = [gb_scr[rows, cs] for cs in cols]
            q = [q_scr[rows, cs] for cs in cols]
            k = [k_scr[rows, cs] for cs in cols]
            vb = [i_ref[0, rows, cs].astype(BF16) for cs in cols]
            s0 = [s_ref[h] for h in hs]
            q_in = [q[j] * jnp.exp(g[j] - gb[j]) for j in par]
            at = []
            for j in par:
                kts, qts = [], []
                for si in sub_order[1:]:
                    gb_i = gb[j][si * SUB:si * SUB + 1, :]
                    earlier = (row_sub > si) if reverse else (row_sub < si)
                    kts.append(jnp.where(earlier, k[j] * jnp.exp(jnp.minimum(gb_i - g[j], 0.0)), 0.0).astype(BF16))
                    qts.append(jnp.where(row_sub == si, q_in[j], 0.0).astype(BF16))
                at.append(_nt(jnp.concatenate(kts, axis=-1), jnp.concatenate(qts, axis=-1)))
            diag = [[] for _ in par]
            for si in range(n_sub):
                blk = slice(si * SUB, (si + 1) * SUB)
                acc = [jnp.zeros((SUB, LANES), F32) for _ in par]
                for tt in range(SUB):
                    t = si * SUB + tt
                    valid = jnp.logical_and(lane == t, (srow >= tt) if reverse else (srow <= tt))
                    for j in par:
                        e = jnp.exp(jnp.minimum(g[j][t:t + 1, :] - g[j][blk], 0.0))
                        col = jnp.sum(k[j][blk] * e * q[j][t:t + 1, :], axis=-1, keepdims=True)
                        acc[j] = jnp.where(valid, col, acc[j])
                for j in par:
                    diag[j].append(acc[j])
            for j in par:
                a = (at[j] + jnp.concatenate(diag[j], axis=0)[:, :CHUNK]).astype(BF16)
                o = _tn(a, vb[j]) + _nt((q[j] * jnp.exp(g[j])).astype(BF16), s0[j].astype(BF16))
                o_ref[0, rows, cols[j]] = o
                g_last = g[j][last_row:last_row + 1, :]
                k_out = (k[j] * jnp.exp(g_last - g[j])).astype(BF16)
                s_ref[hs[j]] = s0[j] * jnp.exp(g_last) + _tn(vb[j], k_out)
            return carry

        lax.fori_loop(0, n_heads // HGRN_PAR, heads, 0)


def _hgrn_scan(p_hgrn, lb_d, d, reverse):
    b, lt, w4 = p_hgrn.shape
    w = w4 // 4
    nblk = lt // TOK_BLOCK
    bidx = (lambda i: nblk - 1 - i) if reverse else (lambda i: jnp.where(i == 0, nblk - 1, i - 1))
    col = lambda j: pl.BlockSpec((1, TOK_BLOCK, w), lambda bb, i: (bb, bidx(i), j))
    cm = jnp.asarray(_hgrn_consts(reverse), dtype=BF16)
    return pl.pallas_call(
        functools.partial(_hgrn_scan_kernel, reverse),
        grid=(b, nblk),
        in_specs=[col(0), col(1 + d), col(3),
                  pl.BlockSpec((1, w), lambda bb, i: (0, 0)),
                  pl.BlockSpec(cm.shape, lambda bb, i: (0, 0))],
        out_specs=col(0),
        out_shape=jax.ShapeDtypeStruct((b, lt, w), F32),
        scratch_shapes=[pltpu.VMEM((w // HGRN_HEAD, HGRN_HEAD, HGRN_HEAD), F32)]
        + [pltpu.VMEM((TOK_BLOCK, w), F32)] * 4,
        compiler_params=_cparams("arbitrary", "arbitrary"),
        name="hgrn_scan_rev" if reverse else "hgrn_scan_fwd",
    )(p_hgrn, p_hgrn, p_hgrn, lb_d.reshape(1, w), cm)


def _residual_ln(alpha, x, gm, proj, lng, lnb):
    z = alpha * x + gm * proj
    mu = jnp.mean(z, axis=-1, keepdims=True)
    zc = z - mu
    var = jnp.mean(zc * zc, axis=-1, keepdims=True)
    return zc * lax.rsqrt(var + LN_EPS) * lng + lnb


def _even_out_kernel(alpha, ya0_ref, ya1_ref, bonus_ref, yb0_ref, yb1_ref, gate_ref, x_ref, gm_ref, w_ref,
                     lnxg_ref, lnxb_ref, ng_ref, lng_ref, lnb_ref, o_ref):
    y = ya0_ref[0] + ya1_ref[0]
    mu = _group64_sum(y) * (1.0 / RWKV_HEAD)
    yc = y - mu
    var = _group64_sum(yc * yc) * (1.0 / RWKV_HEAD)
    ya = yc * lax.rsqrt(var + RWKV_GN_EPS) * lnxg_ref[...] + lnxb_ref[...] + bonus_ref[0]
    o = yb0_ref[0] + yb1_ref[0]
    yb = o * lax.rsqrt(_group128_mean(o * o) + LN_EPS) * ng_ref[...]
    ycat = jnp.concatenate([ya, yb], axis=-1) * _silu(gate_ref[0])
    proj = _mm(ycat.astype(BF16), w_ref[...])
    o_ref[0] = _residual_ln(alpha, x_ref[0], gm_ref[0, 0], proj, lng_ref[...], lnb_ref[...])


def _odd_out_kernel(alpha, y_ref, x_ref, gm_ref, w_ref, lng_ref, lnb_ref, o_ref):
    proj = _mm(y_ref[0].astype(BF16), w_ref[...])
    o_ref[0] = _residual_ln(alpha, x_ref[0], gm_ref[0, 0], proj, lng_ref[...], lnb_ref[...])


def _tok_spec(n):
    return pl.BlockSpec((1, TOK_BLOCK, n), lambda bb, i: (bb, i, 0))


def _full2(shape):
    return pl.BlockSpec(shape, lambda bb, i: (0,) * len(shape))


def _seg_spec(d, nblk):
    return pl.BlockSpec((1, 1, 1, d), lambda bb, i: (bb, _seg(i, nblk), 0, 0))


def _even_out(alpha, ya0, ya1, bonus, yb0, yb1, gate, xs, gmod, w_out, lnx_g, lnx_b, norm_g, ln_g, ln_b):
    b, lt, d = xs.shape
    w = ya0.shape[-1]
    di = gate.shape[-1]
    row = lambda u: u.reshape(1, -1)
    return pl.pallas_call(
        functools.partial(_even_out_kernel, alpha),
        grid=(b, lt // TOK_BLOCK),
        in_specs=[_tok_spec(w)] * 5 + [_tok_spec(di), _tok_spec(d), _seg_spec(d, lt // TOK_BLOCK), _full2((di, d)),
                                       _full2((1, w)), _full2((1, w)), _full2((1, w)), _full2((1, d)), _full2((1, d))],
        out_specs=_tok_spec(d),
        out_shape=jax.ShapeDtypeStruct((b, lt, d), F32),
        compiler_params=_cparams("arbitrary", "arbitrary"),
        name="even_out",
    )(ya0, ya1, bonus, yb0, yb1, gate, xs, gmod, w_out, row(lnx_g), row(lnx_b),
      row(jnp.tile(norm_g, w // HGRN_HEAD)), row(ln_g), row(ln_b))


def _odd_out(alpha, y, xs, gmod, w_out, ln_g, ln_b):
    b, lt, d = xs.shape
    di = y.shape[-1]
    row = lambda u: u.reshape(1, -1)
    return pl.pallas_call(
        functools.partial(_odd_out_kernel, alpha),
        grid=(b, lt // TOK_BLOCK),
        in_specs=[_tok_spec(di), _tok_spec(d), _seg_spec(d, lt // TOK_BLOCK), _full2((di, d)), _full2((1, d)),
                  _full2((1, d))],
        out_specs=_tok_spec(d),
        out_shape=jax.ShapeDtypeStruct((b, lt, d), F32),
        compiler_params=_cparams("arbitrary", "arbitrary"),
        name="odd_out",
    )(y, xs, gmod, w_out, row(ln_g), row(ln_b))


def _rope(x, cos, sin):
    width = x.shape[-1]
    lane = lax.broadcasted_iota(jnp.int32, (1, width), 1)
    first = (lane % 32) < 16
    partner = jnp.where(first, pltpu.roll(x, width - 16, 1), pltpu.roll(x, 16, 1))
    return x * cos + partner * sin


def _attn_kernel(lam_init, tk, n_full, tail, lam_ref, q_ref, k_ref, vt_ref, g_ref, sg_ref, *rest):
    y_ref, m_scr, acc_scr = rest[-3:]
    tq = q_ref.shape[1]
    ones_rows = 16
    tile = min(ATTN_TILE, 2 * tq)
    n_tiles = 2 * tq // tile
    q = q_ref[0].astype(F32) * (DIFF_HEAD ** -0.5 * float(np.log2(np.e)))
    lane = lax.broadcasted_iota(jnp.int32, (1, LANES), 1)
    q_cat = jnp.concatenate([jnp.where(lane < DIFF_HEAD, q, 0.0), jnp.where(lane >= DIFF_HEAD, q, 0.0)],
                            axis=0).astype(BF16)
    m_scr[...] = jnp.full(m_scr.shape, -jnp.inf, F32)
    acc_scr[...] = jnp.zeros(acc_scr.shape, F32)

    def scores(item, size):
        start, c = item
        return _nt(k_ref[0, pl.ds(start, size), :], q_cat[c * tile:(c + 1) * tile])

    def run(starts, size):
        cols = [slice(c * tile, (c + 1) * tile) for c in range(n_tiles)]
        m_run = [m_scr[:, cs] for cs in cols]
        acc = [acc_scr[:, cs] for cs in cols]
        vta = None
        items =[(start, c) for start in starts for c in range(n_tiles)]
        st_next = scores(items[0], size)
        for i, (start, c) in enumerate(items):
            st = st_next
            if i + 1 < len(items):
                st_next = scores(items[i + 1], size)
            if c == 0:
                vta = jnp.concatenate([vt_ref[0, :, pl.ds(start, size)], jnp.ones((ones_rows, size), BF16)], axis=0)
            m_new = jnp.maximum(m_run[c], jnp.max(st, axis=0, keepdims=True))
            corr = jnp.exp2(m_run[c] - m_new)
            pt = jnp.exp2(st - m_new).astype(BF16)
            acc[c] = acc[c] * corr + _mm(vta, pt)
            m_run[c] = m_new
        for c, cs in enumerate(cols):
            m_scr[:, cs] = m_run[c]
            acc_scr[:, cs] = acc[c]

    if n_full:
        unroll = ATTN_UNROLL if n_full % ATTN_UNROLL == 0 else 1
        def body(j, carry):
            run([pl.multiple_of((j * unroll + u) * tk, tk) for u in range(unroll)], tk)
            return carry
        lax.fori_loop(0, n_full // unroll, body, 0)
    run([n_full * tk], tail)

    a0 = acc_scr[:, :tq]
    a1 = acc_scr[:, tq:]
    o_t = a0[:LANES] / a0[LANES:LANES + 1] - lam_ref[0] * (a1[:LANES] / a1[LANES:LANES + 1])
    o = jnp.transpose(o_t)
    y = o * lax.rsqrt(jnp.mean(o * o, axis=-1, keepdims=True) + LN_EPS) * sg_ref[...] * (1.0 - lam_init)
    y_ref[0] = y * _silu(g_ref[0].astype(F32))


def _diff_attention(p_qk, p_vt, p_g, lam, subln_g, lam_init):
    b, lt, di = p_g.shape
    nh = di // LANES
    t = lt - TOK_BLOCK
    tq = tk = 512
    assert t % tq == 0
    ctx_blk = t // TOK_BLOCK
    sm = pl.BlockSpec(memory_space=pltpu.SMEM)
    sg_spec = pl.BlockSpec((1, LANES), lambda bb, h, i: (0, 0))
    scratch = lambda n: [pltpu.VMEM((1, 2 * n), F32), pltpu.VMEM((LANES + 16, 2 * n), F32)]
    args = (lam.reshape(1), p_qk, p_qk, p_vt, p_g, subln_g.reshape(1, LANES))
    y = pl.pallas_call(
        functools.partial(_attn_kernel, lam_init, tk, t // tk, TOK_BLOCK),
        grid=(b, nh, t // tq),
        in_specs=[sm,
                  pl.BlockSpec((1, tq, LANES), lambda bb, h, i: (bb, i, h)),
                  pl.BlockSpec((1, lt, LANES), lambda bb, h, i: (bb, 0, nh + h)),
                  pl.BlockSpec((1, LANES, lt), lambda bb, h, i: (bb, h, 0)),
                  pl.BlockSpec((1, tq, LANES), lambda bb, h, i: (bb, i, h)),
                  sg_spec],
        out_specs=pl.BlockSpec((1, tq, LANES), lambda bb, h, i: (bb, i, h)),
        out_shape=jax.ShapeDtypeStruct((b, lt, di), F32),
        scratch_shapes=scratch(tq),
        compiler_params=_cparams("arbitrary", "arbitrary", "arbitrary"),
        name="diff_attn",
    )(*args)
    blk = lambda bb, h, i: (bb, ctx_blk, h)
    return pl.pallas_call(
        functools.partial(_attn_kernel, lam_init, tk, 0, TOK_BLOCK),
        grid=(b, nh, 1),
        in_specs=[sm,
                  pl.BlockSpec((1, TOK_BLOCK, LANES), blk),
                  pl.BlockSpec((1, TOK_BLOCK, LANES), lambda bb, h, i: (bb, ctx_blk, nh + h)),
                  pl.BlockSpec((1, LANES, TOK_BLOCK), lambda bb, h, i: (bb, h, ctx_blk)),
                  pl.BlockSpec((1, TOK_BLOCK, LANES), blk),
                  sg_spec,
                  pl.BlockSpec(memory_space=pl.ANY)],
        out_specs=pl.BlockSpec((1, TOK_BLOCK, LANES), blk),
        out_shape=jax.ShapeDtypeStruct((b, lt, di), F32),
        scratch_shapes=scratch(TOK_BLOCK),
        input_output_aliases={6: 0},
        compiler_params=_cparams("arbitrary", "arbitrary", "arbitrary"),
        name="diff_attn_ctx",
    )(*args, y)


def _rope_tables(n_ctx, t):
    quarter = DIFF_HEAD // 4
    inv = ROPE_BASE ** (-jnp.arange(quarter, dtype=F32) / quarter)
    pos = jnp.arange(t)
    rows = (pos // GRID_W).astype(F32)[:, None] * inv
    cols = (pos % GRID_W).astype(F32)[:, None] * inv
    cos64 = jnp.concatenate([jnp.cos(rows), jnp.cos(rows), jnp.cos(cols), jnp.cos(cols)], -1)
    sin64 = jnp.concatenate([-jnp.sin(rows), jnp.sin(rows), -jnp.sin(cols), jnp.sin(cols)], -1)
    cos = jnp.concatenate([jnp.tile(cos64, (1, 2)), jnp.ones((n_ctx, LANES), F32)], 0)
    sin = jnp.concatenate([jnp.tile(sin64, (1, 2)), jnp.zeros((n_ctx, LANES), F32)], 0)
    return cos, sin


def _even_layer(alpha, xs, scale, shift, gmod, w_in, w_out, mu_prev, mu_next, w0, w2, a0, a2, k_k, k_a, r_k,
                lnx_g, lnx_b, lb, norm_g, ln_g, ln_b):
    w = k_k.shape[-1]
    sw = mu_prev.shape[-1]
    hw = 4 * lb.shape[-1]
    w_in = w_in.astype(BF16)
    p_rwkv = _project(xs, scale, shift, w_in[:, :sw], sw, F32)
    p_hgrn = _project(xs, scale, shift, w_in[:, sw:sw + hw], hw // 2, F32)
    gate = _project(xs, scale, shift, w_in[:, sw + hw:], w_in.shape[1] - sw - hw, F32)
    v, bonus, ops_f, ops_r = _rwkv_prep(p_rwkv, mu_prev, mu_next, w0, w2, a0, a2, k_k, k_a, r_k)
    ya0 = _rwkv_scan(ops_f, v, False)
    ya1 = _rwkv_scan(ops_r, v, True)
    yb0 = _hgrn_scan(p_hgrn, lb[0], 0, False)
    yb1 = _hgrn_scan(p_hgrn, lb[1], 1, True)
    return _even_out(alpha, ya0, ya1, bonus, yb0, yb1, gate, xs, gmod, w_out.astype(BF16),
                     lnx_g, lnx_b, norm_g, ln_g, ln_b)


def _odd_layer(alpha, xs, scale, shift, gmod, w_in, w_out, lam_p, subln_g, lam_init, cos, sin, ln_g, ln_b):
    di = w_out.shape[0]
    w_in = w_in.astype(BF16)
    p_qk = _project(xs, scale, shift, w_in[:, :2 * di], di, BF16, rope=(cos, sin))
    p_vt = _project_t(xs, scale, shift, jnp.transpose(w_in[:, 2 * di:3 * di]), BF16)
    p_g = _project(xs, scale, shift, w_in[:, 3 * di:], di, BF16)
    lam = jnp.exp(jnp.sum(lam_p[0] * lam_p[1])) - jnp.exp(jnp.sum(lam_p[2] * lam_p[3])) + lam_init
    y = _diff_attention(p_qk, p_vt, p_g, lam, subln_g, lam_init)
    return _odd_out(alpha, y, xs, gmod, w_out.astype(BF16), ln_g, ln_b)


def kernel(x, c, ctx, c_ctx, ada_w, ada_b, ln_g, ln_b, even_w_in, even_w_out, rwkv_mu_prev, rwkv_mu_next, rwkv_w0, rwkv_w2, rwkv_a0, rwkv_a2, rwkv_k_k, rwkv_k_a, rwkv_r_k, rwkv_lnx_g, rwkv_lnx_b, hgrn_lb_logits, hgrn_norm_g, odd_w_in, odd_w_out, diff_lambda, diff_subln_g):
    b, t, d = x.shape
    n_ctx = ctx.shape[1]
    depth = ada_w.shape[0]
    assert n_ctx == TOK_BLOCK and t % TOK_BLOCK == 0 and b + 1 <= 8
    alpha = (2.0 * depth) ** 0.25

    xs = jnp.concatenate([x, ctx], axis=1)
    cvec = jnp.concatenate([c, c_ctx[None], jnp.zeros((8 - b - 1, d), F32)], axis=0)
    mods = _ada_mods(cvec, ada_w, ada_b)
    lb_all = jax.nn.softmax(hgrn_lb_logits.astype(F32), axis=0)
    lb_all = jnp.cumsum(lb_all, axis=0) - lb_all[0]
    cos, sin = _rope_tables(n_ctx, t)

    for layer in range(depth):
        m = mods[layer]
        per_seg = lambda u: jnp.stack([jnp.broadcast_to(u[b], (b, d)), u[:b]], axis=1)[:, :, None, :]
        shift, scale, gmod = (per_seg(m[:, j * d:(j + 1) * d]) for j in range(3))
        j = layer // 2
        if layer % 2 == 0:
            xs = _even_layer(alpha, xs, scale, shift, gmod, even_w_in[j], even_w_out[j], rwkv_mu_prev[j],
                             rwkv_mu_next[j], rwkv_w0[j], rwkv_w2[j], rwkv_a0[j], rwkv_a2[j], rwkv_k_k[j],
                             rwkv_k_a[j], rwkv_r_k[j], rwkv_lnx_g[j], rwkv_lnx_b[j], lb_all[j], hgrn_norm_g[j],
                             ln_g[layer], ln_b[layer])
        else:
            lam_init = 0.8 - 0.6 * float(np.exp(-0.3 * layer))
            xs = _odd_layer(alpha, xs, scale, shift, gmod, odd_w_in[j], odd_w_out[j], diff_lambda[j],
                            diff_subln_g[j], lam_init, cos, sin, ln_g[layer], ln_b[layer])
    return xs[:, :t]
```

```python
import functools

import numpy as np
import jax
import jax.numpy as jnp
from jax import lax
from jax.experimental import pallas as pl
from jax.experimental.pallas import tpu as pltpu

F32 = jnp.float32
BF16 = jnp.bfloat16
HI = lax.Precision.HIGHEST

GRID_W = 64
RWKV_HEAD = 64
RWKV_LORA = 64
RWKV_GN_EPS = 64e-5
HGRN_HEAD = 128
DIFF_HEAD = 64
ROPE_BASE = 10000.0
LN_EPS = 1e-5

LANES = 128
MXU_DIM = 256
VMEM_LIMIT = 56 * 1024 * 1024

CHUNK = 64
SUB = 16
TOK_BLOCK = 256
GROUP_HEADS = MXU_DIM // RWKV_HEAD
HGRN_PAR = 4
ATTN_UNROLL = 16
ATTN_TILE = 1024


def _cparams(*sem):
    return pltpu.CompilerParams(dimension_semantics=sem, vmem_limit_bytes=VMEM_LIMIT)


def _nt(a, b):
    return lax.dot_general(a, b, (((1,), (1,)), ((), ())), preferred_element_type=F32)


def _tn(a, b):
    return lax.dot_general(a, b, (((0,), (0,)), ((), ())), preferred_element_type=F32)


def _mm(a, b):
    return jnp.dot(a, b, preferred_element_type=F32)


def _mm_exact(a, b):
    return jnp.dot(a, b, preferred_element_type=F32, precision=HI)


def _sigmoid(x):
    return 1.0 / (1.0 + jnp.exp(-x))


def _silu(x):
    return x * _sigmoid(x)


def _lane_tiles(x):
    return [x[:, j * LANES:(j + 1) * LANES] for j in range(x.shape[-1] // LANES)]


def _group64_sum(x):
    lane = lax.broadcasted_iota(jnp.int32, (1, LANES), 1)
    low = lane < RWKV_HEAD
    out = []
    for xt in _lane_tiles(x):
        s_all = jnp.sum(xt, axis=-1, keepdims=True)
        s_lo = jnp.sum(jnp.where(low, xt, 0.0), axis=-1, keepdims=True)
        out.append(jnp.where(low, s_lo, s_all - s_lo))
    return jnp.concatenate(out, axis=-1)


def _group128_mean(x):
    out = []
    for xt in _lane_tiles(x):
        out.append(jnp.broadcast_to(jnp.mean(xt, axis=-1, keepdims=True), xt.shape))
    return jnp.concatenate(out, axis=-1)


def _ada_kernel(c_ref, w_ref, b_ref, o_ref):
    cond = _silu(c_ref[...])
    o_ref[0] = _mm_exact(cond, w_ref[0]) + b_ref[0]


def _ada_mods(cvec, ada_w, ada_b):
    depth, d, d3 = ada_w.shape
    tn = 1024
    return pl.pallas_call(
        _ada_kernel,
        grid=(depth, d3 // tn),
        in_specs=[
            pl.BlockSpec((8, d), lambda l, j: (0, 0)),
            pl.BlockSpec((1, d, tn), lambda l, j: (l, 0, j)),
            pl.BlockSpec((1, 1, tn), lambda l, j: (l, 0, j)),
        ],
        out_specs=pl.BlockSpec((1, 8, tn), lambda l, j: (l, 0, j)),
        out_shape=jax.ShapeDtypeStruct((depth, 8, d3), F32),
        compiler_params=_cparams("arbitrary", "arbitrary"),
        name="ada_mods",
    )(cvec, ada_w, ada_b.reshape(depth, 1, d3))


def _seg(i, nblk):
    return jnp.where(i == nblk - 1, 0, 1)


def _modulate(x_ref, sc_ref, sh_ref):
    return (x_ref[0] * (1.0 + sc_ref[0, 0]) + sh_ref[0, 0]).astype(BF16)


def _proj_kernel(x_ref, sc_ref, sh_ref, w_ref, o_ref):
    o_ref[0] = _mm(_modulate(x_ref, sc_ref, sh_ref), w_ref[...]).astype(o_ref.dtype)


def _proj_rope_kernel(x_ref, sc_ref, sh_ref, w_ref, cos_ref, sin_ref, o_ref):
    p = _mm(_modulate(x_ref, sc_ref, sh_ref), w_ref[...])
    reps = p.shape[-1] // LANES
    cos = jnp.concatenate([cos_ref[...]] * reps, axis=-1)
    sin = jnp.concatenate([sin_ref[...]] * reps, axis=-1)
    o_ref[0] = _rope(p, cos, sin).astype(o_ref.dtype)


def _proj_t_kernel(x_ref, sc_ref, sh_ref, wt_ref, o_ref):
    o_ref[0] = _nt(wt_ref[...], _modulate(x_ref, sc_ref, sh_ref)).astype(o_ref.dtype)


def _project(xs, scale, shift, w, tn, out_dtype, rope=None):
    b, lt, d = xs.shape
    n = w.shape[1]
    nblk = lt // TOK_BLOCK
    in_specs = [
        pl.BlockSpec((1, TOK_BLOCK, d), lambda j, bb, i: (bb, i, 0)),
        pl.BlockSpec((1, 1, 1, d), lambda j, bb, i: (bb, _seg(i, nblk), 0, 0)),
        pl.BlockSpec((1, 1, 1, d), lambda j, bb, i: (bb, _seg(i, nblk), 0, 0)),
        pl.BlockSpec((d, tn), lambda j, bb, i: (0, j)),
    ]
    args = (xs, scale, shift, w)
    if rope is not None:
        in_specs += [pl.BlockSpec((TOK_BLOCK, LANES), lambda j, bb, i: (i, 0))] * 2
        args += tuple(rope)
    return pl.pallas_call(
        _proj_kernel if rope is None else _proj_rope_kernel,
        grid=(n // tn, b, nblk),
        in_specs=in_specs,
        out_specs=pl.BlockSpec((1, TOK_BLOCK, tn), lambda j, bb, i: (bb, i, j)),
        out_shape=jax.ShapeDtypeStruct((b, lt, n), out_dtype),
        compiler_params=_cparams("arbitrary", "arbitrary", "arbitrary"),
        name="mod_proj" if rope is None else "mod_proj_rope",
    )(*args)


def _project_t(xs, scale, shift, wt, out_dtype):
    b, lt, d = xs.shape
    n = wt.shape[0]
    nblk = lt // TOK_BLOCK
    return pl.pallas_call(
        _proj_t_kernel,
        grid=(b, nblk),
        in_specs=[
            pl.BlockSpec((1, TOK_BLOCK, d), lambda bb, i: (bb, i, 0)),
            pl.BlockSpec((1, 1, 1, d), lambda bb, i: (bb, _seg(i, nblk), 0, 0)),
            pl.BlockSpec((1, 1, 1, d), lambda bb, i: (bb, _seg(i, nblk), 0, 0)),
            pl.BlockSpec((n, d), lambda bb, i: (0, 0)),
        ],
        out_specs=pl.BlockSpec((1, n, TOK_BLOCK), lambda bb, i: (bb, 0, i)),
        out_shape=jax.ShapeDtypeStruct((b, n, lt), out_dtype),
        compiler_params=_cparams("arbitrary", "arbitrary"),
        name="mod_proj_t",
    )(xs, scale, shift, wt)


def _lora(x, w_ref):
    x1 = x.astype(BF16)
    x2 = (x - x1.astype(F32)).astype(BF16)
    return _mm(jnp.concatenate([x1, x1, x2], axis=-1), w_ref[...])


def _rwkv_prep_kernel(nblk, p_ref, prev_ref, next_ref, mup_ref, mun_ref, w0_ref, w2_ref, a0_ref, a2_ref,
                      kk_ref, ka_ref, rk_ref, cm_ref,
                      v_out, bonus_out, *dir_outs):
    i = pl.program_id(1)
    w = kk_ref.shape[-1]
    p = p_ref[0]
    row = lax.broadcasted_iota(jnp.int32, (TOK_BLOCK, 1), 0)
    prev_row = jnp.where(jnp.logical_and(i >= 1, i <= nblk - 2), prev_ref[0][7:8, :], 0.0)
    next_row = jnp.where(i <= nblk - 3, next_ref[0][0:1, :], 0.0)
    prev = jnp.where(row == 0, prev_row, pltpu.roll(p, 1, 0))
    nxt = jnp.where(row == TOK_BLOCK - 1, next_row, pltpu.roll(p, TOK_BLOCK - 1, 0))
    ps = p + mup_ref[...] * (prev - p) + mun_ref[...] * (nxt - p)

    r = ps[:, 0:w]
    k = ps[:, w:2 * w]
    v = ps[:, 2 * w:3 * w]
    wlo = ps[:, 3 * w:3 * w + 2 * RWKV_LORA]
    alo = ps[:, 3 * w + 2 * RWKV_LORA:3 * w + 4 * RWKV_LORA]

    z = _lora(jnp.tanh(wlo), w2_ref) + w0_ref[...]
    w_log = -(jnp.maximum(-z, 0.0) + jnp.log(1.0 + jnp.exp(-jnp.abs(z)))) - 0.5
    lw = -jnp.exp(w_log)
    a = _sigmoid(_lora(alo, a2_ref) + a0_ref[...])

    kk = k * kk_ref[...]
    kk = kk * lax.rsqrt(_group64_sum(kk * kk) + 1e-12)
    bonus_out[0] = _group64_sum(r * k * rk_ref[...]) * v
    v_out[0] = v.astype(BF16)

    for d in range(2):
        rh_o, kh_o, bh_o, kap_o, kg_o, bg_o, gl_o = dir_outs[7 * d:7 * d + 7]
        lw_d = lw[:, d * w:(d + 1) * w]
        a_d = a[:, d * w:(d + 1) * w]
        kd = k * (1.0 + (a_d - 1.0) * ka_ref[...])
        b_d = kk * a_d
        sums = _mm(cm_ref[d], _split3(lw_d))
        g = sums[:TOK_BLOCK]
        g_all = sums[TOK_BLOCK:]
        e_neg = jnp.exp(-g)
        e_rest = jnp.exp(g_all - g)
        rh_o[0] = (r * jnp.exp(g)).astype(BF16)
        kh_o[0] = (kd * e_neg).astype(BF16)
        bh_o[0] = (b_d * e_neg).astype(BF16)
        kap_o[0] = (kk * jnp.exp(g - lw_d)).astype(BF16)
        kg_o[0] = (kd * e_rest).astype(BF16)
        bg_o[0] = (b_d * e_rest).astype(BF16)
        e_all = jnp.exp(g_all)
        for c in range(TOK_BLOCK // CHUNK):
            gl_o[0, c] = e_all[c * CHUNK:c * CHUNK + 1, :]


def _rwkv_sum_consts(reverse):
    t = np.arange(TOK_BLOCK)[:, None]
    s = np.arange(TOK_BLOCK)[None, :]
    same = (t // CHUNK) == (s // CHUNK)
    order = (s >= t) if reverse else (s <= t)
    both = np.concatenate([same & order, same], axis=0).astype(np.float32)
    return np.concatenate([both] * 3, axis=1)


def _lora_pieces(m):
    hi = m.astype(BF16)
    lo = (m - hi.astype(F32)).astype(BF16)
    return jnp.concatenate([hi, lo, hi], axis=0)


def _rwkv_prep(p_rwkv, mu_prev, mu_next, w0, w2, a0, a2, k_k, k_a, r_k):
    b, lt, sw = p_rwkv.shape
    w = k_k.shape[-1]
    nblk = lt // TOK_BLOCK
    nch = lt // CHUNK
    halo = TOK_BLOCK // 8
    zl = jnp.zeros((RWKV_LORA, w), F32)
    w2cat = _lora_pieces(jnp.concatenate([jnp.concatenate([w2[0], zl], 1), jnp.concatenate([zl, w2[1]], 1)], 0))
    a2cat = _lora_pieces(jnp.concatenate([jnp.concatenate([a2[0], zl], 1), jnp.concatenate([zl, a2[1]], 1)], 0))
    cm = jnp.asarray(np.stack([_rwkv_sum_consts(False), _rwkv_sum_consts(True)]), dtype=BF16)
    row = lambda u: u.reshape(1, -1)
    full = lambda shape: pl.BlockSpec(shape, lambda bb, i: (0,) * len(shape))
    tok = lambda n, dt: (pl.BlockSpec((1, TOK_BLOCK, n), lambda bb, i: (bb, i, 0)), jax.ShapeDtypeStruct((b, lt, n), dt))
    gl = (pl.BlockSpec((1, TOK_BLOCK // CHUNK, 1, w), lambda bb, i: (bb, i, 0, 0)),
          jax.ShapeDtypeStruct((b, nch, 1, w), F32))
    outs = [tok(w, BF16), tok(w, F32)] + 2 * ([tok(w, BF16)] * 6 + [gl])
    res = pl.pallas_call(
        functools.partial(_rwkv_prep_kernel, nblk),
        grid=(b, nblk),
        in_specs=[
            pl.BlockSpec((1, TOK_BLOCK, sw), lambda bb, i: (bb, i, 0)),
            pl.BlockSpec((1, 8, sw), lambda bb, i: (bb, jnp.maximum(i * halo - 1, 0), 0)),
            pl.BlockSpec((1, 8, sw), lambda bb, i: (bb, jnp.minimum((i + 1) * halo, lt // 8 - 1), 0)),
            full((1, sw)), full((1, sw)), full((1, 2 * w)), full(w2cat.shape),
            full((1, 2 * w)), full(a2cat.shape), full((1, w)), full((1, w)), full((1, w)),
            full(cm.shape),
        ],
        out_specs=[o[0] for o in outs],
        out_shape=[o[1] for o in outs],
        compiler_params=_cparams("arbitrary", "arbitrary"),
        name="rwkv_prep",
    )(p_rwkv, p_rwkv, p_rwkv, row(mu_prev), row(mu_next), row(w0), w2cat, row(a0), a2cat,
      row(k_k), row(k_a), row(r_k), cm)
    v, bonus = res[0], res[1]
    return v, bonus, res[2:9], res[9:16]


def _rwkv_masks(reverse):
    n = GROUP_HEADS * CHUNK
    i = np.arange(n)[:, None]
    j = np.arange(n)[None, :]
    same_head = (i // CHUNK) == (j // CHUNK)
    before = (j > i) if reverse else (j < i)
    strict = same_head & before
    incl = same_head & (before | (i == j))
    levels = []
    size = 1
    while size < CHUNK:
        levels.append(same_head & ((i // (2 * size)) == (j // (2 * size))) & ((i // size) != (j // size)))
        size *= 2
    head_lane = (i // CHUNK) == (j // RWKV_HEAD)
    tri = np.stack([strict, incl]).astype(np.float32)
    return tri, np.stack(levels).astype(np.float32), head_lane.astype(np.float32)


def _rwkv_scan_kernel(rh_ref, kh_ref, bh_ref, kap_ref, kg_ref, bg_ref, v_ref, gl_ref,
                      tri_ref, lvl_ref, hm_ref, y_ref, s_ref):
    @pl.when(pl.program_id(1) == 0)
    def _():
        s_ref[...] = jnp.zeros_like(s_ref)

    n = GROUP_HEADS * CHUNK
    hm = hm_ref[...]
    strict = tri_ref[0]
    incl = tri_ref[1]
    eye = incl - strict
    n_levels = lvl_ref.shape[0]

    def stack(ref, g):
        x = ref[0, :, g * MXU_DIM:(g + 1) * MXU_DIM]
        return jnp.concatenate([x] * GROUP_HEADS, axis=0) * hm

    groups = range(y_ref.shape[-1] // MXU_DIM)
    kap = [stack(kap_ref, g) for g in groups]
    bh = [stack(bh_ref, g) for g in groups]
    kh = [stack(kh_ref, g) for g in groups]
    rh = [stack(rh_ref, g) for g in groups]
    vs = [stack(v_ref, g) for g in groups]
    a_ab = [_nt(kap[g], bh[g]) * strict for g in groups]
    a_ak = [(_nt(kap[g], kh[g]) * strict).astype(BF16) for g in groups]
    p_rb = [(_nt(rh[g], bh[g]) * incl).astype(BF16) for g in groups]
    p_rk = [(_nt(rh[g], kh[g]) * incl).astype(BF16) for g in groups]
    t = [eye - a_ab[g] * lvl_ref[0] for g in groups]
    for lv in range(1, n_levels):
        tb = [t[g].astype(BF16) for g in groups]
        x = [_mm(tb[g], (a_ab[g] * lvl_ref[lv]).astype(BF16)).astype(BF16) for g in groups]
        t = [t[g] - _mm(x[g], tb[g]) for g in groups]
    s0 = [s_ref[g] for g in groups]
    s0b = [s0[g].astype(BF16) for g in groups]
    wmat = [(_nt(kap[g], s0b[g]) + _mm(a_ak[g], vs[g])).astype(BF16) for g in groups]
    ub = [(-_mm(t[g].astype(BF16), wmat[g])).astype(BF16) for g in groups]
    ys = [_nt(rh[g], s0b[g]) + _mm(p_rb[g], ub[g]) + _mm(p_rk[g], vs[g]) for g in groups]
    for g in groups:
        y = ys[g][0:CHUNK]
        for h in range(1, GROUP_HEADS):
            y = y + ys[g][h * CHUNK:(h + 1) * CHUNK]
        y_ref[0, :, g * MXU_DIM:(g + 1) * MXU_DIM] = y
        gam = gl_ref[0, 0][:, g * MXU_DIM:(g + 1) * MXU_DIM]
        s_ref[g] = s0[g] * gam + _tn(ub[g], stack(bg_ref, g)) + _tn(vs[g], stack(kg_ref, g))


def _scan_chunk_index(c, nch, reverse):
    if reverse:
        return nch - 1 - c
    n_ctx = TOK_BLOCK // CHUNK
    return jnp.where(c < n_ctx, nch - n_ctx + c, c - n_ctx)


def _rwkv_scan(dir_ops, v, reverse):
    rh, kh, bh, kap, kg, bg, gl = dir_ops
    b, lt, w = v.shape
    nch = lt // CHUNK
    n = GROUP_HEADS * CHUNK
    tri, lvl, hm = _rwkv_masks(reverse)
    cidx = lambda c: _scan_chunk_index(c, nch, reverse)
    tok = pl.BlockSpec((1, CHUNK, w), lambda bb, c: (bb, cidx(c), 0))
    full = lambda shape: pl.BlockSpec(shape, lambda bb, c: (0,) * len(shape))
    return pl.pallas_call(
        _rwkv_scan_kernel,
        grid=(b, nch),
        in_specs=[tok] * 7 + [
            pl.BlockSpec((1, 1, 1, w), lambda bb, c: (bb, cidx(c), 0, 0)),
            full(tri.shape), full(lvl.shape), full(hm.shape),
        ],
        out_specs=tok,
        out_shape=jax.ShapeDtypeStruct((b, lt, w), F32),
        scratch_shapes=[pltpu.VMEM((w // MXU_DIM, n, n), F32)],
        compiler_params=_cparams("arbitrary", "arbitrary"),
        name="rwkv_scan_rev" if reverse else "rwkv_scan_fwd",
    )(rh, kh, bh, kap, kg, bg, v, gl, jnp.asarray(tri), jnp.asarray(lvl), jnp.asarray(hm, dtype=BF16))


def _hgrn_consts(reverse):
    t = np.arange(TOK_BLOCK)[:, None]
    s = np.arange(TOK_BLOCK)[None, :]
    same = (t // CHUNK) == (s // CHUNK)
    order = (s >= t) if reverse else (s <= t)
    earlier_sub = ((s // SUB) > (t // SUB)) if reverse else ((s // SUB) < (t // SUB))
    both = np.concatenate([same & order, same & earlier_sub], axis=0).astype(np.float32)
    return np.concatenate([both] * 3, axis=1)


def _split3(x):
    x1 = x.astype(BF16)
    r1 = x - x1.astype(F32)
    x2 = r1.astype(BF16)
    x3 = (r1 - x2.astype(F32)).astype(BF16)
    return jnp.concatenate([x1, x2, x3], axis=0)


def _hgrn_scan_kernel(reverse, q_ref, f_ref, i_ref, lb_ref, cm_ref, o_ref, s_ref, g_scr, gb_scr, q_scr, k_scr):
    @pl.when(pl.program_id(1) == 0)
    def _():
        s_ref[...] = jnp.zeros_like(s_ref)

    n_sub = CHUNK // SUB
    n_heads = q_ref.shape[-1] // HGRN_HEAD
    n_chunks = TOK_BLOCK // CHUNK
    sub_order = list(range(n_sub))[::-1] if reverse else list(range(n_sub))
    row = lax.broadcasted_iota(jnp.int32, (CHUNK, 1), 0)
    row_sub = row // SUB
    srow = lax.broadcasted_iota(jnp.int32, (SUB, 1), 0)
    lane = lax.broadcasted_iota(jnp.int32, (1, LANES), 1)
    last_row = 0 if reverse else CHUNK - 1

    kgate = (1.0 - lb_ref[...]) * _sigmoid(-f_ref[0])
    sums = _mm(cm_ref[...], _split3(jnp.log1p(-kgate)))
    g_scr[...] = sums[:TOK_BLOCK]
    gb_scr[...] = sums[TOK_BLOCK:]
    q_scr[...] = _silu(q_ref[0])
    k_scr[...] = kgate

    for ci in range(n_chunks):
        c = n_chunks - 1 - ci if reverse else ci
        rows = pl.ds(c * CHUNK, CHUNK)

        def heads(hp, carry):
            par = range(HGRN_PAR)
            hs = [hp * HGRN_PAR + j for j in par]
            cols = [pl.ds(pl.multiple_of(h * HGRN_HEAD, HGRN_HEAD), HGRN_HEAD) for h in hs]
            g = [g_scr[rows, cs] for cs in cols]
            gb = [gb_scr[rows, cs] for cs in cols]
            q = [q_scr[rows, cs] for cs in cols]
            k = [k_scr[rows, cs] for cs in cols]
            vb = [i_ref[0, rows, cs].astype(BF16) for cs in cols]
            s0 = [s_ref[h] for h in hs]
            q_in = [q[j] * jnp.exp(g[j] - gb[j]) for j in par]
            at = []
            for j in par:
                kts, qts = [], []
                for si in sub_order[1:]:
                    gb_i = gb[j][si * SUB:si * SUB + 1, :]
                    earlier = (row_sub > si) if reverse else (row_sub < si)
                    kts.append(jnp.where(earlier, k[j] * jnp.exp(jnp.minimum(gb_i - g[j], 0.0)), 0.0).astype(BF16))
                    qts.append(jnp.where(row_sub == si, q_in[j], 0.0).astype(BF16))
                at.append(_nt(jnp.concatenate(kts, axis=-1), jnp.concatenate(qts, axis=-1)))
            diag = [[] for _ in par]
            for si in range(n_sub):
                blk = slice(si * SUB, (si + 1) * SUB)
                acc = [jnp.zeros((SUB, LANES), F32) for _ in par]
                for tt in range(SUB):
                    t = si * SUB + tt
                    valid = jnp.logical_and(lane == t, (srow >= tt) if reverse else (srow <= tt))
                    for j in par:
                        e = jnp.exp(jnp.minimum(g[j][t:t + 1, :] - g[j][blk], 0.0))
                        col = jnp.sum(k[j][blk] * e * q[j][t:t + 1, :], axis=-1, keepdims=True)
                        acc[j] = jnp.where(valid, col, acc[j])
                for j in par:
                    diag[j].append(acc[j])
            for j in par:
                a = (at[j] + jnp.concatenate(diag[j], axis=0)[:, :CHUNK]).astype(BF16)
                o = _tn(a, vb[j]) + _nt((q[j] * jnp.exp(g[j])).astype(BF16), s0[j].astype(BF16))
                o_ref[0, rows, cols[j]] = o
                g_last = g[j][last_row:last_row + 1, :]
                k_out = (k[j] * jnp.exp(g_last - g[j])).astype(BF16)
                s_ref[hs[j]] = s0[j] * jnp.exp(g_last) + _tn(vb[j], k_out)
            return carry

        lax.fori_loop(0, n_heads // HGRN_PAR, heads, 0)


def _hgrn_scan(p_hgrn, lb_d, d, reverse):
    b, lt, w4 = p_hgrn.shape
    w = w4 // 4
    nblk = lt // TOK_BLOCK
    bidx = (lambda i: nblk - 1 - i) if reverse else (lambda i: jnp.where(i == 0, nblk - 1, i - 1))
    col = lambda j: pl.BlockSpec((1, TOK_BLOCK, w), lambda bb, i: (bb, bidx(i), j))
    cm = jnp.asarray(_hgrn_consts(reverse), dtype=BF16)
    return pl.pallas_call(
        functools.partial(_hgrn_scan_kernel, reverse),
        grid=(b, nblk),
        in_specs=[col(0), col(1 + d), col(3),
                  pl.BlockSpec((1, w), lambda bb, i: (0, 0)),
                  pl.BlockSpec(cm.shape, lambda bb, i: (0, 0))],
        out_specs=col(0),
        out_shape=jax.ShapeDtypeStruct((b, lt, w), F32),
        scratch_shapes=[pltpu.VMEM((w // HGRN_HEAD, HGRN_HEAD, HGRN_HEAD), F32)]
        + [pltpu.VMEM((TOK_BLOCK, w), F32)] * 4,
        compiler_params=_cparams("arbitrary", "arbitrary"),
        name="hgrn_scan_rev" if reverse else "hgrn_scan_fwd",
    )(p_hgrn, p_hgrn, p_hgrn, lb_d.reshape(1, w), cm)


def _residual_ln(alpha, x, gm, proj, lng, lnb):
    z = alpha * x + gm * proj
    mu = jnp.mean(z, axis=-1, keepdims=True)
    zc = z - mu
    var = jnp.mean(zc * zc, axis=-1, keepdims=True)
    return zc * lax.rsqrt(var + LN_EPS) * lng + lnb


def _even_out_kernel(alpha, ya0_ref, ya1_ref, bonus_ref, yb0_ref, yb1_ref, gate_ref, x_ref, gm_ref, w_ref,
                     lnxg_ref, lnxb_ref, ng_ref, lng_ref, lnb_ref, o_ref):
    y = ya0_ref[0] + ya1_ref[0]
    mu = _group64_sum(y) * (1.0 / RWKV_HEAD)
    yc = y - mu
    var = _group64_sum(yc * yc) * (1.0 / RWKV_HEAD)
    ya = yc * lax.rsqrt(var + RWKV_GN_EPS) * lnxg_ref[...] + lnxb_ref[...] + bonus_ref[0]
    o = yb0_ref[0] + yb1_ref[0]
    yb = o * lax.rsqrt(_group128_mean(o * o) + LN_EPS) * ng_ref[...]
    ycat = jnp.concatenate([ya, yb], axis=-1) * _silu(gate_ref[0])
    proj = _mm(ycat.astype(BF16), w_ref[...])
    o_ref[0] = _residual_ln(alpha, x_ref[0], gm_ref[0, 0], proj, lng_ref[...], lnb_ref[...])


def _odd_out_kernel(alpha, y_ref, x_ref, gm_ref, w_ref, lng_ref, lnb_ref, o_ref):
    proj = _mm(y_ref[0].astype(BF16), w_ref[...])
    o_ref[0] = _residual_ln(alpha, x_ref[0], gm_ref[0, 0], proj, lng_ref[...], lnb_ref[...])


def _tok_spec(n):
    return pl.BlockSpec((1, TOK_BLOCK, n), lambda bb, i: (bb, i, 0))


def _full2(shape):
    return pl.BlockSpec(shape, lambda bb, i: (0,) * len(shape))


def _seg_spec(d, nblk):
    return pl.BlockSpec((1, 1, 1, d), lambda bb, i: (bb, _seg(i, nblk), 0, 0))


def _even_out(alpha, ya0, ya1, bonus, yb0, yb1, gate, xs, gmod, w_out, lnx_g, lnx_b, norm_g, ln_g, ln_b):
    b, lt, d = xs.shape
    w = ya0.shape[-1]
    di = gate.shape[-1]
    row = lambda u: u.reshape(1, -1)
    return pl.pallas_call(
        functools.partial(_even_out_kernel, alpha),
        grid=(b, lt // TOK_BLOCK),
        in_specs=[_tok_spec(w)] * 5 + [_tok_spec(di), _tok_spec(d), _seg_spec(d, lt // TOK_BLOCK), _full2((di, d)),
                                       _full2((1, w)), _full2((1, w)), _full2((1, w)), _full2((1, d)), _full2((1, d))],
        out_specs=_tok_spec(d),
        out_shape=jax.ShapeDtypeStruct((b, lt, d), F32),
        compiler_params=_cparams("arbitrary", "arbitrary"),
        name="even_out",
    )(ya0, ya1, bonus, yb0, yb1, gate, xs, gmod, w_out, row(lnx_g), row(lnx_b),
      row(jnp.tile(norm_g, w // HGRN_HEAD)), row(ln_g), row(ln_b))


def _odd_out(alpha, y, xs, gmod, w_out, ln_g, ln_b, rows_out):
    b, lt, d = xs.shape
    di = y.shape[-1]
    row = lambda u: u.reshape(1, -1)
    return pl.pallas_call(
        functools.partial(_odd_out_kernel, alpha),
        grid=(b, rows_out // TOK_BLOCK),
        in_specs=[_tok_spec(di), _tok_spec(d), _seg_spec(d, lt // TOK_BLOCK), _full2((di, d)), _full2((1, d)),
                  _full2((1, d))],
        out_specs=_tok_spec(d),
        out_shape=jax.ShapeDtypeStruct((b, rows_out, d), F32),
        compiler_params=_cparams("arbitrary", "arbitrary"),
        name="odd_out",
    )(y, xs, gmod, w_out, row(ln_g), row(ln_b))


def _rope(x, cos, sin):
    width = x.shape[-1]
    lane = lax.broadcasted_iota(jnp.int32, (1, width), 1)
    first = (lane % 32) < 16
    partner = jnp.where(first, pltpu.roll(x, width - 16, 1), pltpu.roll(x, 16, 1))
    return x * cos + partner * sin


def _attn_kernel(lam_init, tk, n_full, tail, lam_ref, q_ref, k_ref, vt_ref, g_ref, sg_ref, *rest):
    y_ref, m_scr, acc_scr = rest[-3:]
    tq = q_ref.shape[1]
    ones_rows = 16
    tile = min(ATTN_TILE, 2 * tq)
    n_tiles = 2 * tq // tile
    q = q_ref[0].astype(F32) * (DIFF_HEAD ** -0.5 * float(np.log2(np.e)))
    lane = lax.broadcasted_iota(jnp.int32, (1, LANES), 1)
    q_cat = jnp.concatenate([jnp.where(lane < DIFF_HEAD, q, 0.0), jnp.where(lane >= DIFF_HEAD, q, 0.0)],
                            axis=0).astype(BF16)
    m_scr[...] = jnp.full(m_scr.shape, -jnp.inf, F32)
    acc_scr[...] = jnp.zeros(acc_scr.shape, F32)

    def scores(item, size):
        start, c = item
        return _nt(k_ref[0, pl.ds(start, size), :], q_cat[c * tile:(c + 1) * tile])

    def run(starts, size):
        cols = [slice(c * tile, (c + 1) * tile) for c in range(n_tiles)]
        m_run = [m_scr[:, cs] for cs in cols]
        acc = [acc_scr[:, cs] for cs in cols]
        vta = None
        pending = None
        items = [(start, c) for start in starts for c in range(n_tiles)]
        st_next = scores(items[0], size)
        for i, (start, c) in enumerate(items):
            st = st_next
            if i + 1 < len(items):
                st_next = scores(items[i + 1], size)
            if pending is not None:
                pc, pcorr, pvta, ppt = pending
                acc[pc] = acc[pc] * pcorr + _mm(pvta, ppt)
            if c == 0:
                vta = jnp.concatenate([vt_ref[0, :, pl.ds(start, size)], jnp.ones((ones_rows, size), BF16)], axis=0)
            m_new = jnp.maximum(m_run[c], jnp.max(st, axis=0, keepdims=True))
            pending = (c, jnp.exp2(m_run[c] - m_new), vta, jnp.exp2(st - m_new).astype(BF16))
            m_run[c] = m_new
        pc, pcorr, pvta, ppt = pending
        acc[pc] = acc[pc] * pcorr + _mm(pvta, ppt)
        for c, cs in enumerate(cols):
            m_scr[:, cs] = m_run[c]
            acc_scr[:, cs] = acc[c]

    if n_full:
        unroll = ATTN_UNROLL if n_full % ATTN_UNROLL == 0 else 1
        def body(j, carry):
            run([pl.multiple_of((j * unroll + u) * tk, tk) for u in range(unroll)], tk)
            return carry
        lax.fori_loop(0, n_full // unroll, body, 0)
    run([n_full * tk], tail)

    a0 = acc_scr[:, :tq]
    a1 = acc_scr[:, tq:]
    o_t = a0[:LANES] / a0[LANES:LANES + 1] - lam_ref[0] * (a1[:LANES] / a1[LANES:LANES + 1])
    o = jnp.transpose(o_t)
    y = o * lax.rsqrt(jnp.mean(o * o, axis=-1, keepdims=True) + LN_EPS) * sg_ref[...] * (1.0 - lam_init)
    y_ref[0] = y * _silu(g_ref[0].astype(F32))


def _diff_attention(p_qk, p_vt, p_g, lam, subln_g, lam_init):
    b, lt, di = p_g.shape
    nh = di // LANES
    t = lt - TOK_BLOCK
    tq = tk = 512
    assert t % tq == 0
    ctx_blk = t // TOK_BLOCK
    sm = pl.BlockSpec(memory_space=pltpu.SMEM)
    sg_spec = pl.BlockSpec((1, LANES), lambda bb, h, i: (0, 0))
    scratch = lambda n: [pltpu.VMEM((1, 2 * n), F32), pltpu.VMEM((LANES + 16, 2 * n), F32)]
    args = (lam.reshape(1), p_qk, p_qk, p_vt, p_g, subln_g.reshape(1, LANES))
    y = pl.pallas_call(
        functools.partial(_attn_kernel, lam_init, tk, t // tk, TOK_BLOCK),
        grid=(b, nh, t // tq),
        in_specs=[sm,
                  pl.BlockSpec((1, tq, LANES), lambda bb, h, i: (bb, i, h)),
                  pl.BlockSpec((1, lt, LANES), lambda bb, h, i: (bb, 0, nh + h)),
                  pl.BlockSpec((1, LANES, lt), lambda bb, h, i: (bb, h, 0)),
                  pl.BlockSpec((1, tq, LANES), lambda bb, h, i: (bb, i, h)),
                  sg_spec],
        out_specs=pl.BlockSpec((1, tq, LANES), lambda bb, h, i: (bb, i, h)),
        out_shape=jax.ShapeDtypeStruct((b, lt, di), F32),
        scratch_shapes=scratch(tq),
        compiler_params=_cparams("arbitrary", "arbitrary", "arbitrary"),
        name="diff_attn",
    )(*args)
    blk = lambda bb, h, i: (bb, ctx_blk, h)
    return pl.pallas_call(
        functools.partial(_attn_kernel, lam_init, tk, 0, TOK_BLOCK),
        grid=(b, nh, 1),
        in_specs=[sm,
                  pl.BlockSpec((1, TOK_BLOCK, LANES), blk),
                  pl.BlockSpec((1, TOK_BLOCK, LANES), lambda bb, h, i: (bb, ctx_blk, nh + h)),
                  pl.BlockSpec((1, LANES, TOK_BLOCK), lambda bb, h, i: (bb, h, ctx_blk)),
                  pl.BlockSpec((1, TOK_BLOCK, LANES), blk),
                  sg_spec,
                  pl.BlockSpec(memory_space=pl.ANY)],
        out_specs=pl.BlockSpec((1, TOK_BLOCK, LANES), blk),
        out_shape=jax.ShapeDtypeStruct((b, lt, di), F32),
        scratch_shapes=scratch(TOK_BLOCK),
        input_output_aliases={6: 0},
        compiler_params=_cparams("arbitrary", "arbitrary", "arbitrary"),
        name="diff_attn_ctx",
    )(*args, y)


def _rope_tables(n_ctx, t):
    quarter = DIFF_HEAD // 4
    inv = ROPE_BASE ** (-jnp.arange(quarter, dtype=F32) / quarter)
    pos = jnp.arange(t)
    rows = (pos // GRID_W).astype(F32)[:, None] * inv
    cols = (pos % GRID_W).astype(F32)[:, None] * inv
    cos64 = jnp.concatenate([jnp.cos(rows), jnp.cos(rows), jnp.cos(cols), jnp.cos(cols)], -1)
    sin64 = jnp.concatenate([-jnp.sin(rows), jnp.sin(rows), -jnp.sin(cols), jnp.sin(cols)], -1)
    cos = jnp.concatenate([jnp.tile(cos64, (1, 2)), jnp.ones((n_ctx, LANES), F32)], 0)
    sin = jnp.concatenate([jnp.tile(sin64, (1, 2)), jnp.zeros((n_ctx, LANES), F32)], 0)
    return cos, sin


def _even_layer(alpha, xs, scale, shift, gmod, w_in, w_out, mu_prev, mu_next, w0, w2, a0, a2, k_k, k_a, r_k,
                lnx_g, lnx_b, lb, norm_g, ln_g, ln_b):
    w = k_k.shape[-1]
    sw = mu_prev.shape[-1]
    hw = 4 * lb.shape[-1]
    w_in = w_in.astype(BF16)
    p_rwkv = _project(xs, scale, shift, w_in[:, :sw], sw, F32)
    p_hgrn = _project(xs, scale, shift, w_in[:, sw:sw + hw], hw // 2, F32)
    gate = _project(xs, scale, shift, w_in[:, sw + hw:], w_in.shape[1] - sw - hw, F32)
    v, bonus, ops_f, ops_r = _rwkv_prep(p_rwkv, mu_prev, mu_next, w0, w2, a0, a2, k_k, k_a, r_k)
    ya0 = _rwkv_scan(ops_f, v, False)
    ya1 = _rwkv_scan(ops_r, v, True)
    yb0 = _hgrn_scan(p_hgrn, lb[0], 0, False)
    yb1 = _hgrn_scan(p_hgrn, lb[1], 1, True)
    return _even_out(alpha, ya0, ya1, bonus, yb0, yb1, gate, xs, gmod, w_out.astype(BF16),
                     lnx_g, lnx_b, norm_g, ln_g, ln_b)


def _odd_layer(alpha, xs, scale, shift, gmod, w_in, w_out, lam_p, subln_g, lam_init, cos, sin, ln_g, ln_b,
               rows_out):
    di = w_out.shape[0]
    w_in = w_in.astype(BF16)
    p_qk = _project(xs, scale, shift, w_in[:, :2 * di], di, BF16, rope=(cos, sin))
    p_vt = _project_t(xs, scale, shift, jnp.transpose(w_in[:, 2 * di:3 * di]), BF16)
    p_g = _project(xs, scale, shift, w_in[:, 3 * di:], di, BF16)
    lam = jnp.exp(jnp.sum(lam_p[0] * lam_p[1])) - jnp.exp(jnp.sum(lam_p[2] * lam_p[3])) + lam_init
    y = _diff_attention(p_qk, p_vt, p_g, lam, subln_g, lam_init)
    return _odd_out(alpha, y, xs, gmod, w_out.astype(BF16), ln_g, ln_b, rows_out)


def kernel(x, c, ctx, c_ctx, ada_w, ada_b, ln_g, ln_b, even_w_in, even_w_out, rwkv_mu_prev, rwkv_mu_next, rwkv_w0, rwkv_w2, rwkv_a0, rwkv_a2, rwkv_k_k, rwkv_k_a, rwkv_r_k, rwkv_lnx_g, rwkv_lnx_b, hgrn_lb_logits, hgrn_norm_g, odd_w_in, odd_w_out, diff_lambda, diff_subln_g):
    b, t, d = x.shape
    n_ctx = ctx.shape[1]
    depth = ada_w.shape[0]
    assert n_ctx == TOK_BLOCK and t % TOK_BLOCK == 0 and b + 1 <= 8
    alpha = (2.0 * depth) ** 0.25

    xs = jnp.concatenate([x, ctx], axis=1)
    cvec = jnp.concatenate([c, c_ctx[None], jnp.zeros((8 - b - 1, d), F32)], axis=0)
    mods = _ada_mods(cvec, ada_w, ada_b)
    lb_all = jax.nn.softmax(hgrn_lb_logits.astype(F32), axis=0)
    lb_all = jnp.cumsum(lb_all, axis=0) - lb_all[0]
    cos, sin = _rope_tables(n_ctx, t)

    for layer in range(depth):
        m = mods[layer]
        per_seg = lambda u: jnp.stack([jnp.broadcast_to(u[b], (b, d)), u[:b]], axis=1)[:, :, None, :]
        shift, scale, gmod = (per_seg(m[:, j * d:(j + 1) * d]) for j in range(3))
        j = layer // 2
        if layer % 2 == 0:
            xs = _even_layer(alpha, xs, scale, shift, gmod, even_w_in[j], even_w_out[j], rwkv_mu_prev[j],
                             rwkv_mu_next[j], rwkv_w0[j], rwkv_w2[j], rwkv_a0[j], rwkv_a2[j], rwkv_k_k[j],
                             rwkv_k_a[j], rwkv_r_k[j], rwkv_lnx_g[j], rwkv_lnx_b[j], lb_all[j], hgrn_norm_g[j],
                             ln_g[layer], ln_b[layer])
        else:
            lam_init = 0.8 - 0.6 * float(np.exp(-0.3 * layer))
            xs = _odd_layer(alpha, xs, scale, shift, gmod, odd_w_in[j], odd_w_out[j], diff_lambda[j],
                            diff_subln_g[j], lam_init, cos, sin, ln_g[layer], ln_b[layer],
                            t if layer == depth - 1 else t + n_ctx)
    return xs[:, :t]
```

```python
import functools

import numpy as np
import jax
import jax.numpy as jnp
from jax import lax
from jax.experimental import pallas as pl
from jax.experimental.pallas import tpu as pltpu

F32 = jnp.float32
BF16 = jnp.bfloat16
HI = lax.Precision.HIGHEST

GRID_W = 64
RWKV_HEAD = 64
RWKV_LORA = 64
RWKV_GN_EPS = 64e-5
HGRN_HEAD = 128
DIFF_HEAD = 64
ROPE_BASE = 10000.0
LN_EPS = 1e-5
LOG2E = float(np.log2(np.e))

LANES = 128
MXU_DIM = 256
VMEM_LIMIT = 56 * 1024 * 1024

CHUNK = 64
SUB = 16
TOK_BLOCK = 256
GROUP_HEADS = MXU_DIM // RWKV_HEAD
HGRN_PAR = 4
ATTN_UNROLL = 16
ATTN_TILE = 1024


def _cparams(*sem):
    return pltpu.CompilerParams(dimension_semantics=sem, vmem_limit_bytes=VMEM_LIMIT)


def _nt(a, b):
    return lax.dot_general(a, b, (((1,), (1,)), ((), ())), preferred_element_type=F32)


def _tn(a, b):
    return lax.dot_general(a, b, (((0,), (0,)), ((), ())), preferred_element_type=F32)


def _mm(a, b):
    return jnp.dot(a, b, preferred_element_type=F32)


def _mm_exact(a, b):
    return jnp.dot(a, b, preferred_element_type=F32, precision=HI)


def _sigmoid(x):
    return 1.0 / (1.0 + jnp.exp(-x))


def _silu(x):
    return x * _sigmoid(x)


def _lane_tiles(x):
    return [x[:, j * LANES:(j + 1) * LANES] for j in range(x.shape[-1] // LANES)]


def _group64_sum(x):
    lane = lax.broadcasted_iota(jnp.int32, (1, LANES), 1)
    low = lane < RWKV_HEAD
    out = []
    for xt in _lane_tiles(x):
        s_all = jnp.sum(xt, axis=-1, keepdims=True)
        s_lo = jnp.sum(jnp.where(low, xt, 0.0), axis=-1, keepdims=True)
        out.append(jnp.where(low, s_lo, s_all - s_lo))
    return jnp.concatenate(out, axis=-1)


def _group128_mean(x):
    out = []
    for xt in _lane_tiles(x):
        out.append(jnp.broadcast_to(jnp.mean(xt, axis=-1, keepdims=True), xt.shape))
    return jnp.concatenate(out, axis=-1)


def _ada_kernel(c_ref, w_ref, b_ref, o_ref):
    cond = _silu(c_ref[...])
    o_ref[0] = _mm_exact(cond, w_ref[0]) + b_ref[0]


def _ada_mods(cvec, ada_w, ada_b):
    depth, d, d3 = ada_w.shape
    tn = 1024
    return pl.pallas_call(
        _ada_kernel,
        grid=(depth, d3 // tn),
        in_specs=[
            pl.BlockSpec((8, d), lambda l, j: (0, 0)),
            pl.BlockSpec((1, d, tn), lambda l, j: (l, 0, j)),
            pl.BlockSpec((1, 1, tn), lambda l, j: (l, 0, j)),
        ],
        out_specs=pl.BlockSpec((1, 8, tn), lambda l, j: (l, 0, j)),
        out_shape=jax.ShapeDtypeStruct((depth, 8, d3), F32),
        compiler_params=_cparams("arbitrary", "arbitrary"),
        name="ada_mods",
    )(cvec, ada_w, ada_b.reshape(depth, 1, d3))


def _seg(i, nblk):
    return jnp.where(i == nblk - 1, 0, 1)


def _modulate(x_ref, sc_ref, sh_ref):
    return (x_ref[0] * (1.0 + sc_ref[0, 0]) + sh_ref[0, 0]).astype(BF16)


def _proj_kernel(x_ref, sc_ref, sh_ref, w_ref, o_ref):
    o_ref[0] = _mm(_modulate(x_ref, sc_ref, sh_ref), w_ref[...]).astype(o_ref.dtype)


def _proj_rope_kernel(x_ref, sc_ref, sh_ref, w_ref, cos_ref, sin_ref, o_ref):
    p = _mm(_modulate(x_ref, sc_ref, sh_ref), w_ref[...])
    reps = p.shape[-1] // LANES
    cos = jnp.concatenate([cos_ref[...]] * reps, axis=-1)
    sin = jnp.concatenate([sin_ref[...]] * reps, axis=-1)
    o_ref[0] = _rope(p, cos, sin).astype(o_ref.dtype)


def _proj_t_kernel(x_ref, sc_ref, sh_ref, wt_ref, o_ref):
    o_ref[0] = _nt(wt_ref[...], _modulate(x_ref, sc_ref, sh_ref)).astype(o_ref.dtype)


def _project(xs, scale, shift, w, tn, out_dtype, rope=None):
    b, lt, d = xs.shape
    n = w.shape[1]
    nblk = lt // TOK_BLOCK
    in_specs = [
        pl.BlockSpec((1, TOK_BLOCK, d), lambda j, bb, i: (bb, i, 0)),
        pl.BlockSpec((1, 1, 1, d), lambda j, bb, i: (bb, _seg(i, nblk), 0, 0)),
        pl.BlockSpec((1, 1, 1, d), lambda j, bb, i: (bb, _seg(i, nblk), 0, 0)),
        pl.BlockSpec((d, tn), lambda j, bb, i: (0, j)),
    ]
    args = (xs, scale, shift, w)
    if rope is not None:
        in_specs += [pl.BlockSpec((TOK_BLOCK, LANES), lambda j, bb, i: (i, 0))] * 2
        args += tuple(rope)
    return pl.pallas_call(
        _proj_kernel if rope is None else _proj_rope_kernel,
        grid=(n // tn, b, nblk),
        in_specs=in_specs,
        out_specs=pl.BlockSpec((1, TOK_BLOCK, tn), lambda j, bb, i: (bb, i, j)),
        out_shape=jax.ShapeDtypeStruct((b, lt, n), out_dtype),
        compiler_params=_cparams("arbitrary", "arbitrary", "arbitrary"),
        name="mod_proj" if rope is None else "mod_proj_rope",
    )(*args)


def _project_t(xs, scale, shift, wt, out_dtype):
    b, lt, d = xs.shape
    n = wt.shape[0]
    nblk = lt // TOK_BLOCK
    return pl.pallas_call(
        _proj_t_kernel,
        grid=(b, nblk),
        in_specs=[
            pl.BlockSpec((1, TOK_BLOCK, d), lambda bb, i: (bb, i, 0)),
            pl.BlockSpec((1, 1, 1, d), lambda bb, i: (bb, _seg(i, nblk), 0, 0)),
            pl.BlockSpec((1, 1, 1, d), lambda bb, i: (bb, _seg(i, nblk), 0, 0)),
            pl.BlockSpec((n, d), lambda bb, i: (0, 0)),
        ],
        out_specs=pl.BlockSpec((1, n, TOK_BLOCK), lambda bb, i: (bb, 0, i)),
        out_shape=jax.ShapeDtypeStruct((b, n, lt), out_dtype),
        compiler_params=_cparams("arbitrary", "arbitrary"),
        name="mod_proj_t",
    )(xs, scale, shift, wt)


def _lora(x, w_ref):
    x1 = x.astype(BF16)
    x2 = (x - x1.astype(F32)).astype(BF16)
    return _mm(jnp.concatenate([x1, x1, x2], axis=-1), w_ref[...])


def _rwkv_prep_kernel(nblk, p_ref, prev_ref, next_ref, mup_ref, mun_ref, w0_ref, w2_ref, a0_ref, a2_ref,
                      kk_ref, ka_ref, rk_ref, cm_ref,
                      v_out, bonus_out, *dir_outs):
    i = pl.program_id(1)
    w = kk_ref.shape[-1]
    p = p_ref[0]
    row = lax.broadcasted_iota(jnp.int32, (TOK_BLOCK, 1), 0)
    prev_row = jnp.where(jnp.logical_and(i >= 1, i <= nblk - 2), prev_ref[0][7:8, :], 0.0)
    next_row = jnp.where(i <= nblk - 3, next_ref[0][0:1, :], 0.0)
    prev = jnp.where(row == 0, prev_row, pltpu.roll(p, 1, 0))
    nxt = jnp.where(row == TOK_BLOCK - 1, next_row, pltpu.roll(p, TOK_BLOCK - 1, 0))
    ps = p + mup_ref[...] * (prev - p) + mun_ref[...] * (nxt - p)

    r = ps[:, 0:w]
    k = ps[:, w:2 * w]
    v = ps[:, 2 * w:3 * w]
    wlo = ps[:, 3 * w:3 * w + 2 * RWKV_LORA]
    alo = ps[:, 3 * w + 2 * RWKV_LORA:3 * w + 4 * RWKV_LORA]

    z = _lora(jnp.tanh(wlo), w2_ref) + w0_ref[...]
    w_log = -(jnp.maximum(-z, 0.0) + jnp.log(1.0 + jnp.exp(-jnp.abs(z)))) - 0.5
    lw = jnp.exp(w_log) * (-LOG2E)
    a = _sigmoid(_lora(alo, a2_ref) + a0_ref[...])

    kk = k * kk_ref[...]
    kk = kk * lax.rsqrt(_group64_sum(kk * kk) + 1e-12)
    bonus_out[0] = (_group64_sum(r * k * rk_ref[...]) * v).astype(bonus_out.dtype)
    v_out[0] = v.astype(BF16)

    for d in range(2):
        rh_o, kh_o, bh_o, kap_o, kg_o, bg_o, gl_o = dir_outs[7 * d:7 * d + 7]
        lw_d = lw[:, d * w:(d + 1) * w]
        a_d = a[:, d * w:(d + 1) * w]
        kd = k * (1.0 + (a_d - 1.0) * ka_ref[...])
        b_d = kk * a_d
        sums = _mm(cm_ref[d], _split3(lw_d))
        g = sums[:TOK_BLOCK]
        g_all = sums[TOK_BLOCK:]
        e_neg = jnp.exp2(-g)
        e_rest = jnp.exp2(g_all - g)
        rh_o[0] = (r * jnp.exp2(g)).astype(BF16)
        kh_o[0] = (kd * e_neg).astype(BF16)
        bh_o[0] = (b_d * e_neg).astype(BF16)
        kap_o[0] = (kk * jnp.exp2(g - lw_d)).astype(BF16)
        kg_o[0] = (kd * e_rest).astype(BF16)
        bg_o[0] = (b_d * e_rest).astype(BF16)
        e_all = jnp.exp2(g_all)
        for c in range(TOK_BLOCK // CHUNK):
            gl_o[0, c] = e_all[c * CHUNK:c * CHUNK + 1, :]


def _rwkv_sum_consts(reverse):
    t = np.arange(TOK_BLOCK)[:, None]
    s = np.arange(TOK_BLOCK)[None, :]
    same = (t // CHUNK) == (s // CHUNK)
    order = (s >= t) if reverse else (s <= t)
    both = np.concatenate([same & order, same], axis=0).astype(np.float32)
    return np.concatenate([both] * 3, axis=1)


def _lora_pieces(m):
    hi = m.astype(BF16)
    lo = (m - hi.astype(F32)).astype(BF16)
    return jnp.concatenate([hi, lo, hi], axis=0)


def _rwkv_prep(p_rwkv, mu_prev, mu_next, w0, w2, a0, a2, k_k, k_a, r_k):
    b, lt, sw = p_rwkv.shape
    w = k_k.shape[-1]
    nblk = lt // TOK_BLOCK
    nch = lt // CHUNK
    halo = TOK_BLOCK // 8
    zl = jnp.zeros((RWKV_LORA, w), F32)
    w2cat = _lora_pieces(jnp.concatenate([jnp.concatenate([w2[0], zl], 1), jnp.concatenate([zl, w2[1]], 1)], 0))
    a2cat = _lora_pieces(jnp.concatenate([jnp.concatenate([a2[0], zl], 1), jnp.concatenate([zl, a2[1]], 1)], 0))
    cm = jnp.asarray(np.stack([_rwkv_sum_consts(False), _rwkv_sum_consts(True)]), dtype=BF16)
    row = lambda u: u.reshape(1, -1)
    full = lambda shape: pl.BlockSpec(shape, lambda bb, i: (0,) * len(shape))
    tok = lambda n, dt: (pl.BlockSpec((1, TOK_BLOCK, n), lambda bb, i: (bb, i, 0)), jax.ShapeDtypeStruct((b, lt, n), dt))
    gl = (pl.BlockSpec((1, TOK_BLOCK // CHUNK, 1, w), lambda bb, i: (bb, i, 0, 0)),
          jax.ShapeDtypeStruct((b, nch, 1, w), F32))
    outs = [tok(w, BF16), tok(w, BF16)] + 2 * ([tok(w, BF16)] * 6 + [gl])
    res = pl.pallas_call(
        functools.partial(_rwkv_prep_kernel, nblk),
        grid=(b, nblk),
        in_specs=[
            pl.BlockSpec((1, TOK_BLOCK, sw), lambda bb, i: (bb, i, 0)),
            pl.BlockSpec((1, 8, sw), lambda bb, i: (bb, jnp.maximum(i * halo - 1, 0), 0)),
            pl.BlockSpec((1, 8, sw), lambda bb, i: (bb, jnp.minimum((i + 1) * halo, lt // 8 - 1), 0)),
            full((1, sw)), full((1, sw)), full((1, 2 * w)), full(w2cat.shape),
            full((1, 2 * w)), full(a2cat.shape), full((1, w)), full((1, w)), full((1, w)),
            full(cm.shape),
        ],
        out_specs=[o[0] for o in outs],
        out_shape=[o[1] for o in outs],
        compiler_params=_cparams("arbitrary", "arbitrary"),
        name="rwkv_prep",
    )(p_rwkv, p_rwkv, p_rwkv, row(mu_prev), row(mu_next), row(w0), w2cat, row(a0), a2cat,
      row(k_k), row(k_a), row(r_k), cm)
    v, bonus = res[0], res[1]
    return v, bonus, res[2:9], res[9:16]


def _rwkv_masks(reverse):
    n = GROUP_HEADS * CHUNK
    i = np.arange(n)[:, None]
    j = np.arange(n)[None, :]
    same_head = (i // CHUNK) == (j // CHUNK)
    before = (j > i) if reverse else (j < i)
    strict = same_head & before
    incl = same_head & (before | (i == j))
    levels = []
    size = 1
    while size < CHUNK:
        levels.append(same_head & ((i // (2 * size)) == (j // (2 * size))) & ((i // size) != (j // size)))
        size *= 2
    head_lane = (i // CHUNK) == (j // RWKV_HEAD)
    tri = np.stack([strict, incl]).astype(np.float32)
    return tri, np.stack(levels).astype(np.float32), head_lane.astype(np.float32)


def _rwkv_scan_kernel(rh_ref, kh_ref, bh_ref, kap_ref, kg_ref, bg_ref, v_ref, gl_ref,
                      tri_ref, lvl_ref, hm_ref, y_ref, s_ref):
    @pl.when(pl.program_id(1) == 0)
    def _():
        s_ref[...] = jnp.zeros_like(s_ref)

    n = GROUP_HEADS * CHUNK
    hm = hm_ref[...]
    strict = tri_ref[0]
    incl = tri_ref[1]
    eye = incl - strict
    n_levels = lvl_ref.shape[0]

    def stack(ref, g):
        x = ref[0, :, g * MXU_DIM:(g + 1) * MXU_DIM]
        return jnp.concatenate([x] * GROUP_HEADS, axis=0) * hm

    groups = range(y_ref.shape[-1] // MXU_DIM)
    kap = [stack(kap_ref, g) for g in groups]
    bh = [stack(bh_ref, g) for g in groups]
    kh = [stack(kh_ref, g) for g in groups]
    rh = [stack(rh_ref, g) for g in groups]
    vs = [stack(v_ref, g) for g in groups]
    a_ab = [_nt(kap[g], bh[g]) * strict for g in groups]
    a_ak = [(_nt(kap[g], kh[g]) * strict).astype(BF16) for g in groups]
    p_rb = [(_nt(rh[g], bh[g]) * incl).astype(BF16) for g in groups]
    p_rk = [(_nt(rh[g], kh[g]) * incl).astype(BF16) for g in groups]
    t = [eye - a_ab[g] * lvl_ref[0] for g in groups]
    for lv in range(1, n_levels):
        tb = [t[g].astype(BF16) for g in groups]
        x = [_mm(tb[g], (a_ab[g] * lvl_ref[lv]).astype(BF16)).astype(BF16) for g in groups]
        t = [t[g] - _mm(x[g], tb[g]) for g in groups]
    s0 = [s_ref[g] for g in groups]
    s0b = [s0[g].astype(BF16) for g in groups]
    lanes = lambda ref, g: ref[0, :, g * MXU_DIM:(g + 1) * MXU_DIM]
    hm32 = hm.astype(F32)
    ks0 = [jnp.concatenate([_nt(lanes(kap_ref, g), s0b[g])] * GROUP_HEADS, axis=0) * hm32 for g in groups]
    wmat = [(ks0[g] + _mm(a_ak[g], vs[g])).astype(BF16) for g in groups]
    ub = [(-_mm(t[g].astype(BF16), wmat[g])).astype(BF16) for g in groups]
    ys = [_mm(p_rb[g], ub[g]) + _mm(p_rk[g], vs[g]) for g in groups]
    for g in groups:
        y = _nt(lanes(rh_ref, g), s0b[g])
        for h in range(GROUP_HEADS):
            y = y + ys[g][h * CHUNK:(h + 1) * CHUNK]
        y_ref[0, :, g * MXU_DIM:(g + 1) * MXU_DIM] = y.astype(y_ref.dtype)
        gam = gl_ref[0, 0][:, g * MXU_DIM:(g + 1) * MXU_DIM]
        s_ref[g] = s0[g] * gam + _tn(ub[g], stack(bg_ref, g)) + _tn(vs[g], stack(kg_ref, g))


def _scan_chunk_index(c, nch, reverse):
    if reverse:
        return nch - 1 - c
    n_ctx = TOK_BLOCK // CHUNK
    return jnp.where(c < n_ctx, nch - n_ctx + c, c - n_ctx)


def _rwkv_scan(dir_ops, v, reverse):
    rh, kh, bh, kap, kg, bg, gl = dir_ops
    b, lt, w = v.shape
    nch = lt // CHUNK
    n = GROUP_HEADS * CHUNK
    tri, lvl, hm = _rwkv_masks(reverse)
    cidx = lambda c: _scan_chunk_index(c, nch, reverse)
    tok = pl.BlockSpec((1, CHUNK, w), lambda bb, c: (bb, cidx(c), 0))
    full = lambda shape: pl.BlockSpec(shape, lambda bb, c: (0,) * len(shape))
    return pl.pallas_call(
        _rwkv_scan_kernel,
        grid=(b, nch),
        in_specs=[tok] * 7 + [
            pl.BlockSpec((1, 1, 1, w), lambda bb, c: (bb, cidx(c), 0, 0)),
            full(tri.shape), full(lvl.shape), full(hm.shape),
        ],
        out_specs=tok,
        out_shape=jax.ShapeDtypeStruct((b, lt, w), BF16),
        scratch_shapes=[pltpu.VMEM((w // MXU_DIM, n, n), F32)],
        compiler_params=_cparams("arbitrary", "arbitrary"),
        name="rwkv_scan_rev" if reverse else "rwkv_scan_fwd",
    )(rh, kh, bh, kap, kg, bg, v, gl, jnp.asarray(tri), jnp.asarray(lvl), jnp.asarray(hm, dtype=BF16))


def _hgrn_consts(reverse):
    t = np.arange(TOK_BLOCK)[:, None]
    s = np.arange(TOK_BLOCK)[None, :]
    same = (t // CHUNK) == (s // CHUNK)
    order = (s >= t) if reverse else (s <= t)
    earlier_sub = ((s // SUB) > (t // SUB)) if reverse else ((s // SUB) < (t // SUB))
    both = np.concatenate([same & order, same & earlier_sub], axis=0).astype(np.float32)
    return np.concatenate([both] * 3, axis=1)


def _split3(x):
    x1 = x.astype(BF16)
    r1 = x - x1.astype(F32)
    x2 = r1.astype(BF16)
    x3 = (r1 - x2.astype(F32)).astype(BF16)
    return jnp.concatenate([x1, x2, x3], axis=0)


def _hgrn_scan_kernel(reverse, q_ref, f_ref, i_ref, lb_ref, cm_ref, o_ref, s_ref, g_scr, gb_scr, q_scr, k_scr):
    @pl.when(pl.program_id(1) == 0)
    def _():
        s_ref[...] = jnp.zeros_like(s_ref)

    n_sub = CHUNK // SUB
    n_heads = q_ref.shape[-1] // HGRN_HEAD
    n_chunks = TOK_BLOCK // CHUNK
    sub_order = list(range(n_sub))[::-1] if reverse else list(range(n_sub))
    row = lax.broadcasted_iota(jnp.int32, (CHUNK, 1), 0)
    row_sub = row // SUB
    srow = lax.broadcasted_iota(jnp.int32, (SUB, 1), 0)
    lane = lax.broadcasted_iota(jnp.int32, (1, LANES), 1)
    last_row = 0 if reverse else CHUNK - 1

    kgate = (1.0 - lb_ref[...]) * _sigmoid(-f_ref[0])
    sums = _mm(cm_ref[...], _split3(jnp.log1p(-kgate) * LOG2E))
    g_scr[...] = sums[:TOK_BLOCK]
    gb_scr[...] = sums[TOK_BLOCK:]
    q_scr[...] = _silu(q_ref[0])
    k_scr[...] = kgate

    for ci in range(n_chunks):
        c = n_chunks - 1 - ci if reverse else ci
        rows = pl.ds(c * CHUNK, CHUNK)

        def heads(hp, carry):
            par = range(HGRN_PAR)
            hs = [hp * HGRN_PAR + j for j in par]
            cols = [pl.ds(pl.multiple_of(h * HGRN_HEAD, HGRN_HEAD), HGRN_HEAD) for h in hs]
            g = [g_scr[rows, cs] for cs in cols]
            gb = [gb_scr[rows, cs] for cs in cols]
            q = [q_scr[rows, cs] for cs in cols]
            k = [k_scr[rows, cs] for cs in cols]
            vb = [i_ref[0, rows, cs].astype(BF16) for cs in cols]
            s0 = [s_ref[h] for h in hs]
            q_in = [q[j] * jnp.exp2(g[j] - gb[j]) for j in par]
            at = []
            for j in par:
                kts, qts = [], []
                for si in sub_order[1:]:
                    gb_i = gb[j][si * SUB:si * SUB + 1, :]
                    earlier = (row_sub > si) if reverse else (row_sub < si)
                    kts.append(jnp.where(earlier, k[j] * jnp.exp2(jnp.minimum(gb_i - g[j], 0.0)), 0.0).astype(BF16))
                    qts.append(jnp.where(row_sub == si, q_in[j], 0.0).astype(BF16))
                at.append(_nt(jnp.concatenate(kts, axis=-1), jnp.concatenate(qts, axis=-1)))
            diag = [[] for _ in par]
            for si in range(n_sub):
                blk = slice(si * SUB, (si + 1) * SUB)
                acc = [jnp.zeros((SUB, LANES), F32) for _ in par]
                for tt in range(SUB):
                    t = si * SUB + tt
                    valid = jnp.logical_and(lane == t, (srow >= tt) if reverse else (srow <= tt))
                    trow = pl.ds(c * CHUNK + t, 1)
                    for j in par:
                        e = jnp.exp2(jnp.minimum(g_scr[trow, cols[j]] - g[j][blk], 0.0))
                        col = jnp.sum(k[j][blk] * e * q_scr[trow, cols[j]], axis=-1, keepdims=True)
                        acc[j] = jnp.where(valid, col, acc[j])
                for j in par:
                    diag[j].append(acc[j])
            for j in par:
                a = (at[j] + jnp.concatenate(diag[j], axis=0)[:, :CHUNK]).astype(BF16)
                o = _tn(a, vb[j]) + _nt((q[j] * jnp.exp2(g[j])).astype(BF16), s0[j].astype(BF16))
                o_ref[0, rows, cols[j]] = o.astype(o_ref.dtype)
                g_last = g[j][last_row:last_row + 1, :]
                k_out = (k[j] * jnp.exp2(g_last - g[j])).astype(BF16)
                s_ref[hs[j]] = s0[j] * jnp.exp2(g_last) + _tn(vb[j], k_out)
            return carry

        lax.fori_loop(0, n_heads // HGRN_PAR, heads, 0)


def _hgrn_scan(p_hgrn, lb_d, d, reverse):
    b, lt, w4 = p_hgrn.shape
    w = w4 // 4
    nblk = lt // TOK_BLOCK
    bidx = (lambda i: nblk - 1 - i) if reverse else (lambda i: jnp.where(i == 0, nblk - 1, i - 1))
    col = lambda j: pl.BlockSpec((1, TOK_BLOCK, w), lambda bb, i: (bb, bidx(i), j))
    cm = jnp.asarray(_hgrn_consts(reverse), dtype=BF16)
    return pl.pallas_call(
        functools.partial(_hgrn_scan_kernel, reverse),
        grid=(b, nblk),
        in_specs=[col(0), col(1 + d), col(3),
                  pl.BlockSpec((1, w), lambda bb, i: (0, 0)),
                  pl.BlockSpec(cm.shape, lambda bb, i: (0, 0))],
        out_specs=col(0),
        out_shape=jax.ShapeDtypeStruct((b, lt, w), BF16),
        scratch_shapes=[pltpu.VMEM((w // HGRN_HEAD, HGRN_HEAD, HGRN_HEAD), F32)]
        + [pltpu.VMEM((TOK_BLOCK, w), F32)] * 4,
        compiler_params=_cparams("arbitrary", "arbitrary"),
        name="hgrn_scan_rev" if reverse else "hgrn_scan_fwd",
    )(p_hgrn, p_hgrn, p_hgrn, lb_d.reshape(1, w), cm)


def _residual_ln(alpha, x, gm, proj, lng, lnb):
    z = alpha * x + gm * proj
    mu = jnp.mean(z, axis=-1, keepdims=True)
    zc = z - mu
    var = jnp.mean(zc * zc, axis=-1, keepdims=True)
    return zc * lax.rsqrt(var + LN_EPS) * lng + lnb


def _even_out_kernel(alpha, ya0_ref, ya1_ref, bonus_ref, yb0_ref, yb1_ref, gate_ref, x_ref, gm_ref, w_ref,
                     lnxg_ref, lnxb_ref, ng_ref, lng_ref, lnb_ref, o_ref):
    y = ya0_ref[0].astype(F32) + ya1_ref[0].astype(F32)
    mu = _group64_sum(y) * (1.0 / RWKV_HEAD)
    yc = y - mu
    var = _group64_sum(yc * yc) * (1.0 / RWKV_HEAD)
    ya = yc * lax.rsqrt(var + RWKV_GN_EPS) * lnxg_ref[...] + lnxb_ref[...] + bonus_ref[0].astype(F32)
    o = yb0_ref[0].astype(F32) + yb1_ref[0].astype(F32)
    yb = o * lax.rsqrt(_group128_mean(o * o) + LN_EPS) * ng_ref[...]
    ycat = jnp.concatenate([ya, yb], axis=-1) * _silu(gate_ref[0].astype(F32))
    proj = _mm(ycat.astype(BF16), w_ref[...])
    o_ref[0] = _residual_ln(alpha, x_ref[0], gm_ref[0, 0], proj, lng_ref[...], lnb_ref[...])


def _odd_out_kernel(alpha, y_ref, x_ref, gm_ref, w_ref, lng_ref, lnb_ref, o_ref):
    proj = _mm(y_ref[0].astype(BF16), w_ref[...])
    o_ref[0] = _residual_ln(alpha, x_ref[0], gm_ref[0, 0], proj, lng_ref[...], lnb_ref[...])


def _tok_spec(n):
    return pl.BlockSpec((1, TOK_BLOCK, n), lambda bb, i: (bb, i, 0))


def _full2(shape):
    return pl.BlockSpec(shape, lambda bb, i: (0,) * len(shape))


def _seg_spec(d, nblk):
    return pl.BlockSpec((1, 1, 1, d), lambda bb, i: (bb, _seg(i, nblk), 0, 0))


def _even_out(alpha, ya0, ya1, bonus, yb0, yb1, gate, xs, gmod, w_out, lnx_g, lnx_b, norm_g, ln_g, ln_b):
    b, lt, d = xs.shape
    w = ya0.shape[-1]
    di = gate.shape[-1]
    row = lambda u: u.reshape(1, -1)
    return pl.pallas_call(
        functools.partial(_even_out_kernel, alpha),
        grid=(b, lt // TOK_BLOCK),
        in_specs=[_tok_spec(w)] * 5 + [_tok_spec(di), _tok_spec(d), _seg_spec(d, lt // TOK_BLOCK), _full2((di, d)),
                                       _full2((1, w)), _full2((1, w)), _full2((1, w)), _full2((1, d)), _full2((1, d))],
        out_specs=_tok_spec(d),
        out_shape=jax.ShapeDtypeStruct((b, lt, d), F32),
        compiler_params=_cparams("arbitrary", "arbitrary"),
        name="even_out",
    )(ya0, ya1, bonus, yb0, yb1, gate, xs, gmod, w_out, row(lnx_g), row(lnx_b),
      row(jnp.tile(norm_g, w // HGRN_HEAD)), row(ln_g), row(ln_b))


def _odd_out(alpha, y, xs, gmod, w_out, ln_g, ln_b, rows_out):
    b, lt, d = xs.shape
    di = y.shape[-1]
    row = lambda u: u.reshape(1, -1)
    return pl.pallas_call(
        functools.partial(_odd_out_kernel, alpha),
        grid=(b, rows_out // TOK_BLOCK),
        in_specs=[_tok_spec(di), _tok_spec(d), _seg_spec(d, lt // TOK_BLOCK), _full2((di, d)), _full2((1, d)),
                  _full2((1, d))],
        out_specs=_tok_spec(d),
        out_shape=jax.ShapeDtypeStruct((b, rows_out, d), F32),
        compiler_params=_cparams("arbitrary", "arbitrary"),
        name="odd_out",
    )(y, xs, gmod, w_out, row(ln_g), row(ln_b))


def _rope(x, cos, sin):
    width = x.shape[-1]
    lane = lax.broadcasted_iota(jnp.int32, (1, width), 1)
    first = (lane % 32) < 16
    partner = jnp.where(first, pltpu.roll(x, width - 16, 1), pltpu.roll(x, 16, 1))
    return x * cos + partner * sin


def _attn_kernel(lam_init, tk, n_full, tail, lam_ref, q_ref, k_ref, vt_ref, g_ref, sg_ref, *rest):
    y_ref, m_scr, acc_scr = rest[-3:]
    tq = q_ref.shape[1]
    ones_rows = 16
    tile = min(ATTN_TILE, 2 * tq)
    n_tiles = 2 * tq // tile
    q = q_ref[0].astype(F32) * (DIFF_HEAD ** -0.5 * float(np.log2(np.e)))
    lane = lax.broadcasted_iota(jnp.int32, (1, LANES), 1)
    q_cat = jnp.concatenate([jnp.where(lane < DIFF_HEAD, q, 0.0), jnp.where(lane >= DIFF_HEAD, q, 0.0)],
                            axis=0).astype(BF16)
    m_scr[...] = jnp.full(m_scr.shape, -jnp.inf, F32)
    acc_scr[...] = jnp.zeros(acc_scr.shape, F32)

    def scores(item, size):
        start, c = item
        return _nt(k_ref[0, pl.ds(start, size), :], q_cat[c * tile:(c + 1) * tile])

    def run(starts, size):
        cols = [slice(c * tile, (c + 1) * tile) for c in range(n_tiles)]
        m_run = [m_scr[:, cs] for cs in cols]
        acc = [acc_scr[:, cs] for cs in cols]
        vta = None
        pending = None
        items = [(start, c) for start in starts for c in range(n_tiles)]
        st_next = scores(items[0], size)
        for i, (start, c) in enumerate(items):
            st = st_next
            if i + 1 < len(items):
                st_next = scores(items[i + 1], size)
            if pending is not None:
                pc, pcorr, pvta, ppt = pending
                acc[pc] = acc[pc] * pcorr + _mm(pvta, ppt)
            if c == 0:
                vta = jnp.concatenate([vt_ref[0, :, pl.ds(start, size)], jnp.ones((ones_rows, size), BF16)], axis=0)
            m_new = jnp.maximum(m_run[c], jnp.max(st, axis=0, keepdims=True))
            pending = (c, jnp.exp2(m_run[c] - m_new), vta, jnp.exp2(st - m_new).astype(BF16))
            m_run[c] = m_new
        pc, pcorr, pvta, ppt = pending
        acc[pc] = acc[pc] * pcorr + _mm(pvta, ppt)
        for c, cs in enumerate(cols):
            m_scr[:, cs] = m_run[c]
            acc_scr[:, cs] = acc[c]

    if n_full:
        unroll = ATTN_UNROLL if n_full % ATTN_UNROLL == 0 else 1
        def body(j, carry):
            run([pl.multiple_of((j * unroll + u) * tk, tk) for u in range(unroll)], tk)
            return carry
        lax.fori_loop(0, n_full // unroll, body, 0)
    run([n_full * tk], tail)

    a0 = acc_scr[:, :tq]
    a1 = acc_scr[:, tq:]
    o_t = a0[:LANES] / a0[LANES:LANES + 1] - lam_ref[0] * (a1[:LANES] / a1[LANES:LANES + 1])
    o = jnp.transpose(o_t)
    y = o * lax.rsqrt(jnp.mean(o * o, axis=-1, keepdims=True) + LN_EPS) * sg_ref[...] * (1.0 - lam_init)
    y_ref[0] = (y * _silu(g_ref[0].astype(F32))).astype(y_ref.dtype)


def _diff_attention(p_qk, p_vt, p_g, lam, subln_g, lam_init):
    b, lt, di = p_g.shape
    nh = di // LANES
    t = lt - TOK_BLOCK
    tq = tk = 512
    assert t % tq == 0
    ctx_blk = t // TOK_BLOCK
    sm = pl.BlockSpec(memory_space=pltpu.SMEM)
    sg_spec = pl.BlockSpec((1, LANES), lambda bb, h, i: (0, 0))
    scratch = lambda n: [pltpu.VMEM((1, 2 * n), F32), pltpu.VMEM((LANES + 16, 2 * n), F32)]
    args = (lam.reshape(1), p_qk, p_qk, p_vt, p_g, subln_g.reshape(1, LANES))
    y = pl.pallas_call(
        functools.partial(_attn_kernel, lam_init, tk, t // tk, TOK_BLOCK),
        grid=(b, nh, t // tq),
        in_specs=[sm,
                  pl.BlockSpec((1, tq, LANES), lambda bb, h, i: (bb, i, h)),
                  pl.BlockSpec((1, lt, LANES), lambda bb, h, i: (bb, 0, nh + h)),
                  pl.BlockSpec((1, LANES, lt), lambda bb, h, i: (bb, h, 0)),
                  pl.BlockSpec((1, tq, LANES), lambda bb, h, i: (bb, i, h)),
                  sg_spec],
        out_specs=pl.BlockSpec((1, tq, LANES), lambda bb, h, i: (bb, i, h)),
        out_shape=jax.ShapeDtypeStruct((b, lt, di), BF16),
        scratch_shapes=scratch(tq),
        compiler_params=_cparams("arbitrary", "arbitrary", "arbitrary"),
        name="diff_attn",
    )(*args)
    blk = lambda bb, h, i: (bb, ctx_blk, h)
    return pl.pallas_call(
        functools.partial(_attn_kernel, lam_init, tk, 0, TOK_BLOCK),
        grid=(b, nh, 1),
        in_specs=[sm,
                  pl.BlockSpec((1, TOK_BLOCK, LANES), blk),
                  pl.BlockSpec((1, TOK_BLOCK, LANES), lambda bb, h, i: (bb, ctx_blk, nh + h)),
                  pl.BlockSpec((1, LANES, TOK_BLOCK), lambda bb, h, i: (bb, h, ctx_blk)),
                  pl.BlockSpec((1, TOK_BLOCK, LANES), blk),
                  sg_spec,
                  pl.BlockSpec(memory_space=pl.ANY)],
        out_specs=pl.BlockSpec((1, TOK_BLOCK, LANES), blk),
        out_shape=jax.ShapeDtypeStruct((b, lt, di), BF16),
        scratch_shapes=scratch(TOK_BLOCK),
        input_output_aliases={6: 0},
        compiler_params=_cparams("arbitrary", "arbitrary", "arbitrary"),
        name="diff_attn_ctx",
    )(*args, y)


def _rope_tables(n_ctx, t):
    quarter = DIFF_HEAD // 4
    inv = ROPE_BASE ** (-jnp.arange(quarter, dtype=F32) / quarter)
    pos = jnp.arange(t)
    rows = (pos // GRID_W).astype(F32)[:, None] * inv
    cols = (pos % GRID_W).astype(F32)[:, None] * inv
    cos64 = jnp.concatenate([jnp.cos(rows), jnp.cos(rows), jnp.cos(cols), jnp.cos(cols)], -1)
    sin64 = jnp.concatenate([-jnp.sin(rows), jnp.sin(rows), -jnp.sin(cols), jnp.sin(cols)], -1)
    cos = jnp.concatenate([jnp.tile(cos64, (1, 2)), jnp.ones((n_ctx, LANES), F32)], 0)
    sin = jnp.concatenate([jnp.tile(sin64, (1, 2)), jnp.zeros((n_ctx, LANES), F32)], 0)
    return cos, sin


def _even_layer(alpha, xs, scale, shift, gmod, w_in, w_out, mu_prev, mu_next, w0, w2, a0, a2, k_k, k_a, r_k,
                lnx_g, lnx_b, lb, norm_g, ln_g, ln_b):
    w = k_k.shape[-1]
    sw = mu_prev.shape[-1]
    hw = 4 * lb.shape[-1]
    w_in = w_in.astype(BF16)
    p_rwkv = _project(xs, scale, shift, w_in[:, :sw], sw, F32)
    p_hgrn = _project(xs, scale, shift, w_in[:, sw:sw + hw], hw // 2, F32)
    gate = _project(xs, scale, shift, w_in[:, sw + hw:], w_in.shape[1] - sw - hw, BF16)
    v, bonus, ops_f, ops_r = _rwkv_prep(p_rwkv, mu_prev, mu_next, w0, w2, a0, a2, k_k, k_a, r_k)
    ya0 = _rwkv_scan(ops_f, v, False)
    ya1 = _rwkv_scan(ops_r, v, True)
    yb0 = _hgrn_scan(p_hgrn, lb[0], 0, False)
    yb1 = _hgrn_scan(p_hgrn, lb[1], 1, True)
    return _even_out(alpha, ya0, ya1, bonus, yb0, yb1, gate, xs, gmod, w_out.astype(BF16),
                     lnx_g, lnx_b, norm_g, ln_g, ln_b)


def _odd_layer(alpha, xs, scale, shift, gmod, w_in, w_out, lam_p, subln_g, lam_init, cos, sin, ln_g, ln_b,
               rows_out):
    di = w_out.shape[0]
    w_in = w_in.astype(BF16)
    p_qk = _project(xs, scale, shift, w_in[:, :2 * di], di, BF16, rope=(cos, sin))
    p_vt = _project_t(xs, scale, shift, jnp.transpose(w_in[:, 2 * di:3 * di]), BF16)
    p_g = _project(xs, scale, shift, w_in[:, 3 * di:], di, BF16)
    lam = jnp.exp(jnp.sum(lam_p[0] * lam_p[1])) - jnp.exp(jnp.sum(lam_p[2] * lam_p[3])) + lam_init
    y = _diff_attention(p_qk, p_vt, p_g, lam, subln_g, lam_init)
    return _odd_out(alpha, y, xs, gmod, w_out.astype(BF16), ln_g, ln_b, rows_out)


def kernel(x, c, ctx, c_ctx, ada_w, ada_b, ln_g, ln_b, even_w_in, even_w_out, rwkv_mu_prev, rwkv_mu_next, rwkv_w0, rwkv_w2, rwkv_a0, rwkv_a2, rwkv_k_k, rwkv_k_a, rwkv_r_k, rwkv_lnx_g, rwkv_lnx_b, hgrn_lb_logits, hgrn_norm_g, odd_w_in, odd_w_out, diff_lambda, diff_subln_g):
    b, t, d = x.shape
    n_ctx = ctx.shape[1]
    depth = ada_w.shape[0]
    assert n_ctx == TOK_BLOCK and t % TOK_BLOCK == 0 and b + 1 <= 8
    alpha = (2.0 * depth) ** 0.25

    xs = jnp.concatenate([x, ctx], axis=1)
    cvec = jnp.concatenate([c, c_ctx[None], jnp.zeros((8 - b - 1, d), F32)], axis=0)
    mods = _ada_mods(cvec, ada_w, ada_b)
    lb_all = jax.nn.softmax(hgrn_lb_logits.astype(F32), axis=0)
    lb_all = jnp.cumsum(lb_all, axis=0) - lb_all[0]
    cos, sin = _rope_tables(n_ctx, t)

    for layer in range(depth):
        m = mods[layer]
        per_seg = lambda u: jnp.stack([jnp.broadcast_to(u[b], (b, d)), u[:b]], axis=1)[:, :, None, :]
        shift, scale, gmod = (per_seg(m[:, j * d:(j + 1) * d]) for j in range(3))
        j = layer // 2
        if layer % 2 == 0:
            xs = _even_layer(alpha, xs, scale, shift, gmod, even_w_in[j], even_w_out[j], rwkv_mu_prev[j],
                             rwkv_mu_next[j], rwkv_w0[j], rwkv_w2[j], rwkv_a0[j], rwkv_a2[j], rwkv_k_k[j],
                             rwkv_k_a[j], rwkv_r_k[j], rwkv_lnx_g[j], rwkv_lnx_b[j], lb_all[j], hgrn_norm_g[j],
                             ln_g[layer], ln_b[layer])
        else:
            lam_init = 0.8 - 0.6 * float(np.exp(-0.3 * layer))
            xs = _odd_layer(alpha, xs, scale, shift, gmod, odd_w_in[j], odd_w_out[j], diff_lambda[j],
                            diff_subln_g[j], lam_init, cos, sin, ln_g[layer], ln_b[layer],
                            t if layer == depth - 1 else t + n_ctx)
    return xs[:, :t]
```

```python
import functools

import numpy as np
import jax
import jax.numpy as jnp
from jax import lax
from jax.experimental import pallas as pl
from jax.experimental.pallas import tpu as pltpu

F32 = jnp.float32
BF16 = jnp.bfloat16
HI = lax.Precision.HIGHEST

GRID_W = 64
RWKV_HEAD = 64
RWKV_LORA = 64
RWKV_GN_EPS = 64e-5
HGRN_HEAD = 128
DIFF_HEAD = 64
ROPE_BASE = 10000.0
LN_EPS = 1e-5
LOG2E = float(np.log2(np.e))

LANES = 128
MXU_DIM = 256
VMEM_LIMIT = 56 * 1024 * 1024

CHUNK = 64
SUB = 16
TOK_BLOCK = 256
GROUP_HEADS = MXU_DIM // RWKV_HEAD
HGRN_PAR = 4
ATTN_UNROLL = 16
ATTN_TILE = 1024


def _cparams(*sem):
    return pltpu.CompilerParams(dimension_semantics=sem, vmem_limit_bytes=VMEM_LIMIT)


def _nt(a, b):
    return lax.dot_general(a, b, (((1,), (1,)), ((), ())), preferred_element_type=F32)


def _tn(a, b):
    return lax.dot_general(a, b, (((0,), (0,)), ((), ())), preferred_element_type=F32)


def _mm(a, b):
    return jnp.dot(a, b, preferred_element_type=F32)


def _mm_exact(a, b):
    return jnp.dot(a, b, preferred_element_type=F32, precision=HI)


def _sigmoid(x):
    return 1.0 / (1.0 + jnp.exp(-x))


def _silu(x):
    return x * _sigmoid(x)


def _lane_tiles(x):
    return [x[:, j * LANES:(j + 1) * LANES] for j in range(x.shape[-1] // LANES)]


def _group64_sum(x):
    lane = lax.broadcasted_iota(jnp.int32, (1, LANES), 1)
    low = lane < RWKV_HEAD
    out = []
    for xt in _lane_tiles(x):
        s_all = jnp.sum(xt, axis=-1, keepdims=True)
        s_lo = jnp.sum(jnp.where(low, xt, 0.0), axis=-1, keepdims=True)
        out.append(jnp.where(low, s_lo, s_all - s_lo))
    return jnp.concatenate(out, axis=-1)


def _group128_mean(x):
    out = []
    for xt in _lane_tiles(x):
        out.append(jnp.broadcast_to(jnp.mean(xt, axis=-1, keepdims=True), xt.shape))
    return jnp.concatenate(out, axis=-1)


def _ada_kernel(c_ref, w_ref, b_ref, o_ref):
    cond = _silu(c_ref[...])
    o_ref[0] = _mm_exact(cond, w_ref[0]) + b_ref[0]


def _ada_mods(cvec, ada_w, ada_b):
    depth, d, d3 = ada_w.shape
    tn = 1024
    return pl.pallas_call(
        _ada_kernel,
        grid=(depth, d3 // tn),
        in_specs=[
            pl.BlockSpec((8, d), lambda l, j: (0, 0)),
            pl.BlockSpec((1, d, tn), lambda l, j: (l, 0, j)),
            pl.BlockSpec((1, 1, tn), lambda l, j: (l, 0, j)),
        ],
        out_specs=pl.BlockSpec((1, 8, tn), lambda l, j: (l, 0, j)),
        out_shape=jax.ShapeDtypeStruct((depth, 8, d3), F32),
        compiler_params=_cparams("arbitrary", "arbitrary"),
        name="ada_mods",
    )(cvec, ada_w, ada_b.reshape(depth, 1, d3))


def _seg(i, nblk):
    return jnp.where(i == nblk - 1, 0, 1)


def _modulate(x_ref, sc_ref, sh_ref):
    return (x_ref[0] * (1.0 + sc_ref[0, 0]) + sh_ref[0, 0]).astype(BF16)


def _proj_kernel(x_ref, sc_ref, sh_ref, w_ref, o_ref):
    o_ref[0] = _mm(_modulate(x_ref, sc_ref, sh_ref), w_ref[...]).astype(o_ref.dtype)


def _proj_rope_kernel(x_ref, sc_ref, sh_ref, w_ref, cos_ref, sin_ref, o_ref):
    p = _mm(_modulate(x_ref, sc_ref, sh_ref), w_ref[...])
    reps = p.shape[-1] // LANES
    cos = jnp.concatenate([cos_ref[...]] * reps, axis=-1)
    sin = jnp.concatenate([sin_ref[...]] * reps, axis=-1)
    o_ref[0] = _rope(p, cos, sin).astype(o_ref.dtype)


def _proj_t_kernel(x_ref, sc_ref, sh_ref, wt_ref, o_ref):
    o_ref[0] = _nt(wt_ref[...], _modulate(x_ref, sc_ref, sh_ref)).astype(o_ref.dtype)


def _project(xs, scale, shift, w, tn, out_dtype, rope=None):
    b, lt, d = xs.shape
    n = w.shape[1]
    nblk = lt // TOK_BLOCK
    in_specs = [
        pl.BlockSpec((1, TOK_BLOCK, d), lambda j, bb, i: (bb, i, 0)),
        pl.BlockSpec((1, 1, 1, d), lambda j, bb, i: (bb, _seg(i, nblk), 0, 0)),
        pl.BlockSpec((1, 1, 1, d), lambda j, bb, i: (bb, _seg(i, nblk), 0, 0)),
        pl.BlockSpec((d, tn), lambda j, bb, i: (0, j)),
    ]
    args = (xs, scale, shift, w)
    if rope is not None:
        in_specs += [pl.BlockSpec((TOK_BLOCK, LANES), lambda j, bb, i: (i, 0))] * 2
        args += tuple(rope)
    return pl.pallas_call(
        _proj_kernel if rope is None else _proj_rope_kernel,
        grid=(n // tn, b, nblk),
        in_specs=in_specs,
        out_specs=pl.BlockSpec((1, TOK_BLOCK, tn), lambda j, bb, i: (bb, i, j)),
        out_shape=jax.ShapeDtypeStruct((b, lt, n), out_dtype),
        compiler_params=_cparams("arbitrary", "arbitrary", "arbitrary"),
        name="mod_proj" if rope is None else "mod_proj_rope",
    )(*args)


def _project_t(xs, scale, shift, wt, out_dtype):
    b, lt, d = xs.shape
    n = wt.shape[0]
    nblk = lt // TOK_BLOCK
    return pl.pallas_call(
        _proj_t_kernel,
        grid=(b, nblk),
        in_specs=[
            pl.BlockSpec((1, TOK_BLOCK, d), lambda bb, i: (bb, i, 0)),
            pl.BlockSpec((1, 1, 1, d), lambda bb, i: (bb, _seg(i, nblk), 0, 0)),
            pl.BlockSpec((1, 1, 1, d), lambda bb, i: (bb, _seg(i, nblk), 0, 0)),
            pl.BlockSpec((n, d), lambda bb, i: (0, 0)),
        ],
        out_specs=pl.BlockSpec((1, n, TOK_BLOCK), lambda bb, i: (bb, 0, i)),
        out_shape=jax.ShapeDtypeStruct((b, n, lt), out_dtype),
        compiler_params=_cparams("arbitrary", "arbitrary"),
        name="mod_proj_t",
    )(xs, scale, shift, wt)


def _lora(x, w_ref):
    x1 = x.astype(BF16)
    x2 = (x - x1.astype(F32)).astype(BF16)
    return _mm(jnp.concatenate([x1, x1, x2], axis=-1), w_ref[...])


def _rwkv_prep_kernel(nblk, p_ref, prev_ref, next_ref, mup_ref, mun_ref, w0_ref, w2_ref, a0_ref, a2_ref,
                      kk_ref, ka_ref, rk_ref, cm_ref,
                      v_out, bonus_out, *dir_outs):
    i = pl.program_id(1)
    w = kk_ref.shape[-1]
    p = p_ref[0]
    row = lax.broadcasted_iota(jnp.int32, (TOK_BLOCK, 1), 0)
    prev_row = jnp.where(jnp.logical_and(i >= 1, i <= nblk - 2), prev_ref[0][7:8, :], 0.0)
    next_row = jnp.where(i <= nblk - 3, next_ref[0][0:1, :], 0.0)
    prev = jnp.where(row == 0, prev_row, pltpu.roll(p, 1, 0))
    nxt = jnp.where(row == TOK_BLOCK - 1, next_row, pltpu.roll(p, TOK_BLOCK - 1, 0))
    ps = p + mup_ref[...] * (prev - p) + mun_ref[...] * (nxt - p)

    r = ps[:, 0:w]
    k = ps[:, w:2 * w]
    v = ps[:, 2 * w:3 * w]
    wlo = ps[:, 3 * w:3 * w + 2 * RWKV_LORA]
    alo = ps[:, 3 * w + 2 * RWKV_LORA:3 * w + 4 * RWKV_LORA]

    z = _lora(jnp.tanh(wlo), w2_ref) + w0_ref[...]
    w_log = -(jnp.maximum(-z, 0.0) + jnp.log(1.0 + jnp.exp(-jnp.abs(z)))) - 0.5
    lw = jnp.exp(w_log) * (-LOG2E)
    a = _sigmoid(_lora(alo, a2_ref) + a0_ref[...])

    kk = k * kk_ref[...]
    kk = kk * lax.rsqrt(_group64_sum(kk * kk) + 1e-12)
    bonus_out[0] = (_group64_sum(r * k * rk_ref[...]) * v).astype(bonus_out.dtype)
    v_out[0] = v.astype(BF16)

    for d in range(2):
        rh_o, kh_o, bh_o, kap_o, kg_o, bg_o, gl_o = dir_outs[7 * d:7 * d + 7]
        lw_d = lw[:, d * w:(d + 1) * w]
        a_d = a[:, d * w:(d + 1) * w]
        kd = k * (1.0 + (a_d - 1.0) * ka_ref[...])
        b_d = kk * a_d
        sums = _mm(cm_ref[d], _split3(lw_d))
        g = sums[:TOK_BLOCK]
        g_all = sums[TOK_BLOCK:]
        e_neg = jnp.exp2(-g)
        e_rest = jnp.exp2(g_all - g)
        rh_o[0] = (r * jnp.exp2(g)).astype(BF16)
        kh_o[0] = (kd * e_neg).astype(BF16)
        bh_o[0] = (b_d * e_neg).astype(BF16)
        kap_o[0] = (kk * jnp.exp2(g - lw_d)).astype(BF16)
        kg_o[0] = (kd * e_rest).astype(BF16)
        bg_o[0] = (b_d * e_rest).astype(BF16)
        e_all = jnp.exp2(g_all)
        for c in range(TOK_BLOCK // CHUNK):
            gl_o[0, c] = e_all[c * CHUNK:c * CHUNK + 1, :]


def _rwkv_sum_consts(reverse):
    t = np.arange(TOK_BLOCK)[:, None]
    s = np.arange(TOK_BLOCK)[None, :]
    same = (t // CHUNK) == (s // CHUNK)
    order = (s >= t) if reverse else (s <= t)
    both = np.concatenate([same & order, same], axis=0).astype(np.float32)
    return np.concatenate([both] * 3, axis=1)


def _lora_pieces(m):
    hi = m.astype(BF16)
    lo = (m - hi.astype(F32)).astype(BF16)
    return jnp.concatenate([hi, lo, hi], axis=0)


def _rwkv_prep(p_rwkv, mu_prev, mu_next, w0, w2, a0, a2, k_k, k_a, r_k):
    b, lt, sw = p_rwkv.shape
    w = k_k.shape[-1]
    nblk = lt // TOK_BLOCK
    nch = lt // CHUNK
    halo = TOK_BLOCK // 8
    zl = jnp.zeros((RWKV_LORA, w), F32)
    w2cat = _lora_pieces(jnp.concatenate([jnp.concatenate([w2[0], zl], 1), jnp.concatenate([zl, w2[1]], 1)], 0))
    a2cat = _lora_pieces(jnp.concatenate([jnp.concatenate([a2[0], zl], 1), jnp.concatenate([zl, a2[1]], 1)], 0))
    cm = jnp.asarray(np.stack([_rwkv_sum_consts(False), _rwkv_sum_consts(True)]), dtype=BF16)
    row = lambda u: u.reshape(1, -1)
    full = lambda shape: pl.BlockSpec(shape, lambda bb, i: (0,) * len(shape))
    tok = lambda n, dt: (pl.BlockSpec((1, TOK_BLOCK, n), lambda bb, i: (bb, i, 0)), jax.ShapeDtypeStruct((b, lt, n), dt))
    gl = (pl.BlockSpec((1, TOK_BLOCK // CHUNK, 1, w), lambda bb, i: (bb, i, 0, 0)),
          jax.ShapeDtypeStruct((b, nch, 1, w), F32))
    outs = [tok(w, BF16), tok(w, BF16)] + 2 * ([tok(w, BF16)] * 6 + [gl])
    res = pl.pallas_call(
        functools.partial(_rwkv_prep_kernel, nblk),
        grid=(b, nblk),
        in_specs=[
            pl.BlockSpec((1, TOK_BLOCK, sw), lambda bb, i: (bb, i, 0)),
            pl.BlockSpec((1, 8, sw), lambda bb, i: (bb, jnp.maximum(i * halo - 1, 0), 0)),
            pl.BlockSpec((1, 8, sw), lambda bb, i: (bb, jnp.minimum((i + 1) * halo, lt // 8 - 1), 0)),
            full((1, sw)), full((1, sw)), full((1, 2 * w)), full(w2cat.shape),
            full((1, 2 * w)), full(a2cat.shape), full((1, w)), full((1, w)), full((1, w)),
            full(cm.shape),
        ],
        out_specs=[o[0] for o in outs],
        out_shape=[o[1] for o in outs],
        compiler_params=_cparams("arbitrary", "arbitrary"),
        name="rwkv_prep",
    )(p_rwkv, p_rwkv, p_rwkv, row(mu_prev), row(mu_next), row(w0), w2cat, row(a0), a2cat,
      row(k_k), row(k_a), row(r_k), cm)
    v, bonus = res[0], res[1]
    return v, bonus, res[2:9], res[9:16]


def _rwkv_masks(reverse):
    n = GROUP_HEADS * CHUNK
    i = np.arange(n)[:, None]
    j = np.arange(n)[None, :]
    same_head = (i // CHUNK) == (j // CHUNK)
    before = (j > i) if reverse else (j < i)
    strict = same_head & before
    incl = same_head & (before | (i == j))
    levels = []
    size = 1
    while size < CHUNK:
        levels.append(same_head & ((i // (2 * size)) == (j // (2 * size))) & ((i // size) != (j // size)))
        size *= 2
    head_lane = (i // CHUNK) == (j // RWKV_HEAD)
    tri = np.stack([strict, incl]).astype(np.float32)
    return tri, np.stack(levels).astype(np.float32), head_lane.astype(np.float32)


def _rwkv_scan_kernel(rh_ref, kh_ref, bh_ref, kap_ref, kg_ref, bg_ref, v_ref, gl_ref,
                      tri_ref, lvl_ref, hm_ref, y_ref, s_ref):
    @pl.when(pl.program_id(1) == 0)
    def _():
        s_ref[...] = jnp.zeros_like(s_ref)

    n = GROUP_HEADS * CHUNK
    hm = hm_ref[...]
    strict = tri_ref[0]
    incl = tri_ref[1]
    eye = incl - strict
    n_levels = lvl_ref.shape[0]

    def stack(ref, g):
        x = ref[0, :, g * MXU_DIM:(g + 1) * MXU_DIM]
        return jnp.concatenate([x] * GROUP_HEADS, axis=0) * hm

    groups = range(y_ref.shape[-1] // MXU_DIM)
    kap = [stack(kap_ref, g) for g in groups]
    bh = [stack(bh_ref, g) for g in groups]
    kh = [stack(kh_ref, g) for g in groups]
    rh = [stack(rh_ref, g) for g in groups]
    vs = [stack(v_ref, g) for g in groups]
    a_ab = [_nt(kap[g], bh[g]) * strict for g in groups]
    a_ak = [(_nt(kap[g], kh[g]) * strict).astype(BF16) for g in groups]
    p_rb = [(_nt(rh[g], bh[g]) * incl).astype(BF16) for g in groups]
    p_rk = [(_nt(rh[g], kh[g]) * incl).astype(BF16) for g in groups]
    t = [eye - a_ab[g] * lvl_ref[0] for g in groups]
    for lv in range(1, n_levels):
        tb = [t[g].astype(BF16) for g in groups]
        x = [_mm(tb[g], (a_ab[g] * lvl_ref[lv]).astype(BF16)).astype(BF16) for g in groups]
        t = [t[g] - _mm(x[g], tb[g]) for g in groups]
    s0 = [s_ref[g] for g in groups]
    s0b = [s0[g].astype(BF16) for g in groups]
    lanes = lambda ref, g: ref[0, :, g * MXU_DIM:(g + 1) * MXU_DIM]
    hm32 = hm.astype(F32)
    ks0 = [jnp.concatenate([_nt(lanes(kap_ref, g), s0b[g])] * GROUP_HEADS, axis=0) * hm32 for g in groups]
    wmat = [(ks0[g] + _mm(a_ak[g], vs[g])).astype(BF16) for g in groups]
    ub = [(-_mm(t[g].astype(BF16), wmat[g])).astype(BF16) for g in groups]
    ys = [_mm(p_rb[g], ub[g]) + _mm(p_rk[g], vs[g]) for g in groups]
    for g in groups:
        y = _nt(lanes(rh_ref, g), s0b[g])
        for h in range(GROUP_HEADS):
            y = y + ys[g][h * CHUNK:(h + 1) * CHUNK]
        y_ref[0, :, g * MXU_DIM:(g + 1) * MXU_DIM] = y.astype(y_ref.dtype)
        gam = gl_ref[0, 0][:, g * MXU_DIM:(g + 1) * MXU_DIM]
        s_ref[g] = s0[g] * gam + _tn(ub[g], stack(bg_ref, g)) + _tn(vs[g], stack(kg_ref, g))


def _scan_chunk_index(c, nch, reverse):
    if reverse:
        return nch - 1 - c
    n_ctx = TOK_BLOCK // CHUNK
    return jnp.where(c < n_ctx, nch - n_ctx + c, c - n_ctx)


def _rwkv_scan(dir_ops, v, reverse):
    rh, kh, bh, kap, kg, bg, gl = dir_ops
    b, lt, w = v.shape
    nch = lt // CHUNK
    n = GROUP_HEADS * CHUNK
    tri, lvl, hm = _rwkv_masks(reverse)
    cidx = lambda c: _scan_chunk_index(c, nch, reverse)
    tok = pl.BlockSpec((1, CHUNK, w), lambda bb, c: (bb, cidx(c), 0))
    full = lambda shape: pl.BlockSpec(shape, lambda bb, c: (0,) * len(shape))
    return pl.pallas_call(
        _rwkv_scan_kernel,
        grid=(b, nch),
        in_specs=[tok] * 7 + [
            pl.BlockSpec((1, 1, 1, w), lambda bb, c: (bb, cidx(c), 0, 0)),
            full(tri.shape), full(lvl.shape), full(hm.shape),
        ],
        out_specs=tok,
        out_shape=jax.ShapeDtypeStruct((b, lt, w), BF16),
        scratch_shapes=[pltpu.VMEM((w // MXU_DIM, n, n), F32)],
        compiler_params=_cparams("arbitrary", "arbitrary"),
        name="rwkv_scan_rev" if reverse else "rwkv_scan_fwd",
    )(rh, kh, bh, kap, kg, bg, v, gl, jnp.asarray(tri), jnp.asarray(lvl), jnp.asarray(hm, dtype=BF16))


def _hgrn_consts(reverse):
    t = np.arange(TOK_BLOCK)[:, None]
    s = np.arange(TOK_BLOCK)[None, :]
    same = (t // CHUNK) == (s // CHUNK)
    order = (s >= t) if reverse else (s <= t)
    earlier_sub = ((s // SUB) > (t // SUB)) if reverse else ((s // SUB) < (t // SUB))
    both = np.concatenate([same & order, same & earlier_sub], axis=0).astype(np.float32)
    return np.concatenate([both] * 3, axis=1)


def _split3(x):
    x1 = x.astype(BF16)
    r1 = x - x1.astype(F32)
    x2 = r1.astype(BF16)
    x3 = (r1 - x2.astype(F32)).astype(BF16)
    return jnp.concatenate([x1, x2, x3], axis=0)


def _hgrn_scan_kernel(reverse, q_ref, f_ref, i_ref, lb_ref, cm_ref, o_ref, s_ref, g_scr, gb_scr, q_scr, k_scr):
    @pl.when(pl.program_id(1) == 0)
    def _():
        s_ref[...] = jnp.zeros_like(s_ref)

    n_sub = CHUNK // SUB
    n_heads = q_ref.shape[-1] // HGRN_HEAD
    n_chunks = TOK_BLOCK // CHUNK
    sub_order = list(range(n_sub))[::-1] if reverse else list(range(n_sub))
    row = lax.broadcasted_iota(jnp.int32, (CHUNK, 1), 0)
    row_sub = row // SUB
    srow = lax.broadcasted_iota(jnp.int32, (SUB, 1), 0)
    lane = lax.broadcasted_iota(jnp.int32, (1, LANES), 1)
    last_row = 0 if reverse else CHUNK - 1

    kgate = (1.0 - lb_ref[...]) * _sigmoid(-f_ref[0])
    sums = _mm(cm_ref[...], _split3(jnp.log1p(-kgate) * LOG2E))
    g_scr[...] = sums[:TOK_BLOCK]
    gb_scr[...] = sums[TOK_BLOCK:]
    q_scr[...] = _silu(q_ref[0])
    k_scr[...] = kgate

    for ci in range(n_chunks):
        c = n_chunks - 1 - ci if reverse else ci
        rows = pl.ds(c * CHUNK, CHUNK)

        def heads(hp, carry):
            par = range(HGRN_PAR)
            hs = [hp * HGRN_PAR + j for j in par]
            cols = [pl.ds(pl.multiple_of(h * HGRN_HEAD, HGRN_HEAD), HGRN_HEAD) for h in hs]
            g = [g_scr[rows, cs] for cs in cols]
            gb = [gb_scr[rows, cs] for cs in cols]
            q = [q_scr[rows, cs] for cs in cols]
            k = [k_scr[rows, cs] for cs in cols]
            vb = [i_ref[0, rows, cs].astype(BF16) for cs in cols]
            s0 = [s_ref[h] for h in hs]
            q_in = [q[j] * jnp.exp2(g[j] - gb[j]) for j in par]
            at = []
            for j in par:
                kts, qts = [], []
                for si in sub_order[1:]:
                    gb_i = gb[j][si * SUB:si * SUB + 1, :]
                    earlier = (row_sub > si) if reverse else (row_sub < si)
                    kts.append(jnp.where(earlier, k[j] * jnp.exp2(jnp.minimum(gb_i - g[j], 0.0)), 0.0).astype(BF16))
                    qts.append(jnp.where(row_sub == si, q_in[j], 0.0).astype(BF16))
                at.append(_nt(jnp.concatenate(kts, axis=-1), jnp.concatenate(qts, axis=-1)))
            diag = [[] for _ in par]
            for si in range(n_sub):
                blk = slice(si * SUB, (si + 1) * SUB)
                acc = [jnp.zeros((SUB, LANES), F32) for _ in par]
                for tt in range(SUB):
                    t = si * SUB + tt
                    valid = jnp.logical_and(lane == t, (srow >= tt) if reverse else (srow <= tt))
                    trow = pl.ds(c * CHUNK + t, 1)
                    for j in par:
                        e = jnp.exp2(jnp.minimum(g_scr[trow, cols[j]] - g[j][blk], 0.0))
                        col = jnp.sum(k[j][blk] * e * q_scr[trow, cols[j]], axis=-1, keepdims=True)
                        acc[j] = jnp.where(valid, col, acc[j])
                for j in par:
                    diag[j].append(acc[j])
            for j in par:
                a = (at[j] + jnp.concatenate(diag[j], axis=0)[:, :CHUNK]).astype(BF16)
                o = _tn(a, vb[j]) + _nt((q[j] * jnp.exp2(g[j])).astype(BF16), s0[j].astype(BF16))
                o_ref[0, rows, cols[j]] = o.astype(o_ref.dtype)
                g_last = g[j][last_row:last_row + 1, :]
                k_out = (k[j] * jnp.exp2(g_last - g[j])).astype(BF16)
                s_ref[hs[j]] = s0[j] * jnp.exp2(g_last) + _tn(vb[j], k_out)
            return carry

        lax.fori_loop(0, n_heads // HGRN_PAR, heads, 0)


def _hgrn_scan(p_hgrn, lb_d, d, reverse):
    b, lt, w4 = p_hgrn.shape
    w = w4 // 4
    nblk = lt // TOK_BLOCK
    bidx = (lambda i: nblk - 1 - i) if reverse else (lambda i: jnp.where(i == 0, nblk - 1, i - 1))
    col = lambda j: pl.BlockSpec((1, TOK_BLOCK, w), lambda bb, i: (bb, bidx(i), j))
    cm = jnp.asarray(_hgrn_consts(reverse), dtype=BF16)
    return pl.pallas_call(
        functools.partial(_hgrn_scan_kernel, reverse),
        grid=(b, nblk),
        in_specs=[col(0), col(1 + d), col(3),
                  pl.BlockSpec((1, w), lambda bb, i: (0, 0)),
                  pl.BlockSpec(cm.shape, lambda bb, i: (0, 0))],
        out_specs=col(0),
        out_shape=jax.ShapeDtypeStruct((b, lt, w), BF16),
        scratch_shapes=[pltpu.VMEM((w // HGRN_HEAD, HGRN_HEAD, HGRN_HEAD), F32)]
        + [pltpu.VMEM((TOK_BLOCK, w), F32)] * 4,
        compiler_params=_cparams("arbitrary", "arbitrary"),
        name="hgrn_scan_rev" if reverse else "hgrn_scan_fwd",
    )(p_hgrn, p_hgrn, p_hgrn, lb_d.reshape(1, w), cm)


def _hgrn_chunk_consts(reverse):
    t = np.arange(CHUNK)[:, None]
    s = np.arange(CHUNK)[None, :]
    order = (s >= t) if reverse else (s <= t)
    earlier_sub = ((s // SUB) > (t // SUB)) if reverse else ((s // SUB) < (t // SUB))
    both = np.concatenate([order, earlier_sub], axis=0).astype(np.float32)
    return np.concatenate([both] * 3, axis=1)


def _rwkv_chunk_stages(rh_ref, kh_ref, bh_ref, kap_ref, kg_ref, bg_ref, v_ref, gl_ref, tri_ref, lvl_ref, hm_ref,
                       y_ref, s_ref):
    hm = hm_ref[...]
    strict = tri_ref[0]
    incl = tri_ref[1]
    eye = incl - strict
    n_levels = lvl_ref.shape[0]
    groups = range(y_ref.shape[-1] // MXU_DIM)
    lanes = lambda ref, g: ref[0, :, g * MXU_DIM:(g + 1) * MXU_DIM]
    stack = lambda ref, g: jnp.concatenate([lanes(ref, g)] * GROUP_HEADS, axis=0) * hm

    kap = [stack(kap_ref, g) for g in groups]
    bh = [stack(bh_ref, g) for g in groups]
    kh = [stack(kh_ref, g) for g in groups]
    rh = [stack(rh_ref, g) for g in groups]
    vs = [stack(v_ref, g) for g in groups]
    a_ab = [_nt(kap[g], bh[g]) * strict for g in groups]
    yield
    a_ak = [(_nt(kap[g], kh[g]) * strict).astype(BF16) for g in groups]
    p_rb = [(_nt(rh[g], bh[g]) * incl).astype(BF16) for g in groups]
    p_rk = [(_nt(rh[g], kh[g]) * incl).astype(BF16) for g in groups]
    t = [eye - a_ab[g] * lvl_ref[0] for g in groups]
    for lv in range(1, n_levels):
        yield
        tb = [t[g].astype(BF16) for g in groups]
        x = [_mm(tb[g], (a_ab[g] * lvl_ref[lv]).astype(BF16)).astype(BF16) for g in groups]
        yield
        t = [t[g] - _mm(x[g], tb[g]) for g in groups]
    yield
    s0 = [s_ref[g] for g in groups]
    s0b = [s0[g].astype(BF16) for g in groups]
    hm32 = hm.astype(F32)
    ks0 = [jnp.concatenate([_nt(lanes(kap_ref, g), s0b[g])] * GROUP_HEADS, axis=0) * hm32 for g in groups]
    wmat = [(ks0[g] + _mm(a_ak[g], vs[g])).astype(BF16) for g in groups]
    yield
    ub = [(-_mm(t[g].astype(BF16), wmat[g])).astype(BF16) for g in groups]
    yield
    ys = [_mm(p_rb[g], ub[g]) + _mm(p_rk[g], vs[g]) for g in groups]
    for g in groups:
        y = _nt(lanes(rh_ref, g), s0b[g])
        for h in range(GROUP_HEADS):
            y = y + ys[g][h * CHUNK:(h + 1) * CHUNK]
        y_ref[0, :, g * MXU_DIM:(g + 1) * MXU_DIM] = y.astype(y_ref.dtype)
    yield
    for g in groups:
        gam = gl_ref[0, 0][:, g * MXU_DIM:(g + 1) * MXU_DIM]
        s_ref[g] = s0[g] * gam + _tn(ub[g], stack(bg_ref, g)) + _tn(vs[g], stack(kg_ref, g))


def _hgrn_chunk_stages(reverse, q_ref, f_ref, i_ref, lb_ref, cm_ref, o_ref, s_ref, g_scr, q_scr):
    n_sub = CHUNK // SUB
    n_heads = q_ref.shape[-1] // HGRN_HEAD
    sub_order = list(range(n_sub))[::-1] if reverse else list(range(n_sub))
    row_sub = lax.broadcasted_iota(jnp.int32, (CHUNK, 1), 0) // SUB
    srow = lax.broadcasted_iota(jnp.int32, (SUB, 1), 0)
    lane = lax.broadcasted_iota(jnp.int32, (1, LANES), 1)
    last_row = 0 if reverse else CHUNK - 1

    kgate = (1.0 - lb_ref[...]) * _sigmoid(-f_ref[0])
    sums = _mm(cm_ref[...], _split3(jnp.log1p(-kgate) * LOG2E))
    g_all = sums[:CHUNK]
    gb_all = sums[CHUNK:]
    q_all = _silu(q_ref[0])
    g_scr[...] = g_all
    q_scr[...] = q_all
    yield
    for hg in range(n_heads // HGRN_PAR):
        par = range(HGRN_PAR)
        hs = [hg * HGRN_PAR + j for j in par]
        cols = [slice(h * HGRN_HEAD, (h + 1) * HGRN_HEAD) for h in hs]
        g = [g_all[:, cs] for cs in cols]
        gb = [gb_all[:, cs] for cs in cols]
        q = [q_all[:, cs] for cs in cols]
        k = [kgate[:, cs] for cs in cols]
        vb = [i_ref[0, :, cs].astype(BF16) for cs in cols]
        s0 = [s_ref[h] for h in hs]
        q_in = [q[j] * jnp.exp2(g[j] - gb[j]) for j in par]
        at = []
        for j in par:
            kts, qts = [], []
            for si in sub_order[1:]:
                gb_i = gb[j][si * SUB:si * SUB + 1, :]
                earlier = (row_sub > si) if reverse else (row_sub < si)
                kts.append(jnp.where(earlier, k[j] * jnp.exp2(jnp.minimum(gb_i - g[j], 0.0)), 0.0).astype(BF16))
                qts.append(jnp.where(row_sub == si, q_in[j], 0.0).astype(BF16))
            at.append(_nt(jnp.concatenate(kts, axis=-1), jnp.concatenate(qts, axis=-1)))
        yield
        diag = [[] for _ in par]
        for si in range(n_sub):
            blk = slice(si * SUB, (si + 1) * SUB)
            acc = [jnp.zeros((SUB, LANES), F32) for _ in par]
            for tt in range(SUB):
                t = si * SUB + tt
                valid = jnp.logical_and(lane == t, (srow >= tt) if reverse else (srow <= tt))
                for j in par:
                    e = jnp.exp2(jnp.minimum(g_scr[t:t + 1, cols[j]] - g[j][blk], 0.0))
                    col = jnp.sum(k[j][blk] * e * q_scr[t:t + 1, cols[j]], axis=-1, keepdims=True)
                    acc[j] = jnp.where(valid, col, acc[j])
                if tt % 8 == 7:
                    yield
            for j in par:
                diag[j].append(acc[j])
        for j in par:
            a = (at[j] + jnp.concatenate(diag[j], axis=0)[:, :CHUNK]).astype(BF16)
            o = _tn(a, vb[j]) + _nt((q[j] * jnp.exp2(g[j])).astype(BF16), s0[j].astype(BF16))
            o_ref[0, :, cols[j]] = o.astype(o_ref.dtype)
            g_last = g[j][last_row:last_row + 1, :]
            k_out = (k[j] * jnp.exp2(g_last - g[j])).astype(BF16)
            s_ref[hs[j]] = s0[j] * jnp.exp2(g_last) + _tn(vb[j], k_out)
        yield


def _mix_scan_kernel(reverse, *refs):
    rwkv_in, hgrn_in = refs[:11], refs[11:16]
    y_ref, o_ref, s_ref, hs_ref, g_scr, q_scr = refs[16:]

    @pl.when(pl.program_id(1) == 0)
    def _():
        s_ref[...] = jnp.zeros_like(s_ref)
        hs_ref[...] = jnp.zeros_like(hs_ref)

    streams = [_rwkv_chunk_stages(*rwkv_in, y_ref, s_ref),
               _hgrn_chunk_stages(reverse, *hgrn_in, o_ref, hs_ref, g_scr, q_scr)]
    done = object()
    while streams:
        streams = [st for st in streams if next(st, done) is not done]


def _mix_scan(dir_ops, v, p_hgrn, lb_d, d, reverse):
    rh, kh, bh, kap, kg, bg, gl = dir_ops
    b, lt, w = v.shape
    nch = lt // CHUNK
    n = GROUP_HEADS * CHUNK
    tri, lvl, hm = _rwkv_masks(reverse)
    cm = jnp.asarray(_hgrn_chunk_consts(reverse), dtype=BF16)
    cidx = lambda c: _scan_chunk_index(c, nch, reverse)
    tok = pl.BlockSpec((1, CHUNK, w), lambda bb, c: (bb, cidx(c), 0))
    col = lambda j: pl.BlockSpec((1, CHUNK, w), lambda bb, c: (bb, cidx(c), j))
    full = lambda shape: pl.BlockSpec(shape, lambda bb, c: (0,) * len(shape))
    return pl.pallas_call(
        functools.partial(_mix_scan_kernel, reverse),
        grid=(b, nch),
        in_specs=[tok] * 7 + [
            pl.BlockSpec((1, 1, 1, w), lambda bb, c: (bb, cidx(c), 0, 0)),
            full(tri.shape), full(lvl.shape), full(hm.shape),
            col(0), col(1 + d), col(3), full((1, w)), full(cm.shape),
        ],
        out_specs=[tok, tok],
        out_shape=[jax.ShapeDtypeStruct((b, lt, w), BF16)] * 2,
        scratch_shapes=[pltpu.VMEM((w // MXU_DIM, n, n), F32),
                        pltpu.VMEM((w // HGRN_HEAD, HGRN_HEAD, HGRN_HEAD), F32),
                        pltpu.VMEM((CHUNK, w), F32), pltpu.VMEM((CHUNK, w), F32)],
        compiler_params=_cparams("arbitrary", "arbitrary"),
        name="mix_scan_rev" if reverse else "mix_scan_fwd",
    )(rh, kh, bh, kap, kg, bg, v, gl, jnp.asarray(tri), jnp.asarray(lvl), jnp.asarray(hm, dtype=BF16),
      p_hgrn, p_hgrn, p_hgrn, lb_d.reshape(1, w), cm)


def _residual_ln(alpha, x, gm, proj, lng, lnb):
    z = alpha * x + gm * proj
    mu = jnp.mean(z, axis=-1, keepdims=True)
    zc = z - mu
    var = jnp.mean(zc * zc, axis=-1, keepdims=True)
    return zc * lax.rsqrt(var + LN_EPS) * lng + lnb


def _even_out_kernel(alpha, ya0_ref, ya1_ref, bonus_ref, yb0_ref, yb1_ref, gate_ref, x_ref, gm_ref, w_ref,
                     lnxg_ref, lnxb_ref, ng_ref, lng_ref, lnb_ref, o_ref):
    y = ya0_ref[0].astype(F32) + ya1_ref[0].astype(F32)
    mu = _group64_sum(y) * (1.0 / RWKV_HEAD)
    yc = y - mu
    var = _group64_sum(yc * yc) * (1.0 / RWKV_HEAD)
    ya = yc * lax.rsqrt(var + RWKV_GN_EPS) * lnxg_ref[...] + lnxb_ref[...] + bonus_ref[0].astype(F32)
    o = yb0_ref[0].astype(F32) + yb1_ref[0].astype(F32)
    yb = o * lax.rsqrt(_group128_mean(o * o) + LN_EPS) * ng_ref[...]
    ycat = jnp.concatenate([ya, yb], axis=-1) * _silu(gate_ref[0].astype(F32))
    proj = _mm(ycat.astype(BF16), w_ref[...])
    o_ref[0] = _residual_ln(alpha, x_ref[0], gm_ref[0, 0], proj, lng_ref[...], lnb_ref[...])


def _odd_out_kernel(alpha, y_ref, x_ref, gm_ref, w_ref, lng_ref, lnb_ref, o_ref):
    proj = _mm(y_ref[0].astype(BF16), w_ref[...])
    o_ref[0] = _residual_ln(alpha, x_ref[0], gm_ref[0, 0], proj, lng_ref[...], lnb_ref[...])


def _tok_spec(n):
    return pl.BlockSpec((1, TOK_BLOCK, n), lambda bb, i: (bb, i, 0))


def _full2(shape):
    return pl.BlockSpec(shape, lambda bb, i: (0,) * len(shape))


def _seg_spec(d, nblk):
    return pl.BlockSpec((1, 1, 1, d), lambda bb, i: (bb, _seg(i, nblk), 0, 0))


def _even_out(alpha, ya0, ya1, bonus, yb0, yb1, gate, xs, gmod, w_out, lnx_g, lnx_b, norm_g, ln_g, ln_b):
    b, lt, d = xs.shape
    w = ya0.shape[-1]
    di = gate.shape[-1]
    row = lambda u: u.reshape(1, -1)
    return pl.pallas_call(
        functools.partial(_even_out_kernel, alpha),
        grid=(b, lt // TOK_BLOCK),
        in_specs=[_tok_spec(w)] * 5 + [_tok_spec(di), _tok_spec(d), _seg_spec(d, lt // TOK_BLOCK), _full2((di, d)),
                                       _full2((1, w)), _full2((1, w)), _full2((1, w)), _full2((1, d)), _full2((1, d))],
        out_specs=_tok_spec(d),
        out_shape=jax.ShapeDtypeStruct((b, lt, d), F32),
        compiler_params=_cparams("arbitrary", "arbitrary"),
        name="even_out",
    )(ya0, ya1, bonus, yb0, yb1, gate, xs, gmod, w_out, row(lnx_g), row(lnx_b),
      row(jnp.tile(norm_g, w // HGRN_HEAD)), row(ln_g), row(ln_b))


def _odd_out(alpha, y, xs, gmod, w_out, ln_g, ln_b, rows_out):
    b, lt, d = xs.shape
    di = y.shape[-1]
    row = lambda u: u.reshape(1, -1)
    return pl.pallas_call(
        functools.partial(_odd_out_kernel, alpha),
        grid=(b, rows_out // TOK_BLOCK),
        in_specs=[_tok_spec(di), _tok_spec(d), _seg_spec(d, lt // TOK_BLOCK), _full2((di, d)), _full2((1, d)),
                  _full2((1, d))],
        out_specs=_tok_spec(d),
        out_shape=jax.ShapeDtypeStruct((b, rows_out, d), F32),
        compiler_params=_cparams("arbitrary", "arbitrary"),
        name="odd_out",
    )(y, xs, gmod, w_out, row(ln_g), row(ln_b))


def _rope(x, cos, sin):
    width = x.shape[-1]
    lane = lax.broadcasted_iota(jnp.int32, (1, width), 1)
    first = (lane % 32) < 16
    partner = jnp.where(first, pltpu.roll(x, width - 16, 1), pltpu.roll(x, 16, 1))
    return x * cos + partner * sin


def _attn_kernel(lam_init, tk, n_full, tail, lam_ref, q_ref, k_ref, vt_ref, g_ref, sg_ref, *rest):
    y_ref, m_scr, acc_scr = rest[-3:]
    tq = q_ref.shape[1]
    ones_rows = 16
    tile = min(ATTN_TILE, 2 * tq)
    n_tiles = 2 * tq // tile
    q = q_ref[0].astype(F32) * (DIFF_HEAD ** -0.5 * float(np.log2(np.e)))
    lane = lax.broadcasted_iota(jnp.int32, (1, LANES), 1)
    q_cat = jnp.concatenate([jnp.where(lane < DIFF_HEAD, q, 0.0), jnp.where(lane >= DIFF_HEAD, q, 0.0)],
                            axis=0).astype(BF16)
    m_scr[...] = jnp.full(m_scr.shape, -jnp.inf, F32)
    acc_scr[...] = jnp.zeros(acc_scr.shape, F32)

    def scores(item, size):
        start, c = item
        return _nt(k_ref[0, pl.ds(start, size), :], q_cat[c * tile:(c + 1) * tile])

    def run(starts, size):
        cols = [slice(c * tile, (c + 1) * tile) for c in range(n_tiles)]
        m_run = [m_scr[:, cs] for cs in cols]
        acc = [acc_scr[:, cs] for cs in cols]
        vta = None
        pending = None
        items = [(start, c) for start in starts for c in range(n_tiles)]
        st_next = scores(items[0], size)
        for i, (start, c) in enumerate(items):
            st = st_next
            if i + 1 < len(items):
                st_next = scores(items[i + 1], size)
            if pending is not None:
                pc, pcorr, pvta, ppt = pending
                acc[pc] = acc[pc] * pcorr + _mm(pvta, ppt)
            if c == 0:
                vta = jnp.concatenate([vt_ref[0, :, pl.ds(start, size)], jnp.ones((ones_rows, size), BF16)], axis=0)
            m_new = jnp.maximum(m_run[c], jnp.max(st, axis=0, keepdims=True))
            pending = (c, jnp.exp2(m_run[c] - m_new), vta, jnp.exp2(st - m_new).astype(BF16))
            m_run[c] = m_new
        pc, pcorr, pvta, ppt = pending
        acc[pc] = acc[pc] * pcorr + _mm(pvta, ppt)
        for c, cs in enumerate(cols):
            m_scr[:, cs] = m_run[c]
            acc_scr[:, cs] = acc[c]

    if n_full:
        unroll = ATTN_UNROLL if n_full % ATTN_UNROLL == 0 else 1
        def body(j, carry):
            run([pl.multiple_of((j * unroll + u) * tk, tk) for u in range(unroll)], tk)
            return carry
        lax.fori_loop(0, n_full // unroll, body, 0)
    run([n_full * tk], tail)

    a0 = acc_scr[:, :tq]
    a1 = acc_scr[:, tq:]
    o_t = a0[:LANES] / a0[LANES:LANES + 1] - lam_ref[0] * (a1[:LANES] / a1[LANES:LANES + 1])
    o = jnp.transpose(o_t)
    y = o * lax.rsqrt(jnp.mean(o * o, axis=-1, keepdims=True) + LN_EPS) * sg_ref[...] * (1.0 - lam_init)
    y_ref[0] = (y * _silu(g_ref[0].astype(F32))).astype(y_ref.dtype)


def _diff_attention(p_qk, p_vt, p_g, lam, subln_g, lam_init):
    b, lt, di = p_g.shape
    nh = di // LANES
    t = lt - TOK_BLOCK
    tq = tk = 512
    assert t % tq == 0
    ctx_blk = t // TOK_BLOCK
    sm = pl.BlockSpec(memory_space=pltpu.SMEM)
    sg_spec = pl.BlockSpec((1, LANES), lambda bb, h, i: (0, 0))
    scratch = lambda n: [pltpu.VMEM((1, 2 * n), F32), pltpu.VMEM((LANES + 16, 2 * n), F32)]
    args = (lam.reshape(1), p_qk, p_qk, p_vt, p_g, subln_g.reshape(1, LANES))
    y = pl.pallas_call(
        functools.partial(_attn_kernel, lam_init, tk, t // tk, TOK_BLOCK),
        grid=(b, nh, t // tq),
        in_specs=[sm,
                  pl.BlockSpec((1, tq, LANES), lambda bb, h, i: (bb, i, h)),
                  pl.BlockSpec((1, lt, LANES), lambda bb, h, i: (bb, 0, nh + h)),
                  pl.BlockSpec((1, LANES, lt), lambda bb, h, i: (bb, h, 0)),
                  pl.BlockSpec((1, tq, LANES), lambda bb, h, i: (bb, i, h)),
                  sg_spec],
        out_specs=pl.BlockSpec((1, tq, LANES), lambda bb, h, i: (bb, i, h)),
        out_shape=jax.ShapeDtypeStruct((b, lt, di), BF16),
        scratch_shapes=scratch(tq),
        compiler_params=_cparams("arbitrary", "arbitrary", "arbitrary"),
        name="diff_attn",
    )(*args)
    blk = lambda bb, h, i: (bb, ctx_blk, h)
    return pl.pallas_call(
        functools.partial(_attn_kernel, lam_init, tk, 0, TOK_BLOCK),
        grid=(b, nh, 1),
        in_specs=[sm,
                  pl.BlockSpec((1, TOK_BLOCK, LANES), blk),
                  pl.BlockSpec((1, TOK_BLOCK, LANES), lambda bb, h, i: (bb, ctx_blk, nh + h)),
                  pl.BlockSpec((1, LANES, TOK_BLOCK), lambda bb, h, i: (bb, h, ctx_blk)),
                  pl.BlockSpec((1, TOK_BLOCK, LANES), blk),
                  sg_spec,
                  pl.BlockSpec(memory_space=pl.ANY)],
        out_specs=pl.BlockSpec((1, TOK_BLOCK, LANES), blk),
        out_shape=jax.ShapeDtypeStruct((b, lt, di), BF16),
        scratch_shapes=scratch(TOK_BLOCK),
        input_output_aliases={6: 0},
        compiler_params=_cparams("arbitrary", "arbitrary", "arbitrary"),
        name="diff_attn_ctx",
    )(*args, y)


def _rope_tables(n_ctx, t):
    quarter = DIFF_HEAD // 4
    inv = ROPE_BASE ** (-jnp.arange(quarter, dtype=F32) / quarter)
    pos = jnp.arange(t)
    rows = (pos // GRID_W).astype(F32)[:, None] * inv
    cols = (pos % GRID_W).astype(F32)[:, None] * inv
    cos64 = jnp.concatenate([jnp.cos(rows), jnp.cos(rows), jnp.cos(cols), jnp.cos(cols)], -1)
    sin64 = jnp.concatenate([-jnp.sin(rows), jnp.sin(rows), -jnp.sin(cols), jnp.sin(cols)], -1)
    cos = jnp.concatenate([jnp.tile(cos64, (1, 2)), jnp.ones((n_ctx, LANES), F32)], 0)
    sin = jnp.concatenate([jnp.tile(sin64, (1, 2)), jnp.zeros((n_ctx, LANES), F32)], 0)
    return cos, sin


def _even_layer(alpha, xs, scale, shift, gmod, w_in, w_out, mu_prev, mu_next, w0, w2, a0, a2, k_k, k_a, r_k,
                lnx_g, lnx_b, lb, norm_g, ln_g, ln_b):
    w = k_k.shape[-1]
    sw = mu_prev.shape[-1]
    hw = 4 * lb.shape[-1]
    w_in = w_in.astype(BF16)
    p_rwkv = _project(xs, scale, shift, w_in[:, :sw], sw, F32)
    p_hgrn = _project(xs, scale, shift, w_in[:, sw:sw + hw], hw // 2, F32)
    gate = _project(xs, scale, shift, w_in[:, sw + hw:], w_in.shape[1] - sw - hw, BF16)
    v, bonus, ops_f, ops_r = _rwkv_prep(p_rwkv, mu_prev, mu_next, w0, w2, a0, a2, k_k, k_a, r_k)
    ya0, yb0 = _mix_scan(ops_f, v, p_hgrn, lb[0], 0, False)
    ya1, yb1 = _mix_scan(ops_r, v, p_hgrn, lb[1], 1, True)
    return _even_out(alpha, ya0, ya1, bonus, yb0, yb1, gate, xs, gmod, w_out.astype(BF16),
                     lnx_g, lnx_b, norm_g, ln_g, ln_b)


def _odd_layer(alpha, xs, scale, shift, gmod, w_in, w_out, lam_p, subln_g, lam_init, cos, sin, ln_g, ln_b,
               rows_out):
    di = w_out.shape[0]
    w_in = w_in.astype(BF16)
    p_qk = _project(xs, scale, shift, w_in[:, :2 * di], di, BF16, rope=(cos, sin))
    p_vt = _project_t(xs, scale, shift, jnp.transpose(w_in[:, 2 * di:3 * di]), BF16)
    p_g = _project(xs, scale, shift, w_in[:, 3 * di:], di, BF16)
    lam = jnp.exp(jnp.sum(lam_p[0] * lam_p[1])) - jnp.exp(jnp.sum(lam_p[2] * lam_p[3])) + lam_init
    y = _diff_attention(p_qk, p_vt, p_g, lam, subln_g, lam_init)
    return _odd_out(alpha, y, xs, gmod, w_out.astype(BF16), ln_g, ln_b, rows_out)


def kernel(x, c, ctx, c_ctx, ada_w, ada_b, ln_g, ln_b, even_w_in, even_w_out, rwkv_mu_prev, rwkv_mu_next, rwkv_w0, rwkv_w2, rwkv_a0, rwkv_a2, rwkv_k_k, rwkv_k_a, rwkv_r_k, rwkv_lnx_g, rwkv_lnx_b, hgrn_lb_logits, hgrn_norm_g, odd_w_in, odd_w_out, diff_lambda, diff_subln_g):
    b, t, d = x.shape
    n_ctx = ctx.shape[1]
    depth = ada_w.shape[0]
    assert n_ctx == TOK_BLOCK and t % TOK_BLOCK == 0 and b + 1 <= 8
    alpha = (2.0 * depth) ** 0.25

    xs = jnp.concatenate([x, ctx], axis=1)
    cvec = jnp.concatenate([c, c_ctx[None], jnp.zeros((8 - b - 1, d), F32)], axis=0)
    mods = _ada_mods(cvec, ada_w, ada_b)
    lb_all = jax.nn.softmax(hgrn_lb_logits.astype(F32), axis=0)
    lb_all = jnp.cumsum(lb_all, axis=0) - lb_all[0]
    cos, sin = _rope_tables(n_ctx, t)

    for layer in range(depth):
        m = mods[layer]
        per_seg = lambda u: jnp.stack([jnp.broadcast_to(u[b], (b, d)), u[:b]], axis=1)[:, :, None, :]
        shift, scale, gmod = (per_seg(m[:, j * d:(j + 1) * d]) for j in range(3))
        j = layer // 2
        if layer % 2 == 0:
            xs = _even_layer(alpha, xs, scale, shift, gmod, even_w_in[j], even_w_out[j], rwkv_mu_prev[j],
                             rwkv_mu_next[j], rwkv_w0[j], rwkv_w2[j], rwkv_a0[j], rwkv_a2[j], rwkv_k_k[j],
                             rwkv_k_a[j], rwkv_r_k[j], rwkv_lnx_g[j], rwkv_lnx_b[j], lb_all[j], hgrn_norm_g[j],
                             ln_g[layer], ln_b[layer])
        else:
            lam_init = 0.8 - 0.6 * float(np.exp(-0.3 * layer))
            xs = _odd_layer(alpha, xs, scale, shift, gmod, odd_w_in[j], odd_w_out[j], diff_lambda[j],
                            diff_subln_g[j], lam_init, cos, sin, ln_g[layer], ln_b[layer],
                            t if layer == depth - 1 else t + n_ctx)
    return xs[:, :t]
```

```python
import functools

import numpy as np
import jax
import jax.numpy as jnp
from jax import lax
from jax.experimental import pallas as pl
from jax.experimental.pallas import tpu as pltpu

F32 = jnp.float32
BF16 = jnp.bfloat16
HI = lax.Precision.HIGHEST

GRID_W = 64
RWKV_HEAD = 64
RWKV_LORA = 64
RWKV_GN_EPS = 64e-5
HGRN_HEAD = 128
DIFF_HEAD = 64
ROPE_BASE = 10000.0
LN_EPS = 1e-5
LOG2E = float(np.log2(np.e))

LANES = 128
MXU_DIM = 256
VMEM_LIMIT = 56 * 1024 * 1024

CHUNK = 64
SUB = 16
TOK_BLOCK = 256
GROUP_HEADS = MXU_DIM // RWKV_HEAD
HGRN_PAR = 4
ATTN_UNROLL = 16
ATTN_TILE = 1024


def _cparams(*sem):
    return pltpu.CompilerParams(dimension_semantics=sem, vmem_limit_bytes=VMEM_LIMIT)


def _nt(a, b):
    return lax.dot_general(a, b, (((1,), (1,)), ((), ())), preferred_element_type=F32)


def _tn(a, b):
    return lax.dot_general(a, b, (((0,), (0,)), ((), ())), preferred_element_type=F32)


def _mm(a, b):
    return jnp.dot(a, b, preferred_element_type=F32)


def _mm_exact(a, b):
    return jnp.dot(a, b, preferred_element_type=F32, precision=HI)


def _sigmoid(x):
    return 1.0 / (1.0 + jnp.exp(-x))


def _silu(x):
    return x * _sigmoid(x)


def _lane_tiles(x):
    return [x[:, j * LANES:(j + 1) * LANES] for j in range(x.shape[-1] // LANES)]


def _group64_sum(x):
    lane = lax.broadcasted_iota(jnp.int32, (1, LANES), 1)
    low = lane < RWKV_HEAD
    out = []
    for xt in _lane_tiles(x):
        s_all = jnp.sum(xt, axis=-1, keepdims=True)
        s_lo = jnp.sum(jnp.where(low, xt, 0.0), axis=-1, keepdims=True)
        out.append(jnp.where(low, s_lo, s_all - s_lo))
    return jnp.concatenate(out, axis=-1)


def _group128_mean(x):
    out = []
    for xt in _lane_tiles(x):
        out.append(jnp.broadcast_to(jnp.mean(xt, axis=-1, keepdims=True), xt.shape))
    return jnp.concatenate(out, axis=-1)


def _ada_kernel(c_ref, w_ref, b_ref, o_ref):
    cond = _silu(c_ref[...])
    o_ref[0] = _mm_exact(cond, w_ref[0]) + b_ref[0]


def _ada_mods(cvec, ada_w, ada_b):
    depth, d, d3 = ada_w.shape
    tn = 1024
    return pl.pallas_call(
        _ada_kernel,
        grid=(depth, d3 // tn),
        in_specs=[
            pl.BlockSpec((8, d), lambda l, j: (0, 0)),
            pl.BlockSpec((1, d, tn), lambda l, j: (l, 0, j)),
            pl.BlockSpec((1, 1, tn), lambda l, j: (l, 0, j)),
        ],
        out_specs=pl.BlockSpec((1, 8, tn), lambda l, j: (l, 0, j)),
        out_shape=jax.ShapeDtypeStruct((depth, 8, d3), F32),
        compiler_params=_cparams("arbitrary", "arbitrary"),
        name="ada_mods",
    )(cvec, ada_w, ada_b.reshape(depth, 1, d3))


def _seg(i, nblk):
    return jnp.where(i == nblk - 1, 0, 1)


def _modulate(x_ref, sc_ref, sh_ref):
    return (x_ref[0] * (1.0 + sc_ref[0, 0]) + sh_ref[0, 0]).astype(BF16)


def _proj_kernel(x_ref, sc_ref, sh_ref, w_ref, o_ref):
    o_ref[0] = _mm(_modulate(x_ref, sc_ref, sh_ref), w_ref[...]).astype(o_ref.dtype)


def _proj_rope_kernel(x_ref, sc_ref, sh_ref, w_ref, cos_ref, sin_ref, o_ref):
    p = _mm(_modulate(x_ref, sc_ref, sh_ref), w_ref[...])
    reps = p.shape[-1] // LANES
    cos = jnp.concatenate([cos_ref[...]] * reps, axis=-1)
    sin = jnp.concatenate([sin_ref[...]] * reps, axis=-1)
    o_ref[0] = _rope(p, cos, sin).astype(o_ref.dtype)


def _proj_t_kernel(x_ref, sc_ref, sh_ref, wt_ref, o_ref):
    o_ref[0] = _nt(wt_ref[...], _modulate(x_ref, sc_ref, sh_ref)).astype(o_ref.dtype)


def _project(xs, scale, shift, w, tn, out_dtype, rope=None):
    b, lt, d = xs.shape
    n = w.shape[1]
    nblk = lt // TOK_BLOCK
    in_specs = [
        pl.BlockSpec((1, TOK_BLOCK, d), lambda j, bb, i: (bb, i, 0)),
        pl.BlockSpec((1, 1, 1, d), lambda j, bb, i: (bb, _seg(i, nblk), 0, 0)),
        pl.BlockSpec((1, 1, 1, d), lambda j, bb, i: (bb, _seg(i, nblk), 0, 0)),
        pl.BlockSpec((d, tn), lambda j, bb, i: (0, j)),
    ]
    args = (xs, scale, shift, w)
    if rope is not None:
        in_specs += [pl.BlockSpec((TOK_BLOCK, LANES), lambda j, bb, i: (i, 0))] * 2
        args += tuple(rope)
    return pl.pallas_call(
        _proj_kernel if rope is None else _proj_rope_kernel,
        grid=(n // tn, b, nblk),
        in_specs=in_specs,
        out_specs=pl.BlockSpec((1, TOK_BLOCK, tn), lambda j, bb, i: (bb, i, j)),
        out_shape=jax.ShapeDtypeStruct((b, lt, n), out_dtype),
        compiler_params=_cparams("arbitrary", "arbitrary", "arbitrary"),
        name="mod_proj" if rope is None else "mod_proj_rope",
    )(*args)


def _project_t(xs, scale, shift, wt, out_dtype):
    b, lt, d = xs.shape
    n = wt.shape[0]
    nblk = lt // TOK_BLOCK
    return pl.pallas_call(
        _proj_t_kernel,
        grid=(b, nblk),
        in_specs=[
            pl.BlockSpec((1, TOK_BLOCK, d), lambda bb, i: (bb, i, 0)),
            pl.BlockSpec((1, 1, 1, d), lambda bb, i: (bb, _seg(i, nblk), 0, 0)),
            pl.BlockSpec((1, 1, 1, d), lambda bb, i: (bb, _seg(i, nblk), 0, 0)),
            pl.BlockSpec((n, d), lambda bb, i: (0, 0)),
        ],
        out_specs=pl.BlockSpec((1, n, TOK_BLOCK), lambda bb, i: (bb, 0, i)),
        out_shape=jax.ShapeDtypeStruct((b, n, lt), out_dtype),
        compiler_params=_cparams("arbitrary", "arbitrary"),
        name="mod_proj_t",
    )(xs, scale, shift, wt)


def _lora(x, w_ref):
    x1 = x.astype(BF16)
    x2 = (x - x1.astype(F32)).astype(BF16)
    return _mm(jnp.concatenate([x1, x1, x2], axis=-1), w_ref[...])


def _rwkv_prep_kernel(nblk, p_ref, prev_ref, next_ref, mup_ref, mun_ref, w0_ref, w2_ref, a0_ref, a2_ref,
                      kk_ref, ka_ref, rk_ref, cm_ref,
                      v_out, bonus_out, *dir_outs):
    i = pl.program_id(1)
    w = kk_ref.shape[-1]
    p = p_ref[0]
    row = lax.broadcasted_iota(jnp.int32, (TOK_BLOCK, 1), 0)
    prev_row = jnp.where(jnp.logical_and(i >= 1, i <= nblk - 2), prev_ref[0][7:8, :], 0.0)
    next_row = jnp.where(i <= nblk - 3, next_ref[0][0:1, :], 0.0)
    prev = jnp.where(row == 0, prev_row, pltpu.roll(p, 1, 0))
    nxt = jnp.where(row == TOK_BLOCK - 1, next_row, pltpu.roll(p, TOK_BLOCK - 1, 0))
    ps = p + mup_ref[...] * (prev - p) + mun_ref[...] * (nxt - p)

    r = ps[:, 0:w]
    k = ps[:, w:2 * w]
    v = ps[:, 2 * w:3 * w]
    wlo = ps[:, 3 * w:3 * w + 2 * RWKV_LORA]
    alo = ps[:, 3 * w + 2 * RWKV_LORA:3 * w + 4 * RWKV_LORA]

    z = _lora(jnp.tanh(wlo), w2_ref) + w0_ref[...]
    w_log = -(jnp.maximum(-z, 0.0) + jnp.log(1.0 + jnp.exp(-jnp.abs(z)))) - 0.5
    lw = jnp.exp(w_log) * (-LOG2E)
    a = _sigmoid(_lora(alo, a2_ref) + a0_ref[...])

    kk = k * kk_ref[...]
    kk = kk * lax.rsqrt(_group64_sum(kk * kk) + 1e-12)
    bonus_out[0] = (_group64_sum(r * k * rk_ref[...]) * v).astype(bonus_out.dtype)
    v_out[0] = v.astype(BF16)

    for d in range(2):
        rh_o, kh_o, bh_o, kap_o, kg_o, bg_o, gl_o = dir_outs[7 * d:7 * d + 7]
        lw_d = lw[:, d * w:(d + 1) * w]
        a_d = a[:, d * w:(d + 1) * w]
        kd = k * (1.0 + (a_d - 1.0) * ka_ref[...])
        b_d = kk * a_d
        sums = _mm(cm_ref[d], _split3(lw_d))
        g = sums[:TOK_BLOCK]
        g_all = sums[TOK_BLOCK:]
        e_neg = jnp.exp2(-g)
        e_rest = jnp.exp2(g_all - g)
        rh_o[0] = (r * jnp.exp2(g)).astype(BF16)
        kh_o[0] = (kd * e_neg).astype(BF16)
        bh_o[0] = (b_d * e_neg).astype(BF16)
        kap_o[0] = (kk * jnp.exp2(g - lw_d)).astype(BF16)
        kg_o[0] = (kd * e_rest).astype(BF16)
        bg_o[0] = (b_d * e_rest).astype(BF16)
        e_all = jnp.exp2(g_all)
        for c in range(TOK_BLOCK // CHUNK):
            gl_o[0, c] = e_all[c * CHUNK:c * CHUNK + 1, :]


def _rwkv_sum_consts(reverse):
    t = np.arange(TOK_BLOCK)[:, None]
    s = np.arange(TOK_BLOCK)[None, :]
    same = (t // CHUNK) == (s // CHUNK)
    order = (s >= t) if reverse else (s <= t)
    both = np.concatenate([same & order, same], axis=0).astype(np.float32)
    return np.concatenate([both] * 3, axis=1)


def _lora_pieces(m):
    hi = m.astype(BF16)
    lo = (m - hi.astype(F32)).astype(BF16)
    return jnp.concatenate([hi, lo, hi], axis=0)


def _rwkv_prep(p_rwkv, mu_prev, mu_next, w0, w2, a0, a2, k_k, k_a, r_k):
    b, lt, sw = p_rwkv.shape
    w = k_k.shape[-1]
    nblk = lt // TOK_BLOCK
    nch = lt // CHUNK
    halo = TOK_BLOCK // 8
    zl = jnp.zeros((RWKV_LORA, w), F32)
    w2cat = _lora_pieces(jnp.concatenate([jnp.concatenate([w2[0], zl], 1), jnp.concatenate([zl, w2[1]], 1)], 0))
    a2cat = _lora_pieces(jnp.concatenate([jnp.concatenate([a2[0], zl], 1), jnp.concatenate([zl, a2[1]], 1)], 0))
    cm = jnp.asarray(np.stack([_rwkv_sum_consts(False), _rwkv_sum_consts(True)]), dtype=BF16)
    row = lambda u: u.reshape(1, -1)
    full = lambda shape: pl.BlockSpec(shape, lambda bb, i: (0,) * len(shape))
    tok = lambda n, dt: (pl.BlockSpec((1, TOK_BLOCK, n), lambda bb, i: (bb, i, 0)), jax.ShapeDtypeStruct((b, lt, n), dt))
    gl = (pl.BlockSpec((1, TOK_BLOCK // CHUNK, 1, w), lambda bb, i: (bb, i, 0, 0)),
          jax.ShapeDtypeStruct((b, nch, 1, w), F32))
    outs = [tok(w, BF16), tok(w, BF16)] + 2 * ([tok(w, BF16)] * 6 + [gl])
    res = pl.pallas_call(
        functools.partial(_rwkv_prep_kernel, nblk),
        grid=(b, nblk),
        in_specs=[
            pl.BlockSpec((1, TOK_BLOCK, sw), lambda bb, i: (bb, i, 0)),
            pl.BlockSpec((1, 8, sw), lambda bb, i: (bb, jnp.maximum(i * halo - 1, 0), 0)),
            pl.BlockSpec((1, 8, sw), lambda bb, i: (bb, jnp.minimum((i + 1) * halo, lt // 8 - 1), 0)),
            full((1, sw)), full((1, sw)), full((1, 2 * w)), full(w2cat.shape),
            full((1, 2 * w)), full(a2cat.shape), full((1, w)), full((1, w)), full((1, w)),
            full(cm.shape),
        ],
        out_specs=[o[0] for o in outs],
        out_shape=[o[1] for o in outs],
        compiler_params=_cparams("arbitrary", "arbitrary"),
        name="rwkv_prep",
    )(p_rwkv, p_rwkv, p_rwkv, row(mu_prev), row(mu_next), row(w0), w2cat, row(a0), a2cat,
      row(k_k), row(k_a), row(r_k), cm)
    v, bonus = res[0], res[1]
    return v, bonus, res[2:9], res[9:16]


def _rwkv_masks(reverse):
    n = GROUP_HEADS * CHUNK
    i = np.arange(n)[:, None]
    j = np.arange(n)[None, :]
    same_head = (i // CHUNK) == (j // CHUNK)
    before = (j > i) if reverse else (j < i)
    strict = same_head & before
    incl = same_head & (before | (i == j))
    levels = []
    size = 1
    while size < CHUNK:
        levels.append(same_head & ((i // (2 * size)) == (j // (2 * size))) & ((i // size) != (j // size)))
        size *= 2
    head_lane = (i // CHUNK) == (j // RWKV_HEAD)
    tri = np.stack([strict, incl]).astype(np.float32)
    return tri, np.stack(levels).astype(np.float32), head_lane.astype(np.float32)


def _scan_chunk_index(c, nch, reverse):
    if reverse:
        return nch - 1 - c
    n_ctx = TOK_BLOCK // CHUNK
    return jnp.where(c < n_ctx, nch - n_ctx + c, c - n_ctx)


def _split3(x):
    x1 = x.astype(BF16)
    r1 = x - x1.astype(F32)
    x2 = r1.astype(BF16)
    x3 = (r1 - x2.astype(F32)).astype(BF16)
    return jnp.concatenate([x1, x2, x3], axis=0)


def _hgrn_chunk_consts(reverse):
    t = np.arange(CHUNK)[:, None]
    s = np.arange(CHUNK)[None, :]
    order = (s >= t) if reverse else (s <= t)
    earlier_sub = ((s // SUB) > (t // SUB)) if reverse else ((s // SUB) < (t // SUB))
    both = np.concatenate([order, earlier_sub], axis=0).astype(np.float32)
    return np.concatenate([both] * 3, axis=1)


def _rwkv_chunk_stages(rh_ref, kh_ref, bh_ref, kap_ref, kg_ref, bg_ref, v_ref, gl_ref, tri_ref, lvl_ref, hm_ref,
                       y_ref, s_ref):
    hm = hm_ref[...]
    strict = tri_ref[0]
    incl = tri_ref[1]
    eye = incl - strict
    n_levels = lvl_ref.shape[0]
    groups = range(y_ref.shape[-1] // MXU_DIM)
    lanes = lambda ref, g: ref[0, :, g * MXU_DIM:(g + 1) * MXU_DIM]
    stack = lambda ref, g: jnp.concatenate([lanes(ref, g)] * GROUP_HEADS, axis=0) * hm

    kap, bh, a_ab = [], [], []
    for g in groups:
        kap.append(stack(kap_ref, g))
        bh.append(stack(bh_ref, g))
        a_ab.append(_nt(kap[g], bh[g]) * strict)
    yield
    kh = [stack(kh_ref, g) for g in groups]
    a_ak = [(_nt(kap[g], kh[g]) * strict).astype(BF16) for g in groups]
    rh = [stack(rh_ref, g) for g in groups]
    p_rb = [(_nt(rh[g], bh[g]) * incl).astype(BF16) for g in groups]
    p_rk = [(_nt(rh[g], kh[g]) * incl).astype(BF16) for g in groups]
    vs = [stack(v_ref, g) for g in groups]
    t = [eye - a_ab[g] * lvl_ref[0] for g in groups]
    for lv in range(1, n_levels):
        yield
        tb = [t[g].astype(BF16) for g in groups]
        x = [_mm(tb[g], (a_ab[g] * lvl_ref[lv]).astype(BF16)).astype(BF16) for g in groups]
        yield
        t = [t[g] - _mm(x[g], tb[g]) for g in groups]
    yield
    s0 = [s_ref[g] for g in groups]
    s0b = [s0[g].astype(BF16) for g in groups]
    hm32 = hm.astype(F32)
    ks0 = [jnp.concatenate([_nt(lanes(kap_ref, g), s0b[g])] * GROUP_HEADS, axis=0) * hm32 for g in groups]
    wmat = [(ks0[g] + _mm(a_ak[g], vs[g])).astype(BF16) for g in groups]
    yield
    ub = [(-_mm(t[g].astype(BF16), wmat[g])).astype(BF16) for g in groups]
    yield
    ys = [_mm(p_rb[g], ub[g]) + _mm(p_rk[g], vs[g]) for g in groups]
    for g in groups:
        y = _nt(lanes(rh_ref, g), s0b[g])
        for h in range(GROUP_HEADS):
            y = y + ys[g][h * CHUNK:(h + 1) * CHUNK]
        y_ref[0, :, g * MXU_DIM:(g + 1) * MXU_DIM] = y.astype(y_ref.dtype)
    yield
    for g in groups:
        gam = gl_ref[0, 0][:, g * MXU_DIM:(g + 1) * MXU_DIM]
        s_ref[g] = s0[g] * gam + _tn(ub[g], stack(bg_ref, g)) + _tn(vs[g], stack(kg_ref, g))


def _hgrn_chunk_stages(reverse, q_ref, f_ref, i_ref, lb_ref, cm_ref, o_ref, s_ref, g_scr, q_scr):
    n_sub = CHUNK // SUB
    n_heads = q_ref.shape[-1] // HGRN_HEAD
    sub_order = list(range(n_sub))[::-1] if reverse else list(range(n_sub))
    row_sub = lax.broadcasted_iota(jnp.int32, (CHUNK, 1), 0) // SUB
    srow = lax.broadcasted_iota(jnp.int32, (SUB, 1), 0)
    lane = lax.broadcasted_iota(jnp.int32, (1, LANES), 1)
    last_row = 0 if reverse else CHUNK - 1

    kgate = (1.0 - lb_ref[...]) * _sigmoid(-f_ref[0])
    sums = _mm(cm_ref[...], _split3(jnp.log1p(-kgate) * LOG2E))
    g_all = sums[:CHUNK]
    gb_all = sums[CHUNK:]
    q_all = _silu(q_ref[0])
    g_scr[...] = g_all
    q_scr[...] = q_all
    yield
    for hg in range(n_heads // HGRN_PAR):
        par = range(HGRN_PAR)
        hs = [hg * HGRN_PAR + j for j in par]
        cols = [slice(h * HGRN_HEAD, (h + 1) * HGRN_HEAD) for h in hs]
        g = [g_all[:, cs] for cs in cols]
        gb = [gb_all[:, cs] for cs in cols]
        q = [q_all[:, cs] for cs in cols]
        k = [kgate[:, cs] for cs in cols]
        vb = [i_ref[0, :, cs].astype(BF16) for cs in cols]
        s0 = [s_ref[h] for h in hs]
        q_in = [q[j] * jnp.exp2(g[j] - gb[j]) for j in par]
        at = []
        for j in par:
            kts, qts = [], []
            for si in sub_order[1:]:
                gb_i = gb[j][si * SUB:si * SUB + 1, :]
                earlier = (row_sub > si) if reverse else (row_sub < si)
                kts.append(jnp.where(earlier, k[j] * jnp.exp2(jnp.minimum(gb_i - g[j], 0.0)), 0.0).astype(BF16))
                qts.append(jnp.where(row_sub == si, q_in[j], 0.0).astype(BF16))
            at.append(_nt(jnp.concatenate(kts, axis=-1), jnp.concatenate(qts, axis=-1)))
        yield
        diag = [[] for _ in par]
        for si in range(n_sub):
            blk = slice(si * SUB, (si + 1) * SUB)
            acc = [jnp.zeros((SUB, LANES), F32) for _ in par]
            for tt in range(SUB):
                t = si * SUB + tt
                valid = jnp.logical_and(lane == t, (srow >= tt) if reverse else (srow <= tt))
                for j in par:
                    e = jnp.exp2(jnp.minimum(g_scr[t:t + 1, cols[j]] - g[j][blk], 0.0))
                    col = jnp.sum(k[j][blk] * e * q_scr[t:t + 1, cols[j]], axis=-1, keepdims=True)
                    acc[j] = jnp.where(valid, col, acc[j])
                if tt % 8 == 7:
                    yield
            for j in par:
                diag[j].append(acc[j])
        for j in par:
            a = (at[j] + jnp.concatenate(diag[j], axis=0)[:, :CHUNK]).astype(BF16)
            o = _tn(a, vb[j]) + _nt((q[j] * jnp.exp2(g[j])).astype(BF16), s0[j].astype(BF16))
            o_ref[0, :, cols[j]] = o.astype(o_ref.dtype)
            g_last = g[j][last_row:last_row + 1, :]
            k_out = (k[j] * jnp.exp2(g_last - g[j])).astype(BF16)
            s_ref[hs[j]] = s0[j] * jnp.exp2(g_last) + _tn(vb[j], k_out)
        yield


def _mix_scan_kernel(reverse, *refs):
    rwkv_in, hgrn_in = refs[:11], refs[11:16]
    y_ref, o_ref, s_ref, hs_ref, g_scr, q_scr = refs[16:]

    @pl.when(pl.program_id(1) == 0)
    def _():
        s_ref[...] = jnp.zeros_like(s_ref)
        hs_ref[...] = jnp.zeros_like(hs_ref)

    streams = [_rwkv_chunk_stages(*rwkv_in, y_ref, s_ref),
               _hgrn_chunk_stages(reverse, *hgrn_in, o_ref, hs_ref, g_scr, q_scr)]
    done = object()
    while streams:
        streams = [st for st in streams if next(st, done) is not done]


def _mix_scan(dir_ops, v, p_hgrn, lb_d, d, reverse):
    rh, kh, bh, kap, kg, bg, gl = dir_ops
    b, lt, w = v.shape
    nch = lt // CHUNK
    n = GROUP_HEADS * CHUNK
    tri, lvl, hm = _rwkv_masks(reverse)
    cm = jnp.asarray(_hgrn_chunk_consts(reverse), dtype=BF16)
    cidx = lambda c: _scan_chunk_index(c, nch, reverse)
    tok = pl.BlockSpec((1, CHUNK, w), lambda bb, c: (bb, cidx(c), 0))
    col = lambda j: pl.BlockSpec((1, CHUNK, w), lambda bb, c: (bb, cidx(c), j))
    full = lambda shape: pl.BlockSpec(shape, lambda bb, c: (0,) * len(shape))
    return pl.pallas_call(
        functools.partial(_mix_scan_kernel, reverse),
        grid=(b, nch),
        in_specs=[tok] * 7 + [
            pl.BlockSpec((1, 1, 1, w), lambda bb, c: (bb, cidx(c), 0, 0)),
            full(tri.shape), full(lvl.shape), full(hm.shape),
            col(0), col(1 + d), col(3), full((1, w)), full(cm.shape),
        ],
        out_specs=[tok, tok],
        out_shape=[jax.ShapeDtypeStruct((b, lt, w), BF16)] * 2,
        scratch_shapes=[pltpu.VMEM((w // MXU_DIM, n, n), F32),
                        pltpu.VMEM((w // HGRN_HEAD, HGRN_HEAD, HGRN_HEAD), F32),
                        pltpu.VMEM((CHUNK, w), F32), pltpu.VMEM((CHUNK, w), F32)],
        compiler_params=_cparams("arbitrary", "arbitrary"),
        name="mix_scan_rev" if reverse else "mix_scan_fwd",
    )(rh, kh, bh, kap, kg, bg, v, gl, jnp.asarray(tri), jnp.asarray(lvl), jnp.asarray(hm, dtype=BF16),
      p_hgrn, p_hgrn, p_hgrn, lb_d.reshape(1, w), cm)


def _residual_ln(alpha, x, gm, proj, lng, lnb):
    z = alpha * x + gm * proj
    mu = jnp.mean(z, axis=-1, keepdims=True)
    zc = z - mu
    var = jnp.mean(zc * zc, axis=-1, keepdims=True)
    return zc * lax.rsqrt(var + LN_EPS) * lng + lnb


def _even_out_kernel(alpha, ya0_ref, ya1_ref, bonus_ref, yb0_ref, yb1_ref, gate_ref, x_ref, gm_ref, w_ref,
                     lnxg_ref, lnxb_ref, ng_ref, lng_ref, lnb_ref, o_ref):
    y = ya0_ref[0].astype(F32) + ya1_ref[0].astype(F32)
    mu = _group64_sum(y) * (1.0 / RWKV_HEAD)
    yc = y - mu
    var = _group64_sum(yc * yc) * (1.0 / RWKV_HEAD)
    ya = yc * lax.rsqrt(var + RWKV_GN_EPS) * lnxg_ref[...] + lnxb_ref[...] + bonus_ref[0].astype(F32)
    o = yb0_ref[0].astype(F32) + yb1_ref[0].astype(F32)
    yb = o * lax.rsqrt(_group128_mean(o * o) + LN_EPS) * ng_ref[...]
    ycat = jnp.concatenate([ya, yb], axis=-1) * _silu(gate_ref[0].astype(F32))
    proj = _mm(ycat.astype(BF16), w_ref[...])
    o_ref[0] = _residual_ln(alpha, x_ref[0], gm_ref[0, 0], proj, lng_ref[...], lnb_ref[...])


def _odd_out_kernel(alpha, y_ref, x_ref, gm_ref, w_ref, lng_ref, lnb_ref, o_ref):
    proj = _mm(y_ref[0].astype(BF16), w_ref[...])
    o_ref[0] = _residual_ln(alpha, x_ref[0], gm_ref[0, 0], proj, lng_ref[...], lnb_ref[...])


def _tok_spec(n):
    return pl.BlockSpec((1, TOK_BLOCK, n), lambda bb, i: (bb, i, 0))


def _full2(shape):
    return pl.BlockSpec(shape, lambda bb, i: (0,) * len(shape))


def _seg_spec(d, nblk):
    return pl.BlockSpec((1, 1, 1, d), lambda bb, i: (bb, _seg(i, nblk), 0, 0))


def _even_out(alpha, ya0, ya1, bonus, yb0, yb1, gate, xs, gmod, w_out, lnx_g, lnx_b, norm_g, ln_g, ln_b):
    b, lt, d = xs.shape
    w = ya0.shape[-1]
    di = gate.shape[-1]
    row = lambda u: u.reshape(1, -1)
    return pl.pallas_call(
        functools.partial(_even_out_kernel, alpha),
        grid=(b, lt // TOK_BLOCK),
        in_specs=[_tok_spec(w)] * 5 + [_tok_spec(di), _tok_spec(d), _seg_spec(d, lt // TOK_BLOCK), _full2((di, d)),
                                       _full2((1, w)), _full2((1, w)), _full2((1, w)), _full2((1, d)), _full2((1, d))],
        out_specs=_tok_spec(d),
        out_shape=jax.ShapeDtypeStruct((b, lt, d), F32),
        compiler_params=_cparams("arbitrary", "arbitrary"),
        name="even_out",
    )(ya0, ya1, bonus, yb0, yb1, gate, xs, gmod, w_out, row(lnx_g), row(lnx_b),
      row(jnp.tile(norm_g, w // HGRN_HEAD)), row(ln_g), row(ln_b))


def _odd_out(alpha, y, xs, gmod, w_out, ln_g, ln_b, rows_out):
    b, lt, d = xs.shape
    di = y.shape[-1]
    row = lambda u: u.reshape(1, -1)
    return pl.pallas_call(
        functools.partial(_odd_out_kernel, alpha),
        grid=(b, rows_out // TOK_BLOCK),
        in_specs=[_tok_spec(di), _tok_spec(d), _seg_spec(d, lt // TOK_BLOCK), _full2((di, d)), _full2((1, d)),
                  _full2((1, d))],
        out_specs=_tok_spec(d),
        out_shape=jax.ShapeDtypeStruct((b, rows_out, d), F32),
        compiler_params=_cparams("arbitrary", "arbitrary"),
        name="odd_out",
    )(y, xs, gmod, w_out, row(ln_g), row(ln_b))


def _rope(x, cos, sin):
    width = x.shape[-1]
    lane = lax.broadcasted_iota(jnp.int32, (1, width), 1)
    first = (lane % 32) < 16
    partner = jnp.where(first, pltpu.roll(x, width - 16, 1), pltpu.roll(x, 16, 1))
    return x * cos + partner * sin


def _attn_kernel(lam_init, tk, n_full, tail, lam_ref, q_ref, k_ref, vt_ref, g_ref, sg_ref, *rest):
    y_ref, m_scr, acc_scr = rest[-3:]
    tq = q_ref.shape[1]
    ones_rows = 16
    tile = min(ATTN_TILE, 2 * tq)
    n_tiles = 2 * tq // tile
    q = q_ref[0].astype(F32) * (DIFF_HEAD ** -0.5 * float(np.log2(np.e)))
    lane = lax.broadcasted_iota(jnp.int32, (1, LANES), 1)
    q_cat = jnp.concatenate([jnp.where(lane < DIFF_HEAD, q, 0.0), jnp.where(lane >= DIFF_HEAD, q, 0.0)],
                            axis=0).astype(BF16)
    m_scr[...] = jnp.full(m_scr.shape, -jnp.inf, F32)
    acc_scr[...] = jnp.zeros(acc_scr.shape, F32)

    def scores(item):
        start, size, c = item
        return _nt(k_ref[0, pl.ds(start, size), :], q_cat[c * tile:(c + 1) * tile])

    def run(blocks):
        cols = [slice(c * tile, (c + 1) * tile) for c in range(n_tiles)]
        m_run = [m_scr[:, cs] for cs in cols]
        acc = [acc_scr[:, cs] for cs in cols]
        vta = None
        pending = None
        items = [(start, size, c) for start, size in blocks for c in range(n_tiles)]
        st_next = scores(items[0])
        for i, (start, size, c) in enumerate(items):
            st = st_next
            if i + 1 < len(items):
                st_next = scores(items[i + 1])
            if pending is not None:
                pc, pcorr, pvta, ppt = pending
                acc[pc] = acc[pc] * pcorr + _mm(pvta, ppt)
            if c == 0:
                vta = jnp.concatenate([vt_ref[0, :, pl.ds(start, size)], jnp.ones((ones_rows, size), BF16)], axis=0)
            m_new = jnp.maximum(m_run[c], jnp.max(st, axis=0, keepdims=True))
            pending = (c, jnp.exp2(m_run[c] - m_new), vta, jnp.exp2(st - m_new).astype(BF16))
            m_run[c] = m_new
        pc, pcorr, pvta, ppt = pending
        acc[pc] = acc[pc] * pcorr + _mm(pvta, ppt)
        for c, cs in enumerate(cols):
            m_scr[:, cs] = m_run[c]
            acc_scr[:, cs] = acc[c]

    tail_block = [(n_full * tk, tail)]
    if n_full <= ATTN_UNROLL:
        run([(u * tk, tk) for u in range(n_full)] + tail_block)
    else:
        assert n_full % ATTN_UNROLL == 0
        def body(j, carry):
            run([(pl.multiple_of((j * ATTN_UNROLL + u) * tk, tk), tk) for u in range(ATTN_UNROLL)])
            return carry
        lax.fori_loop(0, n_full // ATTN_UNROLL, body, 0)
        run(tail_block)

    a0 = acc_scr[:, :tq]
    a1 = acc_scr[:, tq:]
    o_t = a0[:LANES] / a0[LANES:LANES + 1] - lam_ref[0] * (a1[:LANES] / a1[LANES:LANES + 1])
    o = jnp.transpose(o_t)
    y = o * lax.rsqrt(jnp.mean(o * o, axis=-1, keepdims=True) + LN_EPS) * sg_ref[...] * (1.0 - lam_init)
    y_ref[0] = (y * _silu(g_ref[0].astype(F32))).astype(y_ref.dtype)


def _diff_attention(p_qk, p_vt, p_g, lam, subln_g, lam_init):
    b, lt, di = p_g.shape
    nh = di // LANES
    t = lt - TOK_BLOCK
    tq = tk = 512
    assert t % tq == 0
    ctx_blk = t // TOK_BLOCK
    sm = pl.BlockSpec(memory_space=pltpu.SMEM)
    sg_spec = pl.BlockSpec((1, LANES), lambda bb, h, i: (0, 0))
    scratch = lambda n: [pltpu.VMEM((1, 2 * n), F32), pltpu.VMEM((LANES + 16, 2 * n), F32)]
    args = (lam.reshape(1), p_qk, p_qk, p_vt, p_g, subln_g.reshape(1, LANES))
    y = pl.pallas_call(
        functools.partial(_attn_kernel, lam_init, tk, t // tk, TOK_BLOCK),
        grid=(b, nh, t // tq),
        in_specs=[sm,
                  pl.BlockSpec((1, tq, LANES), lambda bb, h, i: (bb, i, h)),
                  pl.BlockSpec((1, lt, LANES), lambda bb, h, i: (bb, 0, nh + h)),
                  pl.BlockSpec((1, LANES, lt), lambda bb, h, i: (bb, h, 0)),
                  pl.BlockSpec((1, tq, LANES), lambda bb, h, i: (bb, i, h)),
                  sg_spec],
        out_specs=pl.BlockSpec((1, tq, LANES), lambda bb, h, i: (bb, i, h)),
        out_shape=jax.ShapeDtypeStruct((b, lt, di), BF16),
        scratch_shapes=scratch(tq),
        compiler_params=_cparams("arbitrary", "arbitrary", "arbitrary"),
        name="diff_attn",
    )(*args)
    blk = lambda bb, h, i: (bb, ctx_blk, h)
    return pl.pallas_call(
        functools.partial(_attn_kernel, lam_init, tk, 0, TOK_BLOCK),
        grid=(b, nh, 1),
        in_specs=[sm,
                  pl.BlockSpec((1, TOK_BLOCK, LANES), blk),
                  pl.BlockSpec((1, TOK_BLOCK, LANES), lambda bb, h, i: (bb, ctx_blk, nh + h)),
                  pl.BlockSpec((1, LANES, TOK_BLOCK), lambda bb, h, i: (bb, h, ctx_blk)),
                  pl.BlockSpec((1, TOK_BLOCK, LANES), blk),
                  sg_spec,
                  pl.BlockSpec(memory_space=pl.ANY)],
        out_specs=pl.BlockSpec((1, TOK_BLOCK, LANES), blk),
        out_shape=jax.ShapeDtypeStruct((b, lt, di), BF16),
        scratch_shapes=scratch(TOK_BLOCK),
        input_output_aliases={6: 0},
        compiler_params=_cparams("arbitrary", "arbitrary", "arbitrary"),
        name="diff_attn_ctx",
    )(*args, y)


def _rope_tables(n_ctx, t):
    quarter = DIFF_HEAD // 4
    inv = ROPE_BASE ** (-jnp.arange(quarter, dtype=F32) / quarter)
    pos = jnp.arange(t)
    rows = (pos // GRID_W).astype(F32)[:, None] * inv
    cols = (pos % GRID_W).astype(F32)[:, None] * inv
    cos64 = jnp.concatenate([jnp.cos(rows), jnp.cos(rows), jnp.cos(cols), jnp.cos(cols)], -1)
    sin64 = jnp.concatenate([-jnp.sin(rows), jnp.sin(rows), -jnp.sin(cols), jnp.sin(cols)], -1)
    cos = jnp.concatenate([jnp.tile(cos64, (1, 2)), jnp.ones((n_ctx, LANES), F32)], 0)
    sin = jnp.concatenate([jnp.tile(sin64, (1, 2)), jnp.zeros((n_ctx, LANES), F32)], 0)
    return cos, sin


def _even_layer(alpha, xs, scale, shift, gmod, w_in, w_out, mu_prev, mu_next, w0, w2, a0, a2, k_k, k_a, r_k,
                lnx_g, lnx_b, lb, norm_g, ln_g, ln_b):
    w = k_k.shape[-1]
    sw = mu_prev.shape[-1]
    hw = 4 * lb.shape[-1]
    w_in = w_in.astype(BF16)
    p_rwkv = _project(xs, scale, shift, w_in[:, :sw], sw, F32)
    p_hgrn = _project(xs, scale, shift, w_in[:, sw:sw + hw], hw // 2, F32)
    gate = _project(xs, scale, shift, w_in[:, sw + hw:], w_in.shape[1] - sw - hw, BF16)
    v, bonus, ops_f, ops_r = _rwkv_prep(p_rwkv, mu_prev, mu_next, w0, w2, a0, a2, k_k, k_a, r_k)
    ya0, yb0 = _mix_scan(ops_f, v, p_hgrn, lb[0], 0, False)
    ya1, yb1 = _mix_scan(ops_r, v, p_hgrn, lb[1], 1, True)
    return _even_out(alpha, ya0, ya1, bonus, yb0, yb1, gate, xs, gmod, w_out.astype(BF16),
                     lnx_g, lnx_b, norm_g, ln_g, ln_b)


def _odd_layer(alpha, xs, scale, shift, gmod, w_in, w_out, lam_p, subln_g, lam_init, cos, sin, ln_g, ln_b,
               rows_out):
    di = w_out.shape[0]
    w_in = w_in.astype(BF16)
    p_qk = _project(xs, scale, shift, w_in[:, :2 * di], di, BF16, rope=(cos, sin))
    p_vt = _project_t(xs, scale, shift, jnp.transpose(w_in[:, 2 * di:3 * di]), BF16)
    p_g = _project(xs, scale, shift, w_in[:, 3 * di:], di, BF16)
    lam = jnp.exp(jnp.sum(lam_p[0] * lam_p[1])) - jnp.exp(jnp.sum(lam_p[2] * lam_p[3])) + lam_init
    y = _diff_attention(p_qk, p_vt, p_g, lam, subln_g, lam_init)
    return _odd_out(alpha, y, xs, gmod, w_out.astype(BF16), ln_g, ln_b, rows_out)


def kernel(x, c, ctx, c_ctx, ada_w, ada_b, ln_g, ln_b, even_w_in, even_w_out, rwkv_mu_prev, rwkv_mu_next, rwkv_w0, rwkv_w2, rwkv_a0, rwkv_a2, rwkv_k_k, rwkv_k_a, rwkv_r_k, rwkv_lnx_g, rwkv_lnx_b, hgrn_lb_logits, hgrn_norm_g, odd_w_in, odd_w_out, diff_lambda, diff_subln_g):
    b, t, d = x.shape
    n_ctx = ctx.shape[1]
    depth = ada_w.shape[0]
    assert n_ctx == TOK_BLOCK and t % TOK_BLOCK == 0 and b + 1 <= 8
    alpha = (2.0 * depth) ** 0.25

    xs = jnp.concatenate([x, ctx], axis=1)
    cvec = jnp.concatenate([c, c_ctx[None], jnp.zeros((8 - b - 1, d), F32)], axis=0)
    mods = _ada_mods(cvec, ada_w, ada_b)
    lb_all = jax.nn.softmax(hgrn_lb_logits.astype(F32), axis=0)
    lb_all = jnp.cumsum(lb_all, axis=0) - lb_all[0]
    cos, sin = _rope_tables(n_ctx, t)

    for layer in range(depth):
        m = mods[layer]
        per_seg = lambda u: jnp.stack([jnp.broadcast_to(u[b], (b, d)), u[:b]], axis=1)[:, :, None, :]
        shift, scale, gmod = (per_seg(m[:, j * d:(j + 1) * d]) for j in range(3))
        j = layer // 2
        if layer % 2 == 0:
            xs = _even_layer(alpha, xs, scale, shift, gmod, even_w_in[j], even_w_out[j], rwkv_mu_prev[j],
                             rwkv_mu_next[j], rwkv_w0[j], rwkv_w2[j], rwkv_a0[j], rwkv_a2[j], rwkv_k_k[j],
                             rwkv_k_a[j], rwkv_r_k[j], rwkv_lnx_g[j], rwkv_lnx_b[j], lb_all[j], hgrn_norm_g[j],
                             ln_g[layer], ln_b[layer])
        else:
            lam_init = 0.8 - 0.6 * float(np.exp(-0.3 * layer))
            xs = _odd_layer(alpha, xs, scale, shift, gmod, odd_w_in[j], odd_w_out[j], diff_lambda[j],
                            diff_subln_g[j], lam_init, cos, sin, ln_g[layer], ln_b[layer],
                            t if layer == depth - 1 else t + n_ctx)
    return xs[:, :t]
```

```python
import functools

import numpy as np
import jax
import jax.numpy as jnp
from jax import lax
from jax.experimental import pallas as pl
from jax.experimental.pallas import tpu as pltpu

F32 = jnp.float32
BF16 = jnp.bfloat16
HI = lax.Precision.HIGHEST

GRID_W = 64
RWKV_HEAD = 64
RWKV_LORA = 64
RWKV_GN_EPS = 64e-5
HGRN_HEAD = 128
DIFF_HEAD = 64
ROPE_BASE = 10000.0
LN_EPS = 1e-5
LOG2E = float(np.log2(np.e))

LANES = 128
MXU_DIM = 256
VMEM_LIMIT = 56 * 1024 * 1024

CHUNK = 64
SUB = 16
TOK_BLOCK = 256
PROJ_ROWS = 768
GROUP_HEADS = MXU_DIM // RWKV_HEAD
HGRN_PAR = 4
ATTN_UNROLL = 16
ATTN_TILE = 1024


def _cparams(*sem):
    return pltpu.CompilerParams(dimension_semantics=sem, vmem_limit_bytes=VMEM_LIMIT)


def _nt(a, b):
    return lax.dot_general(a, b, (((1,), (1,)), ((), ())), preferred_element_type=F32)


def _tn(a, b):
    return lax.dot_general(a, b, (((0,), (0,)), ((), ())), preferred_element_type=F32)


def _mm(a, b):
    return jnp.dot(a, b, preferred_element_type=F32)


def _mm_exact(a, b):
    return jnp.dot(a, b, preferred_element_type=F32, precision=HI)


def _sigmoid(x):
    return 1.0 / (1.0 + jnp.exp(-x))


def _silu(x):
    return x * _sigmoid(x)


def _lane_tiles(x):
    return [x[:, j * LANES:(j + 1) * LANES] for j in range(x.shape[-1] // LANES)]


def _group64_sum(x):
    lane = lax.broadcasted_iota(jnp.int32, (1, LANES), 1)
    low = lane < RWKV_HEAD
    out = []
    for xt in _lane_tiles(x):
        s_all = jnp.sum(xt, axis=-1, keepdims=True)
        s_lo = jnp.sum(jnp.where(low, xt, 0.0), axis=-1, keepdims=True)
        out.append(jnp.where(low, s_lo, s_all - s_lo))
    return jnp.concatenate(out, axis=-1)


def _group128_mean(x):
    out = []
    for xt in _lane_tiles(x):
        out.append(jnp.broadcast_to(jnp.mean(xt, axis=-1, keepdims=True), xt.shape))
    return jnp.concatenate(out, axis=-1)


def _ada_kernel(c_ref, w_ref, b_ref, o_ref):
    cond = _silu(c_ref[...])
    o_ref[0] = _mm_exact(cond, w_ref[0]) + b_ref[0]


def _ada_mods(cvec, ada_w, ada_b):
    depth, d, d3 = ada_w.shape
    tn = 1024
    return pl.pallas_call(
        _ada_kernel,
        grid=(depth, d3 // tn),
        in_specs=[
            pl.BlockSpec((8, d), lambda l, j: (0, 0)),
            pl.BlockSpec((1, d, tn), lambda l, j: (l, 0, j)),
            pl.BlockSpec((1, 1, tn), lambda l, j: (l, 0, j)),
        ],
        out_specs=pl.BlockSpec((1, 8, tn), lambda l, j: (l, 0, j)),
        out_shape=jax.ShapeDtypeStruct((depth, 8, d3), F32),
        compiler_params=_cparams("arbitrary", "arbitrary"),
        name="ada_mods",
    )(cvec, ada_w, ada_b.reshape(depth, 1, d3))


def _seg(i, nblk):
    return jnp.where(i == nblk - 1, 0, 1)


def _proj_rows(lt):
    return PROJ_ROWS if lt % PROJ_ROWS == 0 else TOK_BLOCK


def _modulate(n_lat, axis, x_ref, sc_ref, sh_ref):
    tm = x_ref.shape[1]
    rows = pl.program_id(axis) * tm + lax.broadcasted_iota(jnp.int32, (tm, 1), 0)
    is_ctx = rows >= n_lat
    sc = jnp.where(is_ctx, sc_ref[0, 0], sc_ref[0, 1])
    sh = jnp.where(is_ctx, sh_ref[0, 0], sh_ref[0, 1])
    return (x_ref[0] * (1.0 + sc) + sh).astype(BF16)


def _proj_kernel(n_lat, x_ref, sc_ref, sh_ref, w_ref, o_ref):
    o_ref[0] = _mm(_modulate(n_lat, 2, x_ref, sc_ref, sh_ref), w_ref[...]).astype(o_ref.dtype)


def _proj_rope_kernel(n_lat, x_ref, sc_ref, sh_ref, w_ref, cos_ref, sin_ref, o_ref):
    p = _mm(_modulate(n_lat, 2, x_ref, sc_ref, sh_ref), w_ref[...])
    reps = p.shape[-1] // LANES
    cos = jnp.concatenate([cos_ref[...]] * reps, axis=-1)
    sin = jnp.concatenate([sin_ref[...]] * reps, axis=-1)
    o_ref[0] = _rope(p, cos, sin).astype(o_ref.dtype)


def _proj_t_kernel(n_lat, x_ref, sc_ref, sh_ref, wt_ref, o_ref):
    o_ref[0] = _nt(wt_ref[...], _modulate(n_lat, 1, x_ref, sc_ref, sh_ref)).astype(o_ref.dtype)


def _project(xs, scale, shift, w, tn, out_dtype, rope=None):
    b, lt, d = xs.shape
    n = w.shape[1]
    tm = _proj_rows(lt)
    n_lat = lt - TOK_BLOCK
    in_specs = [
        pl.BlockSpec((1, tm, d), lambda j, bb, i: (bb, i, 0)),
        pl.BlockSpec((1, 2, 1, d), lambda j, bb, i: (bb, 0, 0, 0)),
        pl.BlockSpec((1, 2, 1, d), lambda j, bb, i: (bb, 0, 0, 0)),
        pl.BlockSpec((d, tn), lambda j, bb, i: (0, j)),
    ]
    args = (xs, scale, shift, w)
    if rope is not None:
        in_specs += [pl.BlockSpec((tm, LANES), lambda j, bb, i: (i, 0))] * 2
        args += tuple(rope)
    return pl.pallas_call(
        functools.partial(_proj_kernel if rope is None else _proj_rope_kernel, n_lat),
        grid=(n // tn, b, lt // tm),
        in_specs=in_specs,
        out_specs=pl.BlockSpec((1, tm, tn), lambda j, bb, i: (bb, i, j)),
        out_shape=jax.ShapeDtypeStruct((b, lt, n), out_dtype),
        compiler_params=_cparams("arbitrary", "arbitrary", "arbitrary"),
        name="mod_proj" if rope is None else "mod_proj_rope",
    )(*args)


def _project_t(xs, scale, shift, wt, out_dtype):
    b, lt, d = xs.shape
    n = wt.shape[0]
    tm = _proj_rows(lt)
    return pl.pallas_call(
        functools.partial(_proj_t_kernel, lt - TOK_BLOCK),
        grid=(b, lt // tm),
        in_specs=[
            pl.BlockSpec((1, tm, d), lambda bb, i: (bb, i, 0)),
            pl.BlockSpec((1, 2, 1, d), lambda bb, i: (bb, 0, 0, 0)),
            pl.BlockSpec((1, 2, 1, d), lambda bb, i: (bb, 0, 0, 0)),
            pl.BlockSpec((n, d), lambda bb, i: (0, 0)),
        ],
        out_specs=pl.BlockSpec((1, n, tm), lambda bb, i: (bb, 0, i)),
        out_shape=jax.ShapeDtypeStruct((b, n, lt), out_dtype),
        compiler_params=_cparams("arbitrary", "arbitrary"),
        name="mod_proj_t",
    )(xs, scale, shift, wt)


def _lora(x, w_ref):
    x1 = x.astype(BF16)
    x2 = (x - x1.astype(F32)).astype(BF16)
    return _mm(jnp.concatenate([x1, x1, x2], axis=-1), w_ref[...])


def _rwkv_prep_kernel(nblk, p_ref, prev_ref, next_ref, mup_ref, mun_ref, w0_ref, w2_ref, a0_ref, a2_ref,
                      kk_ref, ka_ref, rk_ref, cm_ref,
                      v_out, bonus_out, *dir_outs):
    i = pl.program_id(1)
    w = kk_ref.shape[-1]
    p = p_ref[0]
    row = lax.broadcasted_iota(jnp.int32, (TOK_BLOCK, 1), 0)
    prev_row = jnp.where(jnp.logical_and(i >= 1, i <= nblk - 2), prev_ref[0][7:8, :], 0.0)
    next_row = jnp.where(i <= nblk - 3, next_ref[0][0:1, :], 0.0)
    prev = jnp.where(row == 0, prev_row, pltpu.roll(p, 1, 0))
    nxt = jnp.where(row == TOK_BLOCK - 1, next_row, pltpu.roll(p, TOK_BLOCK - 1, 0))
    ps = p + mup_ref[...] * (prev - p) + mun_ref[...] * (nxt - p)

    r = ps[:, 0:w]
    k = ps[:, w:2 * w]
    v = ps[:, 2 * w:3 * w]
    wlo = ps[:, 3 * w:3 * w + 2 * RWKV_LORA]
    alo = ps[:, 3 * w + 2 * RWKV_LORA:3 * w + 4 * RWKV_LORA]

    z = _lora(jnp.tanh(wlo), w2_ref) + w0_ref[...]
    w_log = -(jnp.maximum(-z, 0.0) + jnp.log(1.0 + jnp.exp(-jnp.abs(z)))) - 0.5
    lw = jnp.exp(w_log) * (-LOG2E)
    a = _sigmoid(_lora(alo, a2_ref) + a0_ref[...])

    kk = k * kk_ref[...]
    kk = kk * lax.rsqrt(_group64_sum(kk * kk) + 1e-12)
    bonus_out[0] = (_group64_sum(r * k * rk_ref[...]) * v).astype(bonus_out.dtype)
    v_out[0] = v.astype(BF16)

    for d in range(2):
        rh_o, kh_o, bh_o, kap_o, kg_o, bg_o, gl_o = dir_outs[7 * d:7 * d + 7]
        lw_d = lw[:, d * w:(d + 1) * w]
        a_d = a[:, d * w:(d + 1) * w]
        kd = k * (1.0 + (a_d - 1.0) * ka_ref[...])
        b_d = kk * a_d
        sums = _mm(cm_ref[d], _split3(lw_d))
        g = sums[:TOK_BLOCK]
        g_all = sums[TOK_BLOCK:]
        e_neg = jnp.exp2(-g)
        e_rest = jnp.exp2(g_all - g)
        rh_o[0] = (r * jnp.exp2(g)).astype(BF16)
        kh_o[0] = (kd * e_neg).astype(BF16)
        bh_o[0] = (b_d * e_neg).astype(BF16)
        kap_o[0] = (kk * jnp.exp2(g - lw_d)).astype(BF16)
        kg_o[0] = (kd * e_rest).astype(BF16)
        bg_o[0] = (b_d * e_rest).astype(BF16)
        e_all = jnp.exp2(g_all)
        for c in range(TOK_BLOCK // CHUNK):
            gl_o[0, c] = e_all[c * CHUNK:c * CHUNK + 1, :]


def _rwkv_sum_consts(reverse):
    t = np.arange(TOK_BLOCK)[:, None]
    s = np.arange(TOK_BLOCK)[None, :]
    same = (t // CHUNK) == (s // CHUNK)
    order = (s >= t) if reverse else (s <= t)
    both = np.concatenate([same & order, same], axis=0).astype(np.float32)
    return np.concatenate([both] * 3, axis=1)


def _lora_pieces(m):
    hi = m.astype(BF16)
    lo = (m - hi.astype(F32)).astype(BF16)
    return jnp.concatenate([hi, lo, hi], axis=0)


def _rwkv_prep(p_rwkv, mu_prev, mu_next, w0, w2, a0, a2, k_k, k_a, r_k):
    b, lt, sw = p_rwkv.shape
    w = k_k.shape[-1]
    nblk = lt // TOK_BLOCK
    nch = lt // CHUNK
    halo = TOK_BLOCK // 8
    zl = jnp.zeros((RWKV_LORA, w), F32)
    w2cat = _lora_pieces(jnp.concatenate([jnp.concatenate([w2[0], zl], 1), jnp.concatenate([zl, w2[1]], 1)], 0))
    a2cat = _lora_pieces(jnp.concatenate([jnp.concatenate([a2[0], zl], 1), jnp.concatenate([zl, a2[1]], 1)], 0))
    cm = jnp.asarray(np.stack([_rwkv_sum_consts(False), _rwkv_sum_consts(True)]), dtype=BF16)
    row = lambda u: u.reshape(1, -1)
    full = lambda shape: pl.BlockSpec(shape, lambda bb, i: (0,) * len(shape))
    tok = lambda n, dt: (pl.BlockSpec((1, TOK_BLOCK, n), lambda bb, i: (bb, i, 0)), jax.ShapeDtypeStruct((b, lt, n), dt))
    gl = (pl.BlockSpec((1, TOK_BLOCK // CHUNK, 1, w), lambda bb, i: (bb, i, 0, 0)),
          jax.ShapeDtypeStruct((b, nch, 1, w), F32))
    outs = [tok(w, BF16), tok(w, BF16)] + 2 * ([tok(w, BF16)] * 6 + [gl])
    res = pl.pallas_call(
        functools.partial(_rwkv_prep_kernel, nblk),
        grid=(b, nblk),
        in_specs=[
            pl.BlockSpec((1, TOK_BLOCK, sw), lambda bb, i: (bb, i, 0)),
            pl.BlockSpec((1, 8, sw), lambda bb, i: (bb, jnp.maximum(i * halo - 1, 0), 0)),
            pl.BlockSpec((1, 8, sw), lambda bb, i: (bb, jnp.minimum((i + 1) * halo, lt // 8 - 1), 0)),
            full((1, sw)), full((1, sw)), full((1, 2 * w)), full(w2cat.shape),
            full((1, 2 * w)), full(a2cat.shape), full((1, w)), full((1, w)), full((1, w)),
            full(cm.shape),
        ],
        out_specs=[o[0] for o in outs],
        out_shape=[o[1] for o in outs],
        compiler_params=_cparams("arbitrary", "arbitrary"),
        name="rwkv_prep",
    )(p_rwkv, p_rwkv, p_rwkv, row(mu_prev), row(mu_next), row(w0), w2cat, row(a0), a2cat,
      row(k_k), row(k_a), row(r_k), cm)
    v, bonus = res[0], res[1]
    return v, bonus, res[2:9], res[9:16]


def _rwkv_masks(reverse):
    n = GROUP_HEADS * CHUNK
    i = np.arange(n)[:, None]
    j = np.arange(n)[None, :]
    same_head = (i // CHUNK) == (j // CHUNK)
    before = (j > i) if reverse else (j < i)
    strict = same_head & before
    incl = same_head & (before | (i == j))
    levels = []
    size = 1
    while size < CHUNK:
        levels.append(same_head & ((i // (2 * size)) == (j // (2 * size))) & ((i // size) != (j // size)))
        size *= 2
    head_lane = (i // CHUNK) == (j // RWKV_HEAD)
    tri = np.stack([strict, incl]).astype(np.float32)
    return tri, np.stack(levels).astype(np.float32), head_lane.astype(np.float32)


def _scan_chunk_index(c, nch, reverse):
    if reverse:
        return nch - 1 - c
    n_ctx = TOK_BLOCK // CHUNK
    return jnp.where(c < n_ctx, nch - n_ctx + c, c - n_ctx)


def _split3(x):
    x1 = x.astype(BF16)
    r1 = x - x1.astype(F32)
    x2 = r1.astype(BF16)
    x3 = (r1 - x2.astype(F32)).astype(BF16)
    return jnp.concatenate([x1, x2, x3], axis=0)


def _hgrn_chunk_consts(reverse):
    t = np.arange(CHUNK)[:, None]
    s = np.arange(CHUNK)[None, :]
    order = (s >= t) if reverse else (s <= t)
    earlier_sub = ((s // SUB) > (t // SUB)) if reverse else ((s // SUB) < (t // SUB))
    both = np.concatenate([order, earlier_sub], axis=0).astype(np.float32)
    return np.concatenate([both] * 3, axis=1)


def _rwkv_chunk_stages(rh_ref, kh_ref, bh_ref, kap_ref, kg_ref, bg_ref, v_ref, gl_ref, tri_ref, lvl_ref, hm_ref,
                       y_ref, s_ref):
    hm = hm_ref[...]
    strict = tri_ref[0]
    incl = tri_ref[1]
    eye = incl - strict
    n_levels = lvl_ref.shape[0]
    groups = range(y_ref.shape[-1] // MXU_DIM)
    lanes = lambda ref, g: ref[0, :, g * MXU_DIM:(g + 1) * MXU_DIM]
    stack = lambda ref, g: jnp.concatenate([lanes(ref, g)] * GROUP_HEADS, axis=0) * hm

    kap, bh, a_ab = [], [], []
    for g in groups:
        kap.append(stack(kap_ref, g))
        bh.append(stack(bh_ref, g))
        a_ab.append(_nt(kap[g], bh[g]) * strict)
    yield
    kh = [stack(kh_ref, g) for g in groups]
    a_ak = [(_nt(kap[g], kh[g]) * strict).astype(BF16) for g in groups]
    rh = [stack(rh_ref, g) for g in groups]
    p_rb = [(_nt(rh[g], bh[g]) * incl).astype(BF16) for g in groups]
    p_rk = [(_nt(rh[g], kh[g]) * incl).astype(BF16) for g in groups]
    vs = [stack(v_ref, g) for g in groups]
    t = [eye - a_ab[g] * lvl_ref[0] for g in groups]
    for lv in range(1, n_levels):
        yield
        tb = [t[g].astype(BF16) for g in groups]
        x = [_mm(tb[g], (a_ab[g] * lvl_ref[lv]).astype(BF16)).astype(BF16) for g in groups]
        yield
        t = [t[g] - _mm(x[g], tb[g]) for g in groups]
    yield
    s0 = [s_ref[g] for g in groups]
    s0b = [s0[g].astype(BF16) for g in groups]
    hm32 = hm.astype(F32)
    ks0 = [jnp.concatenate([_nt(lanes(kap_ref, g), s0b[g])] * GROUP_HEADS, axis=0) * hm32 for g in groups]
    wmat = [(ks0[g] + _mm(a_ak[g], vs[g])).astype(BF16) for g in groups]
    yield
    ub = [(-_mm(t[g].astype(BF16), wmat[g])).astype(BF16) for g in groups]
    yield
    ys = [_mm(p_rb[g], ub[g]) + _mm(p_rk[g], vs[g]) for g in groups]
    for g in groups:
        y = _nt(lanes(rh_ref, g), s0b[g])
        for h in range(GROUP_HEADS):
            y = y + ys[g][h * CHUNK:(h + 1) * CHUNK]
        y_ref[0, :, g * MXU_DIM:(g + 1) * MXU_DIM] = y.astype(y_ref.dtype)
    yield
    for g in groups:
        gam = gl_ref[0, 0][:, g * MXU_DIM:(g + 1) * MXU_DIM]
        s_ref[g] = s0[g] * gam + _tn(ub[g], stack(bg_ref, g)) + _tn(vs[g], stack(kg_ref, g))


def _hgrn_chunk_stages(reverse, q_ref, f_ref, i_ref, lb_ref, cm_ref, o_ref, s_ref, g_scr, q_scr):
    n_sub = CHUNK // SUB
    n_heads = q_ref.shape[-1] // HGRN_HEAD
    sub_order = list(range(n_sub))[::-1] if reverse else list(range(n_sub))
    row_sub = lax.broadcasted_iota(jnp.int32, (CHUNK, 1), 0) // SUB
    srow = lax.broadcasted_iota(jnp.int32, (SUB, 1), 0)
    lane = lax.broadcasted_iota(jnp.int32, (1, LANES), 1)
    last_row = 0 if reverse else CHUNK - 1

    kgate = (1.0 - lb_ref[...]) * _sigmoid(-f_ref[0])
    sums = _mm(cm_ref[...], _split3(jnp.log1p(-kgate) * LOG2E))
    g_all = sums[:CHUNK]
    gb_all = sums[CHUNK:]
    q_all = _silu(q_ref[0])
    g_scr[...] = g_all
    q_scr[...] = q_all
    yield
    for hg in range(n_heads // HGRN_PAR):
        par = range(HGRN_PAR)
        hs = [hg * HGRN_PAR + j for j in par]
        cols = [slice(h * HGRN_HEAD, (h + 1) * HGRN_HEAD) for h in hs]
        g = [g_all[:, cs] for cs in cols]
        gb = [gb_all[:, cs] for cs in cols]
        q = [q_all[:, cs] for cs in cols]
        k = [kgate[:, cs] for cs in cols]
        vb = [i_ref[0, :, cs].astype(BF16) for cs in cols]
        s0 = [s_ref[h] for h in hs]
        q_in = [q[j] * jnp.exp2(g[j] - gb[j]) for j in par]
        at = []
        for j in par:
            kts, qts = [], []
            for si in sub_order[1:]:
                gb_i = gb[j][si * SUB:si * SUB + 1, :]
                earlier = (row_sub > si) if reverse else (row_sub < si)
                kts.append(jnp.where(earlier, k[j] * jnp.exp2(jnp.minimum(gb_i - g[j], 0.0)), 0.0).astype(BF16))
                qts.append(jnp.where(row_sub == si, q_in[j], 0.0).astype(BF16))
            at.append(_nt(jnp.concatenate(kts, axis=-1), jnp.concatenate(qts, axis=-1)))
        yield
        diag = [[] for _ in par]
        for si in range(n_sub):
            blk = slice(si * SUB, (si + 1) * SUB)
            acc = [jnp.zeros((SUB, LANES), F32) for _ in par]
            for tt in range(SUB):
                t = si * SUB + tt
                valid = jnp.logical_and(lane == t, (srow >= tt) if reverse else (srow <= tt))
                for j in par:
                    e = jnp.exp2(jnp.minimum(g_scr[t:t + 1, cols[j]] - g[j][blk], 0.0))
                    col = jnp.sum(k[j][blk] * e * q_scr[t:t + 1, cols[j]], axis=-1, keepdims=True)
                    acc[j] = jnp.where(valid, col, acc[j])
                if tt % 8 == 7:
                    yield
            for j in par:
                diag[j].append(acc[j])
        for j in par:
            a = (at[j] + jnp.concatenate(diag[j], axis=0)[:, :CHUNK]).astype(BF16)
            o = _tn(a, vb[j]) + _nt((q[j] * jnp.exp2(g[j])).astype(BF16), s0[j].astype(BF16))
            o_ref[0, :, cols[j]] = o.astype(o_ref.dtype)
            g_last = g[j][last_row:last_row + 1, :]
            k_out = (k[j] * jnp.exp2(g_last - g[j])).astype(BF16)
            s_ref[hs[j]] = s0[j] * jnp.exp2(g_last) + _tn(vb[j], k_out)
        yield


def _mix_scan_kernel(reverse, *refs):
    rwkv_in, hgrn_in = refs[:11], refs[11:16]
    y_ref, o_ref, s_ref, hs_ref, g_scr, q_scr = refs[16:]

    @pl.when(pl.program_id(1) == 0)
    def _():
        s_ref[...] = jnp.zeros_like(s_ref)
        hs_ref[...] = jnp.zeros_like(hs_ref)

    streams = [_rwkv_chunk_stages(*rwkv_in, y_ref, s_ref),
               _hgrn_chunk_stages(reverse, *hgrn_in, o_ref, hs_ref, g_scr, q_scr)]
    done = object()
    while streams:
        streams = [st for st in streams if next(st, done) is not done]


def _mix_scan(dir_ops, v, p_hgrn, lb_d, d, reverse):
    rh, kh, bh, kap, kg, bg, gl = dir_ops
    b, lt, w = v.shape
    nch = lt // CHUNK
    n = GROUP_HEADS * CHUNK
    tri, lvl, hm = _rwkv_masks(reverse)
    cm = jnp.asarray(_hgrn_chunk_consts(reverse), dtype=BF16)
    cidx = lambda c: _scan_chunk_index(c, nch, reverse)
    tok = pl.BlockSpec((1, CHUNK, w), lambda bb, c: (bb, cidx(c), 0))
    col = lambda j: pl.BlockSpec((1, CHUNK, w), lambda bb, c: (bb, cidx(c), j))
    full = lambda shape: pl.BlockSpec(shape, lambda bb, c: (0,) * len(shape))
    return pl.pallas_call(
        functools.partial(_mix_scan_kernel, reverse),
        grid=(b, nch),
        in_specs=[tok] * 7 + [
            pl.BlockSpec((1, 1, 1, w), lambda bb, c: (bb, cidx(c), 0, 0)),
            full(tri.shape), full(lvl.shape), full(hm.shape),
            col(0), col(1 + d), col(3), full((1, w)), full(cm.shape),
        ],
        out_specs=[tok, tok],
        out_shape=[jax.ShapeDtypeStruct((b, lt, w), BF16)] * 2,
        scratch_shapes=[pltpu.VMEM((w // MXU_DIM, n, n), F32),
                        pltpu.VMEM((w // HGRN_HEAD, HGRN_HEAD, HGRN_HEAD), F32),
                        pltpu.VMEM((CHUNK, w), F32), pltpu.VMEM((CHUNK, w), F32)],
        compiler_params=_cparams("arbitrary", "arbitrary"),
        name="mix_scan_rev" if reverse else "mix_scan_fwd",
    )(rh, kh, bh, kap, kg, bg, v, gl, jnp.asarray(tri), jnp.asarray(lvl), jnp.asarray(hm, dtype=BF16),
      p_hgrn, p_hgrn, p_hgrn, lb_d.reshape(1, w), cm)


def _residual_ln(alpha, x, gm, proj, lng, lnb):
    z = alpha * x + gm * proj
    mu = jnp.mean(z, axis=-1, keepdims=True)
    zc = z - mu
    var = jnp.mean(zc * zc, axis=-1, keepdims=True)
    return zc * lax.rsqrt(var + LN_EPS) * lng + lnb


def _even_out_kernel(alpha, ya0_ref, ya1_ref, bonus_ref, yb0_ref, yb1_ref, gate_ref, x_ref, gm_ref, w_ref,
                     lnxg_ref, lnxb_ref, ng_ref, lng_ref, lnb_ref, o_ref):
    y = ya0_ref[0].astype(F32) + ya1_ref[0].astype(F32)
    mu = _group64_sum(y) * (1.0 / RWKV_HEAD)
    yc = y - mu
    var = _group64_sum(yc * yc) * (1.0 / RWKV_HEAD)
    ya = yc * lax.rsqrt(var + RWKV_GN_EPS) * lnxg_ref[...] + lnxb_ref[...] + bonus_ref[0].astype(F32)
    o = yb0_ref[0].astype(F32) + yb1_ref[0].astype(F32)
    yb = o * lax.rsqrt(_group128_mean(o * o) + LN_EPS) * ng_ref[...]
    ycat = jnp.concatenate([ya, yb], axis=-1) * _silu(gate_ref[0].astype(F32))
    proj = _mm(ycat.astype(BF16), w_ref[...])
    o_ref[0] = _residual_ln(alpha, x_ref[0], gm_ref[0, 0], proj, lng_ref[...], lnb_ref[...])


def _odd_out_kernel(alpha, y_ref, x_ref, gm_ref, w_ref, lng_ref, lnb_ref, o_ref):
    proj = _mm(y_ref[0].astype(BF16), w_ref[...])
    o_ref[0] = _residual_ln(alpha, x_ref[0], gm_ref[0, 0], proj, lng_ref[...], lnb_ref[...])


def _tok_spec(n):
    return pl.BlockSpec((1, TOK_BLOCK, n), lambda bb, i: (bb, i, 0))


def _full2(shape):
    return pl.BlockSpec(shape, lambda bb, i: (0,) * len(shape))


def _seg_spec(d, nblk):
    return pl.BlockSpec((1, 1, 1, d), lambda bb, i: (bb, _seg(i, nblk), 0, 0))


def _even_out(alpha, ya0, ya1, bonus, yb0, yb1, gate, xs, gmod, w_out, lnx_g, lnx_b, norm_g, ln_g, ln_b):
    b, lt, d = xs.shape
    w = ya0.shape[-1]
    di = gate.shape[-1]
    row = lambda u: u.reshape(1, -1)
    return pl.pallas_call(
        functools.partial(_even_out_kernel, alpha),
        grid=(b, lt // TOK_BLOCK),
        in_specs=[_tok_spec(w)] * 5 + [_tok_spec(di), _tok_spec(d), _seg_spec(d, lt // TOK_BLOCK), _full2((di, d)),
                                       _full2((1, w)), _full2((1, w)), _full2((1, w)), _full2((1, d)), _full2((1, d))],
        out_specs=_tok_spec(d),
        out_shape=jax.ShapeDtypeStruct((b, lt, d), F32),
        compiler_params=_cparams("arbitrary", "arbitrary"),
        name="even_out",
    )(ya0, ya1, bonus, yb0, yb1, gate, xs, gmod, w_out, row(lnx_g), row(lnx_b),
      row(jnp.tile(norm_g, w // HGRN_HEAD)), row(ln_g), row(ln_b))


def _odd_out(alpha, y, xs, gmod, w_out, ln_g, ln_b, rows_out):
    b, lt, d = xs.shape
    di = y.shape[-1]
    row = lambda u: u.reshape(1, -1)
    return pl.pallas_call(
        functools.partial(_odd_out_kernel, alpha),
        grid=(b, rows_out // TOK_BLOCK),
        in_specs=[_tok_spec(di), _tok_spec(d), _seg_spec(d, lt // TOK_BLOCK), _full2((di, d)), _full2((1, d)),
                  _full2((1, d))],
        out_specs=_tok_spec(d),
        out_shape=jax.ShapeDtypeStruct((b, rows_out, d), F32),
        compiler_params=_cparams("arbitrary", "arbitrary"),
        name="odd_out",
    )(y, xs, gmod, w_out, row(ln_g), row(ln_b))


def _rope(x, cos, sin):
    width = x.shape[-1]
    lane = lax.broadcasted_iota(jnp.int32, (1, width), 1)
    first = (lane % 32) < 16
    partner = jnp.where(first, pltpu.roll(x, width - 16, 1), pltpu.roll(x, 16, 1))
    return x * cos + partner * sin


def _attn_kernel(lam_init, tk, n_full, tail, lam_ref, q_ref, k_ref, vt_ref, g_ref, sg_ref, *rest):
    y_ref, m_scr, acc_scr = rest[-3:]
    tq = q_ref.shape[1]
    ones_rows = 16
    tile = min(ATTN_TILE, 2 * tq)
    n_tiles = 2 * tq // tile
    q = q_ref[0].astype(F32) * (DIFF_HEAD ** -0.5 * float(np.log2(np.e)))
    lane = lax.broadcasted_iota(jnp.int32, (1, LANES), 1)
    q_cat = jnp.concatenate([jnp.where(lane < DIFF_HEAD, q, 0.0), jnp.where(lane >= DIFF_HEAD, q, 0.0)],
                            axis=0).astype(BF16)
    m_scr[...] = jnp.full(m_scr.shape, -jnp.inf, F32)
    acc_scr[...] = jnp.zeros(acc_scr.shape, F32)

    def scores(item):
        start, size, c = item
        return _nt(k_ref[0, pl.ds(start, size), :], q_cat[c * tile:(c + 1) * tile])

    def run(blocks):
        cols = [slice(c * tile, (c + 1) * tile) for c in range(n_tiles)]
        m_run = [m_scr[:, cs] for cs in cols]
        acc = [acc_scr[:, cs] for cs in cols]
        vta = None
        pending = None
        items = [(start, size, c) for start, size in blocks for c in range(n_tiles)]
        st_next = scores(items[0])
        for i, (start, size, c) in enumerate(items):
            st = st_next
            if i + 1 < len(items):
                st_next = scores(items[i + 1])
            if pending is not None:
                pc, pcorr, pvta, ppt = pending
                acc[pc] = acc[pc] * pcorr + _mm(pvta, ppt)
            if c == 0:
                vta = jnp.concatenate([vt_ref[0, :, pl.ds(start, size)], jnp.ones((ones_rows, size), BF16)], axis=0)
            m_new = jnp.maximum(m_run[c], jnp.max(st, axis=0, keepdims=True))
            pending = (c, jnp.exp2(m_run[c] - m_new), vta, jnp.exp2(st - m_new).astype(BF16))
            m_run[c] = m_new
        pc, pcorr, pvta, ppt = pending
        acc[pc] = acc[pc] * pcorr + _mm(pvta, ppt)
        for c, cs in enumerate(cols):
            m_scr[:, cs] = m_run[c]
            acc_scr[:, cs] = acc[c]

    tail_block = [(n_full * tk, tail)]
    if n_full <= ATTN_UNROLL:
        run([(u * tk, tk) for u in range(n_full)] + tail_block)
    else:
        assert n_full % ATTN_UNROLL == 0
        def body(j, carry):
            run([(pl.multiple_of((j * ATTN_UNROLL + u) * tk, tk), tk) for u in range(ATTN_UNROLL)])
            return carry
        lax.fori_loop(0, n_full // ATTN_UNROLL, body, 0)
        run(tail_block)

    a0 = acc_scr[:, :tq]
    a1 = acc_scr[:, tq:]
    o_t = a0[:LANES] / a0[LANES:LANES + 1] - lam_ref[0] * (a1[:LANES] / a1[LANES:LANES + 1])
    o = jnp.transpose(o_t)
    y = o * lax.rsqrt(jnp.mean(o * o, axis=-1, keepdims=True) + LN_EPS) * sg_ref[...] * (1.0 - lam_init)
    y_ref[0] = (y * _silu(g_ref[0].astype(F32))).astype(y_ref.dtype)


def _diff_attention(p_qk, p_vt, p_g, lam, subln_g, lam_init):
    b, lt, di = p_g.shape
    nh = di // LANES
    t = lt - TOK_BLOCK
    tq = tk = 512
    assert t % tq == 0
    ctx_blk = t // TOK_BLOCK
    sm = pl.BlockSpec(memory_space=pltpu.SMEM)
    sg_spec = pl.BlockSpec((1, LANES), lambda bb, h, i: (0, 0))
    scratch = lambda n: [pltpu.VMEM((1, 2 * n), F32), pltpu.VMEM((LANES + 16, 2 * n), F32)]
    args = (lam.reshape(1), p_qk, p_qk, p_vt, p_g, subln_g.reshape(1, LANES))
    y = pl.pallas_call(
        functools.partial(_attn_kernel, lam_init, tk, t // tk, TOK_BLOCK),
        grid=(b, nh, t // tq),
        in_specs=[sm,
                  pl.BlockSpec((1, tq, LANES), lambda bb, h, i: (bb, i, h)),
                  pl.BlockSpec((1, lt, LANES), lambda bb, h, i: (bb, 0, nh + h)),
                  pl.BlockSpec((1, LANES, lt), lambda bb, h, i: (bb, h, 0)),
                  pl.BlockSpec((1, tq, LANES), lambda bb, h, i: (bb, i, h)),
                  sg_spec],
        out_specs=pl.BlockSpec((1, tq, LANES), lambda bb, h, i: (bb, i, h)),
        out_shape=jax.ShapeDtypeStruct((b, lt, di), BF16),
        scratch_shapes=scratch(tq),
        compiler_params=_cparams("arbitrary", "arbitrary", "arbitrary"),
        name="diff_attn",
    )(*args)
    blk = lambda bb, h, i: (bb, ctx_blk, h)
    return pl.pallas_call(
        functools.partial(_attn_kernel, lam_init, tk, 0, TOK_BLOCK),
        grid=(b, nh, 1),
        in_specs=[sm,
                  pl.BlockSpec((1, TOK_BLOCK, LANES), blk),
                  pl.BlockSpec((1, TOK_BLOCK, LANES), lambda bb, h, i: (bb, ctx_blk, nh + h)),
                  pl.BlockSpec((1, LANES, TOK_BLOCK), lambda bb, h, i: (bb, h, ctx_blk)),
                  pl.BlockSpec((1, TOK_BLOCK, LANES), blk),
                  sg_spec,
                  pl.BlockSpec(memory_space=pl.ANY)],
        out_specs=pl.BlockSpec((1, TOK_BLOCK, LANES), blk),
        out_shape=jax.ShapeDtypeStruct((b, lt, di), BF16),
        scratch_shapes=scratch(TOK_BLOCK),
        input_output_aliases={6: 0},
        compiler_params=_cparams("arbitrary", "arbitrary", "arbitrary"),
        name="diff_attn_ctx",
    )(*args, y)


def _rope_tables(n_ctx, t):
    quarter = DIFF_HEAD // 4
    inv = ROPE_BASE ** (-jnp.arange(quarter, dtype=F32) / quarter)
    pos = jnp.arange(t)
    rows = (pos // GRID_W).astype(F32)[:, None] * inv
    cols = (pos % GRID_W).astype(F32)[:, None] * inv
    cos64 = jnp.concatenate([jnp.cos(rows), jnp.cos(rows), jnp.cos(cols), jnp.cos(cols)], -1)
    sin64 = jnp.concatenate([-jnp.sin(rows), jnp.sin(rows), -jnp.sin(cols), jnp.sin(cols)], -1)
    cos = jnp.concatenate([jnp.tile(cos64, (1, 2)), jnp.ones((n_ctx, LANES), F32)], 0)
    sin = jnp.concatenate([jnp.tile(sin64, (1, 2)), jnp.zeros((n_ctx, LANES), F32)], 0)
    return cos, sin


def _even_layer(alpha, xs, scale, shift, gmod, w_in, w_out, mu_prev, mu_next, w0, w2, a0, a2, k_k, k_a, r_k,
                lnx_g, lnx_b, lb, norm_g, ln_g, ln_b):
    w = k_k.shape[-1]
    sw = mu_prev.shape[-1]
    hw = 4 * lb.shape[-1]
    w_in = w_in.astype(BF16)
    p_rwkv = _project(xs, scale, shift, w_in[:, :sw], sw, F32)
    p_hgrn = _project(xs, scale, shift, w_in[:, sw:sw + hw], hw // 2, F32)
    gate = _project(xs, scale, shift, w_in[:, sw + hw:], w_in.shape[1] - sw - hw, BF16)
    v, bonus, ops_f, ops_r = _rwkv_prep(p_rwkv, mu_prev, mu_next, w0, w2, a0, a2, k_k, k_a, r_k)
    ya0, yb0 = _mix_scan(ops_f, v, p_hgrn, lb[0], 0, False)
    ya1, yb1 = _mix_scan(ops_r, v, p_hgrn, lb[1], 1, True)
    return _even_out(alpha, ya0, ya1, bonus, yb0, yb1, gate, xs, gmod, w_out.astype(BF16),
                     lnx_g, lnx_b, norm_g, ln_g, ln_b)


def _odd_layer(alpha, xs, scale, shift, gmod, w_in, w_out, lam_p, subln_g, lam_init, cos, sin, ln_g, ln_b,
               rows_out):
    di = w_out.shape[0]
    w_in = w_in.astype(BF16)
    p_qk = _project(xs, scale, shift, w_in[:, :2 * di], di, BF16, rope=(cos, sin))
    p_vt = _project_t(xs, scale, shift, jnp.transpose(w_in[:, 2 * di:3 * di]), BF16)
    p_g = _project(xs, scale, shift, w_in[:, 3 * di:], di, BF16)
    lam = jnp.exp(jnp.sum(lam_p[0] * lam_p[1])) - jnp.exp(jnp.sum(lam_p[2] * lam_p[3])) + lam_init
    y = _diff_attention(p_qk, p_vt, p_g, lam, subln_g, lam_init)
    return _odd_out(alpha, y, xs, gmod, w_out.astype(BF16), ln_g, ln_b, rows_out)


def kernel(x, c, ctx, c_ctx, ada_w, ada_b, ln_g, ln_b, even_w_in, even_w_out, rwkv_mu_prev, rwkv_mu_next, rwkv_w0, rwkv_w2, rwkv_a0, rwkv_a2, rwkv_k_k, rwkv_k_a, rwkv_r_k, rwkv_lnx_g, rwkv_lnx_b, hgrn_lb_logits, hgrn_norm_g, odd_w_in, odd_w_out, diff_lambda, diff_subln_g):
    b, t, d = x.shape
    n_ctx = ctx.shape[1]
    depth = ada_w.shape[0]
    assert n_ctx == TOK_BLOCK and t % TOK_BLOCK == 0 and b + 1 <= 8
    alpha = (2.0 * depth) ** 0.25

    xs = jnp.concatenate([x, ctx], axis=1)
    cvec = jnp.concatenate([c, c_ctx[None], jnp.zeros((8 - b - 1, d), F32)], axis=0)
    mods = _ada_mods(cvec, ada_w, ada_b)
    lb_all = jax.nn.softmax(hgrn_lb_logits.astype(F32), axis=0)
    lb_all = jnp.cumsum(lb_all, axis=0) - lb_all[0]
    cos, sin = _rope_tables(n_ctx, t)

    for layer in range(depth):
        m = mods[layer]
        per_seg = lambda u: jnp.stack([jnp.broadcast_to(u[b], (b, d)), u[:b]], axis=1)[:, :, None, :]
        shift, scale, gmod = (per_seg(m[:, j * d:(j + 1) * d]) for j in range(3))
        j = layer // 2
        if layer % 2 == 0:
            xs = _even_layer(alpha, xs, scale, shift, gmod, even_w_in[j], even_w_out[j], rwkv_mu_prev[j],
                             rwkv_mu_next[j], rwkv_w0[j], rwkv_w2[j], rwkv_a0[j], rwkv_a2[j], rwkv_k_k[j],
                             rwkv_k_a[j], rwkv_r_k[j], rwkv_lnx_g[j], rwkv_lnx_b[j], lb_all[j], hgrn_norm_g[j],
                             ln_g[layer], ln_b[layer])
        else:
            lam_init = 0.8 - 0.6 * float(np.exp(-0.3 * layer))
            xs = _odd_layer(alpha, xs, scale, shift, gmod, odd_w_in[j], odd_w_out[j], diff_lambda[j],
                            diff_subln_g[j], lam_init, cos, sin, ln_g[layer], ln_b[layer],
                            t if layer == depth - 1 else t + n_ctx)
    return xs[:, :t]
```

```python
import functools

import numpy as np
import jax
import jax.numpy as jnp
from jax import lax
from jax.experimental import pallas as pl
from jax.experimental.pallas import tpu as pltpu

F32 = jnp.float32
BF16 = jnp.bfloat16
HI = lax.Precision.HIGHEST

GRID_W = 64
RWKV_HEAD = 64
RWKV_LORA = 64
RWKV_GN_EPS = 64e-5
HGRN_HEAD = 128
DIFF_HEAD = 64
ROPE_BASE = 10000.0
LN_EPS = 1e-5
LOG2E = float(np.log2(np.e))

LANES = 128
MXU_DIM = 256
VMEM_LIMIT = 56 * 1024 * 1024

CHUNK = 64
SUB = 16
TOK_BLOCK = 256
PROJ_ROWS = 768
GROUP_HEADS = MXU_DIM // RWKV_HEAD
HGRN_PAR = 4
ATTN_UNROLL = 16
ATTN_TILE = 1024


def _cparams(*sem):
    return pltpu.CompilerParams(dimension_semantics=sem, vmem_limit_bytes=VMEM_LIMIT)


def _nt(a, b):
    return lax.dot_general(a, b, (((1,), (1,)), ((), ())), preferred_element_type=F32)


def _tn(a, b):
    return lax.dot_general(a, b, (((0,), (0,)), ((), ())), preferred_element_type=F32)


def _mm(a, b):
    return jnp.dot(a, b, preferred_element_type=F32)


def _mm_exact(a, b):
    return jnp.dot(a, b, preferred_element_type=F32, precision=HI)


def _sigmoid(x):
    return 1.0 / (1.0 + jnp.exp(-x))


def _silu(x):
    return x * _sigmoid(x)


def _lane_tiles(x):
    return [x[:, j * LANES:(j + 1) * LANES] for j in range(x.shape[-1] // LANES)]


def _group64_sum(x):
    lane = lax.broadcasted_iota(jnp.int32, (1, LANES), 1)
    low = lane < RWKV_HEAD
    out = []
    for xt in _lane_tiles(x):
        s_all = jnp.sum(xt, axis=-1, keepdims=True)
        s_lo = jnp.sum(jnp.where(low, xt, 0.0), axis=-1, keepdims=True)
        out.append(jnp.where(low, s_lo, s_all - s_lo))
    return jnp.concatenate(out, axis=-1)


def _group128_mean(x):
    out = []
    for xt in _lane_tiles(x):
        out.append(jnp.broadcast_to(jnp.mean(xt, axis=-1, keepdims=True), xt.shape))
    return jnp.concatenate(out, axis=-1)


def _ada_kernel(c_ref, w_ref, b_ref, o_ref):
    cond = _silu(c_ref[...])
    o_ref[0] = _mm_exact(cond, w_ref[0]) + b_ref[0]


def _ada_mods(cvec, ada_w, ada_b):
    depth, d, d3 = ada_w.shape
    tn = 1024
    return pl.pallas_call(
        _ada_kernel,
        grid=(depth, d3 // tn),
        in_specs=[
            pl.BlockSpec((8, d), lambda l, j: (0, 0)),
            pl.BlockSpec((1, d, tn), lambda l, j: (l, 0, j)),
            pl.BlockSpec((1, 1, tn), lambda l, j: (l, 0, j)),
        ],
        out_specs=pl.BlockSpec((1, 8, tn), lambda l, j: (l, 0, j)),
        out_shape=jax.ShapeDtypeStruct((depth, 8, d3), F32),
        compiler_params=_cparams("arbitrary", "arbitrary"),
        name="ada_mods",
    )(cvec, ada_w, ada_b.reshape(depth, 1, d3))


def _proj_rows(lt):
    return PROJ_ROWS if lt % PROJ_ROWS == 0 else TOK_BLOCK


def _modulate(n_lat, axis, x_ref, sc_ref, sh_ref):
    tm = x_ref.shape[1]
    rows = pl.program_id(axis) * tm + lax.broadcasted_iota(jnp.int32, (tm, 1), 0)
    is_ctx = rows >= n_lat
    sc = jnp.where(is_ctx, sc_ref[0, 0], sc_ref[0, 1])
    sh = jnp.where(is_ctx, sh_ref[0, 0], sh_ref[0, 1])
    return (x_ref[0] * (1.0 + sc) + sh).astype(BF16)


def _proj_kernel(n_lat, x_ref, sc_ref, sh_ref, w_ref, o_ref):
    o_ref[0] = _mm(_modulate(n_lat, 2, x_ref, sc_ref, sh_ref), w_ref[...]).astype(o_ref.dtype)


def _proj_rope_kernel(n_lat, x_ref, sc_ref, sh_ref, w_ref, cos_ref, sin_ref, o_ref):
    p = _mm(_modulate(n_lat, 2, x_ref, sc_ref, sh_ref), w_ref[...])
    reps = p.shape[-1] // LANES
    cos = jnp.concatenate([cos_ref[...]] * reps, axis=-1)
    sin = jnp.concatenate([sin_ref[...]] * reps, axis=-1)
    o_ref[0] = _rope(p, cos, sin).astype(o_ref.dtype)


def _proj_t_kernel(n_lat, x_ref, sc_ref, sh_ref, wt_ref, o_ref):
    o_ref[0] = _nt(wt_ref[...], _modulate(n_lat, 1, x_ref, sc_ref, sh_ref)).astype(o_ref.dtype)


def _project(xs, scale, shift, w, tn, out_dtype, rope=None):
    b, lt, d = xs.shape
    n = w.shape[1]
    tm = _proj_rows(lt)
    n_lat = lt - TOK_BLOCK
    in_specs = [
        pl.BlockSpec((1, tm, d), lambda j, bb, i: (bb, i, 0)),
        pl.BlockSpec((1, 2, 1, d), lambda j, bb, i: (bb, 0, 0, 0)),
        pl.BlockSpec((1, 2, 1, d), lambda j, bb, i: (bb, 0, 0, 0)),
        pl.BlockSpec((d, tn), lambda j, bb, i: (0, j)),
    ]
    args = (xs, scale, shift, w)
    if rope is not None:
        in_specs += [pl.BlockSpec((tm, LANES), lambda j, bb, i: (i, 0))] * 2
        args += tuple(rope)
    return pl.pallas_call(
        functools.partial(_proj_kernel if rope is None else _proj_rope_kernel, n_lat),
        grid=(n // tn, b, lt // tm),
        in_specs=in_specs,
        out_specs=pl.BlockSpec((1, tm, tn), lambda j, bb, i: (bb, i, j)),
        out_shape=jax.ShapeDtypeStruct((b, lt, n), out_dtype),
        compiler_params=_cparams("arbitrary", "arbitrary", "arbitrary"),
        name="mod_proj" if rope is None else "mod_proj_rope",
    )(*args)


def _project_t(xs, scale, shift, wt, out_dtype):
    b, lt, d = xs.shape
    n = wt.shape[0]
    tm = _proj_rows(lt)
    return pl.pallas_call(
        functools.partial(_proj_t_kernel, lt - TOK_BLOCK),
        grid=(b, lt // tm),
        in_specs=[
            pl.BlockSpec((1, tm, d), lambda bb, i: (bb, i, 0)),
            pl.BlockSpec((1, 2, 1, d), lambda bb, i: (bb, 0, 0, 0)),
            pl.BlockSpec((1, 2, 1, d), lambda bb, i: (bb, 0, 0, 0)),
            pl.BlockSpec((n, d), lambda bb, i: (0, 0)),
        ],
        out_specs=pl.BlockSpec((1, n, tm), lambda bb, i: (bb, 0, i)),
        out_shape=jax.ShapeDtypeStruct((b, n, lt), out_dtype),
        compiler_params=_cparams("arbitrary", "arbitrary"),
        name="mod_proj_t",
    )(xs, scale, shift, wt)


def _lora(x, w_ref):
    x1 = x.astype(BF16)
    x2 = (x - x1.astype(F32)).astype(BF16)
    return _mm(jnp.concatenate([x1, x1, x2], axis=-1), w_ref[...])


def _rwkv_prep_kernel(nblk, p_ref, prev_ref, next_ref, mup_ref, mun_ref, w0_ref, w2_ref, a0_ref, a2_ref,
                      kk_ref, ka_ref, rk_ref, cm_ref,
                      v_out, bonus_out, *dir_outs):
    i = pl.program_id(1)
    w = kk_ref.shape[-1]
    p = p_ref[0]
    row = lax.broadcasted_iota(jnp.int32, (TOK_BLOCK, 1), 0)
    prev_row = jnp.where(jnp.logical_and(i >= 1, i <= nblk - 2), prev_ref[0][7:8, :], 0.0)
    next_row = jnp.where(i <= nblk - 3, next_ref[0][0:1, :], 0.0)
    prev = jnp.where(row == 0, prev_row, pltpu.roll(p, 1, 0))
    nxt = jnp.where(row == TOK_BLOCK - 1, next_row, pltpu.roll(p, TOK_BLOCK - 1, 0))
    ps = p + mup_ref[...] * (prev - p) + mun_ref[...] * (nxt - p)

    r = ps[:, 0:w]
    k = ps[:, w:2 * w]
    v = ps[:, 2 * w:3 * w]
    wlo = ps[:, 3 * w:3 * w + 2 * RWKV_LORA]
    alo = ps[:, 3 * w + 2 * RWKV_LORA:3 * w + 4 * RWKV_LORA]

    z = _lora(jnp.tanh(wlo), w2_ref) + w0_ref[...]
    w_log = -(jnp.maximum(-z, 0.0) + jnp.log(1.0 + jnp.exp(-jnp.abs(z)))) - 0.5
    lw = jnp.exp(w_log) * (-LOG2E)
    a = _sigmoid(_lora(alo, a2_ref) + a0_ref[...])

    kk = k * kk_ref[...]
    kk = kk * lax.rsqrt(_group64_sum(kk * kk) + 1e-12)
    bonus_out[0] = (_group64_sum(r * k * rk_ref[...]) * v).astype(bonus_out.dtype)
    v_out[0] = v.astype(BF16)

    for d in range(2):
        rh_o, kh_o, bh_o, kap_o, kg_o, bg_o, gl_o = dir_outs[7 * d:7 * d + 7]
        lw_d = lw[:, d * w:(d + 1) * w]
        a_d = a[:, d * w:(d + 1) * w]
        kd = k * (1.0 + (a_d - 1.0) * ka_ref[...])
        b_d = kk * a_d
        sums = _mm(cm_ref[d], _split3(lw_d))
        g = sums[:TOK_BLOCK]
        g_all = sums[TOK_BLOCK:]
        e_neg = jnp.exp2(-g)
        e_rest = jnp.exp2(g_all - g)
        rh_o[0] = (r * jnp.exp2(g)).astype(BF16)
        kh_o[0] = (kd * e_neg).astype(BF16)
        bh_o[0] = (b_d * e_neg).astype(BF16)
        kap_o[0] = (kk * jnp.exp2(g - lw_d)).astype(BF16)
        kg_o[0] = (kd * e_rest).astype(BF16)
        bg_o[0] = (b_d * e_rest).astype(BF16)
        e_all = jnp.exp2(g_all)
        for c in range(TOK_BLOCK // CHUNK):
            gl_o[0, c] = e_all[c * CHUNK:c * CHUNK + 1, :]


def _rwkv_sum_consts(reverse):
    t = np.arange(TOK_BLOCK)[:, None]
    s = np.arange(TOK_BLOCK)[None, :]
    same = (t // CHUNK) == (s // CHUNK)
    order = (s >= t) if reverse else (s <= t)
    both = np.concatenate([same & order, same], axis=0).astype(np.float32)
    return np.concatenate([both] * 3, axis=1)


def _lora_pieces(m):
    hi = m.astype(BF16)
    lo = (m - hi.astype(F32)).astype(BF16)
    return jnp.concatenate([hi, lo, hi], axis=0)


def _rwkv_prep(p_rwkv, mu_prev, mu_next, w0, w2, a0, a2, k_k, k_a, r_k):
    b, lt, sw = p_rwkv.shape
    w = k_k.shape[-1]
    nblk = lt // TOK_BLOCK
    nch = lt // CHUNK
    halo = TOK_BLOCK // 8
    zl = jnp.zeros((RWKV_LORA, w), F32)
    w2cat = _lora_pieces(jnp.concatenate([jnp.concatenate([w2[0], zl], 1), jnp.concatenate([zl, w2[1]], 1)], 0))
    a2cat = _lora_pieces(jnp.concatenate([jnp.concatenate([a2[0], zl], 1), jnp.concatenate([zl, a2[1]], 1)], 0))
    cm = jnp.asarray(np.stack([_rwkv_sum_consts(False), _rwkv_sum_consts(True)]), dtype=BF16)
    row = lambda u: u.reshape(1, -1)
    full = lambda shape: pl.BlockSpec(shape, lambda bb, i: (0,) * len(shape))
    tok = lambda n, dt: (pl.BlockSpec((1, TOK_BLOCK, n), lambda bb, i: (bb, i, 0)), jax.ShapeDtypeStruct((b, lt, n), dt))
    gl = (pl.BlockSpec((1, TOK_BLOCK // CHUNK, 1, w), lambda bb, i: (bb, i, 0, 0)),
          jax.ShapeDtypeStruct((b, nch, 1, w), F32))
    outs = [tok(w, BF16), tok(w, BF16)] + 2 * ([tok(w, BF16)] * 6 + [gl])
    res = pl.pallas_call(
        functools.partial(_rwkv_prep_kernel, nblk),
        grid=(b, nblk),
        in_specs=[
            pl.BlockSpec((1, TOK_BLOCK, sw), lambda bb, i: (bb, i, 0)),
            pl.BlockSpec((1, 8, sw), lambda bb, i: (bb, jnp.maximum(i * halo - 1, 0), 0)),
            pl.BlockSpec((1, 8, sw), lambda bb, i: (bb, jnp.minimum((i + 1) * halo, lt // 8 - 1), 0)),
            full((1, sw)), full((1, sw)), full((1, 2 * w)), full(w2cat.shape),
            full((1, 2 * w)), full(a2cat.shape), full((1, w)), full((1, w)), full((1, w)),
            full(cm.shape),
        ],
        out_specs=[o[0] for o in outs],
        out_shape=[o[1] for o in outs],
        compiler_params=_cparams("arbitrary", "arbitrary"),
        name="rwkv_prep",
    )(p_rwkv, p_rwkv, p_rwkv, row(mu_prev), row(mu_next), row(w0), w2cat, row(a0), a2cat,
      row(k_k), row(k_a), row(r_k), cm)
    v, bonus = res[0], res[1]
    return v, bonus, res[2:9], res[9:16]


def _rwkv_masks(reverse):
    n = GROUP_HEADS * CHUNK
    i = np.arange(n)[:, None]
    j = np.arange(n)[None, :]
    same_head = (i // CHUNK) == (j // CHUNK)
    before = (j > i) if reverse else (j < i)
    strict = same_head & before
    incl = same_head & (before | (i == j))
    levels = []
    size = 1
    while size < CHUNK:
        levels.append(same_head & ((i // (2 * size)) == (j // (2 * size))) & ((i // size) != (j // size)))
        size *= 2
    head_lane = (i // CHUNK) == (j // RWKV_HEAD)
    tri = np.stack([strict, incl]).astype(np.float32)
    return tri, np.stack(levels).astype(np.float32), head_lane.astype(np.float32)


def _scan_chunk_index(c, nch, reverse):
    if reverse:
        return nch - 1 - c
    n_ctx = TOK_BLOCK // CHUNK
    return jnp.where(c < n_ctx, nch - n_ctx + c, c - n_ctx)


def _split3(x):
    x1 = x.astype(BF16)
    r1 = x - x1.astype(F32)
    x2 = r1.astype(BF16)
    x3 = (r1 - x2.astype(F32)).astype(BF16)
    return jnp.concatenate([x1, x2, x3], axis=0)


def _hgrn_chunk_consts(reverse):
    t = np.arange(CHUNK)[:, None]
    s = np.arange(CHUNK)[None, :]
    order = (s >= t) if reverse else (s <= t)
    earlier_sub = ((s // SUB) > (t // SUB)) if reverse else ((s // SUB) < (t // SUB))
    both = np.concatenate([order, earlier_sub], axis=0).astype(np.float32)
    return np.concatenate([both] * 3, axis=1)


def _rwkv_chunk_stages(rh_ref, kh_ref, bh_ref, kap_ref, kg_ref, bg_ref, v_ref, gl_ref, tri_ref, lvl_ref, hm_ref,
                       y_ref, s_ref):
    hm = hm_ref[...]
    strict = tri_ref[0]
    incl = tri_ref[1]
    eye = incl - strict
    n_levels = lvl_ref.shape[0]
    groups = range(y_ref.shape[-1] // MXU_DIM)
    lanes = lambda ref, g: ref[0, :, g * MXU_DIM:(g + 1) * MXU_DIM]
    stack = lambda ref, g: jnp.concatenate([lanes(ref, g)] * GROUP_HEADS, axis=0) * hm

    kap, bh, a_ab = [], [], []
    for g in groups:
        kap.append(stack(kap_ref, g))
        bh.append(stack(bh_ref, g))
        a_ab.append(_nt(kap[g], bh[g]) * strict)
    yield
    kh = [stack(kh_ref, g) for g in groups]
    a_ak = [(_nt(kap[g], kh[g]) * strict).astype(BF16) for g in groups]
    rh = [stack(rh_ref, g) for g in groups]
    p_rb = [(_nt(rh[g], bh[g]) * incl).astype(BF16) for g in groups]
    p_rk = [(_nt(rh[g], kh[g]) * incl).astype(BF16) for g in groups]
    vs = [stack(v_ref, g) for g in groups]
    t = [eye - a_ab[g] * lvl_ref[0] for g in groups]
    for lv in range(1, n_levels):
        yield
        tb = [t[g].astype(BF16) for g in groups]
        x = [_mm(tb[g], (a_ab[g] * lvl_ref[lv]).astype(BF16)).astype(BF16) for g in groups]
        yield
        t = [t[g] - _mm(x[g], tb[g]) for g in groups]
    yield
    s0 = [s_ref[g] for g in groups]
    s0b = [s0[g].astype(BF16) for g in groups]
    hm32 = hm.astype(F32)
    ks0 = [jnp.concatenate([_nt(lanes(kap_ref, g), s0b[g])] * GROUP_HEADS, axis=0) * hm32 for g in groups]
    wmat = [(ks0[g] + _mm(a_ak[g], vs[g])).astype(BF16) for g in groups]
    yield
    ub = [(-_mm(t[g].astype(BF16), wmat[g])).astype(BF16) for g in groups]
    yield
    ys = [_mm(p_rb[g], ub[g]) + _mm(p_rk[g], vs[g]) for g in groups]
    for g in groups:
        y = _nt(lanes(rh_ref, g), s0b[g])
        for h in range(GROUP_HEADS):
            y = y + ys[g][h * CHUNK:(h + 1) * CHUNK]
        y_ref[0, :, g * MXU_DIM:(g + 1) * MXU_DIM] = y.astype(y_ref.dtype)
    yield
    for g in groups:
        gam = gl_ref[0, 0][:, g * MXU_DIM:(g + 1) * MXU_DIM]
        s_ref[g] = s0[g] * gam + _tn(ub[g], stack(bg_ref, g)) + _tn(vs[g], stack(kg_ref, g))


def _hgrn_chunk_stages(reverse, q_ref, f_ref, i_ref, lb_ref, cm_ref, o_ref, s_ref, g_scr, q_scr):
    n_sub = CHUNK // SUB
    n_heads = q_ref.shape[-1] // HGRN_HEAD
    sub_order = list(range(n_sub))[::-1] if reverse else list(range(n_sub))
    row_sub = lax.broadcasted_iota(jnp.int32, (CHUNK, 1), 0) // SUB
    srow = lax.broadcasted_iota(jnp.int32, (SUB, 1), 0)
    lane = lax.broadcasted_iota(jnp.int32, (1, LANES), 1)
    last_row = 0 if reverse else CHUNK - 1

    kgate = (1.0 - lb_ref[...]) * _sigmoid(-f_ref[0])
    sums = _mm(cm_ref[...], _split3(jnp.log1p(-kgate) * LOG2E))
    g_all = sums[:CHUNK]
    gb_all = sums[CHUNK:]
    q_all = _silu(q_ref[0])
    g_scr[...] = g_all
    q_scr[...] = q_all
    yield
    for hg in range(n_heads // HGRN_PAR):
        par = range(HGRN_PAR)
        hs = [hg * HGRN_PAR + j for j in par]
        cols = [slice(h * HGRN_HEAD, (h + 1) * HGRN_HEAD) for h in hs]
        g = [g_all[:, cs] for cs in cols]
        gb = [gb_all[:, cs] for cs in cols]
        q = [q_all[:, cs] for cs in cols]
        k = [kgate[:, cs] for cs in cols]
        vb = [i_ref[0, :, cs].astype(BF16) for cs in cols]
        s0 = [s_ref[h] for h in hs]
        q_in = [q[j] * jnp.exp2(g[j] - gb[j]) for j in par]
        at = []
        for j in par:
            kts, qts = [], []
            for si in sub_order[1:]:
                gb_i = gb[j][si * SUB:si * SUB + 1, :]
                earlier = (row_sub > si) if reverse else (row_sub < si)
                kts.append(jnp.where(earlier, k[j] * jnp.exp2(jnp.minimum(gb_i - g[j], 0.0)), 0.0).astype(BF16))
                qts.append(jnp.where(row_sub == si, q_in[j], 0.0).astype(BF16))
            at.append(_nt(jnp.concatenate(kts, axis=-1), jnp.concatenate(qts, axis=-1)))
        yield
        diag = [[] for _ in par]
        for si in range(n_sub):
            blk = slice(si * SUB, (si + 1) * SUB)
            acc = [jnp.zeros((SUB, LANES), F32) for _ in par]
            for tt in range(SUB):
                t = si * SUB + tt
                valid = jnp.logical_and(lane == t, (srow >= tt) if reverse else (srow <= tt))
                for j in par:
                    e = jnp.exp2(jnp.minimum(g_scr[t:t + 1, cols[j]] - g[j][blk], 0.0))
                    col = jnp.sum(k[j][blk] * e * q_scr[t:t + 1, cols[j]], axis=-1, keepdims=True)
                    acc[j] = jnp.where(valid, col, acc[j])
                if tt % 8 == 7:
                    yield
            for j in par:
                diag[j].append(acc[j])
        for j in par:
            a = (at[j] + jnp.concatenate(diag[j], axis=0)[:, :CHUNK]).astype(BF16)
            o = _tn(a, vb[j]) + _nt((q[j] * jnp.exp2(g[j])).astype(BF16), s0[j].astype(BF16))
            o_ref[0, :, cols[j]] = o.astype(o_ref.dtype)
            g_last = g[j][last_row:last_row + 1, :]
            k_out = (k[j] * jnp.exp2(g_last - g[j])).astype(BF16)
            s_ref[hs[j]] = s0[j] * jnp.exp2(g_last) + _tn(vb[j], k_out)
        yield


def _mix_scan_kernel(reverse, *refs):
    rwkv_in, hgrn_in = refs[:11], refs[11:16]
    y_ref, o_ref, s_ref, hs_ref, g_scr, q_scr = refs[16:]

    @pl.when(pl.program_id(1) == 0)
    def _():
        s_ref[...] = jnp.zeros_like(s_ref)
        hs_ref[...] = jnp.zeros_like(hs_ref)

    streams = [_rwkv_chunk_stages(*rwkv_in, y_ref, s_ref),
               _hgrn_chunk_stages(reverse, *hgrn_in, o_ref, hs_ref, g_scr, q_scr)]
    done = object()
    while streams:
        streams = [st for st in streams if next(st, done) is not done]


def _mix_scan(dir_ops, v, p_hgrn, lb_d, d, reverse):
    rh, kh, bh, kap, kg, bg, gl = dir_ops
    b, lt, w = v.shape
    nch = lt // CHUNK
    n = GROUP_HEADS * CHUNK
    tri, lvl, hm = _rwkv_masks(reverse)
    cm = jnp.asarray(_hgrn_chunk_consts(reverse), dtype=BF16)
    cidx = lambda c: _scan_chunk_index(c, nch, reverse)
    tok = pl.BlockSpec((1, CHUNK, w), lambda bb, c: (bb, cidx(c), 0))
    col = lambda j: pl.BlockSpec((1, CHUNK, w), lambda bb, c: (bb, cidx(c), j))
    full = lambda shape: pl.BlockSpec(shape, lambda bb, c: (0,) * len(shape))
    return pl.pallas_call(
        functools.partial(_mix_scan_kernel, reverse),
        grid=(b, nch),
        in_specs=[tok] * 7 + [
            pl.BlockSpec((1, 1, 1, w), lambda bb, c: (bb, cidx(c), 0, 0)),
            full(tri.shape), full(lvl.shape), full(hm.shape),
            col(0), col(1 + d), col(3), full((1, w)), full(cm.shape),
        ],
        out_specs=[tok, tok],
        out_shape=[jax.ShapeDtypeStruct((b, lt, w), BF16)] * 2,
        scratch_shapes=[pltpu.VMEM((w // MXU_DIM, n, n), F32),
                        pltpu.VMEM((w // HGRN_HEAD, HGRN_HEAD, HGRN_HEAD), F32),
                        pltpu.VMEM((CHUNK, w), F32), pltpu.VMEM((CHUNK, w), F32)],
        compiler_params=_cparams("arbitrary", "arbitrary"),
        name="mix_scan_rev" if reverse else "mix_scan_fwd",
    )(rh, kh, bh, kap, kg, bg, v, gl, jnp.asarray(tri), jnp.asarray(lvl), jnp.asarray(hm, dtype=BF16),
      p_hgrn, p_hgrn, p_hgrn, lb_d.reshape(1, w), cm)


def _residual_ln(alpha, n_lat, x, gm_ref, proj, lng, lnb):
    tm = x.shape[0]
    rows = pl.program_id(1) * tm + lax.broadcasted_iota(jnp.int32, (tm, 1), 0)
    gm = jnp.where(rows >= n_lat, gm_ref[0, 0], gm_ref[0, 1])
    z = alpha * x + gm * proj
    mu = jnp.mean(z, axis=-1, keepdims=True)
    zc = z - mu
    var = jnp.mean(zc * zc, axis=-1, keepdims=True)
    return zc * lax.rsqrt(var + LN_EPS) * lng + lnb


def _even_out_kernel(alpha, n_lat, ya0_ref, ya1_ref, bonus_ref, yb0_ref, yb1_ref, gate_ref, x_ref, gm_ref, w_ref,
                     lnxg_ref, lnxb_ref, ng_ref, lng_ref, lnb_ref, o_ref):
    y = ya0_ref[0].astype(F32) + ya1_ref[0].astype(F32)
    mu = _group64_sum(y) * (1.0 / RWKV_HEAD)
    yc = y - mu
    var = _group64_sum(yc * yc) * (1.0 / RWKV_HEAD)
    ya = yc * lax.rsqrt(var + RWKV_GN_EPS) * lnxg_ref[...] + lnxb_ref[...] + bonus_ref[0].astype(F32)
    o = yb0_ref[0].astype(F32) + yb1_ref[0].astype(F32)
    yb = o * lax.rsqrt(_group128_mean(o * o) + LN_EPS) * ng_ref[...]
    ycat = jnp.concatenate([ya, yb], axis=-1) * _silu(gate_ref[0].astype(F32))
    proj = _mm(ycat.astype(BF16), w_ref[...])
    o_ref[0] = _residual_ln(alpha, n_lat, x_ref[0], gm_ref, proj, lng_ref[...], lnb_ref[...])


def _odd_out_kernel(alpha, n_lat, y_ref, x_ref, gm_ref, w_ref, lng_ref, lnb_ref, o_ref):
    proj = _mm(y_ref[0].astype(BF16), w_ref[...])
    o_ref[0] = _residual_ln(alpha, n_lat, x_ref[0], gm_ref, proj, lng_ref[...], lnb_ref[...])


def _tok_spec(n, tm=TOK_BLOCK):
    return pl.BlockSpec((1, tm, n), lambda bb, i: (bb, i, 0))


def _full2(shape):
    return pl.BlockSpec(shape, lambda bb, i: (0,) * len(shape))


def _gate_spec(d):
    return pl.BlockSpec((1, 2, 1, d), lambda bb, i: (bb, 0, 0, 0))


def _even_out(alpha, ya0, ya1, bonus, yb0, yb1, gate, xs, gmod, w_out, lnx_g, lnx_b, norm_g, ln_g, ln_b):
    b, lt, d = xs.shape
    w = ya0.shape[-1]
    di = gate.shape[-1]
    row = lambda u: u.reshape(1, -1)
    tm = _proj_rows(lt)
    return pl.pallas_call(
        functools.partial(_even_out_kernel, alpha, lt - TOK_BLOCK),
        grid=(b, lt // tm),
        in_specs=[_tok_spec(w, tm)] * 5 + [_tok_spec(di, tm), _tok_spec(d, tm), _gate_spec(d), _full2((di, d)),
                                           _full2((1, w)), _full2((1, w)), _full2((1, w)), _full2((1, d)),
                                           _full2((1, d))],
        out_specs=_tok_spec(d, tm),
        out_shape=jax.ShapeDtypeStruct((b, lt, d), F32),
        compiler_params=_cparams("arbitrary", "arbitrary"),
        name="even_out",
    )(ya0, ya1, bonus, yb0, yb1, gate, xs, gmod, w_out, row(lnx_g), row(lnx_b),
      row(jnp.tile(norm_g, w // HGRN_HEAD)), row(ln_g), row(ln_b))


def _odd_out(alpha, y, xs, gmod, w_out, ln_g, ln_b, rows_out):
    b, lt, d = xs.shape
    di = y.shape[-1]
    row = lambda u: u.reshape(1, -1)
    tm = next(r for r in (PROJ_ROWS, 2 * TOK_BLOCK, TOK_BLOCK) if rows_out % r == 0)
    return pl.pallas_call(
        functools.partial(_odd_out_kernel, alpha, lt - TOK_BLOCK),
        grid=(b, rows_out // tm),
        in_specs=[_tok_spec(di, tm), _tok_spec(d, tm), _gate_spec(d), _full2((di, d)), _full2((1, d)),
                  _full2((1, d))],
        out_specs=_tok_spec(d, tm),
        out_shape=jax.ShapeDtypeStruct((b, rows_out, d), F32),
        compiler_params=_cparams("arbitrary", "arbitrary"),
        name="odd_out",
    )(y, xs, gmod, w_out, row(ln_g), row(ln_b))


def _rope(x, cos, sin):
    width = x.shape[-1]
    lane = lax.broadcasted_iota(jnp.int32, (1, width), 1)
    first = (lane % 32) < 16
    partner = jnp.where(first, pltpu.roll(x, width - 16, 1), pltpu.roll(x, 16, 1))
    return x * cos + partner * sin


def _attn_kernel(lam_init, tk, n_full, tail, lam_ref, q_ref, k_ref, vt_ref, g_ref, sg_ref, *rest):
    y_ref, m_scr, acc_scr = rest[-3:]
    tq = q_ref.shape[1]
    ones_rows = 16
    tile = min(ATTN_TILE, 2 * tq)
    n_tiles = 2 * tq // tile
    q = q_ref[0].astype(F32) * (DIFF_HEAD ** -0.5 * float(np.log2(np.e)))
    lane = lax.broadcasted_iota(jnp.int32, (1, LANES), 1)
    q_cat = jnp.concatenate([jnp.where(lane < DIFF_HEAD, q, 0.0), jnp.where(lane >= DIFF_HEAD, q, 0.0)],
                            axis=0).astype(BF16)
    m_scr[...] = jnp.full(m_scr.shape, -jnp.inf, F32)
    acc_scr[...] = jnp.zeros(acc_scr.shape, F32)

    def scores(item):
        start, size, c = item
        return _nt(k_ref[0, pl.ds(start, size), :], q_cat[c * tile:(c + 1) * tile])

    def run(blocks):
        cols = [slice(c * tile, (c + 1) * tile) for c in range(n_tiles)]
        m_run = [m_scr[:, cs] for cs in cols]
        acc = [acc_scr[:, cs] for cs in cols]
        vta = None
        pending = None
        items = [(start, size, c) for start, size in blocks for c in range(n_tiles)]
        st_next = scores(items[0])
        for i, (start, size, c) in enumerate(items):
            st = st_next
            if i + 1 < len(items):
                st_next = scores(items[i + 1])
            if pending is not None:
                pc, pcorr, pvta, ppt = pending
                acc[pc] = acc[pc] * pcorr + _mm(pvta, ppt)
            if c == 0:
                vta = jnp.concatenate([vt_ref[0, :, pl.ds(start, size)], jnp.ones((ones_rows, size), BF16)], axis=0)
            m_new = jnp.maximum(m_run[c], jnp.max(st, axis=0, keepdims=True))
            pending = (c, jnp.exp2(m_run[c] - m_new), vta, jnp.exp2(st - m_new).astype(BF16))
            m_run[c] = m_new
        pc, pcorr, pvta, ppt = pending
        acc[pc] = acc[pc] * pcorr + _mm(pvta, ppt)
        for c, cs in enumerate(cols):
            m_scr[:, cs] = m_run[c]
            acc_scr[:, cs] = acc[c]

    tail_block = [(n_full * tk, tail)]
    if n_full <= ATTN_UNROLL:
        run([(u * tk, tk) for u in range(n_full)] + tail_block)
    else:
        assert n_full % ATTN_UNROLL == 0
        def body(j, carry):
            run([(pl.multiple_of((j * ATTN_UNROLL + u) * tk, tk), tk) for u in range(ATTN_UNROLL)])
            return carry
        lax.fori_loop(0, n_full // ATTN_UNROLL, body, 0)
        run(tail_block)

    a0 = acc_scr[:, :tq]
    a1 = acc_scr[:, tq:]
    o_t = a0[:LANES] / a0[LANES:LANES + 1] - lam_ref[0] * (a1[:LANES] / a1[LANES:LANES + 1])
    o = jnp.transpose(o_t)
    y = o * lax.rsqrt(jnp.mean(o * o, axis=-1, keepdims=True) + LN_EPS) * sg_ref[...] * (1.0 - lam_init)
    y_ref[0] = (y * _silu(g_ref[0].astype(F32))).astype(y_ref.dtype)


def _diff_attention(p_qk, p_vt, p_g, lam, subln_g, lam_init):
    b, lt, di = p_g.shape
    nh = di // LANES
    t = lt - TOK_BLOCK
    tq = tk = 512
    assert t % tq == 0
    ctx_blk = t // TOK_BLOCK
    sm = pl.BlockSpec(memory_space=pltpu.SMEM)
    sg_spec = pl.BlockSpec((1, LANES), lambda bb, h, i: (0, 0))
    scratch = lambda n: [pltpu.VMEM((1, 2 * n), F32), pltpu.VMEM((LANES + 16, 2 * n), F32)]
    args = (lam.reshape(1), p_qk, p_qk, p_vt, p_g, subln_g.reshape(1, LANES))
    y = pl.pallas_call(
        functools.partial(_attn_kernel, lam_init, tk, t // tk, TOK_BLOCK),
        grid=(b, nh, t // tq),
        in_specs=[sm,
                  pl.BlockSpec((1, tq, LANES), lambda bb, h, i: (bb, i, h)),
                  pl.BlockSpec((1, lt, LANES), lambda bb, h, i: (bb, 0, nh + h)),
                  pl.BlockSpec((1, LANES, lt), lambda bb, h, i: (bb, h, 0)),
                  pl.BlockSpec((1, tq, LANES), lambda bb, h, i: (bb, i, h)),
                  sg_spec],
        out_specs=pl.BlockSpec((1, tq, LANES), lambda bb, h, i: (bb, i, h)),
        out_shape=jax.ShapeDtypeStruct((b, lt, di), BF16),
        scratch_shapes=scratch(tq),
        compiler_params=_cparams("arbitrary", "arbitrary", "arbitrary"),
        name="diff_attn",
    )(*args)
    blk = lambda bb, h, i: (bb, ctx_blk, h)
    return pl.pallas_call(
        functools.partial(_attn_kernel, lam_init, tk, 0, TOK_BLOCK),
        grid=(b, nh, 1),
        in_specs=[sm,
                  pl.BlockSpec((1, TOK_BLOCK, LANES), blk),
                  pl.BlockSpec((1, TOK_BLOCK, LANES), lambda bb, h, i: (bb, ctx_blk, nh + h)),
                  pl.BlockSpec((1, LANES, TOK_BLOCK), lambda bb, h, i: (bb, h, ctx_blk)),
                  pl.BlockSpec((1, TOK_BLOCK, LANES), blk),
                  sg_spec,
                  pl.BlockSpec(memory_space=pl.ANY)],
        out_specs=pl.BlockSpec((1, TOK_BLOCK, LANES), blk),
        out_shape=jax.ShapeDtypeStruct((b, lt, di), BF16),
        scratch_shapes=scratch(TOK_BLOCK),
        input_output_aliases={6: 0},
        compiler_params=_cparams("arbitrary", "arbitrary", "arbitrary"),
        name="diff_attn_ctx",
    )(*args, y)


def _rope_tables(n_ctx, t):
    quarter = DIFF_HEAD // 4
    inv = ROPE_BASE ** (-jnp.arange(quarter, dtype=F32) / quarter)
    pos = jnp.arange(t)
    rows = (pos // GRID_W).astype(F32)[:, None] * inv
    cols = (pos % GRID_W).astype(F32)[:, None] * inv
    cos64 = jnp.concatenate([jnp.cos(rows), jnp.cos(rows), jnp.cos(cols), jnp.cos(cols)], -1)
    sin64 = jnp.concatenate([-jnp.sin(rows), jnp.sin(rows), -jnp.sin(cols), jnp.sin(cols)], -1)
    cos = jnp.concatenate([jnp.tile(cos64, (1, 2)), jnp.ones((n_ctx, LANES), F32)], 0)
    sin = jnp.concatenate([jnp.tile(sin64, (1, 2)), jnp.zeros((n_ctx, LANES), F32)], 0)
    return cos, sin


def _even_layer(alpha, xs, scale, shift, gmod, w_in, w_out, mu_prev, mu_next, w0, w2, a0, a2, k_k, k_a, r_k,
                lnx_g, lnx_b, lb, norm_g, ln_g, ln_b):
    w = k_k.shape[-1]
    sw = mu_prev.shape[-1]
    hw = 4 * lb.shape[-1]
    w_in = w_in.astype(BF16)
    p_rwkv = _project(xs, scale, shift, w_in[:, :sw], sw, F32)
    p_hgrn = _project(xs, scale, shift, w_in[:, sw:sw + hw], hw // 2, F32)
    gate = _project(xs, scale, shift, w_in[:, sw + hw:], w_in.shape[1] - sw - hw, BF16)
    v, bonus, ops_f, ops_r = _rwkv_prep(p_rwkv, mu_prev, mu_next, w0, w2, a0, a2, k_k, k_a, r_k)
    ya0, yb0 = _mix_scan(ops_f, v, p_hgrn, lb[0], 0, False)
    ya1, yb1 = _mix_scan(ops_r, v, p_hgrn, lb[1], 1, True)
    return _even_out(alpha, ya0, ya1, bonus, yb0, yb1, gate, xs, gmod, w_out.astype(BF16),
                     lnx_g, lnx_b, norm_g, ln_g, ln_b)


def _odd_layer(alpha, xs, scale, shift, gmod, w_in, w_out, lam_p, subln_g, lam_init, cos, sin, ln_g, ln_b,
               rows_out):
    di = w_out.shape[0]
    w_in = w_in.astype(BF16)
    p_qk = _project(xs, scale, shift, w_in[:, :2 * di], di, BF16, rope=(cos, sin))
    p_vt = _project_t(xs, scale, shift, jnp.transpose(w_in[:, 2 * di:3 * di]), BF16)
    p_g = _project(xs, scale, shift, w_in[:, 3 * di:], di, BF16)
    lam = jnp.exp(jnp.sum(lam_p[0] * lam_p[1])) - jnp.exp(jnp.sum(lam_p[2] * lam_p[3])) + lam_init
    y = _diff_attention(p_qk, p_vt, p_g, lam, subln_g, lam_init)
    return _odd_out(alpha, y, xs, gmod, w_out.astype(BF16), ln_g, ln_b, rows_out)


def kernel(x, c, ctx, c_ctx, ada_w, ada_b, ln_g, ln_b, even_w_in, even_w_out, rwkv_mu_prev, rwkv_mu_next, rwkv_w0, rwkv_w2, rwkv_a0, rwkv_a2, rwkv_k_k, rwkv_k_a, rwkv_r_k, rwkv_lnx_g, rwkv_lnx_b, hgrn_lb_logits, hgrn_norm_g, odd_w_in, odd_w_out, diff_lambda, diff_subln_g):
    b, t, d = x.shape
    n_ctx = ctx.shape[1]
    depth = ada_w.shape[0]
    assert n_ctx == TOK_BLOCK and t % TOK_BLOCK == 0 and b + 1 <= 8
    alpha = (2.0 * depth) ** 0.25

    xs = jnp.concatenate([x, ctx], axis=1)
    cvec = jnp.concatenate([c, c_ctx[None], jnp.zeros((8 - b - 1, d), F32)], axis=0)
    mods = _ada_mods(cvec, ada_w, ada_b)
    lb_all = jax.nn.softmax(hgrn_lb_logits.astype(F32), axis=0)
    lb_all = jnp.cumsum(lb_all, axis=0) - lb_all[0]
    cos, sin = _rope_tables(n_ctx, t)

    for layer in range(depth):
        m = mods[layer]
        per_seg = lambda u: jnp.stack([jnp.broadcast_to(u[b], (b, d)), u[:b]], axis=1)[:, :, None, :]
        shift, scale, gmod = (per_seg(m[:, j * d:(j + 1) * d]) for j in range(3))
        j = layer // 2
        if layer % 2 == 0:
            xs = _even_layer(alpha, xs, scale, shift, gmod, even_w_in[j], even_w_out[j], rwkv_mu_prev[j],
                             rwkv_mu_next[j], rwkv_w0[j], rwkv_w2[j], rwkv_a0[j], rwkv_a2[j], rwkv_k_k[j],
                             rwkv_k_a[j], rwkv_r_k[j], rwkv_lnx_g[j], rwkv_lnx_b[j], lb_all[j], hgrn_norm_g[j],
                             ln_g[layer], ln_b[layer])
        else:
            lam_init = 0.8 - 0.6 * float(np.exp(-0.3 * layer))
            xs = _odd_layer(alpha, xs, scale, shift, gmod, odd_w_in[j], odd_w_out[j], diff_lambda[j],
                            diff_subln_g[j], lam_init, cos, sin, ln_g[layer], ln_b[layer],
                            t if layer == depth - 1 else t + n_ctx)
    return xs[:, :t]
```

```python
import functools

import numpy as np
import jax
import jax.numpy as jnp
from jax import lax
from jax.experimental import pallas as pl
from jax.experimental.pallas import tpu as pltpu

F32 = jnp.float32
BF16 = jnp.bfloat16
HI = lax.Precision.HIGHEST

GRID_W = 64
RWKV_HEAD = 64
RWKV_LORA = 64
RWKV_GN_EPS = 64e-5
HGRN_HEAD = 128
DIFF_HEAD = 64
ROPE_BASE = 10000.0
LN_EPS = 1e-5
LOG2E = float(np.log2(np.e))

LANES = 128
MXU_DIM = 256
VMEM_LIMIT = 56 * 1024 * 1024

CHUNK = 64
SUB = 16
TOK_BLOCK = 256
PROJ_ROWS = 768
GROUP_HEADS = MXU_DIM // RWKV_HEAD
HGRN_PAR = 4
ATTN_UNROLL = 16
ATTN_TILE = 1024


def _cparams(*sem):
    return pltpu.CompilerParams(dimension_semantics=sem, vmem_limit_bytes=VMEM_LIMIT)


def _nt(a, b):
    return lax.dot_general(a, b, (((1,), (1,)), ((), ())), preferred_element_type=F32)


def _tn(a, b):
    return lax.dot_general(a, b, (((0,), (0,)), ((), ())), preferred_element_type=F32)


def _mm(a, b):
    return jnp.dot(a, b, preferred_element_type=F32)


def _mm_exact(a, b):
    return jnp.dot(a, b, preferred_element_type=F32, precision=HI)


def _sigmoid(x):
    return 1.0 / (1.0 + jnp.exp(-x))


def _silu(x):
    return x * _sigmoid(x)


def _lane_tiles(x):
    return [x[:, j * LANES:(j + 1) * LANES] for j in range(x.shape[-1] // LANES)]


def _group64_sum(x):
    lane = lax.broadcasted_iota(jnp.int32, (1, LANES), 1)
    low = lane < RWKV_HEAD
    out = []
    for xt in _lane_tiles(x):
        s_all = jnp.sum(xt, axis=-1, keepdims=True)
        s_lo = jnp.sum(jnp.where(low, xt, 0.0), axis=-1, keepdims=True)
        out.append(jnp.where(low, s_lo, s_all - s_lo))
    return jnp.concatenate(out, axis=-1)


def _group128_mean(x):
    out = []
    for xt in _lane_tiles(x):
        out.append(jnp.broadcast_to(jnp.mean(xt, axis=-1, keepdims=True), xt.shape))
    return jnp.concatenate(out, axis=-1)


def _ada_kernel(c_ref, w_ref, b_ref, o_ref):
    cond = _silu(c_ref[...])
    o_ref[0] = _mm_exact(cond, w_ref[0]) + b_ref[0]


def _ada_mods(cvec, ada_w, ada_b):
    depth, d, d3 = ada_w.shape
    tn = 1024
    return pl.pallas_call(
        _ada_kernel,
        grid=(depth, d3 // tn),
        in_specs=[
            pl.BlockSpec((8, d), lambda l, j: (0, 0)),
            pl.BlockSpec((1, d, tn), lambda l, j: (l, 0, j)),
            pl.BlockSpec((1, 1, tn), lambda l, j: (l, 0, j)),
        ],
        out_specs=pl.BlockSpec((1, 8, tn), lambda l, j: (l, 0, j)),
        out_shape=jax.ShapeDtypeStruct((depth, 8, d3), F32),
        compiler_params=_cparams("arbitrary", "arbitrary"),
        name="ada_mods",
    )(cvec, ada_w, ada_b.reshape(depth, 1, d3))


def _proj_rows(lt):
    return PROJ_ROWS if lt % PROJ_ROWS == 0 else TOK_BLOCK


def _modulate(n_lat, axis, x_ref, sc_ref, sh_ref):
    tm = x_ref.shape[1]
    rows = pl.program_id(axis) * tm + lax.broadcasted_iota(jnp.int32, (tm, 1), 0)
    is_ctx = rows >= n_lat
    sc = jnp.where(is_ctx, sc_ref[0, 0], sc_ref[0, 1])
    sh = jnp.where(is_ctx, sh_ref[0, 0], sh_ref[0, 1])
    return (x_ref[0] * (1.0 + sc) + sh).astype(BF16)


def _proj_kernel(n_lat, x_ref, sc_ref, sh_ref, w_ref, o_ref):
    o_ref[0] = _mm(_modulate(n_lat, 2, x_ref, sc_ref, sh_ref), w_ref[...]).astype(o_ref.dtype)


def _proj_rope_kernel(n_lat, x_ref, sc_ref, sh_ref, w_ref, cos_ref, sin_ref, o_ref):
    p = _mm(_modulate(n_lat, 2, x_ref, sc_ref, sh_ref), w_ref[...])
    reps = p.shape[-1] // LANES
    cos = jnp.concatenate([cos_ref[...]] * reps, axis=-1)
    sin = jnp.concatenate([sin_ref[...]] * reps, axis=-1)
    o_ref[0] = _rope(p, cos, sin).astype(o_ref.dtype)


def _proj_t_kernel(n_lat, x_ref, sc_ref, sh_ref, wt_ref, o_ref):
    o_ref[0] = _nt(wt_ref[...], _modulate(n_lat, 1, x_ref, sc_ref, sh_ref)).astype(o_ref.dtype)


def _project(xs, scale, shift, w, tn, out_dtype, rope=None):
    b, lt, d = xs.shape
    n = w.shape[1]
    tm = _proj_rows(lt)
    n_lat = lt - TOK_BLOCK
    in_specs = [
        pl.BlockSpec((1, tm, d), lambda j, bb, i: (bb, i, 0)),
        pl.BlockSpec((1, 2, 1, d), lambda j, bb, i: (bb, 0, 0, 0)),
        pl.BlockSpec((1, 2, 1, d), lambda j, bb, i: (bb, 0, 0, 0)),
        pl.BlockSpec((d, tn), lambda j, bb, i: (0, j)),
    ]
    args = (xs, scale, shift, w)
    if rope is not None:
        in_specs += [pl.BlockSpec((tm, LANES), lambda j, bb, i: (i, 0))] * 2
        args += tuple(rope)
    return pl.pallas_call(
        functools.partial(_proj_kernel if rope is None else _proj_rope_kernel, n_lat),
        grid=(n // tn, b, lt // tm),
        in_specs=in_specs,
        out_specs=pl.BlockSpec((1, tm, tn), lambda j, bb, i: (bb, i, j)),
        out_shape=jax.ShapeDtypeStruct((b, lt, n), out_dtype),
        compiler_params=_cparams("arbitrary", "arbitrary", "arbitrary"),
        name="mod_proj" if rope is None else "mod_proj_rope",
    )(*args)


def _project_t(xs, scale, shift, wt, out_dtype):
    b, lt, d = xs.shape
    n = wt.shape[0]
    tm = _proj_rows(lt)
    return pl.pallas_call(
        functools.partial(_proj_t_kernel, lt - TOK_BLOCK),
        grid=(b, lt // tm),
        in_specs=[
            pl.BlockSpec((1, tm, d), lambda bb, i: (bb, i, 0)),
            pl.BlockSpec((1, 2, 1, d), lambda bb, i: (bb, 0, 0, 0)),
            pl.BlockSpec((1, 2, 1, d), lambda bb, i: (bb, 0, 0, 0)),
            pl.BlockSpec((n, d), lambda bb, i: (0, 0)),
        ],
        out_specs=pl.BlockSpec((1, n, tm), lambda bb, i: (bb, 0, i)),
        out_shape=jax.ShapeDtypeStruct((b, n, lt), out_dtype),
        compiler_params=_cparams("arbitrary", "arbitrary"),
        name="mod_proj_t",
    )(xs, scale, shift, wt)


def _lora(x, w_ref):
    x1 = x.astype(BF16)
    x2 = (x - x1.astype(F32)).astype(BF16)
    return _mm(jnp.concatenate([x1, x1, x2], axis=-1), w_ref[...])


def _rwkv_prep_kernel(nblk, p_ref, prev_ref, next_ref, mup_ref, mun_ref, w0_ref, w2_ref, a0_ref, a2_ref,
                      kk_ref, ka_ref, rk_ref, cm_ref,
                      v_out, bonus_out, *dir_outs):
    i = pl.program_id(1)
    w = kk_ref.shape[-1]
    p = p_ref[0]
    row = lax.broadcasted_iota(jnp.int32, (TOK_BLOCK, 1), 0)
    prev_row = jnp.where(jnp.logical_and(i >= 1, i <= nblk - 2), prev_ref[0][7:8, :], 0.0)
    next_row = jnp.where(i <= nblk - 3, next_ref[0][0:1, :], 0.0)
    prev = jnp.where(row == 0, prev_row, pltpu.roll(p, 1, 0))
    nxt = jnp.where(row == TOK_BLOCK - 1, next_row, pltpu.roll(p, TOK_BLOCK - 1, 0))
    ps = p * (1.0 - mup_ref[...] - mun_ref[...]) + mup_ref[...] * prev + mun_ref[...] * nxt

    r = ps[:, 0:w]
    k = ps[:, w:2 * w]
    v = ps[:, 2 * w:3 * w]
    wlo = ps[:, 3 * w:3 * w + 2 * RWKV_LORA]
    alo = ps[:, 3 * w + 2 * RWKV_LORA:3 * w + 4 * RWKV_LORA]

    z = _lora(jnp.tanh(wlo), w2_ref) + w0_ref[...]
    lw = _sigmoid(z) * (-LOG2E * float(np.exp(-0.5)))
    a = _sigmoid(_lora(alo, a2_ref) + a0_ref[...])

    kk = k * kk_ref[...]
    kk = kk * lax.rsqrt(_group64_sum(kk * kk) + 1e-12)
    bonus_out[0] = (_group64_sum(r * k * rk_ref[...]) * v).astype(bonus_out.dtype)
    v_out[0] = v.astype(BF16)

    for d in range(2):
        rh_o, kh_o, bh_o, kap_o, kg_o, bg_o, gl_o = dir_outs[7 * d:7 * d + 7]
        lw_d = lw[:, d * w:(d + 1) * w]
        a_d = a[:, d * w:(d + 1) * w]
        kd = k * (1.0 + (a_d - 1.0) * ka_ref[...])
        b_d = kk * a_d
        sums = _mm(cm_ref[d], _split3(lw_d))
        g = sums[:TOK_BLOCK]
        g_all = sums[TOK_BLOCK:]
        e_neg = jnp.exp2(-g)
        e_rest = jnp.exp2(g_all - g)
        rh_o[0] = (r * jnp.exp2(g)).astype(BF16)
        kh_o[0] = (kd * e_neg).astype(BF16)
        bh_o[0] = (b_d * e_neg).astype(BF16)
        kap_o[0] = (kk * jnp.exp2(g - lw_d)).astype(BF16)
        kg_o[0] = (kd * e_rest).astype(BF16)
        bg_o[0] = (b_d * e_rest).astype(BF16)
        e_all = jnp.exp2(g_all)
        for c in range(TOK_BLOCK // CHUNK):
            gl_o[0, c] = e_all[c * CHUNK:c * CHUNK + 1, :]


def _rwkv_sum_consts(reverse):
    t = np.arange(TOK_BLOCK)[:, None]
    s = np.arange(TOK_BLOCK)[None, :]
    same = (t // CHUNK) == (s // CHUNK)
    order = (s >= t) if reverse else (s <= t)
    both = np.concatenate([same & order, same], axis=0).astype(np.float32)
    return np.concatenate([both] * 3, axis=1)


def _lora_pieces(m):
    hi = m.astype(BF16)
    lo = (m - hi.astype(F32)).astype(BF16)
    return jnp.concatenate([hi, lo, hi], axis=0)


def _rwkv_prep(p_rwkv, mu_prev, mu_next, w0, w2, a0, a2, k_k, k_a, r_k):
    b, lt, sw = p_rwkv.shape
    w = k_k.shape[-1]
    nblk = lt // TOK_BLOCK
    nch = lt // CHUNK
    halo = TOK_BLOCK // 8
    zl = jnp.zeros((RWKV_LORA, w), F32)
    w2cat = _lora_pieces(jnp.concatenate([jnp.concatenate([w2[0], zl], 1), jnp.concatenate([zl, w2[1]], 1)], 0))
    a2cat = _lora_pieces(jnp.concatenate([jnp.concatenate([a2[0], zl], 1), jnp.concatenate([zl, a2[1]], 1)], 0))
    cm = jnp.asarray(np.stack([_rwkv_sum_consts(False), _rwkv_sum_consts(True)]), dtype=BF16)
    row = lambda u: u.reshape(1, -1)
    full = lambda shape: pl.BlockSpec(shape, lambda bb, i: (0,) * len(shape))
    tok = lambda n, dt: (pl.BlockSpec((1, TOK_BLOCK, n), lambda bb, i: (bb, i, 0)), jax.ShapeDtypeStruct((b, lt, n), dt))
    gl = (pl.BlockSpec((1, TOK_BLOCK // CHUNK, 1, w), lambda bb, i: (bb, i, 0, 0)),
          jax.ShapeDtypeStruct((b, nch, 1, w), F32))
    outs = [tok(w, BF16), tok(w, BF16)] + 2 * ([tok(w, BF16)] * 6 + [gl])
    res = pl.pallas_call(
        functools.partial(_rwkv_prep_kernel, nblk),
        grid=(b, nblk),
        in_specs=[
            pl.BlockSpec((1, TOK_BLOCK, sw), lambda bb, i: (bb, i, 0)),
            pl.BlockSpec((1, 8, sw), lambda bb, i: (bb, jnp.maximum(i * halo - 1, 0), 0)),
            pl.BlockSpec((1, 8, sw), lambda bb, i: (bb, jnp.minimum((i + 1) * halo, lt // 8 - 1), 0)),
            full((1, sw)), full((1, sw)), full((1, 2 * w)), full(w2cat.shape),
            full((1, 2 * w)), full(a2cat.shape), full((1, w)), full((1, w)), full((1, w)),
            full(cm.shape),
        ],
        out_specs=[o[0] for o in outs],
        out_shape=[o[1] for o in outs],
        compiler_params=_cparams("arbitrary", "arbitrary"),
        name="rwkv_prep",
    )(p_rwkv, p_rwkv, p_rwkv, row(mu_prev), row(mu_next), row(w0), w2cat, row(a0), a2cat,
      row(k_k), row(k_a), row(r_k), cm)
    v, bonus = res[0], res[1]
    return v, bonus, res[2:9], res[9:16]


def _rwkv_masks(reverse):
    n = GROUP_HEADS * CHUNK
    i = np.arange(n)[:, None]
    j = np.arange(n)[None, :]
    same_head = (i // CHUNK) == (j // CHUNK)
    before = (j > i) if reverse else (j < i)
    strict = same_head & before
    incl = same_head & (before | (i == j))
    levels = []
    size = 1
    while size < CHUNK:
        levels.append(same_head & ((i // (2 * size)) == (j // (2 * size))) & ((i // size) != (j // size)))
        size *= 2
    head_lane = (i // CHUNK) == (j // RWKV_HEAD)
    tri = np.stack([strict, incl]).astype(np.float32)
    return tri, np.stack(levels).astype(np.float32), head_lane.astype(np.float32)


def _scan_chunk_index(c, nch, reverse):
    if reverse:
        return nch - 1 - c
    n_ctx = TOK_BLOCK // CHUNK
    return jnp.where(c < n_ctx, nch - n_ctx + c, c - n_ctx)


def _split3(x):
    x1 = x.astype(BF16)
    r1 = x - x1.astype(F32)
    x2 = r1.astype(BF16)
    x3 = (r1 - x2.astype(F32)).astype(BF16)
    return jnp.concatenate([x1, x2, x3], axis=0)


def _hgrn_chunk_consts(reverse):
    t = np.arange(CHUNK)[:, None]
    s = np.arange(CHUNK)[None, :]
    order = (s >= t) if reverse else (s <= t)
    earlier_sub = ((s // SUB) > (t // SUB)) if reverse else ((s // SUB) < (t // SUB))
    both = np.concatenate([order, earlier_sub], axis=0).astype(np.float32)
    return np.concatenate([both] * 3, axis=1)


def _rwkv_chunk_stages(rh_ref, kh_ref, bh_ref, kap_ref, kg_ref, bg_ref, v_ref, gl_ref, tri_ref, lvl_ref, hm_ref,
                       y_ref, s_ref):
    hm = hm_ref[...]
    strict = tri_ref[0]
    incl = tri_ref[1]
    eye = incl - strict
    n_levels = lvl_ref.shape[0]
    groups = range(y_ref.shape[-1] // MXU_DIM)
    lanes = lambda ref, g: ref[0, :, g * MXU_DIM:(g + 1) * MXU_DIM]
    stack = lambda ref, g: jnp.concatenate([lanes(ref, g)] * GROUP_HEADS, axis=0) * hm

    kap, bh, a_ab = [], [], []
    for g in groups:
        kap.append(stack(kap_ref, g))
        bh.append(stack(bh_ref, g))
        a_ab.append(_nt(kap[g], bh[g]) * strict)
    yield
    kh = [stack(kh_ref, g) for g in groups]
    a_ak = [(_nt(kap[g], kh[g]) * strict).astype(BF16) for g in groups]
    rh = [stack(rh_ref, g) for g in groups]
    p_rb = [(_nt(rh[g], bh[g]) * incl).astype(BF16) for g in groups]
    p_rk = [(_nt(rh[g], kh[g]) * incl).astype(BF16) for g in groups]
    vs = [stack(v_ref, g) for g in groups]
    t = [eye - a_ab[g] * lvl_ref[0] for g in groups]
    a_abb = [a_ab[g].astype(BF16) for g in groups]
    for lv in range(1, n_levels):
        yield
        tb = [t[g].astype(BF16) for g in groups]
        x = [_mm(tb[g], a_abb[g]).astype(BF16) for g in groups]
        yield
        t = [t[g] - _mm(x[g], tb[g]) * lvl_ref[lv] for g in groups]
    yield
    s0 = [s_ref[g] for g in groups]
    s0b = [s0[g].astype(BF16) for g in groups]
    hm32 = hm.astype(F32)
    ks0 = [jnp.concatenate([_nt(lanes(kap_ref, g), s0b[g])] * GROUP_HEADS, axis=0) * hm32 for g in groups]
    wmat = [(ks0[g] + _mm(a_ak[g], vs[g])).astype(BF16) for g in groups]
    yield
    ub = [(-_mm(t[g].astype(BF16), wmat[g])).astype(BF16) for g in groups]
    yield
    ys = [_mm(p_rb[g], ub[g]) + _mm(p_rk[g], vs[g]) for g in groups]
    for g in groups:
        y = _nt(lanes(rh_ref, g), s0b[g])
        for h in range(GROUP_HEADS):
            y = y + ys[g][h * CHUNK:(h + 1) * CHUNK]
        y_ref[0, :, g * MXU_DIM:(g + 1) * MXU_DIM] = y.astype(y_ref.dtype)
    yield
    for g in groups:
        gam = gl_ref[0, 0][:, g * MXU_DIM:(g + 1) * MXU_DIM]
        s_ref[g] = s0[g] * gam + _tn(ub[g], stack(bg_ref, g)) + _tn(vs[g], stack(kg_ref, g))


def _hgrn_chunk_stages(reverse, q_ref, f_ref, i_ref, lb_ref, cm_ref, o_ref, s_ref, g_scr, q_scr):
    n_sub = CHUNK // SUB
    n_heads = q_ref.shape[-1] // HGRN_HEAD
    sub_order = list(range(n_sub))[::-1] if reverse else list(range(n_sub))
    row_sub = lax.broadcasted_iota(jnp.int32, (CHUNK, 1), 0) // SUB
    srow = lax.broadcasted_iota(jnp.int32, (SUB, 1), 0)
    lane = lax.broadcasted_iota(jnp.int32, (1, LANES), 1)
    last_row = 0 if reverse else CHUNK - 1

    kgate = (1.0 - lb_ref[...]) * _sigmoid(-f_ref[0])
    sums = _mm(cm_ref[...], _split3(jnp.log1p(-kgate) * LOG2E))
    g_all = sums[:CHUNK]
    gb_all = sums[CHUNK:]
    q_all = _silu(q_ref[0])
    g_scr[...] = g_all
    q_scr[...] = q_all
    yield
    for hg in range(n_heads // HGRN_PAR):
        par = range(HGRN_PAR)
        hs = [hg * HGRN_PAR + j for j in par]
        cols = [slice(h * HGRN_HEAD, (h + 1) * HGRN_HEAD) for h in hs]
        g = [g_all[:, cs] for cs in cols]
        gb = [gb_all[:, cs] for cs in cols]
        q = [q_all[:, cs] for cs in cols]
        k = [kgate[:, cs] for cs in cols]
        vb = [i_ref[0, :, cs].astype(BF16) for cs in cols]
        s0 = [s_ref[h] for h in hs]
        q_in = [q[j] * jnp.exp2(g[j] - gb[j]) for j in par]
        at = []
        for j in par:
            kts, qts = [], []
            for si in sub_order[1:]:
                gb_i = gb[j][si * SUB:si * SUB + 1, :]
                earlier = (row_sub > si) if reverse else (row_sub < si)
                kts.append(jnp.where(earlier, k[j] * jnp.exp2(jnp.minimum(gb_i - g[j], 0.0)), 0.0).astype(BF16))
                qts.append(jnp.where(row_sub == si, q_in[j], 0.0).astype(BF16))
            at.append(_nt(jnp.concatenate(kts, axis=-1), jnp.concatenate(qts, axis=-1)))
        yield
        diag = [[] for _ in par]
        for si in range(n_sub):
            blk = slice(si * SUB, (si + 1) * SUB)
            acc = [jnp.zeros((SUB, LANES), F32) for _ in par]
            for tt in range(SUB):
                t = si * SUB + tt
                valid = jnp.logical_and(lane == t, (srow >= tt) if reverse else (srow <= tt))
                for j in par:
                    e = jnp.exp2(jnp.minimum(g_scr[t:t + 1, cols[j]] - g[j][blk], 0.0))
                    col = jnp.sum(k[j][blk] * e * q_scr[t:t + 1, cols[j]], axis=-1, keepdims=True)
                    acc[j] = jnp.where(valid, col, acc[j])
                if tt % 8 == 7:
                    yield
            for j in par:
                diag[j].append(acc[j])
        for j in par:
            a = (at[j] + jnp.concatenate(diag[j], axis=0)[:, :CHUNK]).astype(BF16)
            o = _tn(a, vb[j]) + _nt((q[j] * jnp.exp2(g[j])).astype(BF16), s0[j].astype(BF16))
            o_ref[0, :, cols[j]] = o.astype(o_ref.dtype)
            g_last = g[j][last_row:last_row + 1, :]
            k_out = (k[j] * jnp.exp2(g_last - g[j])).astype(BF16)
            s_ref[hs[j]] = s0[j] * jnp.exp2(g_last) + _tn(vb[j], k_out)
        yield


def _mix_scan_kernel(reverse, *refs):
    rwkv_in, hgrn_in = refs[:11], refs[11:16]
    y_ref, o_ref, s_ref, hs_ref, g_scr, q_scr = refs[16:]

    @pl.when(pl.program_id(1) == 0)
    def _():
        s_ref[...] = jnp.zeros_like(s_ref)
        hs_ref[...] = jnp.zeros_like(hs_ref)

    streams = [_rwkv_chunk_stages(*rwkv_in, y_ref, s_ref),
               _hgrn_chunk_stages(reverse, *hgrn_in, o_ref, hs_ref, g_scr, q_scr)]
    done = object()
    while streams:
        streams = [st for st in streams if next(st, done) is not done]


def _mix_scan(dir_ops, v, p_hgrn, lb_d, d, reverse):
    rh, kh, bh, kap, kg, bg, gl = dir_ops
    b, lt, w = v.shape
    nch = lt // CHUNK
    n = GROUP_HEADS * CHUNK
    tri, lvl, hm = _rwkv_masks(reverse)
    cm = jnp.asarray(_hgrn_chunk_consts(reverse), dtype=BF16)
    cidx = lambda c: _scan_chunk_index(c, nch, reverse)
    tok = pl.BlockSpec((1, CHUNK, w), lambda bb, c: (bb, cidx(c), 0))
    col = lambda j: pl.BlockSpec((1, CHUNK, w), lambda bb, c: (bb, cidx(c), j))
    full = lambda shape: pl.BlockSpec(shape, lambda bb, c: (0,) * len(shape))
    return pl.pallas_call(
        functools.partial(_mix_scan_kernel, reverse),
        grid=(b, nch),
        in_specs=[tok] * 7 + [
            pl.BlockSpec((1, 1, 1, w), lambda bb, c: (bb, cidx(c), 0, 0)),
            full(tri.shape), full(lvl.shape), full(hm.shape),
            col(0), col(1 + d), col(3), full((1, w)), full(cm.shape),
        ],
        out_specs=[tok, tok],
        out_shape=[jax.ShapeDtypeStruct((b, lt, w), BF16)] * 2,
        scratch_shapes=[pltpu.VMEM((w // MXU_DIM, n, n), F32),
                        pltpu.VMEM((w // HGRN_HEAD, HGRN_HEAD, HGRN_HEAD), F32),
                        pltpu.VMEM((CHUNK, w), F32), pltpu.VMEM((CHUNK, w), F32)],
        compiler_params=_cparams("arbitrary", "arbitrary"),
        name="mix_scan_rev" if reverse else "mix_scan_fwd",
    )(rh, kh, bh, kap, kg, bg, v, gl, jnp.asarray(tri), jnp.asarray(lvl), jnp.asarray(hm, dtype=BF16),
      p_hgrn, p_hgrn, p_hgrn, lb_d.reshape(1, w), cm)


def _residual_ln(alpha, n_lat, x, gm_ref, proj, lng, lnb):
    tm = x.shape[0]
    rows = pl.program_id(1) * tm + lax.broadcasted_iota(jnp.int32, (tm, 1), 0)
    gm = jnp.where(rows >= n_lat, gm_ref[0, 0], gm_ref[0, 1])
    z = alpha * x + gm * proj
    mu = jnp.mean(z, axis=-1, keepdims=True)
    zc = z - mu
    var = jnp.mean(zc * zc, axis=-1, keepdims=True)
    return zc * lax.rsqrt(var + LN_EPS) * lng + lnb


def _even_out_kernel(alpha, n_lat, ya0_ref, ya1_ref, bonus_ref, yb0_ref, yb1_ref, gate_ref, x_ref, gm_ref, w_ref,
                     lnxg_ref, lnxb_ref, ng_ref, lng_ref, lnb_ref, o_ref):
    y = ya0_ref[0].astype(F32) + ya1_ref[0].astype(F32)
    mu = _group64_sum(y) * (1.0 / RWKV_HEAD)
    yc = y - mu
    var = _group64_sum(yc * yc) * (1.0 / RWKV_HEAD)
    ya = yc * lax.rsqrt(var + RWKV_GN_EPS) * lnxg_ref[...] + lnxb_ref[...] + bonus_ref[0].astype(F32)
    o = yb0_ref[0].astype(F32) + yb1_ref[0].astype(F32)
    yb = o * lax.rsqrt(_group128_mean(o * o) + LN_EPS) * ng_ref[...]
    ycat = jnp.concatenate([ya, yb], axis=-1) * _silu(gate_ref[0].astype(F32))
    proj = _mm(ycat.astype(BF16), w_ref[...])
    o_ref[0] = _residual_ln(alpha, n_lat, x_ref[0], gm_ref, proj, lng_ref[...], lnb_ref[...])


def _odd_out_kernel(alpha, n_lat, y_ref, x_ref, gm_ref, w_ref, lng_ref, lnb_ref, o_ref):
    proj = _mm(y_ref[0].astype(BF16), w_ref[...])
    o_ref[0] = _residual_ln(alpha, n_lat, x_ref[0], gm_ref, proj, lng_ref[...], lnb_ref[...])


def _tok_spec(n, tm=TOK_BLOCK):
    return pl.BlockSpec((1, tm, n), lambda bb, i: (bb, i, 0))


def _full2(shape):
    return pl.BlockSpec(shape, lambda bb, i: (0,) * len(shape))


def _gate_spec(d):
    return pl.BlockSpec((1, 2, 1, d), lambda bb, i: (bb, 0, 0, 0))


def _even_out(alpha, ya0, ya1, bonus, yb0, yb1, gate, xs, gmod, w_out, lnx_g, lnx_b, norm_g, ln_g, ln_b):
    b, lt, d = xs.shape
    w = ya0.shape[-1]
    di = gate.shape[-1]
    row = lambda u: u.reshape(1, -1)
    tm = _proj_rows(lt)
    return pl.pallas_call(
        functools.partial(_even_out_kernel, alpha, lt - TOK_BLOCK),
        grid=(b, lt // tm),
        in_specs=[_tok_spec(w, tm)] * 5 + [_tok_spec(di, tm), _tok_spec(d, tm), _gate_spec(d), _full2((di, d)),
                                           _full2((1, w)), _full2((1, w)), _full2((1, w)), _full2((1, d)),
                                           _full2((1, d))],
        out_specs=_tok_spec(d, tm),
        out_shape=jax.ShapeDtypeStruct((b, lt, d), F32),
        compiler_params=_cparams("arbitrary", "arbitrary"),
        name="even_out",
    )(ya0, ya1, bonus, yb0, yb1, gate, xs, gmod, w_out, row(lnx_g), row(lnx_b),
      row(jnp.tile(norm_g, w // HGRN_HEAD)), row(ln_g), row(ln_b))


def _odd_out(alpha, y, xs, gmod, w_out, ln_g, ln_b, rows_out):
    b, lt, d = xs.shape
    di = y.shape[-1]
    row = lambda u: u.reshape(1, -1)
    tm = next(r for r in (PROJ_ROWS, 2 * TOK_BLOCK, TOK_BLOCK) if rows_out % r == 0)
    return pl.pallas_call(
        functools.partial(_odd_out_kernel, alpha, lt - TOK_BLOCK),
        grid=(b, rows_out // tm),
        in_specs=[_tok_spec(di, tm), _tok_spec(d, tm), _gate_spec(d), _full2((di, d)), _full2((1, d)),
                  _full2((1, d))],
        out_specs=_tok_spec(d, tm),
        out_shape=jax.ShapeDtypeStruct((b, rows_out, d), F32),
        compiler_params=_cparams("arbitrary", "arbitrary"),
        name="odd_out",
    )(y, xs, gmod, w_out, row(ln_g), row(ln_b))


def _rope(x, cos, sin):
    width = x.shape[-1]
    lane = lax.broadcasted_iota(jnp.int32, (1, width), 1)
    first = (lane % 32) < 16
    partner = jnp.where(first, pltpu.roll(x, width - 16, 1), pltpu.roll(x, 16, 1))
    return x * cos + partner * sin


def _attn_kernel(lam_init, tk, n_full, tail, lam_ref, q_ref, k_ref, vt_ref, g_ref, sg_ref, *rest):
    y_ref, m_scr, acc_scr = rest[-3:]
    tq = q_ref.shape[1]
    ones_rows = 16
    tile = min(ATTN_TILE, 2 * tq)
    n_tiles = 2 * tq // tile
    q = q_ref[0].astype(F32) * (DIFF_HEAD ** -0.5 * float(np.log2(np.e)))
    lane = lax.broadcasted_iota(jnp.int32, (1, LANES), 1)
    q_cat = jnp.concatenate([jnp.where(lane < DIFF_HEAD, q, 0.0), jnp.where(lane >= DIFF_HEAD, q, 0.0)],
                            axis=0).astype(BF16)
    m_scr[...] = jnp.full(m_scr.shape, -jnp.inf, F32)
    acc_scr[...] = jnp.zeros(acc_scr.shape, F32)

    def scores(item):
        start, size, c = item
        return _nt(k_ref[0, pl.ds(start, size), :], q_cat[c * tile:(c + 1) * tile])

    def run(blocks):
        cols = [slice(c * tile, (c + 1) * tile) for c in range(n_tiles)]
        m_run = [m_scr[:, cs] for cs in cols]
        acc = [acc_scr[:, cs] for cs in cols]
        vta = None
        pending = None
        items = [(start, size, c) for start, size in blocks for c in range(n_tiles)]
        st_next = scores(items[0])
        for i, (start, size, c) in enumerate(items):
            st = st_next
            if i + 1 < len(items):
                st_next = scores(items[i + 1])
            if pending is not None:
                pc, pcorr, pvta, ppt = pending
                acc[pc] = acc[pc] * pcorr + _mm(pvta, ppt)
            if c == 0:
                vta = jnp.concatenate([vt_ref[0, :, pl.ds(start, size)], jnp.ones((ones_rows, size), BF16)], axis=0)
            m_new = jnp.maximum(m_run[c], jnp.max(st, axis=0, keepdims=True))
            pending = (c, jnp.exp2(m_run[c] - m_new), vta, jnp.exp2(st - m_new).astype(BF16))
            m_run[c] = m_new
        pc, pcorr, pvta, ppt = pending
        acc[pc] = acc[pc] * pcorr + _mm(pvta, ppt)
        for c, cs in enumerate(cols):
            m_scr[:, cs] = m_run[c]
            acc_scr[:, cs] = acc[c]

    tail_block = [(n_full * tk, tail)]
    if n_full <= ATTN_UNROLL:
        run([(u * tk, tk) for u in range(n_full)] + tail_block)
    else:
        assert n_full % ATTN_UNROLL == 0
        def body(j, carry):
            run([(pl.multiple_of((j * ATTN_UNROLL + u) * tk, tk), tk) for u in range(ATTN_UNROLL)])
            return carry
        lax.fori_loop(0, n_full // ATTN_UNROLL, body, 0)
        run(tail_block)

    a0 = acc_scr[:, :tq]
    a1 = acc_scr[:, tq:]
    o_t = a0[:LANES] / a0[LANES:LANES + 1] - lam_ref[0] * (a1[:LANES] / a1[LANES:LANES + 1])
    o = jnp.transpose(o_t)
    y = o * lax.rsqrt(jnp.mean(o * o, axis=-1, keepdims=True) + LN_EPS) * sg_ref[...] * (1.0 - lam_init)
    y_ref[0] = (y * _silu(g_ref[0].astype(F32))).astype(y_ref.dtype)


def _diff_attention(p_qk, p_vt, p_g, lam, subln_g, lam_init):
    b, lt, di = p_g.shape
    nh = di // LANES
    t = lt - TOK_BLOCK
    tq = tk = 512
    assert t % tq == 0
    ctx_blk = t // TOK_BLOCK
    sm = pl.BlockSpec(memory_space=pltpu.SMEM)
    sg_spec = pl.BlockSpec((1, LANES), lambda bb, h, i: (0, 0))
    scratch = lambda n: [pltpu.VMEM((1, 2 * n), F32), pltpu.VMEM((LANES + 16, 2 * n), F32)]
    args = (lam.reshape(1), p_qk, p_qk, p_vt, p_g, subln_g.reshape(1, LANES))
    y = pl.pallas_call(
        functools.partial(_attn_kernel, lam_init, tk, t // tk, TOK_BLOCK),
        grid=(b, nh, t // tq),
        in_specs=[sm,
                  pl.BlockSpec((1, tq, LANES), lambda bb, h, i: (bb, i, h)),
                  pl.BlockSpec((1, lt, LANES), lambda bb, h, i: (bb, 0, nh + h)),
                  pl.BlockSpec((1, LANES, lt), lambda bb, h, i: (bb, h, 0)),
                  pl.BlockSpec((1, tq, LANES), lambda bb, h, i: (bb, i, h)),
                  sg_spec],
        out_specs=pl.BlockSpec((1, tq, LANES), lambda bb, h, i: (bb, i, h)),
        out_shape=jax.ShapeDtypeStruct((b, lt, di), BF16),
        scratch_shapes=scratch(tq),
        compiler_params=_cparams("arbitrary", "arbitrary", "arbitrary"),
        name="diff_attn",
    )(*args)
    blk = lambda bb, h, i: (bb, ctx_blk, h)
    return pl.pallas_call(
        functools.partial(_attn_kernel, lam_init, tk, 0, TOK_BLOCK),
        grid=(b, nh, 1),
        in_specs=[sm,
                  pl.BlockSpec((1, TOK_BLOCK, LANES), blk),
                  pl.BlockSpec((1, TOK_BLOCK, LANES), lambda bb, h, i: (bb, ctx_blk, nh + h)),
                  pl.BlockSpec((1, LANES, TOK_BLOCK), lambda bb, h, i: (bb, h, ctx_blk)),
                  pl.BlockSpec((1, TOK_BLOCK, LANES), blk),
                  sg_spec,
                  pl.BlockSpec(memory_space=pl.ANY)],
        out_specs=pl.BlockSpec((1, TOK_BLOCK, LANES), blk),
        out_shape=jax.ShapeDtypeStruct((b, lt, di), BF16),
        scratch_shapes=scratch(TOK_BLOCK),
        input_output_aliases={6: 0},
        compiler_params=_cparams("arbitrary", "arbitrary", "arbitrary"),
        name="diff_attn_ctx",
    )(*args, y)


def _rope_tables(n_ctx, t):
    quarter = DIFF_HEAD // 4
    inv = ROPE_BASE ** (-jnp.arange(quarter, dtype=F32) / quarter)
    pos = jnp.arange(t)
    rows = (pos // GRID_W).astype(F32)[:, None] * inv
    cols = (pos % GRID_W).astype(F32)[:, None] * inv
    cos64 = jnp.concatenate([jnp.cos(rows), jnp.cos(rows), jnp.cos(cols), jnp.cos(cols)], -1)
    sin64 = jnp.concatenate([-jnp.sin(rows), jnp.sin(rows), -jnp.sin(cols), jnp.sin(cols)], -1)
    cos = jnp.concatenate([jnp.tile(cos64, (1, 2)), jnp.ones((n_ctx, LANES), F32)], 0)
    sin = jnp.concatenate([jnp.tile(sin64, (1, 2)), jnp.zeros((n_ctx, LANES), F32)], 0)
    return cos, sin


def _even_layer(alpha, xs, scale, shift, gmod, w_in, w_out, mu_prev, mu_next, w0, w2, a0, a2, k_k, k_a, r_k,
                lnx_g, lnx_b, lb, norm_g, ln_g, ln_b):
    w = k_k.shape[-1]
    sw = mu_prev.shape[-1]
    hw = 4 * lb.shape[-1]
    w_in = w_in.astype(BF16)
    p_rwkv = _project(xs, scale, shift, w_in[:, :sw], sw, F32)
    p_hgrn = _project(xs, scale, shift, w_in[:, sw:sw + hw], hw // 2, F32)
    gate = _project(xs, scale, shift, w_in[:, sw + hw:], w_in.shape[1] - sw - hw, BF16)
    v, bonus, ops_f, ops_r = _rwkv_prep(p_rwkv, mu_prev, mu_next, w0, w2, a0, a2, k_k, k_a, r_k)
    ya0, yb0 = _mix_scan(ops_f, v, p_hgrn, lb[0], 0, False)
    ya1, yb1 = _mix_scan(ops_r, v, p_hgrn, lb[1], 1, True)
    return _even_out(alpha, ya0, ya1, bonus, yb0, yb1, gate, xs, gmod, w_out.astype(BF16),
                     lnx_g, lnx_b, norm_g, ln_g, ln_b)


def _odd_layer(alpha, xs, scale, shift, gmod, w_in, w_out, lam_p, subln_g, lam_init, cos, sin, ln_g, ln_b,
               rows_out):
    di = w_out.shape[0]
    w_in = w_in.astype(BF16)
    p_qk = _project(xs, scale, shift, w_in[:, :2 * di], di, BF16, rope=(cos, sin))
    p_vt = _project_t(xs, scale, shift, jnp.transpose(w_in[:, 2 * di:3 * di]), BF16)
    p_g = _project(xs, scale, shift, w_in[:, 3 * di:], di, BF16)
    lam = jnp.exp(jnp.sum(lam_p[0] * lam_p[1])) - jnp.exp(jnp.sum(lam_p[2] * lam_p[3])) + lam_init
    y = _diff_attention(p_qk, p_vt, p_g, lam, subln_g, lam_init)
    return _odd_out(alpha, y, xs, gmod, w_out.astype(BF16), ln_g, ln_b, rows_out)


def kernel(x, c, ctx, c_ctx, ada_w, ada_b, ln_g, ln_b, even_w_in, even_w_out, rwkv_mu_prev, rwkv_mu_next, rwkv_w0, rwkv_w2, rwkv_a0, rwkv_a2, rwkv_k_k, rwkv_k_a, rwkv_r_k, rwkv_lnx_g, rwkv_lnx_b, hgrn_lb_logits, hgrn_norm_g, odd_w_in, odd_w_out, diff_lambda, diff_subln_g):
    b, t, d = x.shape
    n_ctx = ctx.shape[1]
    depth = ada_w.shape[0]
    assert n_ctx == TOK_BLOCK and t % TOK_BLOCK == 0 and b + 1 <= 8
    alpha = (2.0 * depth) ** 0.25

    xs = jnp.concatenate([x, ctx], axis=1)
    cvec = jnp.concatenate([c, c_ctx[None], jnp.zeros((8 - b - 1, d), F32)], axis=0)
    mods = _ada_mods(cvec, ada_w, ada_b)
    lb_all = jax.nn.softmax(hgrn_lb_logits.astype(F32), axis=0)
    lb_all = jnp.cumsum(lb_all, axis=0) - lb_all[0]
    cos, sin = _rope_tables(n_ctx, t)

    for layer in range(depth):
        m = mods[layer]
        per_seg = lambda u: jnp.stack([jnp.broadcast_to(u[b], (b, d)), u[:b]], axis=1)[:, :, None, :]
        shift, scale, gmod = (per_seg(m[:, j * d:(j + 1) * d]) for j in range(3))
        j = layer // 2
        if layer % 2 == 0:
            xs = _even_layer(alpha, xs, scale, shift, gmod, even_w_in[j], even_w_out[j], rwkv_mu_prev[j],
                             rwkv_mu_next[j], rwkv_w0[j], rwkv_w2[j], rwkv_a0[j], rwkv_a2[j], rwkv_k_k[j],
                             rwkv_k_a[j], rwkv_r_k[j], rwkv_lnx_g[j], rwkv_lnx_b[j], lb_all[j], hgrn_norm_g[j],
                             ln_g[layer], ln_b[layer])
        else:
            lam_init = 0.8 - 0.6 * float(np.exp(-0.3 * layer))
            xs = _odd_layer(alpha, xs, scale, shift, gmod, odd_w_in[j], odd_w_out[j], diff_lambda[j],
                            diff_subln_g[j], lam_init, cos, sin, ln_g[layer], ln_b[layer],
                            t if layer == depth - 1 else t + n_ctx)
    return xs[:, :t]
```

```python
import functools

import numpy as np
import jax
import jax.numpy as jnp
from jax import lax
from jax.experimental import pallas as pl
from jax.experimental.pallas import tpu as pltpu

F32 = jnp.float32
BF16 = jnp.bfloat16
HI = lax.Precision.HIGHEST

GRID_W = 64
RWKV_HEAD = 64
RWKV_LORA = 64
RWKV_GN_EPS = 64e-5
HGRN_HEAD = 128
DIFF_HEAD = 64
ROPE_BASE = 10000.0
LN_EPS = 1e-5
LOG2E = float(np.log2(np.e))

LANES = 128
MXU_DIM = 256
VMEM_LIMIT = 56 * 1024 * 1024

CHUNK = 64
SUB = 16
TOK_BLOCK = 256
PROJ_ROWS = 768
GROUP_HEADS = MXU_DIM // RWKV_HEAD
HGRN_PAR = 4
ATTN_UNROLL = 16
ATTN_TILE = 2048


def _cparams(*sem):
    return pltpu.CompilerParams(dimension_semantics=sem, vmem_limit_bytes=VMEM_LIMIT)


def _nt(a, b):
    return lax.dot_general(a, b, (((1,), (1,)), ((), ())), preferred_element_type=F32)


def _tn(a, b):
    return lax.dot_general(a, b, (((0,), (0,)), ((), ())), preferred_element_type=F32)


def _mm(a, b):
    return jnp.dot(a, b, preferred_element_type=F32)


def _mm_exact(a, b):
    return jnp.dot(a, b, preferred_element_type=F32, precision=HI)


def _sigmoid(x):
    return 1.0 / (1.0 + jnp.exp(-x))


def _silu(x):
    return x * _sigmoid(x)


def _lane_tiles(x):
    return [x[:, j * LANES:(j + 1) * LANES] for j in range(x.shape[-1] // LANES)]


def _group64_sum(x):
    lane = lax.broadcasted_iota(jnp.int32, (1, LANES), 1)
    low = lane < RWKV_HEAD
    out = []
    for xt in _lane_tiles(x):
        s_all = jnp.sum(xt, axis=-1, keepdims=True)
        s_lo = jnp.sum(jnp.where(low, xt, 0.0), axis=-1, keepdims=True)
        out.append(jnp.where(low, s_lo, s_all - s_lo))
    return jnp.concatenate(out, axis=-1)


def _group128_mean(x):
    out = []
    for xt in _lane_tiles(x):
        out.append(jnp.broadcast_to(jnp.mean(xt, axis=-1, keepdims=True), xt.shape))
    return jnp.concatenate(out, axis=-1)


def _ada_kernel(c_ref, w_ref, b_ref, o_ref):
    cond = _silu(c_ref[...])
    o_ref[0] = _mm_exact(cond, w_ref[0]) + b_ref[0]


def _ada_mods(cvec, ada_w, ada_b):
    depth, d, d3 = ada_w.shape
    tn = 1024
    return pl.pallas_call(
        _ada_kernel,
        grid=(depth, d3 // tn),
        in_specs=[
            pl.BlockSpec((8, d), lambda l, j: (0, 0)),
            pl.BlockSpec((1, d, tn), lambda l, j: (l, 0, j)),
            pl.BlockSpec((1, 1, tn), lambda l, j: (l, 0, j)),
        ],
        out_specs=pl.BlockSpec((1, 8, tn), lambda l, j: (l, 0, j)),
        out_shape=jax.ShapeDtypeStruct((depth, 8, d3), F32),
        compiler_params=_cparams("arbitrary", "arbitrary"),
        name="ada_mods",
    )(cvec, ada_w, ada_b.reshape(depth, 1, d3))


def _proj_rows(lt):
    return PROJ_ROWS if lt % PROJ_ROWS == 0 else TOK_BLOCK


def _modulate(n_lat, axis, x_ref, sc_ref, sh_ref):
    tm = x_ref.shape[1]
    rows = pl.program_id(axis) * tm + lax.broadcasted_iota(jnp.int32, (tm, 1), 0)
    is_ctx = rows >= n_lat
    sc = jnp.where(is_ctx, sc_ref[0, 0], sc_ref[0, 1])
    sh = jnp.where(is_ctx, sh_ref[0, 0], sh_ref[0, 1])
    return (x_ref[0] * (1.0 + sc) + sh).astype(BF16)


def _proj_kernel(n_lat, x_ref, sc_ref, sh_ref, w_ref, o_ref):
    o_ref[0] = _mm(_modulate(n_lat, 2, x_ref, sc_ref, sh_ref), w_ref[...]).astype(o_ref.dtype)


def _proj_rope_kernel(n_lat, x_ref, sc_ref, sh_ref, w_ref, cos_ref, sin_ref, o_ref):
    p = _mm(_modulate(n_lat, 2, x_ref, sc_ref, sh_ref), w_ref[...])
    reps = p.shape[-1] // LANES
    cos = jnp.concatenate([cos_ref[...]] * reps, axis=-1)
    sin = jnp.concatenate([sin_ref[...]] * reps, axis=-1)
    o_ref[0] = _rope(p, cos, sin).astype(o_ref.dtype)


def _proj_t_kernel(n_lat, x_ref, sc_ref, sh_ref, wt_ref, o_ref):
    o_ref[0] = _nt(wt_ref[...], _modulate(n_lat, 1, x_ref, sc_ref, sh_ref)).astype(o_ref.dtype)


def _project(xs, scale, shift, w, tn, out_dtype, rope=None):
    b, lt, d = xs.shape
    n = w.shape[1]
    tm = _proj_rows(lt)
    n_lat = lt - TOK_BLOCK
    in_specs = [
        pl.BlockSpec((1, tm, d), lambda j, bb, i: (bb, i, 0)),
        pl.BlockSpec((1, 2, 1, d), lambda j, bb, i: (bb, 0, 0, 0)),
        pl.BlockSpec((1, 2, 1, d), lambda j, bb, i: (bb, 0, 0, 0)),
        pl.BlockSpec((d, tn), lambda j, bb, i: (0, j)),
    ]
    args = (xs, scale, shift, w)
    if rope is not None:
        in_specs += [pl.BlockSpec((tm, LANES), lambda j, bb, i: (i, 0))] * 2
        args += tuple(rope)
    return pl.pallas_call(
        functools.partial(_proj_kernel if rope is None else _proj_rope_kernel, n_lat),
        grid=(n // tn, b, lt // tm),
        in_specs=in_specs,
        out_specs=pl.BlockSpec((1, tm, tn), lambda j, bb, i: (bb, i, j)),
        out_shape=jax.ShapeDtypeStruct((b, lt, n), out_dtype),
        compiler_params=_cparams("arbitrary", "arbitrary", "arbitrary"),
        name="mod_proj" if rope is None else "mod_proj_rope",
    )(*args)


def _project_t(xs, scale, shift, wt, out_dtype):
    b, lt, d = xs.shape
    n = wt.shape[0]
    tm = _proj_rows(lt)
    return pl.pallas_call(
        functools.partial(_proj_t_kernel, lt - TOK_BLOCK),
        grid=(b, lt // tm),
        in_specs=[
            pl.BlockSpec((1, tm, d), lambda bb, i: (bb, i, 0)),
            pl.BlockSpec((1, 2, 1, d), lambda bb, i: (bb, 0, 0, 0)),
            pl.BlockSpec((1, 2, 1, d), lambda bb, i: (bb, 0, 0, 0)),
            pl.BlockSpec((n, d), lambda bb, i: (0, 0)),
        ],
        out_specs=pl.BlockSpec((1, n, tm), lambda bb, i: (bb, 0, i)),
        out_shape=jax.ShapeDtypeStruct((b, n, lt), out_dtype),
        compiler_params=_cparams("arbitrary", "arbitrary"),
        name="mod_proj_t",
    )(xs, scale, shift, wt)


def _lora(x, w_ref):
    x1 = x.astype(BF16)
    x2 = (x - x1.astype(F32)).astype(BF16)
    return _mm(jnp.concatenate([x1, x1, x2], axis=-1), w_ref[...])


def _rwkv_prep_kernel(nblk, p_ref, prev_ref, next_ref, mup_ref, mun_ref, w0_ref, w2_ref, a0_ref, a2_ref,
                      kk_ref, ka_ref, rk_ref, cm_ref,
                      v_out, bonus_out, *dir_outs):
    i = pl.program_id(1)
    w = kk_ref.shape[-1]
    p = p_ref[0]
    row = lax.broadcasted_iota(jnp.int32, (TOK_BLOCK, 1), 0)
    prev_row = jnp.where(jnp.logical_and(i >= 1, i <= nblk - 2), prev_ref[0][7:8, :], 0.0)
    next_row = jnp.where(i <= nblk - 3, next_ref[0][0:1, :], 0.0)
    prev = jnp.where(row == 0, prev_row, pltpu.roll(p, 1, 0))
    nxt = jnp.where(row == TOK_BLOCK - 1, next_row, pltpu.roll(p, TOK_BLOCK - 1, 0))
    ps = p * (1.0 - mup_ref[...] - mun_ref[...]) + mup_ref[...] * prev + mun_ref[...] * nxt

    r = ps[:, 0:w]
    k = ps[:, w:2 * w]
    v = ps[:, 2 * w:3 * w]
    wlo = ps[:, 3 * w:3 * w + 2 * RWKV_LORA]
    alo = ps[:, 3 * w + 2 * RWKV_LORA:3 * w + 4 * RWKV_LORA]

    z = _lora(jnp.tanh(wlo), w2_ref) + w0_ref[...]
    lw = _sigmoid(z) * (-LOG2E * float(np.exp(-0.5)))
    a = _sigmoid(_lora(alo, a2_ref) + a0_ref[...])

    kk = k * kk_ref[...]
    kk = kk * lax.rsqrt(_group64_sum(kk * kk) + 1e-12)
    bonus_out[0] = (_group64_sum(r * k * rk_ref[...]) * v).astype(bonus_out.dtype)
    v_out[0] = v.astype(BF16)

    for d in range(2):
        rh_o, kh_o, bh_o, kap_o, kg_o, bg_o, gl_o = dir_outs[7 * d:7 * d + 7]
        lw_d = lw[:, d * w:(d + 1) * w]
        a_d = a[:, d * w:(d + 1) * w]
        kd = k * (1.0 + (a_d - 1.0) * ka_ref[...])
        b_d = kk * a_d
        sums = _mm(cm_ref[d], _split3(lw_d))
        g = sums[:TOK_BLOCK]
        g_all = sums[TOK_BLOCK:]
        e_neg = jnp.exp2(-g)
        e_rest = jnp.exp2(g_all - g)
        rh_o[0] = (r * jnp.exp2(g)).astype(BF16)
        kh_o[0] = (kd * e_neg).astype(BF16)
        bh_o[0] = (b_d * e_neg).astype(BF16)
        kap_o[0] = (kk * jnp.exp2(g - lw_d)).astype(BF16)
        kg_o[0] = (kd * e_rest).astype(BF16)
        bg_o[0] = (b_d * e_rest).astype(BF16)
        e_all = jnp.exp2(g_all)
        for c in range(TOK_BLOCK // CHUNK):
            gl_o[0, c] = e_all[c * CHUNK:c * CHUNK + 1, :]


def _rwkv_sum_consts(reverse):
    t = np.arange(TOK_BLOCK)[:, None]
    s = np.arange(TOK_BLOCK)[None, :]
    same = (t // CHUNK) == (s // CHUNK)
    order = (s >= t) if reverse else (s <= t)
    both = np.concatenate([same & order, same], axis=0).astype(np.float32)
    return np.concatenate([both] * 3, axis=1)


def _lora_pieces(m):
    hi = m.astype(BF16)
    lo = (m - hi.astype(F32)).astype(BF16)
    return jnp.concatenate([hi, lo, hi], axis=0)


def _rwkv_prep(p_rwkv, mu_prev, mu_next, w0, w2, a0, a2, k_k, k_a, r_k):
    b, lt, sw = p_rwkv.shape
    w = k_k.shape[-1]
    nblk = lt // TOK_BLOCK
    nch = lt // CHUNK
    halo = TOK_BLOCK // 8
    zl = jnp.zeros((RWKV_LORA, w), F32)
    w2cat = _lora_pieces(jnp.concatenate([jnp.concatenate([w2[0], zl], 1), jnp.concatenate([zl, w2[1]], 1)], 0))
    a2cat = _lora_pieces(jnp.concatenate([jnp.concatenate([a2[0], zl], 1), jnp.concatenate([zl, a2[1]], 1)], 0))
    cm = jnp.asarray(np.stack([_rwkv_sum_consts(False), _rwkv_sum_consts(True)]), dtype=BF16)
    row = lambda u: u.reshape(1, -1)
    full = lambda shape: pl.BlockSpec(shape, lambda bb, i: (0,) * len(shape))
    tok = lambda n, dt: (pl.BlockSpec((1, TOK_BLOCK, n), lambda bb, i: (bb, i, 0)), jax.ShapeDtypeStruct((b, lt, n), dt))
    gl = (pl.BlockSpec((1, TOK_BLOCK // CHUNK, 1, w), lambda bb, i: (bb, i, 0, 0)),
          jax.ShapeDtypeStruct((b, nch, 1, w), F32))
    outs = [tok(w, BF16), tok(w, BF16)] + 2 * ([tok(w, BF16)] * 6 + [gl])
    res = pl.pallas_call(
        functools.partial(_rwkv_prep_kernel, nblk),
        grid=(b, nblk),
        in_specs=[
            pl.BlockSpec((1, TOK_BLOCK, sw), lambda bb, i: (bb, i, 0)),
            pl.BlockSpec((1, 8, sw), lambda bb, i: (bb, jnp.maximum(i * halo - 1, 0), 0)),
            pl.BlockSpec((1, 8, sw), lambda bb, i: (bb, jnp.minimum((i + 1) * halo, lt // 8 - 1), 0)),
            full((1, sw)), full((1, sw)), full((1, 2 * w)), full(w2cat.shape),
            full((1, 2 * w)), full(a2cat.shape), full((1, w)), full((1, w)), full((1, w)),
            full(cm.shape),
        ],
        out_specs=[o[0] for o in outs],
        out_shape=[o[1] for o in outs],
        compiler_params=_cparams("arbitrary", "arbitrary"),
        name="rwkv_prep",
    )(p_rwkv, p_rwkv, p_rwkv, row(mu_prev), row(mu_next), row(w0), w2cat, row(a0), a2cat,
      row(k_k), row(k_a), row(r_k), cm)
    v, bonus = res[0], res[1]
    return v, bonus, res[2:9], res[9:16]


def _rwkv_masks(reverse):
    n = GROUP_HEADS * CHUNK
    i = np.arange(n)[:, None]
    j = np.arange(n)[None, :]
    same_head = (i // CHUNK) == (j // CHUNK)
    before = (j > i) if reverse else (j < i)
    strict = same_head & before
    incl = same_head & (before | (i == j))
    levels = []
    size = 1
    while size < CHUNK:
        levels.append(same_head & ((i // (2 * size)) == (j // (2 * size))) & ((i // size) != (j // size)))
        size *= 2
    head_lane = (i // CHUNK) == (j // RWKV_HEAD)
    tri = np.stack([strict, incl]).astype(np.float32)
    return tri, np.stack(levels).astype(np.float32), head_lane.astype(np.float32)


def _scan_chunk_index(c, nch, reverse):
    if reverse:
        return nch - 1 - c
    n_ctx = TOK_BLOCK // CHUNK
    return jnp.where(c < n_ctx, nch - n_ctx + c, c - n_ctx)


def _split3(x):
    x1 = x.astype(BF16)
    r1 = x - x1.astype(F32)
    x2 = r1.astype(BF16)
    x3 = (r1 - x2.astype(F32)).astype(BF16)
    return jnp.concatenate([x1, x2, x3], axis=0)


def _hgrn_chunk_consts(reverse):
    t = np.arange(CHUNK)[:, None]
    s = np.arange(CHUNK)[None, :]
    order = (s >= t) if reverse else (s <= t)
    earlier_sub = ((s // SUB) > (t // SUB)) if reverse else ((s // SUB) < (t // SUB))
    both = np.concatenate([order, earlier_sub], axis=0).astype(np.float32)
    return np.concatenate([both] * 3, axis=1)


def _rwkv_chunk_stages(rh_ref, kh_ref, bh_ref, kap_ref, kg_ref, bg_ref, v_ref, gl_ref, tri_ref, lvl_ref, hm_ref,
                       y_ref, s_ref):
    hm = hm_ref[...]
    strict = tri_ref[0]
    incl = tri_ref[1]
    eye = incl - strict
    n_levels = lvl_ref.shape[0]
    groups = range(y_ref.shape[-1] // MXU_DIM)
    lanes = lambda ref, g: ref[0, :, g * MXU_DIM:(g + 1) * MXU_DIM]
    stack = lambda ref, g: jnp.concatenate([lanes(ref, g)] * GROUP_HEADS, axis=0) * hm

    kap, bh, a_ab = [], [], []
    for g in groups:
        kap.append(stack(kap_ref, g))
        bh.append(stack(bh_ref, g))
        a_ab.append(_nt(kap[g], bh[g]) * strict)
    yield
    kh = [stack(kh_ref, g) for g in groups]
    a_ak = [(_nt(kap[g], kh[g]) * strict).astype(BF16) for g in groups]
    rh = [stack(rh_ref, g) for g in groups]
    p_rb = [(_nt(rh[g], bh[g]) * incl).astype(BF16) for g in groups]
    p_rk = [(_nt(rh[g], kh[g]) * incl).astype(BF16) for g in groups]
    vs = [stack(v_ref, g) for g in groups]
    t = [eye - a_ab[g] * lvl_ref[0] for g in groups]
    a_abb = [a_ab[g].astype(BF16) for g in groups]
    for lv in range(1, n_levels):
        yield
        tb = [t[g].astype(BF16) for g in groups]
        x = [_mm(tb[g], a_abb[g]).astype(BF16) for g in groups]
        yield
        t = [t[g] - _mm(x[g], tb[g]) * lvl_ref[lv] for g in groups]
    yield
    s0 = [s_ref[g] for g in groups]
    s0b = [s0[g].astype(BF16) for g in groups]
    hm32 = hm.astype(F32)
    ks0 = [jnp.concatenate([_nt(lanes(kap_ref, g), s0b[g])] * GROUP_HEADS, axis=0) * hm32 for g in groups]
    wmat = [(ks0[g] + _mm(a_ak[g], vs[g])).astype(BF16) for g in groups]
    yield
    ub = [(-_mm(t[g].astype(BF16), wmat[g])).astype(BF16) for g in groups]
    yield
    ys = [_mm(p_rb[g], ub[g]) + _mm(p_rk[g], vs[g]) for g in groups]
    for g in groups:
        y = _nt(lanes(rh_ref, g), s0b[g])
        for h in range(GROUP_HEADS):
            y = y + ys[g][h * CHUNK:(h + 1) * CHUNK]
        y_ref[0, :, g * MXU_DIM:(g + 1) * MXU_DIM] = y.astype(y_ref.dtype)
    yield
    for g in groups:
        gam = gl_ref[0, 0][:, g * MXU_DIM:(g + 1) * MXU_DIM]
        s_ref[g] = s0[g] * gam + _tn(ub[g], stack(bg_ref, g)) + _tn(vs[g], stack(kg_ref, g))


def _hgrn_chunk_stages(reverse, q_ref, f_ref, i_ref, lb_ref, cm_ref, o_ref, s_ref, g_scr, q_scr):
    n_sub = CHUNK // SUB
    n_heads = q_ref.shape[-1] // HGRN_HEAD
    sub_order = list(range(n_sub))[::-1] if reverse else list(range(n_sub))
    row_sub = lax.broadcasted_iota(jnp.int32, (CHUNK, 1), 0) // SUB
    srow = lax.broadcasted_iota(jnp.int32, (SUB, 1), 0)
    lane = lax.broadcasted_iota(jnp.int32, (1, LANES), 1)
    last_row = 0 if reverse else CHUNK - 1

    kgate = (1.0 - lb_ref[...]) * _sigmoid(-f_ref[0])
    sums = _mm(cm_ref[...], _split3(jnp.log1p(-kgate) * LOG2E))
    g_all = sums[:CHUNK]
    gb_all = sums[CHUNK:]
    q_all = _silu(q_ref[0])
    g_scr[...] = g_all
    q_scr[...] = q_all
    yield
    for hg in range(n_heads // HGRN_PAR):
        par = range(HGRN_PAR)
        hs = [hg * HGRN_PAR + j for j in par]
        cols = [slice(h * HGRN_HEAD, (h + 1) * HGRN_HEAD) for h in hs]
        g = [g_all[:, cs] for cs in cols]
        gb = [gb_all[:, cs] for cs in cols]
        q = [q_all[:, cs] for cs in cols]
        k = [kgate[:, cs] for cs in cols]
        vb = [i_ref[0, :, cs].astype(BF16) for cs in cols]
        s0 = [s_ref[h] for h in hs]
        q_in = [q[j] * jnp.exp2(g[j] - gb[j]) for j in par]
        at = []
        for j in par:
            kts, qts = [], []
            for si in sub_order[1:]:
                gb_i = gb[j][si * SUB:si * SUB + 1, :]
                earlier = (row_sub > si) if reverse else (row_sub < si)
                kts.append(jnp.where(earlier, k[j] * jnp.exp2(jnp.minimum(gb_i - g[j], 0.0)), 0.0).astype(BF16))
                qts.append(jnp.where(row_sub == si, q_in[j], 0.0).astype(BF16))
            at.append(_nt(jnp.concatenate(kts, axis=-1), jnp.concatenate(qts, axis=-1)))
        yield
        diag = [[] for _ in par]
        for si in range(n_sub):
            blk = slice(si * SUB, (si + 1) * SUB)
            acc = [jnp.zeros((SUB, LANES), F32) for _ in par]
            for tt in range(SUB):
                t = si * SUB + tt
                valid = jnp.logical_and(lane == t, (srow >= tt) if reverse else (srow <= tt))
                for j in par:
                    e = jnp.exp2(jnp.minimum(g_scr[t:t + 1, cols[j]] - g[j][blk], 0.0))
                    col = jnp.sum(k[j][blk] * e * q_scr[t:t + 1, cols[j]], axis=-1, keepdims=True)
                    acc[j] = jnp.where(valid, col, acc[j])
                if tt % 8 == 7:
                    yield
            for j in par:
                diag[j].append(acc[j])
        for j in par:
            a = (at[j] + jnp.concatenate(diag[j], axis=0)[:, :CHUNK]).astype(BF16)
            o = _tn(a, vb[j]) + _nt((q[j] * jnp.exp2(g[j])).astype(BF16), s0[j].astype(BF16))
            o_ref[0, :, cols[j]] = o.astype(o_ref.dtype)
            g_last = g[j][last_row:last_row + 1, :]
            k_out = (k[j] * jnp.exp2(g_last - g[j])).astype(BF16)
            s_ref[hs[j]] = s0[j] * jnp.exp2(g_last) + _tn(vb[j], k_out)
        yield


def _mix_scan_kernel(reverse, *refs):
    rwkv_in, hgrn_in = refs[:11], refs[11:16]
    y_ref, o_ref, s_ref, hs_ref, g_scr, q_scr = refs[16:]

    @pl.when(pl.program_id(1) == 0)
    def _():
        s_ref[...] = jnp.zeros_like(s_ref)
        hs_ref[...] = jnp.zeros_like(hs_ref)

    streams = [_rwkv_chunk_stages(*rwkv_in, y_ref, s_ref),
               _hgrn_chunk_stages(reverse, *hgrn_in, o_ref, hs_ref, g_scr, q_scr)]
    done = object()
    while streams:
        streams = [st for st in streams if next(st, done) is not done]


def _mix_scan(dir_ops, v, p_hgrn, lb_d, d, reverse):
    rh, kh, bh, kap, kg, bg, gl = dir_ops
    b, lt, w = v.shape
    nch = lt // CHUNK
    n = GROUP_HEADS * CHUNK
    tri, lvl, hm = _rwkv_masks(reverse)
    cm = jnp.asarray(_hgrn_chunk_consts(reverse), dtype=BF16)
    cidx = lambda c: _scan_chunk_index(c, nch, reverse)
    tok = pl.BlockSpec((1, CHUNK, w), lambda bb, c: (bb, cidx(c), 0))
    col = lambda j: pl.BlockSpec((1, CHUNK, w), lambda bb, c: (bb, cidx(c), j))
    full = lambda shape: pl.BlockSpec(shape, lambda bb, c: (0,) * len(shape))
    return pl.pallas_call(
        functools.partial(_mix_scan_kernel, reverse),
        grid=(b, nch),
        in_specs=[tok] * 7 + [
            pl.BlockSpec((1, 1, 1, w), lambda bb, c: (bb, cidx(c), 0, 0)),
            full(tri.shape), full(lvl.shape), full(hm.shape),
            col(0), col(1 + d), col(3), full((1, w)), full(cm.shape),
        ],
        out_specs=[tok, tok],
        out_shape=[jax.ShapeDtypeStruct((b, lt, w), BF16)] * 2,
        scratch_shapes=[pltpu.VMEM((w // MXU_DIM, n, n), F32),
                        pltpu.VMEM((w // HGRN_HEAD, HGRN_HEAD, HGRN_HEAD), F32),
                        pltpu.VMEM((CHUNK, w), F32), pltpu.VMEM((CHUNK, w), F32)],
        compiler_params=_cparams("arbitrary", "arbitrary"),
        name="mix_scan_rev" if reverse else "mix_scan_fwd",
    )(rh, kh, bh, kap, kg, bg, v, gl, jnp.asarray(tri), jnp.asarray(lvl), jnp.asarray(hm, dtype=BF16),
      p_hgrn, p_hgrn, p_hgrn, lb_d.reshape(1, w), cm)


def _residual_ln(alpha, n_lat, x, gm_ref, proj, lng, lnb):
    tm = x.shape[0]
    rows = pl.program_id(1) * tm + lax.broadcasted_iota(jnp.int32, (tm, 1), 0)
    gm = jnp.where(rows >= n_lat, gm_ref[0, 0], gm_ref[0, 1])
    z = alpha * x + gm * proj
    mu = jnp.mean(z, axis=-1, keepdims=True)
    zc = z - mu
    var = jnp.mean(zc * zc, axis=-1, keepdims=True)
    return zc * lax.rsqrt(var + LN_EPS) * lng + lnb


def _even_out_kernel(alpha, n_lat, ya0_ref, ya1_ref, bonus_ref, yb0_ref, yb1_ref, gate_ref, x_ref, gm_ref, w_ref,
                     lnxg_ref, lnxb_ref, ng_ref, lng_ref, lnb_ref, o_ref):
    y = ya0_ref[0].astype(F32) + ya1_ref[0].astype(F32)
    mu = _group64_sum(y) * (1.0 / RWKV_HEAD)
    yc = y - mu
    var = _group64_sum(yc * yc) * (1.0 / RWKV_HEAD)
    ya = yc * lax.rsqrt(var + RWKV_GN_EPS) * lnxg_ref[...] + lnxb_ref[...] + bonus_ref[0].astype(F32)
    o = yb0_ref[0].astype(F32) + yb1_ref[0].astype(F32)
    yb = o * lax.rsqrt(_group128_mean(o * o) + LN_EPS) * ng_ref[...]
    ycat = jnp.concatenate([ya, yb], axis=-1) * _silu(gate_ref[0].astype(F32))
    proj = _mm(ycat.astype(BF16), w_ref[...])
    o_ref[0] = _residual_ln(alpha, n_lat, x_ref[0], gm_ref, proj, lng_ref[...], lnb_ref[...])


def _odd_out_kernel(alpha, n_lat, y_ref, x_ref, gm_ref, w_ref, lng_ref, lnb_ref, o_ref):
    proj = _mm(y_ref[0].astype(BF16), w_ref[...])
    o_ref[0] = _residual_ln(alpha, n_lat, x_ref[0], gm_ref, proj, lng_ref[...], lnb_ref[...])


def _tok_spec(n, tm=TOK_BLOCK):
    return pl.BlockSpec((1, tm, n), lambda bb, i: (bb, i, 0))


def _full2(shape):
    return pl.BlockSpec(shape, lambda bb, i: (0,) * len(shape))


def _gate_spec(d):
    return pl.BlockSpec((1, 2, 1, d), lambda bb, i: (bb, 0, 0, 0))


def _even_out(alpha, ya0, ya1, bonus, yb0, yb1, gate, xs, gmod, w_out, lnx_g, lnx_b, norm_g, ln_g, ln_b):
    b, lt, d = xs.shape
    w = ya0.shape[-1]
    di = gate.shape[-1]
    row = lambda u: u.reshape(1, -1)
    tm = _proj_rows(lt)
    return pl.pallas_call(
        functools.partial(_even_out_kernel, alpha, lt - TOK_BLOCK),
        grid=(b, lt // tm),
        in_specs=[_tok_spec(w, tm)] * 5 + [_tok_spec(di, tm), _tok_spec(d, tm), _gate_spec(d), _full2((di, d)),
                                           _full2((1, w)), _full2((1, w)), _full2((1, w)), _full2((1, d)),
                                           _full2((1, d))],
        out_specs=_tok_spec(d, tm),
        out_shape=jax.ShapeDtypeStruct((b, lt, d), F32),
        compiler_params=_cparams("arbitrary", "arbitrary"),
        name="even_out",
    )(ya0, ya1, bonus, yb0, yb1, gate, xs, gmod, w_out, row(lnx_g), row(lnx_b),
      row(jnp.tile(norm_g, w // HGRN_HEAD)), row(ln_g), row(ln_b))


def _odd_out(alpha, y, xs, gmod, w_out, ln_g, ln_b, rows_out):
    b, lt, d = xs.shape
    di = y.shape[-1]
    row = lambda u: u.reshape(1, -1)
    tm = next(r for r in (PROJ_ROWS, 2 * TOK_BLOCK, TOK_BLOCK) if rows_out % r == 0)
    return pl.pallas_call(
        functools.partial(_odd_out_kernel, alpha, lt - TOK_BLOCK),
        grid=(b, rows_out // tm),
        in_specs=[_tok_spec(di, tm), _tok_spec(d, tm), _gate_spec(d), _full2((di, d)), _full2((1, d)),
                  _full2((1, d))],
        out_specs=_tok_spec(d, tm),
        out_shape=jax.ShapeDtypeStruct((b, rows_out, d), F32),
        compiler_params=_cparams("arbitrary", "arbitrary"),
        name="odd_out",
    )(y, xs, gmod, w_out, row(ln_g), row(ln_b))


def _rope(x, cos, sin):
    width = x.shape[-1]
    lane = lax.broadcasted_iota(jnp.int32, (1, width), 1)
    first = (lane % 32) < 16
    partner = jnp.where(first, pltpu.roll(x, width - 16, 1), pltpu.roll(x, 16, 1))
    return x * cos + partner * sin


def _attn_kernel(lam_init, tk, n_full, tail, lam_ref, q_ref, k_ref, vt_ref, g_ref, sg_ref, *rest):
    y_ref, m_scr, acc_scr = rest[-3:]
    tq = q_ref.shape[1]
    ones_rows = 16
    tile = min(ATTN_TILE, 2 * tq)
    n_tiles = 2 * tq // tile
    q = q_ref[0].astype(F32) * (DIFF_HEAD ** -0.5 * float(np.log2(np.e)))
    lane = lax.broadcasted_iota(jnp.int32, (1, LANES), 1)
    q_cat = jnp.concatenate([jnp.where(lane < DIFF_HEAD, q, 0.0), jnp.where(lane >= DIFF_HEAD, q, 0.0)],
                            axis=0).astype(BF16)
    m_scr[...] = jnp.full(m_scr.shape, -jnp.inf, F32)
    acc_scr[...] = jnp.zeros(acc_scr.shape, F32)

    def scores(item):
        start, size, c = item
        return _nt(k_ref[0, pl.ds(start, size), :], q_cat[c * tile:(c + 1) * tile])

    def run(blocks):
        cols = [slice(c * tile, (c + 1) * tile) for c in range(n_tiles)]
        m_run = [m_scr[:, cs] for cs in cols]
        acc = [acc_scr[:, cs] for cs in cols]
        vta = None
        pending = None
        items = [(start, size, c) for start, size in blocks for c in range(n_tiles)]
        st_next = scores(items[0])
        for i, (start, size, c) in enumerate(items):
            st = st_next
            if i + 1 < len(items):
                st_next = scores(items[i + 1])
            if pending is not None:
                pc, pcorr, pvta, ppt = pending
                acc[pc] = acc[pc] * pcorr + _mm(pvta, ppt)
            if c == 0:
                vta = jnp.concatenate([vt_ref[0, :, pl.ds(start, size)], jnp.ones((ones_rows, size), BF16)], axis=0)
            m_new = jnp.maximum(m_run[c], jnp.max(st, axis=0, keepdims=True))
            pending = (c, jnp.exp2(m_run[c] - m_new), vta, jnp.exp2(st - m_new).astype(BF16))
            m_run[c] = m_new
        pc, pcorr, pvta, ppt = pending
        acc[pc] = acc[pc] * pcorr + _mm(pvta, ppt)
        for c, cs in enumerate(cols):
            m_scr[:, cs] = m_run[c]
            acc_scr[:, cs] = acc[c]

    tail_block = [(n_full * tk, tail)]
    if n_full <= ATTN_UNROLL:
        run([(u * tk, tk) for u in range(n_full)] + tail_block)
    else:
        assert n_full % ATTN_UNROLL == 0
        def body(j, carry):
            run([(pl.multiple_of((j * ATTN_UNROLL + u) * tk, tk), tk) for u in range(ATTN_UNROLL)])
            return carry
        lax.fori_loop(0, n_full // ATTN_UNROLL, body, 0)
        run(tail_block)

    a0 = acc_scr[:, :tq]
    a1 = acc_scr[:, tq:]
    o_t = a0[:LANES] / a0[LANES:LANES + 1] - lam_ref[0] * (a1[:LANES] / a1[LANES:LANES + 1])
    o = jnp.transpose(o_t)
    y = o * lax.rsqrt(jnp.mean(o * o, axis=-1, keepdims=True) + LN_EPS) * sg_ref[...] * (1.0 - lam_init)
    y_ref[0] = (y * _silu(g_ref[0].astype(F32))).astype(y_ref.dtype)


def _diff_attention(p_qk, p_vt, p_g, lam, subln_g, lam_init):
    b, lt, di = p_g.shape
    nh = di // LANES
    t = lt - TOK_BLOCK
    tk = 512
    tq = next(r for r in (1024, 512) if t % r == 0)
    ctx_blk = t // TOK_BLOCK
    sm = pl.BlockSpec(memory_space=pltpu.SMEM)
    sg_spec = pl.BlockSpec((1, LANES), lambda bb, h, i: (0, 0))
    scratch = lambda n: [pltpu.VMEM((1, 2 * n), F32), pltpu.VMEM((LANES + 16, 2 * n), F32)]
    args = (lam.reshape(1), p_qk, p_qk, p_vt, p_g, subln_g.reshape(1, LANES))
    y = pl.pallas_call(
        functools.partial(_attn_kernel, lam_init, tk, t // tk, TOK_BLOCK),
        grid=(b, nh, t // tq),
        in_specs=[sm,
                  pl.BlockSpec((1, tq, LANES), lambda bb, h, i: (bb, i, h)),
                  pl.BlockSpec((1, lt, LANES), lambda bb, h, i: (bb, 0, nh + h)),
                  pl.BlockSpec((1, LANES, lt), lambda bb, h, i: (bb, h, 0)),
                  pl.BlockSpec((1, tq, LANES), lambda bb, h, i: (bb, i, h)),
                  sg_spec],
        out_specs=pl.BlockSpec((1, tq, LANES), lambda bb, h, i: (bb, i, h)),
        out_shape=jax.ShapeDtypeStruct((b, lt, di), BF16),
        scratch_shapes=scratch(tq),
        compiler_params=_cparams("arbitrary", "arbitrary", "arbitrary"),
        name="diff_attn",
    )(*args)
    blk = lambda bb, h, i: (bb, ctx_blk, h)
    return pl.pallas_call(
        functools.partial(_attn_kernel, lam_init, tk, 0, TOK_BLOCK),
        grid=(b, nh, 1),
        in_specs=[sm,
                  pl.BlockSpec((1, TOK_BLOCK, LANES), blk),
                  pl.BlockSpec((1, TOK_BLOCK, LANES), lambda bb, h, i: (bb, ctx_blk, nh + h)),
                  pl.BlockSpec((1, LANES, TOK_BLOCK), lambda bb, h, i: (bb, h, ctx_blk)),
                  pl.BlockSpec((1, TOK_BLOCK, LANES), blk),
                  sg_spec,
                  pl.BlockSpec(memory_space=pl.ANY)],
        out_specs=pl.BlockSpec((1, TOK_BLOCK, LANES), blk),
        out_shape=jax.ShapeDtypeStruct((b, lt, di), BF16),
        scratch_shapes=scratch(TOK_BLOCK),
        input_output_aliases={6: 0},
        compiler_params=_cparams("arbitrary", "arbitrary", "arbitrary"),
        name="diff_attn_ctx",
    )(*args, y)


def _rope_tables(n_ctx, t):
    quarter = DIFF_HEAD // 4
    inv = ROPE_BASE ** (-jnp.arange(quarter, dtype=F32) / quarter)
    pos = jnp.arange(t)
    rows = (pos // GRID_W).astype(F32)[:, None] * inv
    cols = (pos % GRID_W).astype(F32)[:, None] * inv
    cos64 = jnp.concatenate([jnp.cos(rows), jnp.cos(rows), jnp.cos(cols), jnp.cos(cols)], -1)
    sin64 = jnp.concatenate([-jnp.sin(rows), jnp.sin(rows), -jnp.sin(cols), jnp.sin(cols)], -1)
    cos = jnp.concatenate([jnp.tile(cos64, (1, 2)), jnp.ones((n_ctx, LANES), F32)], 0)
    sin = jnp.concatenate([jnp.tile(sin64, (1, 2)), jnp.zeros((n_ctx, LANES), F32)], 0)
    return cos, sin


def _even_layer(alpha, xs, scale, shift, gmod, w_in, w_out, mu_prev, mu_next, w0, w2, a0, a2, k_k, k_a, r_k,
                lnx_g, lnx_b, lb, norm_g, ln_g, ln_b):
    w = k_k.shape[-1]
    sw = mu_prev.shape[-1]
    hw = 4 * lb.shape[-1]
    w_in = w_in.astype(BF16)
    p_rwkv = _project(xs, scale, shift, w_in[:, :sw], sw, F32)
    p_hgrn = _project(xs, scale, shift, w_in[:, sw:sw + hw], hw // 2, F32)
    gate = _project(xs, scale, shift, w_in[:, sw + hw:], w_in.shape[1] - sw - hw, BF16)
    v, bonus, ops_f, ops_r = _rwkv_prep(p_rwkv, mu_prev, mu_next, w0, w2, a0, a2, k_k, k_a, r_k)
    ya0, yb0 = _mix_scan(ops_f, v, p_hgrn, lb[0], 0, False)
    ya1, yb1 = _mix_scan(ops_r, v, p_hgrn, lb[1], 1, True)
    return _even_out(alpha, ya0, ya1, bonus, yb0, yb1, gate, xs, gmod, w_out.astype(BF16),
                     lnx_g, lnx_b, norm_g, ln_g, ln_b)


def _odd_layer(alpha, xs, scale, shift, gmod, w_in, w_out, lam_p, subln_g, lam_init, cos, sin, ln_g, ln_b,
               rows_out):
    di = w_out.shape[0]
    w_in = w_in.astype(BF16)
    p_qk = _project(xs, scale, shift, w_in[:, :2 * di], di, BF16, rope=(cos, sin))
    p_vt = _project_t(xs, scale, shift, jnp.transpose(w_in[:, 2 * di:3 * di]), BF16)
    p_g = _project(xs, scale, shift, w_in[:, 3 * di:], di, BF16)
    lam = jnp.exp(jnp.sum(lam_p[0] * lam_p[1])) - jnp.exp(jnp.sum(lam_p[2] * lam_p[3])) + lam_init
    y = _diff_attention(p_qk, p_vt, p_g, lam, subln_g, lam_init)
    return _odd_out(alpha, y, xs, gmod, w_out.astype(BF16), ln_g, ln_b, rows_out)


def kernel(x, c, ctx, c_ctx, ada_w, ada_b, ln_g, ln_b, even_w_in, even_w_out, rwkv_mu_prev, rwkv_mu_next, rwkv_w0, rwkv_w2, rwkv_a0, rwkv_a2, rwkv_k_k, rwkv_k_a, rwkv_r_k, rwkv_lnx_g, rwkv_lnx_b, hgrn_lb_logits, hgrn_norm_g, odd_w_in, odd_w_out, diff_lambda, diff_subln_g):
    b, t, d = x.shape
    n_ctx = ctx.shape[1]
    depth = ada_w.shape[0]
    assert n_ctx == TOK_BLOCK and t % TOK_BLOCK == 0 and b + 1 <= 8
    alpha = (2.0 * depth) ** 0.25

    xs = jnp.concatenate([x, ctx], axis=1)
    cvec = jnp.concatenate([c, c_ctx[None], jnp.zeros((8 - b - 1, d), F32)], axis=0)
    mods = _ada_mods(cvec, ada_w, ada_b)
    lb_all = jax.nn.softmax(hgrn_lb_logits.astype(F32), axis=0)
    lb_all = jnp.cumsum(lb_all, axis=0) - lb_all[0]
    cos, sin = _rope_tables(n_ctx, t)

    for layer in range(depth):
        m = mods[layer]
        per_seg = lambda u: jnp.stack([jnp.broadcast_to(u[b], (b, d)), u[:b]], axis=1)[:, :, None, :]
        shift, scale, gmod = (per_seg(m[:, j * d:(j + 1) * d]) for j in range(3))
        j = layer // 2
        if layer % 2 == 0:
            xs = _even_layer(alpha, xs, scale, shift, gmod, even_w_in[j], even_w_out[j], rwkv_mu_prev[j],
                             rwkv_mu_next[j], rwkv_w0[j], rwkv_w2[j], rwkv_a0[j], rwkv_a2[j], rwkv_k_k[j],
                             rwkv_k_a[j], rwkv_r_k[j], rwkv_lnx_g[j], rwkv_lnx_b[j], lb_all[j], hgrn_norm_g[j],
                             ln_g[layer], ln_b[layer])
        else:
            lam_init = 0.8 - 0.6 * float(np.exp(-0.3 * layer))
            xs = _odd_layer(alpha, xs, scale, shift, gmod, odd_w_in[j], odd_w_out[j], diff_lambda[j],
                            diff_subln_g[j], lam_init, cos, sin, ln_g[layer], ln_b[layer],
                            t if layer == depth - 1 else t + n_ctx)
    return xs[:, :t]
```

```python
import functools

import numpy as np
import jax
import jax.numpy as jnp
from jax import lax
from jax.experimental import pallas as pl
from jax.experimental.pallas import tpu as pltpu

F32 = jnp.float32
BF16 = jnp.bfloat16
HI = lax.Precision.HIGHEST

GRID_W = 64
RWKV_HEAD = 64
RWKV_LORA = 64
RWKV_GN_EPS = 64e-5
HGRN_HEAD = 128
DIFF_HEAD = 64
ROPE_BASE = 10000.0
LN_EPS = 1e-5
LOG2E = float(np.log2(np.e))

LANES = 128
MXU_DIM = 256
VMEM_LIMIT = 56 * 1024 * 1024

CHUNK = 64
SUB = 16
TOK_BLOCK = 256
PROJ_ROWS = 768
GROUP_HEADS = MXU_DIM // RWKV_HEAD
HGRN_PAR = 4
ATTN_UNROLL = 16
ATTN_TILE = 4096


def _cparams(*sem):
    return pltpu.CompilerParams(dimension_semantics=sem, vmem_limit_bytes=VMEM_LIMIT)


def _nt(a, b):
    return lax.dot_general(a, b, (((1,), (1,)), ((), ())), preferred_element_type=F32)


def _tn(a, b):
    return lax.dot_general(a, b, (((0,), (0,)), ((), ())), preferred_element_type=F32)


def _mm(a, b):
    return jnp.dot(a, b, preferred_element_type=F32)


def _mm_exact(a, b):
    return jnp.dot(a, b, preferred_element_type=F32, precision=HI)


def _sigmoid(x):
    return 1.0 / (1.0 + jnp.exp(-x))


def _silu(x):
    return x * _sigmoid(x)


def _lane_tiles(x):
    return [x[:, j * LANES:(j + 1) * LANES] for j in range(x.shape[-1] // LANES)]


def _group64_sum(x):
    lane = lax.broadcasted_iota(jnp.int32, (1, LANES), 1)
    low = lane < RWKV_HEAD
    out = []
    for xt in _lane_tiles(x):
        s_all = jnp.sum(xt, axis=-1, keepdims=True)
        s_lo = jnp.sum(jnp.where(low, xt, 0.0), axis=-1, keepdims=True)
        out.append(jnp.where(low, s_lo, s_all - s_lo))
    return jnp.concatenate(out, axis=-1)


def _group128_mean(x):
    out = []
    for xt in _lane_tiles(x):
        out.append(jnp.broadcast_to(jnp.mean(xt, axis=-1, keepdims=True), xt.shape))
    return jnp.concatenate(out, axis=-1)


def _ada_kernel(c_ref, w_ref, b_ref, o_ref):
    cond = _silu(c_ref[...])
    o_ref[0] = _mm_exact(cond, w_ref[0]) + b_ref[0]


def _ada_mods(cvec, ada_w, ada_b):
    depth, d, d3 = ada_w.shape
    tn = 1024
    return pl.pallas_call(
        _ada_kernel,
        grid=(depth, d3 // tn),
        in_specs=[
            pl.BlockSpec((8, d), lambda l, j: (0, 0)),
            pl.BlockSpec((1, d, tn), lambda l, j: (l, 0, j)),
            pl.BlockSpec((1, 1, tn), lambda l, j: (l, 0, j)),
        ],
        out_specs=pl.BlockSpec((1, 8, tn), lambda l, j: (l, 0, j)),
        out_shape=jax.ShapeDtypeStruct((depth, 8, d3), F32),
        compiler_params=_cparams("arbitrary", "arbitrary"),
        name="ada_mods",
    )(cvec, ada_w, ada_b.reshape(depth, 1, d3))


def _proj_rows(lt):
    return PROJ_ROWS if lt % PROJ_ROWS == 0 else TOK_BLOCK


def _modulate(n_lat, axis, x_ref, sc_ref, sh_ref):
    tm = x_ref.shape[1]
    rows = pl.program_id(axis) * tm + lax.broadcasted_iota(jnp.int32, (tm, 1), 0)
    is_ctx = rows >= n_lat
    sc = jnp.where(is_ctx, sc_ref[0, 0], sc_ref[0, 1])
    sh = jnp.where(is_ctx, sh_ref[0, 0], sh_ref[0, 1])
    return (x_ref[0] * (1.0 + sc) + sh).astype(BF16)


def _proj_kernel(n_lat, x_ref, sc_ref, sh_ref, w_ref, o_ref):
    o_ref[0] = _mm(_modulate(n_lat, 2, x_ref, sc_ref, sh_ref), w_ref[...]).astype(o_ref.dtype)


def _proj_rope_kernel(n_lat, x_ref, sc_ref, sh_ref, w_ref, cos_ref, sin_ref, o_ref):
    p = _mm(_modulate(n_lat, 2, x_ref, sc_ref, sh_ref), w_ref[...])
    reps = p.shape[-1] // LANES
    cos = jnp.concatenate([cos_ref[...]] * reps, axis=-1)
    sin = jnp.concatenate([sin_ref[...]] * reps, axis=-1)
    o_ref[0] = _rope(p, cos, sin).astype(o_ref.dtype)


def _proj_t_kernel(n_lat, x_ref, sc_ref, sh_ref, wt_ref, o_ref):
    o_ref[0] = _nt(wt_ref[...], _modulate(n_lat, 1, x_ref, sc_ref, sh_ref)).astype(o_ref.dtype)


def _project(xs, scale, shift, w, tn, out_dtype, rope=None):
    b, lt, d = xs.shape
    n = w.shape[1]
    tm = _proj_rows(lt)
    n_lat = lt - TOK_BLOCK
    in_specs = [
        pl.BlockSpec((1, tm, d), lambda j, bb, i: (bb, i, 0)),
        pl.BlockSpec((1, 2, 1, d), lambda j, bb, i: (bb, 0, 0, 0)),
        pl.BlockSpec((1, 2, 1, d), lambda j, bb, i: (bb, 0, 0, 0)),
        pl.BlockSpec((d, tn), lambda j, bb, i: (0, j)),
    ]
    args = (xs, scale, shift, w)
    if rope is not None:
        in_specs += [pl.BlockSpec((tm, LANES), lambda j, bb, i: (i, 0))] * 2
        args += tuple(rope)
    return pl.pallas_call(
        functools.partial(_proj_kernel if rope is None else _proj_rope_kernel, n_lat),
        grid=(n // tn, b, lt // tm),
        in_specs=in_specs,
        out_specs=pl.BlockSpec((1, tm, tn), lambda j, bb, i: (bb, i, j)),
        out_shape=jax.ShapeDtypeStruct((b, lt, n), out_dtype),
        compiler_params=_cparams("arbitrary", "arbitrary", "arbitrary"),
        name="mod_proj" if rope is None else "mod_proj_rope",
    )(*args)


def _project_t(xs, scale, shift, wt, out_dtype):
    b, lt, d = xs.shape
    n = wt.shape[0]
    tm = _proj_rows(lt)
    return pl.pallas_call(
        functools.partial(_proj_t_kernel, lt - TOK_BLOCK),
        grid=(b, lt // tm),
        in_specs=[
            pl.BlockSpec((1, tm, d), lambda bb, i: (bb, i, 0)),
            pl.BlockSpec((1, 2, 1, d), lambda bb, i: (bb, 0, 0, 0)),
            pl.BlockSpec((1, 2, 1, d), lambda bb, i: (bb, 0, 0, 0)),
            pl.BlockSpec((n, d), lambda bb, i: (0, 0)),
        ],
        out_specs=pl.BlockSpec((1, n, tm), lambda bb, i: (bb, 0, i)),
        out_shape=jax.ShapeDtypeStruct((b, n, lt), out_dtype),
        compiler_params=_cparams("arbitrary", "arbitrary"),
        name="mod_proj_t",
    )(xs, scale, shift, wt)


def _lora(x, w_ref):
    x1 = x.astype(BF16)
    x2 = (x - x1.astype(F32)).astype(BF16)
    return _mm(jnp.concatenate([x1, x1, x2], axis=-1), w_ref[...])


def _rwkv_prep_kernel(nblk, p_ref, prev_ref, next_ref, mup_ref, mun_ref, w0_ref, w2_ref, a0_ref, a2_ref,
                      kk_ref, ka_ref, rk_ref, cm_ref,
                      v_out, bonus_out, *dir_outs):
    i = pl.program_id(1)
    w = kk_ref.shape[-1]
    p = p_ref[0]
    row = lax.broadcasted_iota(jnp.int32, (TOK_BLOCK, 1), 0)
    prev_row = jnp.where(jnp.logical_and(i >= 1, i <= nblk - 2), prev_ref[0][7:8, :], 0.0)
    next_row = jnp.where(i <= nblk - 3, next_ref[0][0:1, :], 0.0)
    prev = jnp.where(row == 0, prev_row, pltpu.roll(p, 1, 0))
    nxt = jnp.where(row == TOK_BLOCK - 1, next_row, pltpu.roll(p, TOK_BLOCK - 1, 0))
    ps = p * (1.0 - mup_ref[...] - mun_ref[...]) + mup_ref[...] * prev + mun_ref[...] * nxt

    r = ps[:, 0:w]
    k = ps[:, w:2 * w]
    v = ps[:, 2 * w:3 * w]
    wlo = ps[:, 3 * w:3 * w + 2 * RWKV_LORA]
    alo = ps[:, 3 * w + 2 * RWKV_LORA:3 * w + 4 * RWKV_LORA]

    z = _lora(jnp.tanh(wlo), w2_ref) + w0_ref[...]
    lw = _sigmoid(z) * (-LOG2E * float(np.exp(-0.5)))
    a = _sigmoid(_lora(alo, a2_ref) + a0_ref[...])

    kk = k * kk_ref[...]
    kk = kk * lax.rsqrt(_group64_sum(kk * kk) + 1e-12)
    bonus_out[0] = (_group64_sum(r * k * rk_ref[...]) * v).astype(bonus_out.dtype)
    v_out[0] = v.astype(BF16)

    for d in range(2):
        rh_o, kh_o, bh_o, kap_o, kg_o, bg_o, gl_o = dir_outs[7 * d:7 * d + 7]
        lw_d = lw[:, d * w:(d + 1) * w]
        a_d = a[:, d * w:(d + 1) * w]
        kd = k * (1.0 + (a_d - 1.0) * ka_ref[...])
        b_d = kk * a_d
        sums = _mm(cm_ref[d], _split3(lw_d))
        g = sums[:TOK_BLOCK]
        g_all = sums[TOK_BLOCK:]
        e_neg = jnp.exp2(-g)
        e_rest = jnp.exp2(g_all - g)
        rh_o[0] = (r * jnp.exp2(g)).astype(BF16)
        kh_o[0] = (kd * e_neg).astype(BF16)
        bh_o[0] = (b_d * e_neg).astype(BF16)
        kap_o[0] = (kk * jnp.exp2(g - lw_d)).astype(BF16)
        kg_o[0] = (kd * e_rest).astype(BF16)
        bg_o[0] = (b_d * e_rest).astype(BF16)
        e_all = jnp.exp2(g_all)
        for c in range(TOK_BLOCK // CHUNK):
            gl_o[0, c] = e_all[c * CHUNK:c * CHUNK + 1, :]


def _rwkv_sum_consts(reverse):
    t = np.arange(TOK_BLOCK)[:, None]
    s = np.arange(TOK_BLOCK)[None, :]
    same = (t // CHUNK) == (s // CHUNK)
    order = (s >= t) if reverse else (s <= t)
    both = np.concatenate([same & order, same], axis=0).astype(np.float32)
    return np.concatenate([both] * 3, axis=1)


def _lora_pieces(m):
    hi = m.astype(BF16)
    lo = (m - hi.astype(F32)).astype(BF16)
    return jnp.concatenate([hi, lo, hi], axis=0)


def _rwkv_prep(p_rwkv, mu_prev, mu_next, w0, w2, a0, a2, k_k, k_a, r_k):
    b, lt, sw = p_rwkv.shape
    w = k_k.shape[-1]
    nblk = lt // TOK_BLOCK
    nch = lt // CHUNK
    halo = TOK_BLOCK // 8
    zl = jnp.zeros((RWKV_LORA, w), F32)
    w2cat = _lora_pieces(jnp.concatenate([jnp.concatenate([w2[0], zl], 1), jnp.concatenate([zl, w2[1]], 1)], 0))
    a2cat = _lora_pieces(jnp.concatenate([jnp.concatenate([a2[0], zl], 1), jnp.concatenate([zl, a2[1]], 1)], 0))
    cm = jnp.asarray(np.stack([_rwkv_sum_consts(False), _rwkv_sum_consts(True)]), dtype=BF16)
    row = lambda u: u.reshape(1, -1)
    full = lambda shape: pl.BlockSpec(shape, lambda bb, i: (0,) * len(shape))
    tok = lambda n, dt: (pl.BlockSpec((1, TOK_BLOCK, n), lambda bb, i: (bb, i, 0)), jax.ShapeDtypeStruct((b, lt, n), dt))
    gl = (pl.BlockSpec((1, TOK_BLOCK // CHUNK, 1, w), lambda bb, i: (bb, i, 0, 0)),
          jax.ShapeDtypeStruct((b, nch, 1, w), F32))
    outs = [tok(w, BF16), tok(w, BF16)] + 2 * ([tok(w, BF16)] * 6 + [gl])
    res = pl.pallas_call(
        functools.partial(_rwkv_prep_kernel, nblk),
        grid=(b, nblk),
        in_specs=[
            pl.BlockSpec((1, TOK_BLOCK, sw), lambda bb, i: (bb, i, 0)),
            pl.BlockSpec((1, 8, sw), lambda bb, i: (bb, jnp.maximum(i * halo - 1, 0), 0)),
            pl.BlockSpec((1, 8, sw), lambda bb, i: (bb, jnp.minimum((i + 1) * halo, lt // 8 - 1), 0)),
            full((1, sw)), full((1, sw)), full((1, 2 * w)), full(w2cat.shape),
            full((1, 2 * w)), full(a2cat.shape), full((1, w)), full((1, w)), full((1, w)),
            full(cm.shape),
        ],
        out_specs=[o[0] for o in outs],
        out_shape=[o[1] for o in outs],
        compiler_params=_cparams("arbitrary", "arbitrary"),
        name="rwkv_prep",
    )(p_rwkv, p_rwkv, p_rwkv, row(mu_prev), row(mu_next), row(w0), w2cat, row(a0), a2cat,
      row(k_k), row(k_a), row(r_k), cm)
    v, bonus = res[0], res[1]
    return v, bonus, res[2:9], res[9:16]


def _rwkv_masks(reverse):
    n = GROUP_HEADS * CHUNK
    i = np.arange(n)[:, None]
    j = np.arange(n)[None, :]
    same_head = (i // CHUNK) == (j // CHUNK)
    before = (j > i) if reverse else (j < i)
    strict = same_head & before
    incl = same_head & (before | (i == j))
    levels = []
    size = 1
    while size < CHUNK:
        levels.append(same_head & ((i // (2 * size)) == (j // (2 * size))) & ((i // size) != (j // size)))
        size *= 2
    head_lane = (i // CHUNK) == (j // RWKV_HEAD)
    tri = np.stack([strict, incl]).astype(np.float32)
    return tri, np.stack(levels).astype(np.float32), head_lane.astype(np.float32)


def _scan_chunk_index(c, nch, reverse):
    if reverse:
        return nch - 1 - c
    n_ctx = TOK_BLOCK // CHUNK
    return jnp.where(c < n_ctx, nch - n_ctx + c, c - n_ctx)


def _split3(x):
    x1 = x.astype(BF16)
    r1 = x - x1.astype(F32)
    x2 = r1.astype(BF16)
    x3 = (r1 - x2.astype(F32)).astype(BF16)
    return jnp.concatenate([x1, x2, x3], axis=0)


def _hgrn_chunk_consts(reverse):
    t = np.arange(CHUNK)[:, None]
    s = np.arange(CHUNK)[None, :]
    order = (s >= t) if reverse else (s <= t)
    earlier_sub = ((s // SUB) > (t // SUB)) if reverse else ((s // SUB) < (t // SUB))
    both = np.concatenate([order, earlier_sub], axis=0).astype(np.float32)
    return np.concatenate([both] * 3, axis=1)


def _rwkv_chunk_stages(rh_ref, kh_ref, bh_ref, kap_ref, kg_ref, bg_ref, v_ref, gl_ref, tri_ref, lvl_ref, hm_ref,
                       y_ref, s_ref):
    hm = hm_ref[...]
    strict = tri_ref[0]
    incl = tri_ref[1]
    eye = incl - strict
    n_levels = lvl_ref.shape[0]
    groups = range(y_ref.shape[-1] // MXU_DIM)
    lanes = lambda ref, g: ref[0, :, g * MXU_DIM:(g + 1) * MXU_DIM]
    stack = lambda ref, g: jnp.concatenate([lanes(ref, g)] * GROUP_HEADS, axis=0) * hm

    kap, bh, a_ab = [], [], []
    for g in groups:
        kap.append(stack(kap_ref, g))
        bh.append(stack(bh_ref, g))
        a_ab.append(_nt(kap[g], bh[g]) * strict)
    yield
    kh = [stack(kh_ref, g) for g in groups]
    a_ak = [(_nt(kap[g], kh[g]) * strict).astype(BF16) for g in groups]
    rh = [stack(rh_ref, g) for g in groups]
    p_rb = [(_nt(rh[g], bh[g]) * incl).astype(BF16) for g in groups]
    p_rk = [(_nt(rh[g], kh[g]) * incl).astype(BF16) for g in groups]
    vs = [stack(v_ref, g) for g in groups]
    t = [eye - a_ab[g] * lvl_ref[0] for g in groups]
    a_abb = [a_ab[g].astype(BF16) for g in groups]
    for lv in range(1, n_levels):
        yield
        tb = [t[g].astype(BF16) for g in groups]
        x = [_mm(tb[g], a_abb[g]).astype(BF16) for g in groups]
        yield
        t = [t[g] - _mm(x[g], tb[g]) * lvl_ref[lv] for g in groups]
    yield
    s0 = [s_ref[g] for g in groups]
    s0b = [s0[g].astype(BF16) for g in groups]
    hm32 = hm.astype(F32)
    ks0 = [jnp.concatenate([_nt(lanes(kap_ref, g), s0b[g])] * GROUP_HEADS, axis=0) * hm32 for g in groups]
    wmat = [(ks0[g] + _mm(a_ak[g], vs[g])).astype(BF16) for g in groups]
    yield
    ub = [(-_mm(t[g].astype(BF16), wmat[g])).astype(BF16) for g in groups]
    yield
    ys = [_mm(p_rb[g], ub[g]) + _mm(p_rk[g], vs[g]) for g in groups]
    for g in groups:
        y = _nt(lanes(rh_ref, g), s0b[g])
        for h in range(GROUP_HEADS):
            y = y + ys[g][h * CHUNK:(h + 1) * CHUNK]
        y_ref[0, :, g * MXU_DIM:(g + 1) * MXU_DIM] = y.astype(y_ref.dtype)
    yield
    for g in groups:
        gam = gl_ref[0, 0][:, g * MXU_DIM:(g + 1) * MXU_DIM]
        s_ref[g] = s0[g] * gam + _tn(ub[g], stack(bg_ref, g)) + _tn(vs[g], stack(kg_ref, g))


def _hgrn_chunk_stages(reverse, q_ref, f_ref, i_ref, lb_ref, cm_ref, o_ref, s_ref, g_scr, q_scr):
    n_sub = CHUNK // SUB
    n_heads = q_ref.shape[-1] // HGRN_HEAD
    sub_order = list(range(n_sub))[::-1] if reverse else list(range(n_sub))
    row_sub = lax.broadcasted_iota(jnp.int32, (CHUNK, 1), 0) // SUB
    srow = lax.broadcasted_iota(jnp.int32, (SUB, 1), 0)
    lane = lax.broadcasted_iota(jnp.int32, (1, LANES), 1)
    last_row = 0 if reverse else CHUNK - 1

    kgate = (1.0 - lb_ref[...]) * _sigmoid(-f_ref[0])
    sums = _mm(cm_ref[...], _split3(jnp.log1p(-kgate) * LOG2E))
    g_all = sums[:CHUNK]
    gb_all = sums[CHUNK:]
    q_all = _silu(q_ref[0])
    g_scr[...] = g_all
    q_scr[...] = q_all
    yield
    for hg in range(n_heads // HGRN_PAR):
        par = range(HGRN_PAR)
        hs = [hg * HGRN_PAR + j for j in par]
        cols = [slice(h * HGRN_HEAD, (h + 1) * HGRN_HEAD) for h in hs]
        g = [g_all[:, cs] for cs in cols]
        gb = [gb_all[:, cs] for cs in cols]
        q = [q_all[:, cs] for cs in cols]
        k = [kgate[:, cs] for cs in cols]
        vb = [i_ref[0, :, cs].astype(BF16) for cs in cols]
        s0 = [s_ref[h] for h in hs]
        q_in = [q[j] * jnp.exp2(g[j] - gb[j]) for j in par]
        at = []
        for j in par:
            kts, qts = [], []
            for si in sub_order[1:]:
                gb_i = gb[j][si * SUB:si * SUB + 1, :]
                earlier = (row_sub > si) if reverse else (row_sub < si)
                kts.append(jnp.where(earlier, k[j] * jnp.exp2(jnp.minimum(gb_i - g[j], 0.0)), 0.0).astype(BF16))
                qts.append(jnp.where(row_sub == si, q_in[j], 0.0).astype(BF16))
            at.append(_nt(jnp.concatenate(kts, axis=-1), jnp.concatenate(qts, axis=-1)))
        yield
        diag = [[] for _ in par]
        for si in range(n_sub):
            blk = slice(si * SUB, (si + 1) * SUB)
            acc = [jnp.zeros((SUB, LANES), F32) for _ in par]
            for tt in range(SUB):
                t = si * SUB + tt
                valid = jnp.logical_and(lane == t, (srow >= tt) if reverse else (srow <= tt))
                for j in par:
                    e = jnp.exp2(jnp.minimum(g_scr[t:t + 1, cols[j]] - g[j][blk], 0.0))
                    col = jnp.sum(k[j][blk] * e * q_scr[t:t + 1, cols[j]], axis=-1, keepdims=True)
                    acc[j] = jnp.where(valid, col, acc[j])
                if tt % 8 == 7:
                    yield
            for j in par:
                diag[j].append(acc[j])
        for j in par:
            a = (at[j] + jnp.concatenate(diag[j], axis=0)[:, :CHUNK]).astype(BF16)
            o = _tn(a, vb[j]) + _nt((q[j] * jnp.exp2(g[j])).astype(BF16), s0[j].astype(BF16))
            o_ref[0, :, cols[j]] = o.astype(o_ref.dtype)
            g_last = g[j][last_row:last_row + 1, :]
            k_out = (k[j] * jnp.exp2(g_last - g[j])).astype(BF16)
            s_ref[hs[j]] = s0[j] * jnp.exp2(g_last) + _tn(vb[j], k_out)
        yield


def _mix_scan_kernel(reverse, *refs):
    rwkv_in, hgrn_in = refs[:11], refs[11:16]
    y_ref, o_ref, s_ref, hs_ref, g_scr, q_scr = refs[16:]

    @pl.when(pl.program_id(1) == 0)
    def _():
        s_ref[...] = jnp.zeros_like(s_ref)
        hs_ref[...] = jnp.zeros_like(hs_ref)

    streams = [_rwkv_chunk_stages(*rwkv_in, y_ref, s_ref),
               _hgrn_chunk_stages(reverse, *hgrn_in, o_ref, hs_ref, g_scr, q_scr)]
    done = object()
    while streams:
        streams = [st for st in streams if next(st, done) is not done]


def _mix_scan(dir_ops, v, p_hgrn, lb_d, d, reverse):
    rh, kh, bh, kap, kg, bg, gl = dir_ops
    b, lt, w = v.shape
    nch = lt // CHUNK
    n = GROUP_HEADS * CHUNK
    tri, lvl, hm = _rwkv_masks(reverse)
    cm = jnp.asarray(_hgrn_chunk_consts(reverse), dtype=BF16)
    cidx = lambda c: _scan_chunk_index(c, nch, reverse)
    tok = pl.BlockSpec((1, CHUNK, w), lambda bb, c: (bb, cidx(c), 0))
    col = lambda j: pl.BlockSpec((1, CHUNK, w), lambda bb, c: (bb, cidx(c), j))
    full = lambda shape: pl.BlockSpec(shape, lambda bb, c: (0,) * len(shape))
    return pl.pallas_call(
        functools.partial(_mix_scan_kernel, reverse),
        grid=(b, nch),
        in_specs=[tok] * 7 + [
            pl.BlockSpec((1, 1, 1, w), lambda bb, c: (bb, cidx(c), 0, 0)),
            full(tri.shape), full(lvl.shape), full(hm.shape),
            col(0), col(1 + d), col(3), full((1, w)), full(cm.shape),
        ],
        out_specs=[tok, tok],
        out_shape=[jax.ShapeDtypeStruct((b, lt, w), BF16)] * 2,
        scratch_shapes=[pltpu.VMEM((w // MXU_DIM, n, n), F32),
                        pltpu.VMEM((w // HGRN_HEAD, HGRN_HEAD, HGRN_HEAD), F32),
                        pltpu.VMEM((CHUNK, w), F32), pltpu.VMEM((CHUNK, w), F32)],
        compiler_params=_cparams("arbitrary", "arbitrary"),
        name="mix_scan_rev" if reverse else "mix_scan_fwd",
    )(rh, kh, bh, kap, kg, bg, v, gl, jnp.asarray(tri), jnp.asarray(lvl), jnp.asarray(hm, dtype=BF16),
      p_hgrn, p_hgrn, p_hgrn, lb_d.reshape(1, w), cm)


def _residual_ln(alpha, n_lat, x, gm_ref, proj, lng, lnb):
    tm = x.shape[0]
    rows = pl.program_id(1) * tm + lax.broadcasted_iota(jnp.int32, (tm, 1), 0)
    gm = jnp.where(rows >= n_lat, gm_ref[0, 0], gm_ref[0, 1])
    z = alpha * x + gm * proj
    mu = jnp.mean(z, axis=-1, keepdims=True)
    zc = z - mu
    var = jnp.mean(zc * zc, axis=-1, keepdims=True)
    return zc * lax.rsqrt(var + LN_EPS) * lng + lnb


def _even_out_kernel(alpha, n_lat, ya0_ref, ya1_ref, bonus_ref, yb0_ref, yb1_ref, gate_ref, x_ref, gm_ref, w_ref,
                     lnxg_ref, lnxb_ref, ng_ref, lng_ref, lnb_ref, o_ref):
    y = ya0_ref[0].astype(F32) + ya1_ref[0].astype(F32)
    mu = _group64_sum(y) * (1.0 / RWKV_HEAD)
    yc = y - mu
    var = _group64_sum(yc * yc) * (1.0 / RWKV_HEAD)
    ya = yc * lax.rsqrt(var + RWKV_GN_EPS) * lnxg_ref[...] + lnxb_ref[...] + bonus_ref[0].astype(F32)
    o = yb0_ref[0].astype(F32) + yb1_ref[0].astype(F32)
    yb = o * lax.rsqrt(_group128_mean(o * o) + LN_EPS) * ng_ref[...]
    ycat = jnp.concatenate([ya, yb], axis=-1) * _silu(gate_ref[0].astype(F32))
    proj = _mm(ycat.astype(BF16), w_ref[...])
    o_ref[0] = _residual_ln(alpha, n_lat, x_ref[0], gm_ref, proj, lng_ref[...], lnb_ref[...])


def _odd_out_kernel(alpha, n_lat, y_ref, x_ref, gm_ref, w_ref, lng_ref, lnb_ref, o_ref):
    proj = _mm(y_ref[0].astype(BF16), w_ref[...])
    o_ref[0] = _residual_ln(alpha, n_lat, x_ref[0], gm_ref, proj, lng_ref[...], lnb_ref[...])


def _tok_spec(n, tm=TOK_BLOCK):
    return pl.BlockSpec((1, tm, n), lambda bb, i: (bb, i, 0))


def _full2(shape):
    return pl.BlockSpec(shape, lambda bb, i: (0,) * len(shape))


def _gate_spec(d):
    return pl.BlockSpec((1, 2, 1, d), lambda bb, i: (bb, 0, 0, 0))


def _even_out(alpha, ya0, ya1, bonus, yb0, yb1, gate, xs, gmod, w_out, lnx_g, lnx_b, norm_g, ln_g, ln_b):
    b, lt, d = xs.shape
    w = ya0.shape[-1]
    di = gate.shape[-1]
    row = lambda u: u.reshape(1, -1)
    tm = _proj_rows(lt)
    return pl.pallas_call(
        functools.partial(_even_out_kernel, alpha, lt - TOK_BLOCK),
        grid=(b, lt // tm),
        in_specs=[_tok_spec(w, tm)] * 5 + [_tok_spec(di, tm), _tok_spec(d, tm), _gate_spec(d), _full2((di, d)),
                                           _full2((1, w)), _full2((1, w)), _full2((1, w)), _full2((1, d)),
                                           _full2((1, d))],
        out_specs=_tok_spec(d, tm),
        out_shape=jax.ShapeDtypeStruct((b, lt, d), F32),
        compiler_params=_cparams("arbitrary", "arbitrary"),
        name="even_out",
    )(ya0, ya1, bonus, yb0, yb1, gate, xs, gmod, w_out, row(lnx_g), row(lnx_b),
      row(jnp.tile(norm_g, w // HGRN_HEAD)), row(ln_g), row(ln_b))


def _odd_out(alpha, y, xs, gmod, w_out, ln_g, ln_b, rows_out):
    b, lt, d = xs.shape
    di = y.shape[-1]
    row = lambda u: u.reshape(1, -1)
    tm = next(r for r in (PROJ_ROWS, 2 * TOK_BLOCK, TOK_BLOCK) if rows_out % r == 0)
    return pl.pallas_call(
        functools.partial(_odd_out_kernel, alpha, lt - TOK_BLOCK),
        grid=(b, rows_out // tm),
        in_specs=[_tok_spec(di, tm), _tok_spec(d, tm), _gate_spec(d), _full2((di, d)), _full2((1, d)),
                  _full2((1, d))],
        out_specs=_tok_spec(d, tm),
        out_shape=jax.ShapeDtypeStruct((b, rows_out, d), F32),
        compiler_params=_cparams("arbitrary", "arbitrary"),
        name="odd_out",
    )(y, xs, gmod, w_out, row(ln_g), row(ln_b))


def _rope(x, cos, sin):
    width = x.shape[-1]
    lane = lax.broadcasted_iota(jnp.int32, (1, width), 1)
    first = (lane % 32) < 16
    partner = jnp.where(first, pltpu.roll(x, width - 16, 1), pltpu.roll(x, 16, 1))
    return x * cos + partner * sin


def _attn_kernel(lam_init, tk, n_full, tail, lam_ref, q_ref, k_ref, vt_ref, g_ref, sg_ref, *rest):
    y_ref, m_scr, acc_scr = rest[-3:]
    tq = q_ref.shape[1]
    ones_rows = 16
    tile = min(ATTN_TILE, 2 * tq)
    n_tiles = 2 * tq // tile
    q = q_ref[0].astype(F32) * (DIFF_HEAD ** -0.5 * float(np.log2(np.e)))
    lane = lax.broadcasted_iota(jnp.int32, (1, LANES), 1)
    q_cat = jnp.concatenate([jnp.where(lane < DIFF_HEAD, q, 0.0), jnp.where(lane >= DIFF_HEAD, q, 0.0)],
                            axis=0).astype(BF16)
    m_scr[...] = jnp.full(m_scr.shape, -jnp.inf, F32)
    acc_scr[...] = jnp.zeros(acc_scr.shape, F32)

    def scores(item):
        start, size, c = item
        return _nt(k_ref[0, pl.ds(start, size), :], q_cat[c * tile:(c + 1) * tile])

    def run(blocks):
        cols = [slice(c * tile, (c + 1) * tile) for c in range(n_tiles)]
        m_run = [m_scr[:, cs] for cs in cols]
        acc = [acc_scr[:, cs] for cs in cols]
        vta = None
        pending = None
        items = [(start, size, c) for start, size in blocks for c in range(n_tiles)]
        st_next = scores(items[0])
        for i, (start, size, c) in enumerate(items):
            st = st_next
            if i + 1 < len(items):
                st_next = scores(items[i + 1])
            if pending is not None:
                pc, pcorr, pvta, ppt = pending
                acc[pc] = acc[pc] * pcorr + _mm(pvta, ppt)
            if c == 0:
                vta = jnp.concatenate([vt_ref[0, :, pl.ds(start, size)], jnp.ones((ones_rows, size), BF16)], axis=0)
            m_new = jnp.maximum(m_run[c], jnp.max(st, axis=0, keepdims=True))
            pending = (c, jnp.exp2(m_run[c] - m_new), vta, jnp.exp2(st - m_new).astype(BF16))
            m_run[c] = m_new
        pc, pcorr, pvta, ppt = pending
        acc[pc] = acc[pc] * pcorr + _mm(pvta, ppt)
        for c, cs in enumerate(cols):
            m_scr[:, cs] = m_run[c]
            acc_scr[:, cs] = acc[c]

    tail_block = [(n_full * tk, tail)]
    if n_full <= ATTN_UNROLL:
        run([(u * tk, tk) for u in range(n_full)] + tail_block)
    else:
        assert n_full % ATTN_UNROLL == 0
        def body(j, carry):
            run([(pl.multiple_of((j * ATTN_UNROLL + u) * tk, tk), tk) for u in range(ATTN_UNROLL)])
            return carry
        lax.fori_loop(0, n_full // ATTN_UNROLL, body, 0)
        run(tail_block)

    a0 = acc_scr[:, :tq]
    a1 = acc_scr[:, tq:]
    o_t = a0[:LANES] / a0[LANES:LANES + 1] - lam_ref[0] * (a1[:LANES] / a1[LANES:LANES + 1])
    o = jnp.transpose(o_t)
    y = o * lax.rsqrt(jnp.mean(o * o, axis=-1, keepdims=True) + LN_EPS) * sg_ref[...] * (1.0 - lam_init)
    y_ref[0] = (y * _silu(g_ref[0].astype(F32))).astype(y_ref.dtype)


def _diff_attention(p_qk, p_vt, p_g, lam, subln_g, lam_init):
    b, lt, di = p_g.shape
    nh = di // LANES
    t = lt - TOK_BLOCK
    tk = 512
    tq = next(r for r in (2048, 1024, 512) if t % r == 0)
    ctx_blk = t // TOK_BLOCK
    sm = pl.BlockSpec(memory_space=pltpu.SMEM)
    sg_spec = pl.BlockSpec((1, LANES), lambda bb, h, i: (0, 0))
    scratch = lambda n: [pltpu.VMEM((1, 2 * n), F32), pltpu.VMEM((LANES + 16, 2 * n), F32)]
    args = (lam.reshape(1), p_qk, p_qk, p_vt, p_g, subln_g.reshape(1, LANES))
    y = pl.pallas_call(
        functools.partial(_attn_kernel, lam_init, tk, t // tk, TOK_BLOCK),
        grid=(b, nh, t // tq),
        in_specs=[sm,
                  pl.BlockSpec((1, tq, LANES), lambda bb, h, i: (bb, i, h)),
                  pl.BlockSpec((1, lt, LANES), lambda bb, h, i: (bb, 0, nh + h)),
                  pl.BlockSpec((1, LANES, lt), lambda bb, h, i: (bb, h, 0)),
                  pl.BlockSpec((1, tq, LANES), lambda bb, h, i: (bb, i, h)),
                  sg_spec],
        out_specs=pl.BlockSpec((1, tq, LANES), lambda bb, h, i: (bb, i, h)),
        out_shape=jax.ShapeDtypeStruct((b, lt, di), BF16),
        scratch_shapes=scratch(tq),
        compiler_params=_cparams("arbitrary", "arbitrary", "arbitrary"),
        name="diff_attn",
    )(*args)
    blk = lambda bb, h, i: (bb, ctx_blk, h)
    return pl.pallas_call(
        functools.partial(_attn_kernel, lam_init, tk, 0, TOK_BLOCK),
        grid=(b, nh, 1),
        in_specs=[sm,
                  pl.BlockSpec((1, TOK_BLOCK, LANES), blk),
                  pl.BlockSpec((1, TOK_BLOCK, LANES), lambda bb, h, i: (bb, ctx_blk, nh + h)),
                  pl.BlockSpec((1, LANES, TOK_BLOCK), lambda bb, h, i: (bb, h, ctx_blk)),
                  pl.BlockSpec((1, TOK_BLOCK, LANES), blk),
                  sg_spec,
                  pl.BlockSpec(memory_space=pl.ANY)],
        out_specs=pl.BlockSpec((1, TOK_BLOCK, LANES), blk),
        out_shape=jax.ShapeDtypeStruct((b, lt, di), BF16),
        scratch_shapes=scratch(TOK_BLOCK),
        input_output_aliases={6: 0},
        compiler_params=_cparams("arbitrary", "arbitrary", "arbitrary"),
        name="diff_attn_ctx",
    )(*args, y)


def _rope_tables(n_ctx, t):
    quarter = DIFF_HEAD // 4
    inv = ROPE_BASE ** (-jnp.arange(quarter, dtype=F32) / quarter)
    pos = jnp.arange(t)
    rows = (pos // GRID_W).astype(F32)[:, None] * inv
    cols = (pos % GRID_W).astype(F32)[:, None] * inv
    cos64 = jnp.concatenate([jnp.cos(rows), jnp.cos(rows), jnp.cos(cols), jnp.cos(cols)], -1)
    sin64 = jnp.concatenate([-jnp.sin(rows), jnp.sin(rows), -jnp.sin(cols), jnp.sin(cols)], -1)
    cos = jnp.concatenate([jnp.tile(cos64, (1, 2)), jnp.ones((n_ctx, LANES), F32)], 0)
    sin = jnp.concatenate([jnp.tile(sin64, (1, 2)), jnp.zeros((n_ctx, LANES), F32)], 0)
    return cos, sin


def _even_layer(alpha, xs, scale, shift, gmod, w_in, w_out, mu_prev, mu_next, w0, w2, a0, a2, k_k, k_a, r_k,
                lnx_g, lnx_b, lb, norm_g, ln_g, ln_b):
    w = k_k.shape[-1]
    sw = mu_prev.shape[-1]
    hw = 4 * lb.shape[-1]
    w_in = w_in.astype(BF16)
    p_rwkv = _project(xs, scale, shift, w_in[:, :sw], sw, F32)
    p_hgrn = _project(xs, scale, shift, w_in[:, sw:sw + hw], hw // 2, F32)
    gate = _project(xs, scale, shift, w_in[:, sw + hw:], w_in.shape[1] - sw - hw, BF16)
    v, bonus, ops_f, ops_r = _rwkv_prep(p_rwkv, mu_prev, mu_next, w0, w2, a0, a2, k_k, k_a, r_k)
    ya0, yb0 = _mix_scan(ops_f, v, p_hgrn, lb[0], 0, False)
    ya1, yb1 = _mix_scan(ops_r, v, p_hgrn, lb[1], 1, True)
    return _even_out(alpha, ya0, ya1, bonus, yb0, yb1, gate, xs, gmod, w_out.astype(BF16),
                     lnx_g, lnx_b, norm_g, ln_g, ln_b)


def _odd_layer(alpha, xs, scale, shift, gmod, w_in, w_out, lam_p, subln_g, lam_init, cos, sin, ln_g, ln_b,
               rows_out):
    di = w_out.shape[0]
    w_in = w_in.astype(BF16)
    p_qk = _project(xs, scale, shift, w_in[:, :2 * di], di, BF16, rope=(cos, sin))
    p_vt = _project_t(xs, scale, shift, jnp.transpose(w_in[:, 2 * di:3 * di]), BF16)
    p_g = _project(xs, scale, shift, w_in[:, 3 * di:], di, BF16)
    lam = jnp.exp(jnp.sum(lam_p[0] * lam_p[1])) - jnp.exp(jnp.sum(lam_p[2] * lam_p[3])) + lam_init
    y = _diff_attention(p_qk, p_vt, p_g, lam, subln_g, lam_init)
    return _odd_out(alpha, y, xs, gmod, w_out.astype(BF16), ln_g, ln_b, rows_out)


def kernel(x, c, ctx, c_ctx, ada_w, ada_b, ln_g, ln_b, even_w_in, even_w_out, rwkv_mu_prev, rwkv_mu_next, rwkv_w0, rwkv_w2, rwkv_a0, rwkv_a2, rwkv_k_k, rwkv_k_a, rwkv_r_k, rwkv_lnx_g, rwkv_lnx_b, hgrn_lb_logits, hgrn_norm_g, odd_w_in, odd_w_out, diff_lambda, diff_subln_g):
    b, t, d = x.shape
    n_ctx = ctx.shape[1]
    depth = ada_w.shape[0]
    assert n_ctx == TOK_BLOCK and t % TOK_BLOCK == 0 and b + 1 <= 8
    alpha = (2.0 * depth) ** 0.25

    xs = jnp.concatenate([x, ctx], axis=1)
    cvec = jnp.concatenate([c, c_ctx[None], jnp.zeros((8 - b - 1, d), F32)], axis=0)
    mods = _ada_mods(cvec, ada_w, ada_b)
    lb_all = jax.nn.softmax(hgrn_lb_logits.astype(F32), axis=0)
    lb_all = jnp.cumsum(lb_all, axis=0) - lb_all[0]
    cos, sin = _rope_tables(n_ctx, t)

    for layer in range(depth):
        m = mods[layer]
        per_seg = lambda u: jnp.stack([jnp.broadcast_to(u[b], (b, d)), u[:b]], axis=1)[:, :, None, :]
        shift, scale, gmod = (per_seg(m[:, j * d:(j + 1) * d]) for j in range(3))
        j = layer // 2
        if layer % 2 == 0:
            xs = _even_layer(alpha, xs, scale, shift, gmod, even_w_in[j], even_w_out[j], rwkv_mu_prev[j],
                             rwkv_mu_next[j], rwkv_w0[j], rwkv_w2[j], rwkv_a0[j], rwkv_a2[j], rwkv_k_k[j],
                             rwkv_k_a[j], rwkv_r_k[j], rwkv_lnx_g[j], rwkv_lnx_b[j], lb_all[j], hgrn_norm_g[j],
                             ln_g[layer], ln_b[layer])
        else:
            lam_init = 0.8 - 0.6 * float(np.exp(-0.3 * layer))
            xs = _odd_layer(alpha, xs, scale, shift, gmod, odd_w_in[j], odd_w_out[j], diff_lambda[j],
                            diff_subln_g[j], lam_init, cos, sin, ln_g[layer], ln_b[layer],
                            t if layer == depth - 1 else t + n_ctx)
    return xs[:, :t]
```

```python
import functools

import numpy as np
import jax
import jax.numpy as jnp
from jax import lax
from jax.experimental import pallas as pl
from jax.experimental.pallas import tpu as pltpu

F32 = jnp.float32
BF16 = jnp.bfloat16
HI = lax.Precision.HIGHEST

GRID_W = 64
RWKV_HEAD = 64
RWKV_LORA = 64
RWKV_GN_EPS = 64e-5
HGRN_HEAD = 128
DIFF_HEAD = 64
ROPE_BASE = 10000.0
LN_EPS = 1e-5
LOG2E = float(np.log2(np.e))

LANES = 128
MXU_DIM = 256
VMEM_LIMIT = 56 * 1024 * 1024

CHUNK = 64
SUB = 16
TOK_BLOCK = 256
PROJ_ROWS = 768
GROUP_HEADS = MXU_DIM // RWKV_HEAD
HGRN_PAR = 4
ATTN_UNROLL = 8
ATTN_TILE = 2048


def _cparams(*sem):
    return pltpu.CompilerParams(dimension_semantics=sem, vmem_limit_bytes=VMEM_LIMIT)


def _nt(a, b):
    return lax.dot_general(a, b, (((1,), (1,)), ((), ())), preferred_element_type=F32)


def _tn(a, b):
    return lax.dot_general(a, b, (((0,), (0,)), ((), ())), preferred_element_type=F32)


def _mm(a, b):
    return jnp.dot(a, b, preferred_element_type=F32)


def _mm_exact(a, b):
    return jnp.dot(a, b, preferred_element_type=F32, precision=HI)


def _sigmoid(x):
    return 1.0 / (1.0 + jnp.exp(-x))


def _silu(x):
    return x * _sigmoid(x)


def _lane_tiles(x):
    return [x[:, j * LANES:(j + 1) * LANES] for j in range(x.shape[-1] // LANES)]


def _group64_sum(x):
    lane = lax.broadcasted_iota(jnp.int32, (1, LANES), 1)
    low = lane < RWKV_HEAD
    out = []
    for xt in _lane_tiles(x):
        s_all = jnp.sum(xt, axis=-1, keepdims=True)
        s_lo = jnp.sum(jnp.where(low, xt, 0.0), axis=-1, keepdims=True)
        out.append(jnp.where(low, s_lo, s_all - s_lo))
    return jnp.concatenate(out, axis=-1)


def _group128_mean(x):
    out = []
    for xt in _lane_tiles(x):
        out.append(jnp.broadcast_to(jnp.mean(xt, axis=-1, keepdims=True), xt.shape))
    return jnp.concatenate(out, axis=-1)


def _ada_kernel(c_ref, w_ref, b_ref, o_ref):
    cond = _silu(c_ref[...])
    o_ref[0] = _mm_exact(cond, w_ref[0]) + b_ref[0]


def _ada_mods(cvec, ada_w, ada_b):
    depth, d, d3 = ada_w.shape
    tn = 1024
    return pl.pallas_call(
        _ada_kernel,
        grid=(depth, d3 // tn),
        in_specs=[
            pl.BlockSpec((8, d), lambda l, j: (0, 0)),
            pl.BlockSpec((1, d, tn), lambda l, j: (l, 0, j)),
            pl.BlockSpec((1, 1, tn), lambda l, j: (l, 0, j)),
        ],
        out_specs=pl.BlockSpec((1, 8, tn), lambda l, j: (l, 0, j)),
        out_shape=jax.ShapeDtypeStruct((depth, 8, d3), F32),
        compiler_params=_cparams("arbitrary", "arbitrary"),
        name="ada_mods",
    )(cvec, ada_w, ada_b.reshape(depth, 1, d3))


def _proj_rows(lt):
    return PROJ_ROWS if lt % PROJ_ROWS == 0 else TOK_BLOCK


def _modulate(n_lat, axis, x_ref, sc_ref, sh_ref):
    tm = x_ref.shape[1]
    rows = pl.program_id(axis) * tm + lax.broadcasted_iota(jnp.int32, (tm, 1), 0)
    is_ctx = rows >= n_lat
    sc = jnp.where(is_ctx, sc_ref[0, 0], sc_ref[0, 1])
    sh = jnp.where(is_ctx, sh_ref[0, 0], sh_ref[0, 1])
    return (x_ref[0] * (1.0 + sc) + sh).astype(BF16)


def _proj_kernel(n_lat, x_ref, sc_ref, sh_ref, w_ref, o_ref):
    o_ref[0] = _mm(_modulate(n_lat, 2, x_ref, sc_ref, sh_ref), w_ref[...]).astype(o_ref.dtype)


def _proj_rope_kernel(n_lat, x_ref, sc_ref, sh_ref, w_ref, cos_ref, sin_ref, o_ref):
    p = _mm(_modulate(n_lat, 2, x_ref, sc_ref, sh_ref), w_ref[...])
    reps = p.shape[-1] // LANES
    cos = jnp.concatenate([cos_ref[...]] * reps, axis=-1)
    sin = jnp.concatenate([sin_ref[...]] * reps, axis=-1)
    o_ref[0] = _rope(p, cos, sin).astype(o_ref.dtype)


def _proj_t_kernel(n_lat, x_ref, sc_ref, sh_ref, wt_ref, o_ref):
    o_ref[0] = _nt(wt_ref[...], _modulate(n_lat, 1, x_ref, sc_ref, sh_ref)).astype(o_ref.dtype)


def _project(xs, scale, shift, w, tn, out_dtype, rope=None):
    b, lt, d = xs.shape
    n = w.shape[1]
    tm = _proj_rows(lt)
    n_lat = lt - TOK_BLOCK
    in_specs = [
        pl.BlockSpec((1, tm, d), lambda j, bb, i: (bb, i, 0)),
        pl.BlockSpec((1, 2, 1, d), lambda j, bb, i: (bb, 0, 0, 0)),
        pl.BlockSpec((1, 2, 1, d), lambda j, bb, i: (bb, 0, 0, 0)),
        pl.BlockSpec((d, tn), lambda j, bb, i: (0, j)),
    ]
    args = (xs, scale, shift, w)
    if rope is not None:
        in_specs += [pl.BlockSpec((tm, LANES), lambda j, bb, i: (i, 0))] * 2
        args += tuple(rope)
    return pl.pallas_call(
        functools.partial(_proj_kernel if rope is None else _proj_rope_kernel, n_lat),
        grid=(n // tn, b, lt // tm),
        in_specs=in_specs,
        out_specs=pl.BlockSpec((1, tm, tn), lambda j, bb, i: (bb, i, j)),
        out_shape=jax.ShapeDtypeStruct((b, lt, n), out_dtype),
        compiler_params=_cparams("arbitrary", "arbitrary", "arbitrary"),
        name="mod_proj" if rope is None else "mod_proj_rope",
    )(*args)


def _project_t(xs, scale, shift, wt, out_dtype):
    b, lt, d = xs.shape
    n = wt.shape[0]
    tm = _proj_rows(lt)
    return pl.pallas_call(
        functools.partial(_proj_t_kernel, lt - TOK_BLOCK),
        grid=(b, lt // tm),
        in_specs=[
            pl.BlockSpec((1, tm, d), lambda bb, i: (bb, i, 0)),
            pl.BlockSpec((1, 2, 1, d), lambda bb, i: (bb, 0, 0, 0)),
            pl.BlockSpec((1, 2, 1, d), lambda bb, i: (bb, 0, 0, 0)),
            pl.BlockSpec((n, d), lambda bb, i: (0, 0)),
        ],
        out_specs=pl.BlockSpec((1, n, tm), lambda bb, i: (bb, 0, i)),
        out_shape=jax.ShapeDtypeStruct((b, n, lt), out_dtype),
        compiler_params=_cparams("arbitrary", "arbitrary"),
        name="mod_proj_t",
    )(xs, scale, shift, wt)


def _lora(x, w_ref):
    x1 = x.astype(BF16)
    x2 = (x - x1.astype(F32)).astype(BF16)
    return _mm(jnp.concatenate([x1, x1, x2], axis=-1), w_ref[...])


def _rwkv_prep_kernel(nblk, p_ref, prev_ref, next_ref, mup_ref, mun_ref, w0_ref, w2_ref, a0_ref, a2_ref,
                      kk_ref, ka_ref, rk_ref, cm_ref,
                      v_out, bonus_out, *dir_outs):
    i = pl.program_id(1)
    w = kk_ref.shape[-1]
    p = p_ref[0]
    row = lax.broadcasted_iota(jnp.int32, (TOK_BLOCK, 1), 0)
    prev_row = jnp.where(jnp.logical_and(i >= 1, i <= nblk - 2), prev_ref[0][7:8, :], 0.0)
    next_row = jnp.where(i <= nblk - 3, next_ref[0][0:1, :], 0.0)
    prev = jnp.where(row == 0, prev_row, pltpu.roll(p, 1, 0))
    nxt = jnp.where(row == TOK_BLOCK - 1, next_row, pltpu.roll(p, TOK_BLOCK - 1, 0))
    ps = p * (1.0 - mup_ref[...] - mun_ref[...]) + mup_ref[...] * prev + mun_ref[...] * nxt

    r = ps[:, 0:w]
    k = ps[:, w:2 * w]
    v = ps[:, 2 * w:3 * w]
    wlo = ps[:, 3 * w:3 * w + 2 * RWKV_LORA]
    alo = ps[:, 3 * w + 2 * RWKV_LORA:3 * w + 4 * RWKV_LORA]

    z = _lora(jnp.tanh(wlo), w2_ref) + w0_ref[...]
    lw = _sigmoid(z) * (-LOG2E * float(np.exp(-0.5)))
    a = _sigmoid(_lora(alo, a2_ref) + a0_ref[...])

    kk = k * kk_ref[...]
    kk = kk * lax.rsqrt(_group64_sum(kk * kk) + 1e-12)
    bonus_out[0] = (_group64_sum(r * k * rk_ref[...]) * v).astype(bonus_out.dtype)
    v_out[0] = v.astype(BF16)

    for d in range(2):
        rh_o, kh_o, bh_o, kap_o, kg_o, bg_o, gl_o = dir_outs[7 * d:7 * d + 7]
        lw_d = lw[:, d * w:(d + 1) * w]
        a_d = a[:, d * w:(d + 1) * w]
        kd = k * (1.0 + (a_d - 1.0) * ka_ref[...])
        b_d = kk * a_d
        sums = _mm(cm_ref[d], _split3(lw_d))
        g = sums[:TOK_BLOCK]
        g_all = sums[TOK_BLOCK:]
        e_neg = jnp.exp2(-g)
        e_rest = jnp.exp2(g_all - g)
        rh_o[0] = (r * jnp.exp2(g)).astype(BF16)
        kh_o[0] = (kd * e_neg).astype(BF16)
        bh_o[0] = (b_d * e_neg).astype(BF16)
        kap_o[0] = (kk * jnp.exp2(g - lw_d)).astype(BF16)
        kg_o[0] = (kd * e_rest).astype(BF16)
        bg_o[0] = (b_d * e_rest).astype(BF16)
        e_all = jnp.exp2(g_all)
        for c in range(TOK_BLOCK // CHUNK):
            gl_o[0, c] = e_all[c * CHUNK:c * CHUNK + 1, :]


def _rwkv_sum_consts(reverse):
    t = np.arange(TOK_BLOCK)[:, None]
    s = np.arange(TOK_BLOCK)[None, :]
    same = (t // CHUNK) == (s // CHUNK)
    order = (s >= t) if reverse else (s <= t)
    both = np.concatenate([same & order, same], axis=0).astype(np.float32)
    return np.concatenate([both] * 3, axis=1)


def _lora_pieces(m):
    hi = m.astype(BF16)
    lo = (m - hi.astype(F32)).astype(BF16)
    return jnp.concatenate([hi, lo, hi], axis=0)


def _rwkv_prep(p_rwkv, mu_prev, mu_next, w0, w2, a0, a2, k_k, k_a, r_k):
    b, lt, sw = p_rwkv.shape
    w = k_k.shape[-1]
    nblk = lt // TOK_BLOCK
    nch = lt // CHUNK
    halo = TOK_BLOCK // 8
    zl = jnp.zeros((RWKV_LORA, w), F32)
    w2cat = _lora_pieces(jnp.concatenate([jnp.concatenate([w2[0], zl], 1), jnp.concatenate([zl, w2[1]], 1)], 0))
    a2cat = _lora_pieces(jnp.concatenate([jnp.concatenate([a2[0], zl], 1), jnp.concatenate([zl, a2[1]], 1)], 0))
    cm = jnp.asarray(np.stack([_rwkv_sum_consts(False), _rwkv_sum_consts(True)]), dtype=BF16)
    row = lambda u: u.reshape(1, -1)
    full = lambda shape: pl.BlockSpec(shape, lambda bb, i: (0,) * len(shape))
    tok = lambda n, dt: (pl.BlockSpec((1, TOK_BLOCK, n), lambda bb, i: (bb, i, 0)), jax.ShapeDtypeStruct((b, lt, n), dt))
    gl = (pl.BlockSpec((1, TOK_BLOCK // CHUNK, 1, w), lambda bb, i: (bb, i, 0, 0)),
          jax.ShapeDtypeStruct((b, nch, 1, w), F32))
    outs = [tok(w, BF16), tok(w, BF16)] + 2 * ([tok(w, BF16)] * 6 + [gl])
    res = pl.pallas_call(
        functools.partial(_rwkv_prep_kernel, nblk),
        grid=(b, nblk),
        in_specs=[
            pl.BlockSpec((1, TOK_BLOCK, sw), lambda bb, i: (bb, i, 0)),
            pl.BlockSpec((1, 8, sw), lambda bb, i: (bb, jnp.maximum(i * halo - 1, 0), 0)),
            pl.BlockSpec((1, 8, sw), lambda bb, i: (bb, jnp.minimum((i + 1) * halo, lt // 8 - 1), 0)),
            full((1, sw)), full((1, sw)), full((1, 2 * w)), full(w2cat.shape),
            full((1, 2 * w)), full(a2cat.shape), full((1, w)), full((1, w)), full((1, w)),
            full(cm.shape),
        ],
        out_specs=[o[0] for o in outs],
        out_shape=[o[1] for o in outs],
        compiler_params=_cparams("arbitrary", "arbitrary"),
        name="rwkv_prep",
    )(p_rwkv, p_rwkv, p_rwkv, row(mu_prev), row(mu_next), row(w0), w2cat, row(a0), a2cat,
      row(k_k), row(k_a), row(r_k), cm)
    v, bonus = res[0], res[1]
    return v, bonus, res[2:9], res[9:16]


def _rwkv_masks(reverse):
    n = GROUP_HEADS * CHUNK
    i = np.arange(n)[:, None]
    j = np.arange(n)[None, :]
    same_head = (i // CHUNK) == (j // CHUNK)
    before = (j > i) if reverse else (j < i)
    strict = same_head & before
    incl = same_head & (before | (i == j))
    levels = []
    size = 1
    while size < CHUNK:
        levels.append(same_head & ((i // (2 * size)) == (j // (2 * size))) & ((i // size) != (j // size)))
        size *= 2
    head_lane = (i // CHUNK) == (j // RWKV_HEAD)
    tri = np.stack([strict, incl]).astype(np.float32)
    return tri, np.stack(levels).astype(np.float32), head_lane.astype(np.float32)


def _scan_chunk_index(c, nch, reverse):
    if reverse:
        return nch - 1 - c
    n_ctx = TOK_BLOCK // CHUNK
    return jnp.where(c < n_ctx, nch - n_ctx + c, c - n_ctx)


def _split3(x):
    x1 = x.astype(BF16)
    r1 = x - x1.astype(F32)
    x2 = r1.astype(BF16)
    x3 = (r1 - x2.astype(F32)).astype(BF16)
    return jnp.concatenate([x1, x2, x3], axis=0)


def _hgrn_chunk_consts(reverse):
    t = np.arange(CHUNK)[:, None]
    s = np.arange(CHUNK)[None, :]
    order = (s >= t) if reverse else (s <= t)
    earlier_sub = ((s // SUB) > (t // SUB)) if reverse else ((s // SUB) < (t // SUB))
    both = np.concatenate([order, earlier_sub], axis=0).astype(np.float32)
    return np.concatenate([both] * 3, axis=1)


def _rwkv_chunk_stages(rh_ref, kh_ref, bh_ref, kap_ref, kg_ref, bg_ref, v_ref, gl_ref, tri_ref, lvl_ref, hm_ref,
                       y_ref, s_ref):
    hm = hm_ref[...]
    strict = tri_ref[0]
    incl = tri_ref[1]
    eye = incl - strict
    n_levels = lvl_ref.shape[0]
    groups = range(y_ref.shape[-1] // MXU_DIM)
    lanes = lambda ref, g: ref[0, :, g * MXU_DIM:(g + 1) * MXU_DIM]
    stack = lambda ref, g: jnp.concatenate([lanes(ref, g)] * GROUP_HEADS, axis=0) * hm

    kap, bh, a_ab = [], [], []
    for g in groups:
        kap.append(stack(kap_ref, g))
        bh.append(stack(bh_ref, g))
        a_ab.append(_nt(kap[g], bh[g]) * strict)
    yield
    kh = [stack(kh_ref, g) for g in groups]
    a_ak = [(_nt(kap[g], kh[g]) * strict).astype(BF16) for g in groups]
    rh = [stack(rh_ref, g) for g in groups]
    p_rb = [(_nt(rh[g], bh[g]) * incl).astype(BF16) for g in groups]
    p_rk = [(_nt(rh[g], kh[g]) * incl).astype(BF16) for g in groups]
    vs = [stack(v_ref, g) for g in groups]
    t = [eye - a_ab[g] * lvl_ref[0] for g in groups]
    a_abb = [a_ab[g].astype(BF16) for g in groups]
    for lv in range(1, n_levels):
        yield
        tb = [t[g].astype(BF16) for g in groups]
        x = [_mm(tb[g], a_abb[g]).astype(BF16) for g in groups]
        yield
        t = [t[g] - _mm(x[g], tb[g]) * lvl_ref[lv] for g in groups]
    yield
    s0 = [s_ref[g] for g in groups]
    s0b = [s0[g].astype(BF16) for g in groups]
    hm32 = hm.astype(F32)
    ks0 = [jnp.concatenate([_nt(lanes(kap_ref, g), s0b[g])] * GROUP_HEADS, axis=0) * hm32 for g in groups]
    wmat = [(ks0[g] + _mm(a_ak[g], vs[g])).astype(BF16) for g in groups]
    yield
    ub = [(-_mm(t[g].astype(BF16), wmat[g])).astype(BF16) for g in groups]
    yield
    ys = [_mm(p_rb[g], ub[g]) + _mm(p_rk[g], vs[g]) for g in groups]
    for g in groups:
        y = _nt(lanes(rh_ref, g), s0b[g])
        for h in range(GROUP_HEADS):
            y = y + ys[g][h * CHUNK:(h + 1) * CHUNK]
        y_ref[0, :, g * MXU_DIM:(g + 1) * MXU_DIM] = y.astype(y_ref.dtype)
    yield
    for g in groups:
        gam = gl_ref[0, 0][:, g * MXU_DIM:(g + 1) * MXU_DIM]
        s_ref[g] = s0[g] * gam + _tn(ub[g], stack(bg_ref, g)) + _tn(vs[g], stack(kg_ref, g))


def _hgrn_chunk_stages(reverse, q_ref, f_ref, i_ref, lb_ref, cm_ref, o_ref, s_ref, g_scr, q_scr):
    n_sub = CHUNK // SUB
    n_heads = q_ref.shape[-1] // HGRN_HEAD
    sub_order = list(range(n_sub))[::-1] if reverse else list(range(n_sub))
    row_sub = lax.broadcasted_iota(jnp.int32, (CHUNK, 1), 0) // SUB
    srow = lax.broadcasted_iota(jnp.int32, (SUB, 1), 0)
    lane = lax.broadcasted_iota(jnp.int32, (1, LANES), 1)
    last_row = 0 if reverse else CHUNK - 1

    kgate = (1.0 - lb_ref[...]) * _sigmoid(-f_ref[0])
    sums = _mm(cm_ref[...], _split3(jnp.log1p(-kgate) * LOG2E))
    g_all = sums[:CHUNK]
    gb_all = sums[CHUNK:]
    q_all = _silu(q_ref[0])
    g_scr[...] = g_all
    q_scr[...] = q_all
    yield
    for hg in range(n_heads // HGRN_PAR):
        par = range(HGRN_PAR)
        hs = [hg * HGRN_PAR + j for j in par]
        cols = [slice(h * HGRN_HEAD, (h + 1) * HGRN_HEAD) for h in hs]
        g = [g_all[:, cs] for cs in cols]
        gb = [gb_all[:, cs] for cs in cols]
        q = [q_all[:, cs] for cs in cols]
        k = [kgate[:, cs] for cs in cols]
        vb = [i_ref[0, :, cs].astype(BF16) for cs in cols]
        s0 = [s_ref[h] for h in hs]
        q_in = [q[j] * jnp.exp2(g[j] - gb[j]) for j in par]
        at = []
        for j in par:
            kts, qts = [], []
            for si in sub_order[1:]:
                gb_i = gb[j][si * SUB:si * SUB + 1, :]
                earlier = (row_sub > si) if reverse else (row_sub < si)
                kts.append(jnp.where(earlier, k[j] * jnp.exp2(jnp.minimum(gb_i - g[j], 0.0)), 0.0).astype(BF16))
                qts.append(jnp.where(row_sub == si, q_in[j], 0.0).astype(BF16))
            at.append(_nt(jnp.concatenate(kts, axis=-1), jnp.concatenate(qts, axis=-1)))
        yield
        diag = [[] for _ in par]
        for si in range(n_sub):
            blk = slice(si * SUB, (si + 1) * SUB)
            acc = [jnp.zeros((SUB, LANES), F32) for _ in par]
            for tt in range(SUB):
                t = si * SUB + tt
                valid = jnp.logical_and(lane == t, (srow >= tt) if reverse else (srow <= tt))
                for j in par:
                    e = jnp.exp2(jnp.minimum(g_scr[t:t + 1, cols[j]] - g[j][blk], 0.0))
                    col = jnp.sum(k[j][blk] * e * q_scr[t:t + 1, cols[j]], axis=-1, keepdims=True)
                    acc[j] = jnp.where(valid, col, acc[j])
                if tt % 8 == 7:
                    yield
            for j in par:
                diag[j].append(acc[j])
        for j in par:
            a = (at[j] + jnp.concatenate(diag[j], axis=0)[:, :CHUNK]).astype(BF16)
            o = _tn(a, vb[j]) + _nt((q[j] * jnp.exp2(g[j])).astype(BF16), s0[j].astype(BF16))
            o_ref[0, :, cols[j]] = o.astype(o_ref.dtype)
            g_last = g[j][last_row:last_row + 1, :]
            k_out = (k[j] * jnp.exp2(g_last - g[j])).astype(BF16)
            s_ref[hs[j]] = s0[j] * jnp.exp2(g_last) + _tn(vb[j], k_out)
        yield


def _mix_scan_kernel(reverse, *refs):
    rwkv_in, hgrn_in = refs[:11], refs[11:16]
    y_ref, o_ref, s_ref, hs_ref, g_scr, q_scr = refs[16:]

    @pl.when(pl.program_id(1) == 0)
    def _():
        s_ref[...] = jnp.zeros_like(s_ref)
        hs_ref[...] = jnp.zeros_like(hs_ref)

    streams = [_rwkv_chunk_stages(*rwkv_in, y_ref, s_ref),
               _hgrn_chunk_stages(reverse, *hgrn_in, o_ref, hs_ref, g_scr, q_scr)]
    done = object()
    while streams:
        streams = [st for st in streams if next(st, done) is not done]


def _mix_scan(dir_ops, v, p_hgrn, lb_d, d, reverse):
    rh, kh, bh, kap, kg, bg, gl = dir_ops
    b, lt, w = v.shape
    nch = lt // CHUNK
    n = GROUP_HEADS * CHUNK
    tri, lvl, hm = _rwkv_masks(reverse)
    cm = jnp.asarray(_hgrn_chunk_consts(reverse), dtype=BF16)
    cidx = lambda c: _scan_chunk_index(c, nch, reverse)
    tok = pl.BlockSpec((1, CHUNK, w), lambda bb, c: (bb, cidx(c), 0))
    col = lambda j: pl.BlockSpec((1, CHUNK, w), lambda bb, c: (bb, cidx(c), j))
    full = lambda shape: pl.BlockSpec(shape, lambda bb, c: (0,) * len(shape))
    return pl.pallas_call(
        functools.partial(_mix_scan_kernel, reverse),
        grid=(b, nch),
        in_specs=[tok] * 7 + [
            pl.BlockSpec((1, 1, 1, w), lambda bb, c: (bb, cidx(c), 0, 0)),
            full(tri.shape), full(lvl.shape), full(hm.shape),
            col(0), col(1 + d), col(3), full((1, w)), full(cm.shape),
        ],
        out_specs=[tok, tok],
        out_shape=[jax.ShapeDtypeStruct((b, lt, w), BF16)] * 2,
        scratch_shapes=[pltpu.VMEM((w // MXU_DIM, n, n), F32),
                        pltpu.VMEM((w // HGRN_HEAD, HGRN_HEAD, HGRN_HEAD), F32),
                        pltpu.VMEM((CHUNK, w), F32), pltpu.VMEM((CHUNK, w), F32)],
        compiler_params=_cparams("arbitrary", "arbitrary"),
        name="mix_scan_rev" if reverse else "mix_scan_fwd",
    )(rh, kh, bh, kap, kg, bg, v, gl, jnp.asarray(tri), jnp.asarray(lvl), jnp.asarray(hm, dtype=BF16),
      p_hgrn, p_hgrn, p_hgrn, lb_d.reshape(1, w), cm)


def _residual_ln(alpha, n_lat, x, gm_ref, proj, lng, lnb):
    tm = x.shape[0]
    rows = pl.program_id(1) * tm + lax.broadcasted_iota(jnp.int32, (tm, 1), 0)
    gm = jnp.where(rows >= n_lat, gm_ref[0, 0], gm_ref[0, 1])
    z = alpha * x + gm * proj
    mu = jnp.mean(z, axis=-1, keepdims=True)
    zc = z - mu
    var = jnp.mean(zc * zc, axis=-1, keepdims=True)
    return zc * lax.rsqrt(var + LN_EPS) * lng + lnb


def _even_out_kernel(alpha, n_lat, ya0_ref, ya1_ref, bonus_ref, yb0_ref, yb1_ref, gate_ref, x_ref, gm_ref, w_ref,
                     lnxg_ref, lnxb_ref, ng_ref, lng_ref, lnb_ref, o_ref):
    y = ya0_ref[0].astype(F32) + ya1_ref[0].astype(F32)
    mu = _group64_sum(y) * (1.0 / RWKV_HEAD)
    yc = y - mu
    var = _group64_sum(yc * yc) * (1.0 / RWKV_HEAD)
    ya = yc * lax.rsqrt(var + RWKV_GN_EPS) * lnxg_ref[...] + lnxb_ref[...] + bonus_ref[0].astype(F32)
    o = yb0_ref[0].astype(F32) + yb1_ref[0].astype(F32)
    yb = o * lax.rsqrt(_group128_mean(o * o) + LN_EPS) * ng_ref[...]
    ycat = jnp.concatenate([ya, yb], axis=-1) * _silu(gate_ref[0].astype(F32))
    proj = _mm(ycat.astype(BF16), w_ref[...])
    o_ref[0] = _residual_ln(alpha, n_lat, x_ref[0], gm_ref, proj, lng_ref[...], lnb_ref[...])


def _odd_out_kernel(alpha, n_lat, y_ref, x_ref, gm_ref, w_ref, lng_ref, lnb_ref, o_ref):
    proj = _mm(y_ref[0].astype(BF16), w_ref[...])
    o_ref[0] = _residual_ln(alpha, n_lat, x_ref[0], gm_ref, proj, lng_ref[...], lnb_ref[...])


def _tok_spec(n, tm=TOK_BLOCK):
    return pl.BlockSpec((1, tm, n), lambda bb, i: (bb, i, 0))


def _full2(shape):
    return pl.BlockSpec(shape, lambda bb, i: (0,) * len(shape))


def _gate_spec(d):
    return pl.BlockSpec((1, 2, 1, d), lambda bb, i: (bb, 0, 0, 0))


def _even_out(alpha, ya0, ya1, bonus, yb0, yb1, gate, xs, gmod, w_out, lnx_g, lnx_b, norm_g, ln_g, ln_b):
    b, lt, d = xs.shape
    w = ya0.shape[-1]
    di = gate.shape[-1]
    row = lambda u: u.reshape(1, -1)
    tm = _proj_rows(lt)
    return pl.pallas_call(
        functools.partial(_even_out_kernel, alpha, lt - TOK_BLOCK),
        grid=(b, lt // tm),
        in_specs=[_tok_spec(w, tm)] * 5 + [_tok_spec(di, tm), _tok_spec(d, tm), _gate_spec(d), _full2((di, d)),
                                           _full2((1, w)), _full2((1, w)), _full2((1, w)), _full2((1, d)),
                                           _full2((1, d))],
        out_specs=_tok_spec(d, tm),
        out_shape=jax.ShapeDtypeStruct((b, lt, d), F32),
        compiler_params=_cparams("arbitrary", "arbitrary"),
        name="even_out",
    )(ya0, ya1, bonus, yb0, yb1, gate, xs, gmod, w_out, row(lnx_g), row(lnx_b),
      row(jnp.tile(norm_g, w // HGRN_HEAD)), row(ln_g), row(ln_b))


def _odd_out(alpha, y, xs, gmod, w_out, ln_g, ln_b, rows_out):
    b, lt, d = xs.shape
    di = y.shape[-1]
    row = lambda u: u.reshape(1, -1)
    tm = next(r for r in (PROJ_ROWS, 2 * TOK_BLOCK, TOK_BLOCK) if rows_out % r == 0)
    return pl.pallas_call(
        functools.partial(_odd_out_kernel, alpha, lt - TOK_BLOCK),
        grid=(b, rows_out // tm),
        in_specs=[_tok_spec(di, tm), _tok_spec(d, tm), _gate_spec(d), _full2((di, d)), _full2((1, d)),
                  _full2((1, d))],
        out_specs=_tok_spec(d, tm),
        out_shape=jax.ShapeDtypeStruct((b, rows_out, d), F32),
        compiler_params=_cparams("arbitrary", "arbitrary"),
        name="odd_out",
    )(y, xs, gmod, w_out, row(ln_g), row(ln_b))


def _rope(x, cos, sin):
    width = x.shape[-1]
    lane = lax.broadcasted_iota(jnp.int32, (1, width), 1)
    first = (lane % 32) < 16
    partner = jnp.where(first, pltpu.roll(x, width - 16, 1), pltpu.roll(x, 16, 1))
    return x * cos + partner * sin


def _attn_kernel(lam_init, tk, n_full, tail, lam_ref, q_ref, k_ref, vt_ref, g_ref, sg_ref, *rest):
    y_ref, m_scr, acc_scr = rest[-3:]
    tq = q_ref.shape[1]
    ones_rows = 16
    tile = min(ATTN_TILE, 2 * tq)
    n_tiles = 2 * tq // tile
    q = q_ref[0].astype(F32) * (DIFF_HEAD ** -0.5 * float(np.log2(np.e)))
    lane = lax.broadcasted_iota(jnp.int32, (1, LANES), 1)
    q_cat = jnp.concatenate([jnp.where(lane < DIFF_HEAD, q, 0.0), jnp.where(lane >= DIFF_HEAD, q, 0.0)],
                            axis=0).astype(BF16)
    m_scr[...] = jnp.full(m_scr.shape, -jnp.inf, F32)
    acc_scr[...] = jnp.zeros(acc_scr.shape, F32)

    def scores(item):
        start, size, c = item
        return _nt(k_ref[0, pl.ds(start, size), :], q_cat[c * tile:(c + 1) * tile])

    def run(blocks):
        cols = [slice(c * tile, (c + 1) * tile) for c in range(n_tiles)]
        m_run = [m_scr[:, cs] for cs in cols]
        acc = [acc_scr[:, cs] for cs in cols]
        vta = None
        pending = None
        items = [(start, size, c) for start, size in blocks for c in range(n_tiles)]
        st_next = scores(items[0])
        for i, (start, size, c) in enumerate(items):
            st = st_next
            if i + 1 < len(items):
                st_next = scores(items[i + 1])
            if pending is not None:
                pc, pcorr, pvta, ppt = pending
                acc[pc] = acc[pc] * pcorr + _mm(pvta, ppt)
            if c == 0:
                vta = jnp.concatenate([vt_ref[0, :, pl.ds(start, size)], jnp.ones((ones_rows, size), BF16)], axis=0)
            m_new = jnp.maximum(m_run[c], jnp.max(st, axis=0, keepdims=True))
            pending = (c, jnp.exp2(m_run[c] - m_new), vta, jnp.exp2(st - m_new).astype(BF16))
            m_run[c] = m_new
        pc, pcorr, pvta, ppt = pending
        acc[pc] = acc[pc] * pcorr + _mm(pvta, ppt)
        for c, cs in enumerate(cols):
            m_scr[:, cs] = m_run[c]
            acc_scr[:, cs] = acc[c]

    tail_block = [(n_full * tk, tail)]
    if n_full <= ATTN_UNROLL:
        run([(u * tk, tk) for u in range(n_full)] + tail_block)
    else:
        assert n_full % ATTN_UNROLL == 0
        def body(j, carry):
            run([(pl.multiple_of((j * ATTN_UNROLL + u) * tk, tk), tk) for u in range(ATTN_UNROLL)])
            return carry
        lax.fori_loop(0, n_full // ATTN_UNROLL, body, 0)
        run(tail_block)

    a0 = acc_scr[:, :tq]
    a1 = acc_scr[:, tq:]
    o_t = a0[:LANES] / a0[LANES:LANES + 1] - lam_ref[0] * (a1[:LANES] / a1[LANES:LANES + 1])
    o = jnp.transpose(o_t)
    y = o * lax.rsqrt(jnp.mean(o * o, axis=-1, keepdims=True) + LN_EPS) * sg_ref[...] * (1.0 - lam_init)
    y_ref[0] = (y * _silu(g_ref[0].astype(F32))).astype(y_ref.dtype)


def _diff_attention(p_qk, p_vt, p_g, lam, subln_g, lam_init):
    b, lt, di = p_g.shape
    nh = di // LANES
    t = lt - TOK_BLOCK
    tk = 512
    tq = next(r for r in (1024, 512) if t % r == 0)
    ctx_blk = t // TOK_BLOCK
    sm = pl.BlockSpec(memory_space=pltpu.SMEM)
    sg_spec = pl.BlockSpec((1, LANES), lambda bb, h, i: (0, 0))
    scratch = lambda n: [pltpu.VMEM((1, 2 * n), F32), pltpu.VMEM((LANES + 16, 2 * n), F32)]
    args = (lam.reshape(1), p_qk, p_qk, p_vt, p_g, subln_g.reshape(1, LANES))
    y = pl.pallas_call(
        functools.partial(_attn_kernel, lam_init, tk, t // tk, TOK_BLOCK),
        grid=(b, nh, t // tq),
        in_specs=[sm,
                  pl.BlockSpec((1, tq, LANES), lambda bb, h, i: (bb, i, h)),
                  pl.BlockSpec((1, lt, LANES), lambda bb, h, i: (bb, 0, nh + h)),
                  pl.BlockSpec((1, LANES, lt), lambda bb, h, i: (bb, h, 0)),
                  pl.BlockSpec((1, tq, LANES), lambda bb, h, i: (bb, i, h)),
                  sg_spec],
        out_specs=pl.BlockSpec((1, tq, LANES), lambda bb, h, i: (bb, i, h)),
        out_shape=jax.ShapeDtypeStruct((b, lt, di), BF16),
        scratch_shapes=scratch(tq),
        compiler_params=_cparams("arbitrary", "arbitrary", "arbitrary"),
        name="diff_attn",
    )(*args)
    blk = lambda bb, h, i: (bb, ctx_blk, h)
    return pl.pallas_call(
        functools.partial(_attn_kernel, lam_init, tk, 0, TOK_BLOCK),
        grid=(b, nh, 1),
        in_specs=[sm,
                  pl.BlockSpec((1, TOK_BLOCK, LANES), blk),
                  pl.BlockSpec((1, TOK_BLOCK, LANES), lambda bb, h, i: (bb, ctx_blk, nh + h)),
                  pl.BlockSpec((1, LANES, TOK_BLOCK), lambda bb, h, i: (bb, h, ctx_blk)),
                  pl.BlockSpec((1, TOK_BLOCK, LANES), blk),
                  sg_spec,
                  pl.BlockSpec(memory_space=pl.ANY)],
        out_specs=pl.BlockSpec((1, TOK_BLOCK, LANES), blk),
        out_shape=jax.ShapeDtypeStruct((b, lt, di), BF16),
        scratch_shapes=scratch(TOK_BLOCK),
        input_output_aliases={6: 0},
        compiler_params=_cparams("arbitrary", "arbitrary", "arbitrary"),
        name="diff_attn_ctx",
    )(*args, y)


def _rope_tables(n_ctx, t):
    quarter = DIFF_HEAD // 4
    inv = ROPE_BASE ** (-jnp.arange(quarter, dtype=F32) / quarter)
    pos = jnp.arange(t)
    rows = (pos // GRID_W).astype(F32)[:, None] * inv
    cols = (pos % GRID_W).astype(F32)[:, None] * inv
    cos64 = jnp.concatenate([jnp.cos(rows), jnp.cos(rows), jnp.cos(cols), jnp.cos(cols)], -1)
    sin64 = jnp.concatenate([-jnp.sin(rows), jnp.sin(rows), -jnp.sin(cols), jnp.sin(cols)], -1)
    cos = jnp.concatenate([jnp.tile(cos64, (1, 2)), jnp.ones((n_ctx, LANES), F32)], 0)
    sin = jnp.concatenate([jnp.tile(sin64, (1, 2)), jnp.zeros((n_ctx, LANES), F32)], 0)
    return cos, sin


def _even_layer(alpha, xs, scale, shift, gmod, w_in, w_out, mu_prev, mu_next, w0, w2, a0, a2, k_k, k_a, r_k,
                lnx_g, lnx_b, lb, norm_g, ln_g, ln_b):
    w = k_k.shape[-1]
    sw = mu_prev.shape[-1]
    hw = 4 * lb.shape[-1]
    w_in = w_in.astype(BF16)
    p_rwkv = _project(xs, scale, shift, w_in[:, :sw], sw, F32)
    p_hgrn = _project(xs, scale, shift, w_in[:, sw:sw + hw], hw // 2, F32)
    gate = _project(xs, scale, shift, w_in[:, sw + hw:], w_in.shape[1] - sw - hw, BF16)
    v, bonus, ops_f, ops_r = _rwkv_prep(p_rwkv, mu_prev, mu_next, w0, w2, a0, a2, k_k, k_a, r_k)
    ya0, yb0 = _mix_scan(ops_f, v, p_hgrn, lb[0], 0, False)
    ya1, yb1 = _mix_scan(ops_r, v, p_hgrn, lb[1], 1, True)
    return _even_out(alpha, ya0, ya1, bonus, yb0, yb1, gate, xs, gmod, w_out.astype(BF16),
                     lnx_g, lnx_b, norm_g, ln_g, ln_b)


def _odd_layer(alpha, xs, scale, shift, gmod, w_in, w_out, lam_p, subln_g, lam_init, cos, sin, ln_g, ln_b,
               rows_out):
    di = w_out.shape[0]
    w_in = w_in.astype(BF16)
    p_qk = _project(xs, scale, shift, w_in[:, :2 * di], di, BF16, rope=(cos, sin))
    p_vt = _project_t(xs, scale, shift, jnp.transpose(w_in[:, 2 * di:3 * di]), BF16)
    p_g = _project(xs, scale, shift, w_in[:, 3 * di:], di, BF16)
    lam = jnp.exp(jnp.sum(lam_p[0] * lam_p[1])) - jnp.exp(jnp.sum(lam_p[2] * lam_p[3])) + lam_init
    y = _diff_attention(p_qk, p_vt, p_g, lam, subln_g, lam_init)
    return _odd_out(alpha, y, xs, gmod, w_out.astype(BF16), ln_g, ln_b, rows_out)


def kernel(x, c, ctx, c_ctx, ada_w, ada_b, ln_g, ln_b, even_w_in, even_w_out, rwkv_mu_prev, rwkv_mu_next, rwkv_w0, rwkv_w2, rwkv_a0, rwkv_a2, rwkv_k_k, rwkv_k_a, rwkv_r_k, rwkv_lnx_g, rwkv_lnx_b, hgrn_lb_logits, hgrn_norm_g, odd_w_in, odd_w_out, diff_lambda, diff_subln_g):
    b, t, d = x.shape
    n_ctx = ctx.shape[1]
    depth = ada_w.shape[0]
    assert n_ctx == TOK_BLOCK and t % TOK_BLOCK == 0 and b + 1 <= 8
    alpha = (2.0 * depth) ** 0.25

    xs = jnp.concatenate([x, ctx], axis=1)
    cvec = jnp.concatenate([c, c_ctx[None], jnp.zeros((8 - b - 1, d), F32)], axis=0)
    mods = _ada_mods(cvec, ada_w, ada_b)
    lb_all = jax.nn.softmax(hgrn_lb_logits.astype(F32), axis=0)
    lb_all = jnp.cumsum(lb_all, axis=0) - lb_all[0]
    cos, sin = _rope_tables(n_ctx, t)

    for layer in range(depth):
        m = mods[layer]
        per_seg = lambda u: jnp.stack([jnp.broadcast_to(u[b], (b, d)), u[:b]], axis=1)[:, :, None, :]
        shift, scale, gmod = (per_seg(m[:, j * d:(j + 1) * d]) for j in range(3))
        j = layer // 2
        if layer % 2 == 0:
            xs = _even_layer(alpha, xs, scale, shift, gmod, even_w_in[j], even_w_out[j], rwkv_mu_prev[j],
                             rwkv_mu_next[j], rwkv_w0[j], rwkv_w2[j], rwkv_a0[j], rwkv_a2[j], rwkv_k_k[j],
                             rwkv_k_a[j], rwkv_r_k[j], rwkv_lnx_g[j], rwkv_lnx_b[j], lb_all[j], hgrn_norm_g[j],
                             ln_g[layer], ln_b[layer])
        else:
            lam_init = 0.8 - 0.6 * float(np.exp(-0.3 * layer))
            xs = _odd_layer(alpha, xs, scale, shift, gmod, odd_w_in[j], odd_w_out[j], diff_lambda[j],
                            diff_subln_g[j], lam_init, cos, sin, ln_g[layer], ln_b[layer],
                            t if layer == depth - 1 else t + n_ctx)
    return xs[:, :t]
```

```python
import functools

import numpy as np
import jax
import jax.numpy as jnp
from jax import lax
from jax.experimental import pallas as pl
from jax.experimental.pallas import tpu as pltpu

F32 = jnp.float32
BF16 = jnp.bfloat16
HI = lax.Precision.HIGHEST

GRID_W = 64
RWKV_HEAD = 64
RWKV_LORA = 64
RWKV_GN_EPS = 64e-5
HGRN_HEAD = 128
DIFF_HEAD = 64
ROPE_BASE = 10000.0
LN_EPS = 1e-5
LOG2E = float(np.log2(np.e))

LANES = 128
MXU_DIM = 256
VMEM_LIMIT = 56 * 1024 * 1024

CHUNK = 64
SUB = 16
TOK_BLOCK = 256
PROJ_ROWS = 768
GROUP_HEADS = MXU_DIM // RWKV_HEAD
HGRN_PAR = 2
ATTN_UNROLL = 16
ATTN_TILE = 2048


def _cparams(*sem):
    return pltpu.CompilerParams(dimension_semantics=sem, vmem_limit_bytes=VMEM_LIMIT)


def _nt(a, b):
    return lax.dot_general(a, b, (((1,), (1,)), ((), ())), preferred_element_type=F32)


def _tn(a, b):
    return lax.dot_general(a, b, (((0,), (0,)), ((), ())), preferred_element_type=F32)


def _mm(a, b):
    return jnp.dot(a, b, preferred_element_type=F32)


def _mm_exact(a, b):
    return jnp.dot(a, b, preferred_element_type=F32, precision=HI)


def _sigmoid(x):
    return 1.0 / (1.0 + jnp.exp(-x))


def _silu(x):
    return x * _sigmoid(x)


def _lane_tiles(x):
    return [x[:, j * LANES:(j + 1) * LANES] for j in range(x.shape[-1] // LANES)]


def _group64_sum(x):
    lane = lax.broadcasted_iota(jnp.int32, (1, LANES), 1)
    low = lane < RWKV_HEAD
    out = []
    for xt in _lane_tiles(x):
        s_all = jnp.sum(xt, axis=-1, keepdims=True)
        s_lo = jnp.sum(jnp.where(low, xt, 0.0), axis=-1, keepdims=True)
        out.append(jnp.where(low, s_lo, s_all - s_lo))
    return jnp.concatenate(out, axis=-1)


def _group128_mean(x):
    out = []
    for xt in _lane_tiles(x):
        out.append(jnp.broadcast_to(jnp.mean(xt, axis=-1, keepdims=True), xt.shape))
    return jnp.concatenate(out, axis=-1)


def _ada_kernel(c_ref, w_ref, b_ref, o_ref):
    cond = _silu(c_ref[...])
    o_ref[0] = _mm_exact(cond, w_ref[0]) + b_ref[0]


def _ada_mods(cvec, ada_w, ada_b):
    depth, d, d3 = ada_w.shape
    tn = 1024
    return pl.pallas_call(
        _ada_kernel,
        grid=(depth, d3 // tn),
        in_specs=[
            pl.BlockSpec((8, d), lambda l, j: (0, 0)),
            pl.BlockSpec((1, d, tn), lambda l, j: (l, 0, j)),
            pl.BlockSpec((1, 1, tn), lambda l, j: (l, 0, j)),
        ],
        out_specs=pl.BlockSpec((1, 8, tn), lambda l, j: (l, 0, j)),
        out_shape=jax.ShapeDtypeStruct((depth, 8, d3), F32),
        compiler_params=_cparams("arbitrary", "arbitrary"),
        name="ada_mods",
    )(cvec, ada_w, ada_b.reshape(depth, 1, d3))


def _proj_rows(lt):
    return PROJ_ROWS if lt % PROJ_ROWS == 0 else TOK_BLOCK


def _modulate(n_lat, axis, x_ref, sc_ref, sh_ref):
    tm = x_ref.shape[1]
    rows = pl.program_id(axis) * tm + lax.broadcasted_iota(jnp.int32, (tm, 1), 0)
    is_ctx = rows >= n_lat
    sc = jnp.where(is_ctx, sc_ref[0, 0], sc_ref[0, 1])
    sh = jnp.where(is_ctx, sh_ref[0, 0], sh_ref[0, 1])
    return (x_ref[0] * (1.0 + sc) + sh).astype(BF16)


def _proj_kernel(n_lat, x_ref, sc_ref, sh_ref, w_ref, o_ref):
    o_ref[0] = _mm(_modulate(n_lat, 2, x_ref, sc_ref, sh_ref), w_ref[...]).astype(o_ref.dtype)


def _proj_rope_kernel(n_lat, x_ref, sc_ref, sh_ref, w_ref, cos_ref, sin_ref, o_ref):
    p = _mm(_modulate(n_lat, 2, x_ref, sc_ref, sh_ref), w_ref[...])
    reps = p.shape[-1] // LANES
    cos = jnp.concatenate([cos_ref[...]] * reps, axis=-1)
    sin = jnp.concatenate([sin_ref[...]] * reps, axis=-1)
    o_ref[0] = _rope(p, cos, sin).astype(o_ref.dtype)


def _proj_t_kernel(n_lat, x_ref, sc_ref, sh_ref, wt_ref, o_ref):
    o_ref[0] = _nt(wt_ref[...], _modulate(n_lat, 1, x_ref, sc_ref, sh_ref)).astype(o_ref.dtype)


def _project(xs, scale, shift, w, tn, out_dtype, rope=None):
    b, lt, d = xs.shape
    n = w.shape[1]
    tm = _proj_rows(lt)
    n_lat = lt - TOK_BLOCK
    in_specs = [
        pl.BlockSpec((1, tm, d), lambda j, bb, i: (bb, i, 0)),
        pl.BlockSpec((1, 2, 1, d), lambda j, bb, i: (bb, 0, 0, 0)),
        pl.BlockSpec((1, 2, 1, d), lambda j, bb, i: (bb, 0, 0, 0)),
        pl.BlockSpec((d, tn), lambda j, bb, i: (0, j)),
    ]
    args = (xs, scale, shift, w)
    if rope is not None:
        in_specs += [pl.BlockSpec((tm, LANES), lambda j, bb, i: (i, 0))] * 2
        args += tuple(rope)
    return pl.pallas_call(
        functools.partial(_proj_kernel if rope is None else _proj_rope_kernel, n_lat),
        grid=(n // tn, b, lt // tm),
        in_specs=in_specs,
        out_specs=pl.BlockSpec((1, tm, tn), lambda j, bb, i: (bb, i, j)),
        out_shape=jax.ShapeDtypeStruct((b, lt, n), out_dtype),
        compiler_params=_cparams("arbitrary", "arbitrary", "arbitrary"),
        name="mod_proj" if rope is None else "mod_proj_rope",
    )(*args)


def _project_t(xs, scale, shift, wt, out_dtype):
    b, lt, d = xs.shape
    n = wt.shape[0]
    tm = _proj_rows(lt)
    return pl.pallas_call(
        functools.partial(_proj_t_kernel, lt - TOK_BLOCK),
        grid=(b, lt // tm),
        in_specs=[
            pl.BlockSpec((1, tm, d), lambda bb, i: (bb, i, 0)),
            pl.BlockSpec((1, 2, 1, d), lambda bb, i: (bb, 0, 0, 0)),
            pl.BlockSpec((1, 2, 1, d), lambda bb, i: (bb, 0, 0, 0)),
            pl.BlockSpec((n, d), lambda bb, i: (0, 0)),
        ],
        out_specs=pl.BlockSpec((1, n, tm), lambda bb, i: (bb, 0, i)),
        out_shape=jax.ShapeDtypeStruct((b, n, lt), out_dtype),
        compiler_params=_cparams("arbitrary", "arbitrary"),
        name="mod_proj_t",
    )(xs, scale, shift, wt)


def _lora(x, w_ref):
    x1 = x.astype(BF16)
    x2 = (x - x1.astype(F32)).astype(BF16)
    return _mm(jnp.concatenate([x1, x1, x2], axis=-1), w_ref[...])


def _rwkv_prep_kernel(nblk, p_ref, prev_ref, next_ref, mup_ref, mun_ref, w0_ref, w2_ref, a0_ref, a2_ref,
                      kk_ref, ka_ref, rk_ref, cm_ref,
                      v_out, bonus_out, *dir_outs):
    i = pl.program_id(1)
    w = kk_ref.shape[-1]
    p = p_ref[0]
    row = lax.broadcasted_iota(jnp.int32, (TOK_BLOCK, 1), 0)
    prev_row = jnp.where(jnp.logical_and(i >= 1, i <= nblk - 2), prev_ref[0][7:8, :], 0.0)
    next_row = jnp.where(i <= nblk - 3, next_ref[0][0:1, :], 0.0)
    prev = jnp.where(row == 0, prev_row, pltpu.roll(p, 1, 0))
    nxt = jnp.where(row == TOK_BLOCK - 1, next_row, pltpu.roll(p, TOK_BLOCK - 1, 0))
    ps = p * (1.0 - mup_ref[...] - mun_ref[...]) + mup_ref[...] * prev + mun_ref[...] * nxt

    r = ps[:, 0:w]
    k = ps[:, w:2 * w]
    v = ps[:, 2 * w:3 * w]
    wlo = ps[:, 3 * w:3 * w + 2 * RWKV_LORA]
    alo = ps[:, 3 * w + 2 * RWKV_LORA:3 * w + 4 * RWKV_LORA]

    z = _lora(jnp.tanh(wlo), w2_ref) + w0_ref[...]
    lw = _sigmoid(z) * (-LOG2E * float(np.exp(-0.5)))
    a = _sigmoid(_lora(alo, a2_ref) + a0_ref[...])

    kk = k * kk_ref[...]
    kk = kk * lax.rsqrt(_group64_sum(kk * kk) + 1e-12)
    bonus_out[0] = (_group64_sum(r * k * rk_ref[...]) * v).astype(bonus_out.dtype)
    v_out[0] = v.astype(BF16)

    for d in range(2):
        rh_o, kh_o, bh_o, kap_o, kg_o, bg_o, gl_o = dir_outs[7 * d:7 * d + 7]
        lw_d = lw[:, d * w:(d + 1) * w]
        a_d = a[:, d * w:(d + 1) * w]
        kd = k * (1.0 + (a_d - 1.0) * ka_ref[...])
        b_d = kk * a_d
        sums = _mm(cm_ref[d], _split3(lw_d))
        g = sums[:TOK_BLOCK]
        g_all = sums[TOK_BLOCK:]
        e_neg = jnp.exp2(-g)
        e_rest = jnp.exp2(g_all - g)
        rh_o[0] = (r * jnp.exp2(g)).astype(BF16)
        kh_o[0] = (kd * e_neg).astype(BF16)
        bh_o[0] = (b_d * e_neg).astype(BF16)
        kap_o[0] = (kk * jnp.exp2(g - lw_d)).astype(BF16)
        kg_o[0] = (kd * e_rest).astype(BF16)
        bg_o[0] = (b_d * e_rest).astype(BF16)
        e_all = jnp.exp2(g_all)
        for c in range(TOK_BLOCK // CHUNK):
            gl_o[0, c] = e_all[c * CHUNK:c * CHUNK + 1, :]


def _rwkv_sum_consts(reverse):
    t = np.arange(TOK_BLOCK)[:, None]
    s = np.arange(TOK_BLOCK)[None, :]
    same = (t // CHUNK) == (s // CHUNK)
    order = (s >= t) if reverse else (s <= t)
    both = np.concatenate([same & order, same], axis=0).astype(np.float32)
    return np.concatenate([both] * 3, axis=1)


def _lora_pieces(m):
    hi = m.astype(BF16)
    lo = (m - hi.astype(F32)).astype(BF16)
    return jnp.concatenate([hi, lo, hi], axis=0)


def _rwkv_prep(p_rwkv, mu_prev, mu_next, w0, w2, a0, a2, k_k, k_a, r_k):
    b, lt, sw = p_rwkv.shape
    w = k_k.shape[-1]
    nblk = lt // TOK_BLOCK
    nch = lt // CHUNK
    halo = TOK_BLOCK // 8
    zl = jnp.zeros((RWKV_LORA, w), F32)
    w2cat = _lora_pieces(jnp.concatenate([jnp.concatenate([w2[0], zl], 1), jnp.concatenate([zl, w2[1]], 1)], 0))
    a2cat = _lora_pieces(jnp.concatenate([jnp.concatenate([a2[0], zl], 1), jnp.concatenate([zl, a2[1]], 1)], 0))
    cm = jnp.asarray(np.stack([_rwkv_sum_consts(False), _rwkv_sum_consts(True)]), dtype=BF16)
    row = lambda u: u.reshape(1, -1)
    full = lambda shape: pl.BlockSpec(shape, lambda bb, i: (0,) * len(shape))
    tok = lambda n, dt: (pl.BlockSpec((1, TOK_BLOCK, n), lambda bb, i: (bb, i, 0)), jax.ShapeDtypeStruct((b, lt, n), dt))
    gl = (pl.BlockSpec((1, TOK_BLOCK // CHUNK, 1, w), lambda bb, i: (bb, i, 0, 0)),
          jax.ShapeDtypeStruct((b, nch, 1, w), F32))
    outs = [tok(w, BF16), tok(w, BF16)] + 2 * ([tok(w, BF16)] * 6 + [gl])
    res = pl.pallas_call(
        functools.partial(_rwkv_prep_kernel, nblk),
        grid=(b, nblk),
        in_specs=[
            pl.BlockSpec((1, TOK_BLOCK, sw), lambda bb, i: (bb, i, 0)),
            pl.BlockSpec((1, 8, sw), lambda bb, i: (bb, jnp.maximum(i * halo - 1, 0), 0)),
            pl.BlockSpec((1, 8, sw), lambda bb, i: (bb, jnp.minimum((i + 1) * halo, lt // 8 - 1), 0)),
            full((1, sw)), full((1, sw)), full((1, 2 * w)), full(w2cat.shape),
            full((1, 2 * w)), full(a2cat.shape), full((1, w)), full((1, w)), full((1, w)),
            full(cm.shape),
        ],
        out_specs=[o[0] for o in outs],
        out_shape=[o[1] for o in outs],
        compiler_params=_cparams("arbitrary", "arbitrary"),
        name="rwkv_prep",
    )(p_rwkv, p_rwkv, p_rwkv, row(mu_prev), row(mu_next), row(w0), w2cat, row(a0), a2cat,
      row(k_k), row(k_a), row(r_k), cm)
    v, bonus = res[0], res[1]
    return v, bonus, res[2:9], res[9:16]


def _rwkv_masks(reverse):
    n = GROUP_HEADS * CHUNK
    i = np.arange(n)[:, None]
    j = np.arange(n)[None, :]
    same_head = (i // CHUNK) == (j // CHUNK)
    before = (j > i) if reverse else (j < i)
    strict = same_head & before
    incl = same_head & (before | (i == j))
    levels = []
    size = 1
    while size < CHUNK:
        levels.append(same_head & ((i // (2 * size)) == (j // (2 * size))) & ((i // size) != (j // size)))
        size *= 2
    head_lane = (i // CHUNK) == (j // RWKV_HEAD)
    tri = np.stack([strict, incl]).astype(np.float32)
    return tri, np.stack(levels).astype(np.float32), head_lane.astype(np.float32)


def _scan_chunk_index(c, nch, reverse):
    if reverse:
        return nch - 1 - c
    n_ctx = TOK_BLOCK // CHUNK
    return jnp.where(c < n_ctx, nch - n_ctx + c, c - n_ctx)


def _split3(x):
    x1 = x.astype(BF16)
    r1 = x - x1.astype(F32)
    x2 = r1.astype(BF16)
    x3 = (r1 - x2.astype(F32)).astype(BF16)
    return jnp.concatenate([x1, x2, x3], axis=0)


def _hgrn_chunk_consts(reverse):
    t = np.arange(CHUNK)[:, None]
    s = np.arange(CHUNK)[None, :]
    order = (s >= t) if reverse else (s <= t)
    earlier_sub = ((s // SUB) > (t // SUB)) if reverse else ((s // SUB) < (t // SUB))
    both = np.concatenate([order, earlier_sub], axis=0).astype(np.float32)
    return np.concatenate([both] * 3, axis=1)


def _rwkv_chunk_stages(rh_ref, kh_ref, bh_ref, kap_ref, kg_ref, bg_ref, v_ref, gl_ref, tri_ref, lvl_ref, hm_ref,
                       y_ref, s_ref):
    hm = hm_ref[...]
    strict = tri_ref[0]
    incl = tri_ref[1]
    eye = incl - strict
    n_levels = lvl_ref.shape[0]
    groups = range(y_ref.shape[-1] // MXU_DIM)
    lanes = lambda ref, g: ref[0, :, g * MXU_DIM:(g + 1) * MXU_DIM]
    stack = lambda ref, g: jnp.concatenate([lanes(ref, g)] * GROUP_HEADS, axis=0) * hm

    kap, bh, a_ab = [], [], []
    for g in groups:
        kap.append(stack(kap_ref, g))
        bh.append(stack(bh_ref, g))
        a_ab.append(_nt(kap[g], bh[g]) * strict)
    yield
    kh = [stack(kh_ref, g) for g in groups]
    a_ak = [(_nt(kap[g], kh[g]) * strict).astype(BF16) for g in groups]
    rh = [stack(rh_ref, g) for g in groups]
    p_rb = [(_nt(rh[g], bh[g]) * incl).astype(BF16) for g in groups]
    p_rk = [(_nt(rh[g], kh[g]) * incl).astype(BF16) for g in groups]
    vs = [stack(v_ref, g) for g in groups]
    t = [eye - a_ab[g] * lvl_ref[0] for g in groups]
    a_abb = [a_ab[g].astype(BF16) for g in groups]
    for lv in range(1, n_levels):
        yield
        tb = [t[g].astype(BF16) for g in groups]
        x = [_mm(tb[g], a_abb[g]).astype(BF16) for g in groups]
        yield
        t = [t[g] - _mm(x[g], tb[g]) * lvl_ref[lv] for g in groups]
    yield
    s0 = [s_ref[g] for g in groups]
    s0b = [s0[g].astype(BF16) for g in groups]
    hm32 = hm.astype(F32)
    ks0 = [jnp.concatenate([_nt(lanes(kap_ref, g), s0b[g])] * GROUP_HEADS, axis=0) * hm32 for g in groups]
    wmat = [(ks0[g] + _mm(a_ak[g], vs[g])).astype(BF16) for g in groups]
    yield
    ub = [(-_mm(t[g].astype(BF16), wmat[g])).astype(BF16) for g in groups]
    yield
    ys = [_mm(p_rb[g], ub[g]) + _mm(p_rk[g], vs[g]) for g in groups]
    for g in groups:
        y = _nt(lanes(rh_ref, g), s0b[g])
        for h in range(GROUP_HEADS):
            y = y + ys[g][h * CHUNK:(h + 1) * CHUNK]
        y_ref[0, :, g * MXU_DIM:(g + 1) * MXU_DIM] = y.astype(y_ref.dtype)
    yield
    for g in groups:
        gam = gl_ref[0, 0][:, g * MXU_DIM:(g + 1) * MXU_DIM]
        s_ref[g] = s0[g] * gam + _tn(ub[g], stack(bg_ref, g)) + _tn(vs[g], stack(kg_ref, g))


def _hgrn_chunk_stages(reverse, q_ref, f_ref, i_ref, lb_ref, cm_ref, o_ref, s_ref, g_scr, q_scr):
    n_sub = CHUNK // SUB
    n_heads = q_ref.shape[-1] // HGRN_HEAD
    sub_order = list(range(n_sub))[::-1] if reverse else list(range(n_sub))
    row_sub = lax.broadcasted_iota(jnp.int32, (CHUNK, 1), 0) // SUB
    srow = lax.broadcasted_iota(jnp.int32, (SUB, 1), 0)
    lane = lax.broadcasted_iota(jnp.int32, (1, LANES), 1)
    last_row = 0 if reverse else CHUNK - 1

    kgate = (1.0 - lb_ref[...]) * _sigmoid(-f_ref[0])
    sums = _mm(cm_ref[...], _split3(jnp.log1p(-kgate) * LOG2E))
    g_all = sums[:CHUNK]
    gb_all = sums[CHUNK:]
    q_all = _silu(q_ref[0])
    g_scr[...] = g_all
    q_scr[...] = q_all
    yield
    for hg in range(n_heads // HGRN_PAR):
        par = range(HGRN_PAR)
        hs = [hg * HGRN_PAR + j for j in par]
        cols = [slice(h * HGRN_HEAD, (h + 1) * HGRN_HEAD) for h in hs]
        g = [g_all[:, cs] for cs in cols]
        gb = [gb_all[:, cs] for cs in cols]
        q = [q_all[:, cs] for cs in cols]
        k = [kgate[:, cs] for cs in cols]
        vb = [i_ref[0, :, cs].astype(BF16) for cs in cols]
        s0 = [s_ref[h] for h in hs]
        q_in = [q[j] * jnp.exp2(g[j] - gb[j]) for j in par]
        at = []
        for j in par:
            kts, qts = [], []
            for si in sub_order[1:]:
                gb_i = gb[j][si * SUB:si * SUB + 1, :]
                earlier = (row_sub > si) if reverse else (row_sub < si)
                kts.append(jnp.where(earlier, k[j] * jnp.exp2(jnp.minimum(gb_i - g[j], 0.0)), 0.0).astype(BF16))
                qts.append(jnp.where(row_sub == si, q_in[j], 0.0).astype(BF16))
            at.append(_nt(jnp.concatenate(kts, axis=-1), jnp.concatenate(qts, axis=-1)))
        yield
        diag = [[] for _ in par]
        for si in range(n_sub):
            blk = slice(si * SUB, (si + 1) * SUB)
            acc = [jnp.zeros((SUB, LANES), F32) for _ in par]
            for tt in range(SUB):
                t = si * SUB + tt
                valid = jnp.logical_and(lane == t, (srow >= tt) if reverse else (srow <= tt))
                for j in par:
                    e = jnp.exp2(jnp.minimum(g_scr[t:t + 1, cols[j]] - g[j][blk], 0.0))
                    col = jnp.sum(k[j][blk] * e * q_scr[t:t + 1, cols[j]], axis=-1, keepdims=True)
                    acc[j] = jnp.where(valid, col, acc[j])
                if tt % 8 == 7:
                    yield
            for j in par:
                diag[j].append(acc[j])
        for j in par:
            a = (at[j] + jnp.concatenate(diag[j], axis=0)[:, :CHUNK]).astype(BF16)
            o = _tn(a, vb[j]) + _nt((q[j] * jnp.exp2(g[j])).astype(BF16), s0[j].astype(BF16))
            o_ref[0, :, cols[j]] = o.astype(o_ref.dtype)
            g_last = g[j][last_row:last_row + 1, :]
            k_out = (k[j] * jnp.exp2(g_last - g[j])).astype(BF16)
            s_ref[hs[j]] = s0[j] * jnp.exp2(g_last) + _tn(vb[j], k_out)
        yield


def _mix_scan_kernel(reverse, *refs):
    rwkv_in, hgrn_in = refs[:11], refs[11:16]
    y_ref, o_ref, s_ref, hs_ref, g_scr, q_scr = refs[16:]

    @pl.when(pl.program_id(1) == 0)
    def _():
        s_ref[...] = jnp.zeros_like(s_ref)
        hs_ref[...] = jnp.zeros_like(hs_ref)

    streams = [_rwkv_chunk_stages(*rwkv_in, y_ref, s_ref),
               _hgrn_chunk_stages(reverse, *hgrn_in, o_ref, hs_ref, g_scr, q_scr)]
    done = object()
    while streams:
        streams = [st for st in streams if next(st, done) is not done]


def _mix_scan(dir_ops, v, p_hgrn, lb_d, d, reverse):
    rh, kh, bh, kap, kg, bg, gl = dir_ops
    b, lt, w = v.shape
    nch = lt // CHUNK
    n = GROUP_HEADS * CHUNK
    tri, lvl, hm = _rwkv_masks(reverse)
    cm = jnp.asarray(_hgrn_chunk_consts(reverse), dtype=BF16)
    cidx = lambda c: _scan_chunk_index(c, nch, reverse)
    tok = pl.BlockSpec((1, CHUNK, w), lambda bb, c: (bb, cidx(c), 0))
    col = lambda j: pl.BlockSpec((1, CHUNK, w), lambda bb, c: (bb, cidx(c), j))
    full = lambda shape: pl.BlockSpec(shape, lambda bb, c: (0,) * len(shape))
    return pl.pallas_call(
        functools.partial(_mix_scan_kernel, reverse),
        grid=(b, nch),
        in_specs=[tok] * 7 + [
            pl.BlockSpec((1, 1, 1, w), lambda bb, c: (bb, cidx(c), 0, 0)),
            full(tri.shape), full(lvl.shape), full(hm.shape),
            col(0), col(1 + d), col(3), full((1, w)), full(cm.shape),
        ],
        out_specs=[tok, tok],
        out_shape=[jax.ShapeDtypeStruct((b, lt, w), BF16)] * 2,
        scratch_shapes=[pltpu.VMEM((w // MXU_DIM, n, n), F32),
                        pltpu.VMEM((w // HGRN_HEAD, HGRN_HEAD, HGRN_HEAD), F32),
                        pltpu.VMEM((CHUNK, w), F32), pltpu.VMEM((CHUNK, w), F32)],
        compiler_params=_cparams("arbitrary", "arbitrary"),
        name="mix_scan_rev" if reverse else "mix_scan_fwd",
    )(rh, kh, bh, kap, kg, bg, v, gl, jnp.asarray(tri), jnp.asarray(lvl), jnp.asarray(hm, dtype=BF16),
      p_hgrn, p_hgrn, p_hgrn, lb_d.reshape(1, w), cm)


def _residual_ln(alpha, n_lat, x, gm_ref, proj, lng, lnb):
    tm = x.shape[0]
    rows = pl.program_id(1) * tm + lax.broadcasted_iota(jnp.int32, (tm, 1), 0)
    gm = jnp.where(rows >= n_lat, gm_ref[0, 0], gm_ref[0, 1])
    z = alpha * x + gm * proj
    mu = jnp.mean(z, axis=-1, keepdims=True)
    zc = z - mu
    var = jnp.mean(zc * zc, axis=-1, keepdims=True)
    return zc * lax.rsqrt(var + LN_EPS) * lng + lnb


def _even_out_kernel(alpha, n_lat, ya0_ref, ya1_ref, bonus_ref, yb0_ref, yb1_ref, gate_ref, x_ref, gm_ref, w_ref,
                     lnxg_ref, lnxb_ref, ng_ref, lng_ref, lnb_ref, o_ref):
    y = ya0_ref[0].astype(F32) + ya1_ref[0].astype(F32)
    mu = _group64_sum(y) * (1.0 / RWKV_HEAD)
    yc = y - mu
    var = _group64_sum(yc * yc) * (1.0 / RWKV_HEAD)
    ya = yc * lax.rsqrt(var + RWKV_GN_EPS) * lnxg_ref[...] + lnxb_ref[...] + bonus_ref[0].astype(F32)
    o = yb0_ref[0].astype(F32) + yb1_ref[0].astype(F32)
    yb = o * lax.rsqrt(_group128_mean(o * o) + LN_EPS) * ng_ref[...]
    ycat = jnp.concatenate([ya, yb], axis=-1) * _silu(gate_ref[0].astype(F32))
    proj = _mm(ycat.astype(BF16), w_ref[...])
    o_ref[0] = _residual_ln(alpha, n_lat, x_ref[0], gm_ref, proj, lng_ref[...], lnb_ref[...])


def _odd_out_kernel(alpha, n_lat, y_ref, x_ref, gm_ref, w_ref, lng_ref, lnb_ref, o_ref):
    proj = _mm(y_ref[0].astype(BF16), w_ref[...])
    o_ref[0] = _residual_ln(alpha, n_lat, x_ref[0], gm_ref, proj, lng_ref[...], lnb_ref[...])


def _tok_spec(n, tm=TOK_BLOCK):
    return pl.BlockSpec((1, tm, n), lambda bb, i: (bb, i, 0))


def _full2(shape):
    return pl.BlockSpec(shape, lambda bb, i: (0,) * len(shape))


def _gate_spec(d):
    return pl.BlockSpec((1, 2, 1, d), lambda bb, i: (bb, 0, 0, 0))


def _even_out(alpha, ya0, ya1, bonus, yb0, yb1, gate, xs, gmod, w_out, lnx_g, lnx_b, norm_g, ln_g, ln_b):
    b, lt, d = xs.shape
    w = ya0.shape[-1]
    di = gate.shape[-1]
    row = lambda u: u.reshape(1, -1)
    tm = _proj_rows(lt)
    return pl.pallas_call(
        functools.partial(_even_out_kernel, alpha, lt - TOK_BLOCK),
        grid=(b, lt // tm),
        in_specs=[_tok_spec(w, tm)] * 5 + [_tok_spec(di, tm), _tok_spec(d, tm), _gate_spec(d), _full2((di, d)),
                                           _full2((1, w)), _full2((1, w)), _full2((1, w)), _full2((1, d)),
                                           _full2((1, d))],
        out_specs=_tok_spec(d, tm),
        out_shape=jax.ShapeDtypeStruct((b, lt, d), F32),
        compiler_params=_cparams("arbitrary", "arbitrary"),
        name="even_out",
    )(ya0, ya1, bonus, yb0, yb1, gate, xs, gmod, w_out, row(lnx_g), row(lnx_b),
      row(jnp.tile(norm_g, w // HGRN_HEAD)), row(ln_g), row(ln_b))


def _odd_out(alpha, y, xs, gmod, w_out, ln_g, ln_b, rows_out):
    b, lt, d = xs.shape
    di = y.shape[-1]
    row = lambda u: u.reshape(1, -1)
    tm = next(r for r in (PROJ_ROWS, 2 * TOK_BLOCK, TOK_BLOCK) if rows_out % r == 0)
    return pl.pallas_call(
        functools.partial(_odd_out_kernel, alpha, lt - TOK_BLOCK),
        grid=(b, rows_out // tm),
        in_specs=[_tok_spec(di, tm), _tok_spec(d, tm), _gate_spec(d), _full2((di, d)), _full2((1, d)),
                  _full2((1, d))],
        out_specs=_tok_spec(d, tm),
        out_shape=jax.ShapeDtypeStruct((b, rows_out, d), F32),
        compiler_params=_cparams("arbitrary", "arbitrary"),
        name="odd_out",
    )(y, xs, gmod, w_out, row(ln_g), row(ln_b))


def _rope(x, cos, sin):
    width = x.shape[-1]
    lane = lax.broadcasted_iota(jnp.int32, (1, width), 1)
    first = (lane % 32) < 16
    partner = jnp.where(first, pltpu.roll(x, width - 16, 1), pltpu.roll(x, 16, 1))
    return x * cos + partner * sin


def _attn_kernel(lam_init, tk, n_full, tail, lam_ref, q_ref, k_ref, vt_ref, g_ref, sg_ref, *rest):
    y_ref, m_scr, acc_scr = rest[-3:]
    tq = q_ref.shape[1]
    ones_rows = 16
    tile = min(ATTN_TILE, 2 * tq)
    n_tiles = 2 * tq // tile
    q = q_ref[0].astype(F32) * (DIFF_HEAD ** -0.5 * float(np.log2(np.e)))
    lane = lax.broadcasted_iota(jnp.int32, (1, LANES), 1)
    q_cat = jnp.concatenate([jnp.where(lane < DIFF_HEAD, q, 0.0), jnp.where(lane >= DIFF_HEAD, q, 0.0)],
                            axis=0).astype(BF16)
    m_scr[...] = jnp.full(m_scr.shape, -jnp.inf, F32)
    acc_scr[...] = jnp.zeros(acc_scr.shape, F32)

    def scores(item):
        start, size, c = item
        return _nt(k_ref[0, pl.ds(start, size), :], q_cat[c * tile:(c + 1) * tile])

    def run(blocks):
        cols = [slice(c * tile, (c + 1) * tile) for c in range(n_tiles)]
        m_run = [m_scr[:, cs] for cs in cols]
        acc = [acc_scr[:, cs] for cs in cols]
        vta = None
        pending = None
        items = [(start, size, c) for start, size in blocks for c in range(n_tiles)]
        st_next = scores(items[0])
        for i, (start, size, c) in enumerate(items):
            st = st_next
            if i + 1 < len(items):
                st_next = scores(items[i + 1])
            if pending is not None:
                pc, pcorr, pvta, ppt = pending
                acc[pc] = acc[pc] * pcorr + _mm(pvta, ppt)
            if c == 0:
                vta = jnp.concatenate([vt_ref[0, :, pl.ds(start, size)], jnp.ones((ones_rows, size), BF16)], axis=0)
            m_new = jnp.maximum(m_run[c], jnp.max(st, axis=0, keepdims=True))
            pending = (c, jnp.exp2(m_run[c] - m_new), vta, jnp.exp2(st - m_new).astype(BF16))
            m_run[c] = m_new
        pc, pcorr, pvta, ppt = pending
        acc[pc] = acc[pc] * pcorr + _mm(pvta, ppt)
        for c, cs in enumerate(cols):
            m_scr[:, cs] = m_run[c]
            acc_scr[:, cs] = acc[c]

    tail_block = [(n_full * tk, tail)]
    if n_full <= ATTN_UNROLL:
        run([(u * tk, tk) for u in range(n_full)] + tail_block)
    else:
        assert n_full % ATTN_UNROLL == 0
        def body(j, carry):
            run([(pl.multiple_of((j * ATTN_UNROLL + u) * tk, tk), tk) for u in range(ATTN_UNROLL)])
            return carry
        lax.fori_loop(0, n_full // ATTN_UNROLL, body, 0)
        run(tail_block)

    a0 = acc_scr[:, :tq]
    a1 = acc_scr[:, tq:]
    o_t = a0[:LANES] / a0[LANES:LANES + 1] - lam_ref[0] * (a1[:LANES] / a1[LANES:LANES + 1])
    o = jnp.transpose(o_t)
    y = o * lax.rsqrt(jnp.mean(o * o, axis=-1, keepdims=True) + LN_EPS) * sg_ref[...] * (1.0 - lam_init)
    y_ref[0] = (y * _silu(g_ref[0].astype(F32))).astype(y_ref.dtype)


def _diff_attention(p_qk, p_vt, p_g, lam, subln_g, lam_init):
    b, lt, di = p_g.shape
    nh = di // LANES
    t = lt - TOK_BLOCK
    tk = 512
    tq = next(r for r in (1024, 512) if t % r == 0)
    ctx_blk = t // TOK_BLOCK
    sm = pl.BlockSpec(memory_space=pltpu.SMEM)
    sg_spec = pl.BlockSpec((1, LANES), lambda bb, h, i: (0, 0))
    scratch = lambda n: [pltpu.VMEM((1, 2 * n), F32), pltpu.VMEM((LANES + 16, 2 * n), F32)]
    args = (lam.reshape(1), p_qk, p_qk, p_vt, p_g, subln_g.reshape(1, LANES))
    y = pl.pallas_call(
        functools.partial(_attn_kernel, lam_init, tk, t // tk, TOK_BLOCK),
        grid=(b, nh, t // tq),
        in_specs=[sm,
                  pl.BlockSpec((1, tq, LANES), lambda bb, h, i: (bb, i, h)),
                  pl.BlockSpec((1, lt, LANES), lambda bb, h, i: (bb, 0, nh + h)),
                  pl.BlockSpec((1, LANES, lt), lambda bb, h, i: (bb, h, 0)),
                  pl.BlockSpec((1, tq, LANES), lambda bb, h, i: (bb, i, h)),
                  sg_spec],
        out_specs=pl.BlockSpec((1, tq, LANES), lambda bb, h, i: (bb, i, h)),
        out_shape=jax.ShapeDtypeStruct((b, lt, di), BF16),
        scratch_shapes=scratch(tq),
        compiler_params=_cparams("arbitrary", "arbitrary", "arbitrary"),
        name="diff_attn",
    )(*args)
    blk = lambda bb, h, i: (bb, ctx_blk, h)
    return pl.pallas_call(
        functools.partial(_attn_kernel, lam_init, tk, 0, TOK_BLOCK),
        grid=(b, nh, 1),
        in_specs=[sm,
                  pl.BlockSpec((1, TOK_BLOCK, LANES), blk),
                  pl.BlockSpec((1, TOK_BLOCK, LANES), lambda bb, h, i: (bb, ctx_blk, nh + h)),
                  pl.BlockSpec((1, LANES, TOK_BLOCK), lambda bb, h, i: (bb, h, ctx_blk)),
                  pl.BlockSpec((1, TOK_BLOCK, LANES), blk),
                  sg_spec,
                  pl.BlockSpec(memory_space=pl.ANY)],
        out_specs=pl.BlockSpec((1, TOK_BLOCK, LANES), blk),
        out_shape=jax.ShapeDtypeStruct((b, lt, di), BF16),
        scratch_shapes=scratch(TOK_BLOCK),
        input_output_aliases={6: 0},
        compiler_params=_cparams("arbitrary", "arbitrary", "arbitrary"),
        name="diff_attn_ctx",
    )(*args, y)


def _rope_tables(n_ctx, t):
    quarter = DIFF_HEAD // 4
    inv = ROPE_BASE ** (-jnp.arange(quarter, dtype=F32) / quarter)
    pos = jnp.arange(t)
    rows = (pos // GRID_W).astype(F32)[:, None] * inv
    cols = (pos % GRID_W).astype(F32)[:, None] * inv
    cos64 = jnp.concatenate([jnp.cos(rows), jnp.cos(rows), jnp.cos(cols), jnp.cos(cols)], -1)
    sin64 = jnp.concatenate([-jnp.sin(rows), jnp.sin(rows), -jnp.sin(cols), jnp.sin(cols)], -1)
    cos = jnp.concatenate([jnp.tile(cos64, (1, 2)), jnp.ones((n_ctx, LANES), F32)], 0)
    sin = jnp.concatenate([jnp.tile(sin64, (1, 2)), jnp.zeros((n_ctx, LANES), F32)], 0)
    return cos, sin


def _even_layer(alpha, xs, scale, shift, gmod, w_in, w_out, mu_prev, mu_next, w0, w2, a0, a2, k_k, k_a, r_k,
                lnx_g, lnx_b, lb, norm_g, ln_g, ln_b):
    w = k_k.shape[-1]
    sw = mu_prev.shape[-1]
    hw = 4 * lb.shape[-1]
    w_in = w_in.astype(BF16)
    p_rwkv = _project(xs, scale, shift, w_in[:, :sw], sw, F32)
    p_hgrn = _project(xs, scale, shift, w_in[:, sw:sw + hw], hw // 2, F32)
    gate = _project(xs, scale, shift, w_in[:, sw + hw:], w_in.shape[1] - sw - hw, BF16)
    v, bonus, ops_f, ops_r = _rwkv_prep(p_rwkv, mu_prev, mu_next, w0, w2, a0, a2, k_k, k_a, r_k)
    ya0, yb0 = _mix_scan(ops_f, v, p_hgrn, lb[0], 0, False)
    ya1, yb1 = _mix_scan(ops_r, v, p_hgrn, lb[1], 1, True)
    return _even_out(alpha, ya0, ya1, bonus, yb0, yb1, gate, xs, gmod, w_out.astype(BF16),
                     lnx_g, lnx_b, norm_g, ln_g, ln_b)


def _odd_layer(alpha, xs, scale, shift, gmod, w_in, w_out, lam_p, subln_g, lam_init, cos, sin, ln_g, ln_b,
               rows_out):
    di = w_out.shape[0]
    w_in = w_in.astype(BF16)
    p_qk = _project(xs, scale, shift, w_in[:, :2 * di], di, BF16, rope=(cos, sin))
    p_vt = _project_t(xs, scale, shift, jnp.transpose(w_in[:, 2 * di:3 * di]), BF16)
    p_g = _project(xs, scale, shift, w_in[:, 3 * di:], di, BF16)
    lam = jnp.exp(jnp.sum(lam_p[0] * lam_p[1])) - jnp.exp(jnp.sum(lam_p[2] * lam_p[3])) + lam_init
    y = _diff_attention(p_qk, p_vt, p_g, lam, subln_g, lam_init)
    return _odd_out(alpha, y, xs, gmod, w_out.astype(BF16), ln_g, ln_b, rows_out)


def kernel(x, c, ctx, c_ctx, ada_w, ada_b, ln_g, ln_b, even_w_in, even_w_out, rwkv_mu_prev, rwkv_mu_next, rwkv_w0, rwkv_w2, rwkv_a0, rwkv_a2, rwkv_k_k, rwkv_k_a, rwkv_r_k, rwkv_lnx_g, rwkv_lnx_b, hgrn_lb_logits, hgrn_norm_g, odd_w_in, odd_w_out, diff_lambda, diff_subln_g):
    b, t, d = x.shape
    n_ctx = ctx.shape[1]
    depth = ada_w.shape[0]
    assert n_ctx == TOK_BLOCK and t % TOK_BLOCK == 0 and b + 1 <= 8
    alpha = (2.0 * depth) ** 0.25

    xs = jnp.concatenate([x, ctx], axis=1)
    cvec = jnp.concatenate([c, c_ctx[None], jnp.zeros((8 - b - 1, d), F32)], axis=0)
    mods = _ada_mods(cvec, ada_w, ada_b)
    lb_all = jax.nn.softmax(hgrn_lb_logits.astype(F32), axis=0)
    lb_all = jnp.cumsum(lb_all, axis=0) - lb_all[0]
    cos, sin = _rope_tables(n_ctx, t)

    for layer in range(depth):
        m = mods[layer]
        per_seg = lambda u: jnp.stack([jnp.broadcast_to(u[b], (b, d)), u[:b]], axis=1)[:, :, None, :]
        shift, scale, gmod = (per_seg(m[:, j * d:(j + 1) * d]) for j in range(3))
        j = layer // 2
        if layer % 2 == 0:
            xs = _even_layer(alpha, xs, scale, shift, gmod, even_w_in[j], even_w_out[j], rwkv_mu_prev[j],
                             rwkv_mu_next[j], rwkv_w0[j], rwkv_w2[j], rwkv_a0[j], rwkv_a2[j], rwkv_k_k[j],
                             rwkv_k_a[j], rwkv_r_k[j], rwkv_lnx_g[j], rwkv_lnx_b[j], lb_all[j], hgrn_norm_g[j],
                             ln_g[layer], ln_b[layer])
        else:
            lam_init = 0.8 - 0.6 * float(np.exp(-0.3 * layer))
            xs = _odd_layer(alpha, xs, scale, shift, gmod, odd_w_in[j], odd_w_out[j], diff_lambda[j],
                            diff_subln_g[j], lam_init, cos, sin, ln_g[layer], ln_b[layer],
                            t if layer == depth - 1 else t + n_ctx)
    return xs[:, :t]
```

```python
import functools

import numpy as np
import jax
import jax.numpy as jnp
from jax import lax
from jax.experimental import pallas as pl
from jax.experimental.pallas import tpu as pltpu

F32 = jnp.float32
BF16 = jnp.bfloat16
HI = lax.Precision.HIGHEST

GRID_W = 64
RWKV_HEAD = 64
RWKV_LORA = 64
RWKV_GN_EPS = 64e-5
HGRN_HEAD = 128
DIFF_HEAD = 64
ROPE_BASE = 10000.0
LN_EPS = 1e-5
LOG2E = float(np.log2(np.e))

LANES = 128
MXU_DIM = 256
VMEM_LIMIT = 56 * 1024 * 1024

CHUNK = 64
SUB = 16
TOK_BLOCK = 256
PROJ_ROWS = 768
GROUP_HEADS = MXU_DIM // RWKV_HEAD
HGRN_PAR = 4
ATTN_UNROLL = 16
ATTN_HEADS = 2


def _cparams(*sem):
    return pltpu.CompilerParams(dimension_semantics=sem, vmem_limit_bytes=VMEM_LIMIT)


def _nt(a, b):
    return lax.dot_general(a, b, (((1,), (1,)), ((), ())), preferred_element_type=F32)


def _tn(a, b):
    return lax.dot_general(a, b, (((0,), (0,)), ((), ())), preferred_element_type=F32)


def _mm(a, b):
    return jnp.dot(a, b, preferred_element_type=F32)


def _mm_exact(a, b):
    return jnp.dot(a, b, preferred_element_type=F32, precision=HI)


def _sigmoid(x):
    return 1.0 / (1.0 + jnp.exp(-x))


def _silu(x):
    return x * _sigmoid(x)


def _lane_tiles(x):
    return [x[:, j * LANES:(j + 1) * LANES] for j in range(x.shape[-1] // LANES)]


def _group64_sum(x):
    lane = lax.broadcasted_iota(jnp.int32, (1, LANES), 1)
    low = lane < RWKV_HEAD
    out = []
    for xt in _lane_tiles(x):
        s_all = jnp.sum(xt, axis=-1, keepdims=True)
        s_lo = jnp.sum(jnp.where(low, xt, 0.0), axis=-1, keepdims=True)
        out.append(jnp.where(low, s_lo, s_all - s_lo))
    return jnp.concatenate(out, axis=-1)


def _group128_mean(x):
    out = []
    for xt in _lane_tiles(x):
        out.append(jnp.broadcast_to(jnp.mean(xt, axis=-1, keepdims=True), xt.shape))
    return jnp.concatenate(out, axis=-1)


def _ada_kernel(c_ref, w_ref, b_ref, o_ref):
    cond = _silu(c_ref[...])
    o_ref[0] = _mm_exact(cond, w_ref[0]) + b_ref[0]


def _ada_mods(cvec, ada_w, ada_b):
    depth, d, d3 = ada_w.shape
    tn = 1024
    return pl.pallas_call(
        _ada_kernel,
        grid=(depth, d3 // tn),
        in_specs=[
            pl.BlockSpec((8, d), lambda l, j: (0, 0)),
            pl.BlockSpec((1, d, tn), lambda l, j: (l, 0, j)),
            pl.BlockSpec((1, 1, tn), lambda l, j: (l, 0, j)),
        ],
        out_specs=pl.BlockSpec((1, 8, tn), lambda l, j: (l, 0, j)),
        out_shape=jax.ShapeDtypeStruct((depth, 8, d3), F32),
        compiler_params=_cparams("arbitrary", "arbitrary"),
        name="ada_mods",
    )(cvec, ada_w, ada_b.reshape(depth, 1, d3))


def _proj_rows(lt):
    return PROJ_ROWS if lt % PROJ_ROWS == 0 else TOK_BLOCK


def _modulate(n_lat, axis, x_ref, sc_ref, sh_ref):
    tm = x_ref.shape[1]
    rows = pl.program_id(axis) * tm + lax.broadcasted_iota(jnp.int32, (tm, 1), 0)
    is_ctx = rows >= n_lat
    sc = jnp.where(is_ctx, sc_ref[0, 0], sc_ref[0, 1])
    sh = jnp.where(is_ctx, sh_ref[0, 0], sh_ref[0, 1])
    return (x_ref[0] * (1.0 + sc) + sh).astype(BF16)


def _proj_kernel(n_lat, x_ref, sc_ref, sh_ref, w_ref, o_ref):
    o_ref[0] = _mm(_modulate(n_lat, 2, x_ref, sc_ref, sh_ref), w_ref[...]).astype(o_ref.dtype)


def _proj_rope_kernel(n_lat, x_ref, sc_ref, sh_ref, w_ref, cos_ref, sin_ref, o_ref):
    p = _mm(_modulate(n_lat, 2, x_ref, sc_ref, sh_ref), w_ref[...])
    reps = p.shape[-1] // LANES
    cos = jnp.concatenate([cos_ref[...]] * reps, axis=-1)
    sin = jnp.concatenate([sin_ref[...]] * reps, axis=-1)
    o_ref[0] = _rope(p, cos, sin).astype(o_ref.dtype)


def _proj_t_kernel(n_lat, x_ref, sc_ref, sh_ref, wt_ref, o_ref):
    o_ref[0] = _nt(wt_ref[...], _modulate(n_lat, 1, x_ref, sc_ref, sh_ref)).astype(o_ref.dtype)


def _project(xs, scale, shift, w, tn, out_dtype, rope=None):
    b, lt, d = xs.shape
    n = w.shape[1]
    tm = _proj_rows(lt)
    n_lat = lt - TOK_BLOCK
    in_specs = [
        pl.BlockSpec((1, tm, d), lambda j, bb, i: (bb, i, 0)),
        pl.BlockSpec((1, 2, 1, d), lambda j, bb, i: (bb, 0, 0, 0)),
        pl.BlockSpec((1, 2, 1, d), lambda j, bb, i: (bb, 0, 0, 0)),
        pl.BlockSpec((d, tn), lambda j, bb, i: (0, j)),
    ]
    args = (xs, scale, shift, w)
    if rope is not None:
        in_specs += [pl.BlockSpec((tm, LANES), lambda j, bb, i: (i, 0))] * 2
        args += tuple(rope)
    return pl.pallas_call(
        functools.partial(_proj_kernel if rope is None else _proj_rope_kernel, n_lat),
        grid=(n // tn, b, lt // tm),
        in_specs=in_specs,
        out_specs=pl.BlockSpec((1, tm, tn), lambda j, bb, i: (bb, i, j)),
        out_shape=jax.ShapeDtypeStruct((b, lt, n), out_dtype),
        compiler_params=_cparams("arbitrary", "arbitrary", "arbitrary"),
        name="mod_proj" if rope is None else "mod_proj_rope",
    )(*args)


def _project_t(xs, scale, shift, wt, out_dtype):
    b, lt, d = xs.shape
    n = wt.shape[0]
    tm = _proj_rows(lt)
    return pl.pallas_call(
        functools.partial(_proj_t_kernel, lt - TOK_BLOCK),
        grid=(b, lt // tm),
        in_specs=[
            pl.BlockSpec((1, tm, d), lambda bb, i: (bb, i, 0)),
            pl.BlockSpec((1, 2, 1, d), lambda bb, i: (bb, 0, 0, 0)),
            pl.BlockSpec((1, 2, 1, d), lambda bb, i: (bb, 0, 0, 0)),
            pl.BlockSpec((n, d), lambda bb, i: (0, 0)),
        ],
        out_specs=pl.BlockSpec((1, n, tm), lambda bb, i: (bb, 0, i)),
        out_shape=jax.ShapeDtypeStruct((b, n, lt), out_dtype),
        compiler_params=_cparams("arbitrary", "arbitrary"),
        name="mod_proj_t",
    )(xs, scale, shift, wt)


def _lora(x, w_ref):
    x1 = x.astype(BF16)
    x2 = (x - x1.astype(F32)).astype(BF16)
    return _mm(jnp.concatenate([x1, x1, x2], axis=-1), w_ref[...])


def _rwkv_prep_kernel(nblk, p_ref, prev_ref, next_ref, mup_ref, mun_ref, w0_ref, w2_ref, a0_ref, a2_ref,
                      kk_ref, ka_ref, rk_ref, cm_ref,
                      v_out, bonus_out, *dir_outs):
    i = pl.program_id(1)
    w = kk_ref.shape[-1]
    p = p_ref[0]
    row = lax.broadcasted_iota(jnp.int32, (TOK_BLOCK, 1), 0)
    prev_row = jnp.where(jnp.logical_and(i >= 1, i <= nblk - 2), prev_ref[0][7:8, :], 0.0)
    next_row = jnp.where(i <= nblk - 3, next_ref[0][0:1, :], 0.0)
    prev = jnp.where(row == 0, prev_row, pltpu.roll(p, 1, 0))
    nxt = jnp.where(row == TOK_BLOCK - 1, next_row, pltpu.roll(p, TOK_BLOCK - 1, 0))
    ps = p * (1.0 - mup_ref[...] - mun_ref[...]) + mup_ref[...] * prev + mun_ref[...] * nxt

    r = ps[:, 0:w]
    k = ps[:, w:2 * w]
    v = ps[:, 2 * w:3 * w]
    wlo = ps[:, 3 * w:3 * w + 2 * RWKV_LORA]
    alo = ps[:, 3 * w + 2 * RWKV_LORA:3 * w + 4 * RWKV_LORA]

    z = _lora(jnp.tanh(wlo), w2_ref) + w0_ref[...]
    lw = _sigmoid(z) * (-LOG2E * float(np.exp(-0.5)))
    a = _sigmoid(_lora(alo, a2_ref) + a0_ref[...])

    kk = k * kk_ref[...]
    kk = kk * lax.rsqrt(_group64_sum(kk * kk) + 1e-12)
    bonus_out[0] = (_group64_sum(r * k * rk_ref[...]) * v).astype(bonus_out.dtype)
    v_out[0] = v.astype(BF16)

    for d in range(2):
        rh_o, kh_o, bh_o, kap_o, kg_o, bg_o, gl_o = dir_outs[7 * d:7 * d + 7]
        lw_d = lw[:, d * w:(d + 1) * w]
        a_d = a[:, d * w:(d + 1) * w]
        kd = k * (1.0 + (a_d - 1.0) * ka_ref[...])
        b_d = kk * a_d
        sums = _mm(cm_ref[d], _split3(lw_d))
        g = sums[:TOK_BLOCK]
        g_all = sums[TOK_BLOCK:]
        e_neg = jnp.exp2(-g)
        e_rest = jnp.exp2(g_all - g)
        rh_o[0] = (r * jnp.exp2(g)).astype(BF16)
        kh_o[0] = (kd * e_neg).astype(BF16)
        bh_o[0] = (b_d * e_neg).astype(BF16)
        kap_o[0] = (kk * jnp.exp2(g - lw_d)).astype(BF16)
        kg_o[0] = (kd * e_rest).astype(BF16)
        bg_o[0] = (b_d * e_rest).astype(BF16)
        e_all = jnp.exp2(g_all)
        for c in range(TOK_BLOCK // CHUNK):
            gl_o[0, c] = e_all[c * CHUNK:c * CHUNK + 1, :]


def _rwkv_sum_consts(reverse):
    t = np.arange(TOK_BLOCK)[:, None]
    s = np.arange(TOK_BLOCK)[None, :]
    same = (t // CHUNK) == (s // CHUNK)
    order = (s >= t) if reverse else (s <= t)
    both = np.concatenate([same & order, same], axis=0).astype(np.float32)
    return np.concatenate([both] * 3, axis=1)


def _lora_pieces(m):
    hi = m.astype(BF16)
    lo = (m - hi.astype(F32)).astype(BF16)
    return jnp.concatenate([hi, lo, hi], axis=0)


def _rwkv_prep(p_rwkv, mu_prev, mu_next, w0, w2, a0, a2, k_k, k_a, r_k):
    b, lt, sw = p_rwkv.shape
    w = k_k.shape[-1]
    nblk = lt // TOK_BLOCK
    nch = lt // CHUNK
    halo = TOK_BLOCK // 8
    zl = jnp.zeros((RWKV_LORA, w), F32)
    w2cat = _lora_pieces(jnp.concatenate([jnp.concatenate([w2[0], zl], 1), jnp.concatenate([zl, w2[1]], 1)], 0))
    a2cat = _lora_pieces(jnp.concatenate([jnp.concatenate([a2[0], zl], 1), jnp.concatenate([zl, a2[1]], 1)], 0))
    cm = jnp.asarray(np.stack([_rwkv_sum_consts(False), _rwkv_sum_consts(True)]), dtype=BF16)
    row = lambda u: u.reshape(1, -1)
    full = lambda shape: pl.BlockSpec(shape, lambda bb, i: (0,) * len(shape))
    tok = lambda n, dt: (pl.BlockSpec((1, TOK_BLOCK, n), lambda bb, i: (bb, i, 0)), jax.ShapeDtypeStruct((b, lt, n), dt))
    gl = (pl.BlockSpec((1, TOK_BLOCK // CHUNK, 1, w), lambda bb, i: (bb, i, 0, 0)),
          jax.ShapeDtypeStruct((b, nch, 1, w), F32))
    outs = [tok(w, BF16), tok(w, BF16)] + 2 * ([tok(w, BF16)] * 6 + [gl])
    res = pl.pallas_call(
        functools.partial(_rwkv_prep_kernel, nblk),
        grid=(b, nblk),
        in_specs=[
            pl.BlockSpec((1, TOK_BLOCK, sw), lambda bb, i: (bb, i, 0)),
            pl.BlockSpec((1, 8, sw), lambda bb, i: (bb, jnp.maximum(i * halo - 1, 0), 0)),
            pl.BlockSpec((1, 8, sw), lambda bb, i: (bb, jnp.minimum((i + 1) * halo, lt // 8 - 1), 0)),
            full((1, sw)), full((1, sw)), full((1, 2 * w)), full(w2cat.shape),
            full((1, 2 * w)), full(a2cat.shape), full((1, w)), full((1, w)), full((1, w)),
            full(cm.shape),
        ],
        out_specs=[o[0] for o in outs],
        out_shape=[o[1] for o in outs],
        compiler_params=_cparams("arbitrary", "arbitrary"),
        name="rwkv_prep",
    )(p_rwkv, p_rwkv, p_rwkv, row(mu_prev), row(mu_next), row(w0), w2cat, row(a0), a2cat,
      row(k_k), row(k_a), row(r_k), cm)
    v, bonus = res[0], res[1]
    return v, bonus, res[2:9], res[9:16]


def _rwkv_masks(reverse):
    n = GROUP_HEADS * CHUNK
    i = np.arange(n)[:, None]
    j = np.arange(n)[None, :]
    same_head = (i // CHUNK) == (j // CHUNK)
    before = (j > i) if reverse else (j < i)
    strict = same_head & before
    incl = same_head & (before | (i == j))
    levels = []
    size = 1
    while size < CHUNK:
        levels.append(same_head & ((i // (2 * size)) == (j // (2 * size))) & ((i // size) != (j // size)))
        size *= 2
    head_lane = (i // CHUNK) == (j // RWKV_HEAD)
    tri = np.stack([strict, incl]).astype(np.float32)
    return tri, np.stack(levels).astype(np.float32), head_lane.astype(np.float32)


def _scan_chunk_index(c, nch, reverse):
    if reverse:
        return nch - 1 - c
    n_ctx = TOK_BLOCK // CHUNK
    return jnp.where(c < n_ctx, nch - n_ctx + c, c - n_ctx)


def _split3(x):
    x1 = x.astype(BF16)
    r1 = x - x1.astype(F32)
    x2 = r1.astype(BF16)
    x3 = (r1 - x2.astype(F32)).astype(BF16)
    return jnp.concatenate([x1, x2, x3], axis=0)


def _hgrn_chunk_consts(reverse):
    t = np.arange(CHUNK)[:, None]
    s = np.arange(CHUNK)[None, :]
    order = (s >= t) if reverse else (s <= t)
    earlier_sub = ((s // SUB) > (t // SUB)) if reverse else ((s // SUB) < (t // SUB))
    both = np.concatenate([order, earlier_sub], axis=0).astype(np.float32)
    return np.concatenate([both] * 3, axis=1)


def _rwkv_chunk_stages(rh_ref, kh_ref, bh_ref, kap_ref, kg_ref, bg_ref, v_ref, gl_ref, tri_ref, lvl_ref, hm_ref,
                       y_ref, s_ref):
    hm = hm_ref[...]
    strict = tri_ref[0]
    incl = tri_ref[1]
    eye = incl - strict
    n_levels = lvl_ref.shape[0]
    groups = range(y_ref.shape[-1] // MXU_DIM)
    lanes = lambda ref, g: ref[0, :, g * MXU_DIM:(g + 1) * MXU_DIM]
    stack = lambda ref, g: jnp.concatenate([lanes(ref, g)] * GROUP_HEADS, axis=0) * hm

    kap, bh, a_ab = [], [], []
    for g in groups:
        kap.append(stack(kap_ref, g))
        bh.append(stack(bh_ref, g))
        a_ab.append(_nt(kap[g], bh[g]) * strict)
    yield
    kh = [stack(kh_ref, g) for g in groups]
    a_ak = [(_nt(kap[g], kh[g]) * strict).astype(BF16) for g in groups]
    rh = [stack(rh_ref, g) for g in groups]
    p_rb = [(_nt(rh[g], bh[g]) * incl).astype(BF16) for g in groups]
    p_rk = [(_nt(rh[g], kh[g]) * incl).astype(BF16) for g in groups]
    vs = [stack(v_ref, g) for g in groups]
    t = [eye - a_ab[g] * lvl_ref[0] for g in groups]
    a_abb = [a_ab[g].astype(BF16) for g in groups]
    for lv in range(1, n_levels):
        yield
        tb = [t[g].astype(BF16) for g in groups]
        x = [_mm(tb[g], a_abb[g]).astype(BF16) for g in groups]
        yield
        t = [t[g] - _mm(x[g], tb[g]) * lvl_ref[lv] for g in groups]
    yield
    s0 = [s_ref[g] for g in groups]
    s0b = [s0[g].astype(BF16) for g in groups]
    hm32 = hm.astype(F32)
    ks0 = [jnp.concatenate([_nt(lanes(kap_ref, g), s0b[g])] * GROUP_HEADS, axis=0) * hm32 for g in groups]
    wmat = [(ks0[g] + _mm(a_ak[g], vs[g])).astype(BF16) for g in groups]
    yield
    ub = [(-_mm(t[g].astype(BF16), wmat[g])).astype(BF16) for g in groups]
    yield
    ys = [_mm(p_rb[g], ub[g]) + _mm(p_rk[g], vs[g]) for g in groups]
    for g in groups:
        y = _nt(lanes(rh_ref, g), s0b[g])
        for h in range(GROUP_HEADS):
            y = y + ys[g][h * CHUNK:(h + 1) * CHUNK]
        y_ref[0, :, g * MXU_DIM:(g + 1) * MXU_DIM] = y.astype(y_ref.dtype)
    yield
    for g in groups:
        gam = gl_ref[0, 0][:, g * MXU_DIM:(g + 1) * MXU_DIM]
        s_ref[g] = s0[g] * gam + _tn(ub[g], stack(bg_ref, g)) + _tn(vs[g], stack(kg_ref, g))


def _hgrn_chunk_stages(reverse, q_ref, f_ref, i_ref, lb_ref, cm_ref, o_ref, s_ref, g_scr, q_scr):
    n_sub = CHUNK // SUB
    n_heads = q_ref.shape[-1] // HGRN_HEAD
    sub_order = list(range(n_sub))[::-1] if reverse else list(range(n_sub))
    row_sub = lax.broadcasted_iota(jnp.int32, (CHUNK, 1), 0) // SUB
    srow = lax.broadcasted_iota(jnp.int32, (SUB, 1), 0)
    lane = lax.broadcasted_iota(jnp.int32, (1, LANES), 1)
    last_row = 0 if reverse else CHUNK - 1

    kgate = (1.0 - lb_ref[...]) * _sigmoid(-f_ref[0])
    sums = _mm(cm_ref[...], _split3(jnp.log1p(-kgate) * LOG2E))
    g_all = sums[:CHUNK]
    gb_all = sums[CHUNK:]
    q_all = _silu(q_ref[0])
    g_scr[...] = g_all
    q_scr[...] = q_all
    yield
    for hg in range(n_heads // HGRN_PAR):
        par = range(HGRN_PAR)
        hs = [hg * HGRN_PAR + j for j in par]
        cols = [slice(h * HGRN_HEAD, (h + 1) * HGRN_HEAD) for h in hs]
        g = [g_all[:, cs] for cs in cols]
        gb = [gb_all[:, cs] for cs in cols]
        q = [q_all[:, cs] for cs in cols]
        k = [kgate[:, cs] for cs in cols]
        vb = [i_ref[0, :, cs].astype(BF16) for cs in cols]
        s0 = [s_ref[h] for h in hs]
        q_in = [q[j] * jnp.exp2(g[j] - gb[j]) for j in par]
        at = []
        for j in par:
            kts, qts = [], []
            for si in sub_order[1:]:
                gb_i = gb[j][si * SUB:si * SUB + 1, :]
                earlier = (row_sub > si) if reverse else (row_sub < si)
                kts.append(jnp.where(earlier, k[j] * jnp.exp2(jnp.minimum(gb_i - g[j], 0.0)), 0.0).astype(BF16))
                qts.append(jnp.where(row_sub == si, q_in[j], 0.0).astype(BF16))
            at.append(_nt(jnp.concatenate(kts, axis=-1), jnp.concatenate(qts, axis=-1)))
        yield
        diag = [[] for _ in par]
        for si in range(n_sub):
            blk = slice(si * SUB, (si + 1) * SUB)
            acc = [jnp.zeros((SUB, LANES), F32) for _ in par]
            for tt in range(SUB):
                t = si * SUB + tt
                valid = jnp.logical_and(lane == t, (srow >= tt) if reverse else (srow <= tt))
                for j in par:
                    e = jnp.exp2(jnp.minimum(g_scr[t:t + 1, cols[j]] - g[j][blk], 0.0))
                    col = jnp.sum(k[j][blk] * e * q_scr[t:t + 1, cols[j]], axis=-1, keepdims=True)
                    acc[j] = jnp.where(valid, col, acc[j])
                if tt % 8 == 7:
                    yield
            for j in par:
                diag[j].append(acc[j])
        for j in par:
            a = (at[j] + jnp.concatenate(diag[j], axis=0)[:, :CHUNK]).astype(BF16)
            o = _tn(a, vb[j]) + _nt((q[j] * jnp.exp2(g[j])).astype(BF16), s0[j].astype(BF16))
            o_ref[0, :, cols[j]] = o.astype(o_ref.dtype)
            g_last = g[j][last_row:last_row + 1, :]
            k_out = (k[j] * jnp.exp2(g_last - g[j])).astype(BF16)
            s_ref[hs[j]] = s0[j] * jnp.exp2(g_last) + _tn(vb[j], k_out)
        yield


def _mix_scan_kernel(reverse, *refs):
    rwkv_in, hgrn_in = refs[:11], refs[11:16]
    y_ref, o_ref, s_ref, hs_ref, g_scr, q_scr = refs[16:]

    @pl.when(pl.program_id(1) == 0)
    def _():
        s_ref[...] = jnp.zeros_like(s_ref)
        hs_ref[...] = jnp.zeros_like(hs_ref)

    streams = [_rwkv_chunk_stages(*rwkv_in, y_ref, s_ref),
               _hgrn_chunk_stages(reverse, *hgrn_in, o_ref, hs_ref, g_scr, q_scr)]
    done = object()
    while streams:
        streams = [st for st in streams if next(st, done) is not done]


def _mix_scan(dir_ops, v, p_hgrn, lb_d, d, reverse):
    rh, kh, bh, kap, kg, bg, gl = dir_ops
    b, lt, w = v.shape
    nch = lt // CHUNK
    n = GROUP_HEADS * CHUNK
    tri, lvl, hm = _rwkv_masks(reverse)
    cm = jnp.asarray(_hgrn_chunk_consts(reverse), dtype=BF16)
    cidx = lambda c: _scan_chunk_index(c, nch, reverse)
    tok = pl.BlockSpec((1, CHUNK, w), lambda bb, c: (bb, cidx(c), 0))
    col = lambda j: pl.BlockSpec((1, CHUNK, w), lambda bb, c: (bb, cidx(c), j))
    full = lambda shape: pl.BlockSpec(shape, lambda bb, c: (0,) * len(shape))
    return pl.pallas_call(
        functools.partial(_mix_scan_kernel, reverse),
        grid=(b, nch),
        in_specs=[tok] * 7 + [
            pl.BlockSpec((1, 1, 1, w), lambda bb, c: (bb, cidx(c), 0, 0)),
            full(tri.shape), full(lvl.shape), full(hm.shape),
            col(0), col(1 + d), col(3), full((1, w)), full(cm.shape),
        ],
        out_specs=[tok, tok],
        out_shape=[jax.ShapeDtypeStruct((b, lt, w), BF16)] * 2,
        scratch_shapes=[pltpu.VMEM((w // MXU_DIM, n, n), F32),
                        pltpu.VMEM((w // HGRN_HEAD, HGRN_HEAD, HGRN_HEAD), F32),
                        pltpu.VMEM((CHUNK, w), F32), pltpu.VMEM((CHUNK, w), F32)],
        compiler_params=_cparams("arbitrary", "arbitrary"),
        name="mix_scan_rev" if reverse else "mix_scan_fwd",
    )(rh, kh, bh, kap, kg, bg, v, gl, jnp.asarray(tri), jnp.asarray(lvl), jnp.asarray(hm, dtype=BF16),
      p_hgrn, p_hgrn, p_hgrn, lb_d.reshape(1, w), cm)


def _residual_ln(alpha, n_lat, x, gm_ref, proj, lng, lnb):
    tm = x.shape[0]
    rows = pl.program_id(1) * tm + lax.broadcasted_iota(jnp.int32, (tm, 1), 0)
    gm = jnp.where(rows >= n_lat, gm_ref[0, 0], gm_ref[0, 1])
    z = alpha * x + gm * proj
    mu = jnp.mean(z, axis=-1, keepdims=True)
    zc = z - mu
    var = jnp.mean(zc * zc, axis=-1, keepdims=True)
    return zc * lax.rsqrt(var + LN_EPS) * lng + lnb


def _even_out_kernel(alpha, n_lat, ya0_ref, ya1_ref, bonus_ref, yb0_ref, yb1_ref, gate_ref, x_ref, gm_ref, w_ref,
                     lnxg_ref, lnxb_ref, ng_ref, lng_ref, lnb_ref, o_ref):
    y = ya0_ref[0].astype(F32) + ya1_ref[0].astype(F32)
    mu = _group64_sum(y) * (1.0 / RWKV_HEAD)
    yc = y - mu
    var = _group64_sum(yc * yc) * (1.0 / RWKV_HEAD)
    ya = yc * lax.rsqrt(var + RWKV_GN_EPS) * lnxg_ref[...] + lnxb_ref[...] + bonus_ref[0].astype(F32)
    o = yb0_ref[0].astype(F32) + yb1_ref[0].astype(F32)
    yb = o * lax.rsqrt(_group128_mean(o * o) + LN_EPS) * ng_ref[...]
    ycat = jnp.concatenate([ya, yb], axis=-1) * _silu(gate_ref[0].astype(F32))
    proj = _mm(ycat.astype(BF16), w_ref[...])
    o_ref[0] = _residual_ln(alpha, n_lat, x_ref[0], gm_ref, proj, lng_ref[...], lnb_ref[...])


def _odd_out_kernel(alpha, n_lat, y_ref, x_ref, gm_ref, w_ref, lng_ref, lnb_ref, o_ref):
    proj = _mm(y_ref[0].astype(BF16), w_ref[...])
    o_ref[0] = _residual_ln(alpha, n_lat, x_ref[0], gm_ref, proj, lng_ref[...], lnb_ref[...])


def _tok_spec(n, tm=TOK_BLOCK):
    return pl.BlockSpec((1, tm, n), lambda bb, i: (bb, i, 0))


def _full2(shape):
    return pl.BlockSpec(shape, lambda bb, i: (0,) * len(shape))


def _gate_spec(d):
    return pl.BlockSpec((1, 2, 1, d), lambda bb, i: (bb, 0, 0, 0))


def _even_out(alpha, ya0, ya1, bonus, yb0, yb1, gate, xs, gmod, w_out, lnx_g, lnx_b, norm_g, ln_g, ln_b):
    b, lt, d = xs.shape
    w = ya0.shape[-1]
    di = gate.shape[-1]
    row = lambda u: u.reshape(1, -1)
    tm = _proj_rows(lt)
    return pl.pallas_call(
        functools.partial(_even_out_kernel, alpha, lt - TOK_BLOCK),
        grid=(b, lt // tm),
        in_specs=[_tok_spec(w, tm)] * 5 + [_tok_spec(di, tm), _tok_spec(d, tm), _gate_spec(d), _full2((di, d)),
                                           _full2((1, w)), _full2((1, w)), _full2((1, w)), _full2((1, d)),
                                           _full2((1, d))],
        out_specs=_tok_spec(d, tm),
        out_shape=jax.ShapeDtypeStruct((b, lt, d), F32),
        compiler_params=_cparams("arbitrary", "arbitrary"),
        name="even_out",
    )(ya0, ya1, bonus, yb0, yb1, gate, xs, gmod, w_out, row(lnx_g), row(lnx_b),
      row(jnp.tile(norm_g, w // HGRN_HEAD)), row(ln_g), row(ln_b))


def _odd_out(alpha, y, xs, gmod, w_out, ln_g, ln_b, rows_out):
    b, lt, d = xs.shape
    di = y.shape[-1]
    row = lambda u: u.reshape(1, -1)
    tm = next(r for r in (PROJ_ROWS, 2 * TOK_BLOCK, TOK_BLOCK) if rows_out % r == 0)
    return pl.pallas_call(
        functools.partial(_odd_out_kernel, alpha, lt - TOK_BLOCK),
        grid=(b, rows_out // tm),
        in_specs=[_tok_spec(di, tm), _tok_spec(d, tm), _gate_spec(d), _full2((di, d)), _full2((1, d)),
                  _full2((1, d))],
        out_specs=_tok_spec(d, tm),
        out_shape=jax.ShapeDtypeStruct((b, rows_out, d), F32),
        compiler_params=_cparams("arbitrary", "arbitrary"),
        name="odd_out",
    )(y, xs, gmod, w_out, row(ln_g), row(ln_b))


def _rope(x, cos, sin):
    width = x.shape[-1]
    lane = lax.broadcasted_iota(jnp.int32, (1, width), 1)
    first = (lane % 32) < 16
    partner = jnp.where(first, pltpu.roll(x, width - 16, 1), pltpu.roll(x, 16, 1))
    return x * cos + partner * sin


def _attn_kernel(lam_init, tk, n_full, tail, lam_ref, q_ref, k_ref, vt_ref, g_ref, sg_ref, *rest):
    y_ref = rest[-1]
    tq = q_ref.shape[1]
    n_heads = q_ref.shape[2] // LANES
    ones_rows = 16
    assert n_full <= ATTN_UNROLL
    blocks = [(u * tk, tk) for u in range(n_full)] + [(n_full * tk, tail)]
    lane = lax.broadcasted_iota(jnp.int32, (1, LANES), 1)

    def head_stream(j):
        hl = slice(j * LANES, (j + 1) * LANES)
        q = q_ref[0, :, hl].astype(F32) * (DIFF_HEAD ** -0.5 * LOG2E)
        q_cat = jnp.concatenate([jnp.where(lane < DIFF_HEAD, q, 0.0), jnp.where(lane >= DIFF_HEAD, q, 0.0)],
                                axis=0).astype(BF16)
        scores = lambda blk: _nt(k_ref[0, pl.ds(blk[0], blk[1]), hl], q_cat)
        m_run = jnp.full((1, 2 * tq), -jnp.inf, F32)
        acc = jnp.zeros((LANES + ones_rows, 2 * tq), F32)
        pending = None
        st_next = scores(blocks[0])
        for i, (start, size) in enumerate(blocks):
            st = st_next
            if i + 1 < len(blocks):
                st_next = scores(blocks[i + 1])
            if pending is not None:
                acc = acc * pending[0] + _mm(pending[1], pending[2])
            vta = jnp.concatenate([vt_ref[0, hl, pl.ds(start, size)], jnp.ones((ones_rows, size), BF16)], axis=0)
            m_new = jnp.maximum(m_run, jnp.max(st, axis=0, keepdims=True))
            pending = (jnp.exp2(m_run - m_new), vta, jnp.exp2(st - m_new).astype(BF16))
            m_run = m_new
            yield
        acc = acc * pending[0] + _mm(pending[1], pending[2])
        a0 = acc[:, :tq]
        a1 = acc[:, tq:]
        o_t = a0[:LANES] / a0[LANES:LANES + 1] - lam_ref[0] * (a1[:LANES] / a1[LANES:LANES + 1])
        o = jnp.transpose(o_t)
        y = o * lax.rsqrt(jnp.mean(o * o, axis=-1, keepdims=True) + LN_EPS) * sg_ref[...] * (1.0 - lam_init)
        y_ref[0, :, hl] = (y * _silu(g_ref[0, :, hl].astype(F32))).astype(y_ref.dtype)

    streams = [head_stream(j) for j in range(n_heads)]
    done = object()
    while streams:
        streams = [st for st in streams if next(st, done) is not done]


def _diff_attention(p_qk, p_vt, p_g, lam, subln_g, lam_init):
    b, lt, di = p_g.shape
    nh = di // LANES
    t = lt - TOK_BLOCK
    tq = tk = 512
    assert t % tq == 0 and nh % ATTN_HEADS == 0
    hw = ATTN_HEADS * LANES
    ng = nh // ATTN_HEADS
    ctx_blk = t // TOK_BLOCK
    sm = pl.BlockSpec(memory_space=pltpu.SMEM)
    sg_spec = pl.BlockSpec((1, LANES), lambda bb, h, i: (0, 0))
    args = (lam.reshape(1), p_qk, p_qk, p_vt, p_g, subln_g.reshape(1, LANES))
    y = pl.pallas_call(
        functools.partial(_attn_kernel, lam_init, tk, t // tk, TOK_BLOCK),
        grid=(b, ng, t // tq),
        in_specs=[sm,
                  pl.BlockSpec((1, tq, hw), lambda bb, h, i: (bb, i, h)),
                  pl.BlockSpec((1, lt, hw), lambda bb, h, i: (bb, 0, ng + h)),
                  pl.BlockSpec((1, hw, lt), lambda bb, h, i: (bb, h, 0)),
                  pl.BlockSpec((1, tq, hw), lambda bb, h, i: (bb, i, h)),
                  sg_spec],
        out_specs=pl.BlockSpec((1, tq, hw), lambda bb, h, i: (bb, i, h)),
        out_shape=jax.ShapeDtypeStruct((b, lt, di), BF16),
        compiler_params=_cparams("arbitrary", "arbitrary", "arbitrary"),
        name="diff_attn",
    )(*args)
    blk = lambda bb, h, i: (bb, ctx_blk, h)
    return pl.pallas_call(
        functools.partial(_attn_kernel, lam_init, tk, 0, TOK_BLOCK),
        grid=(b, ng, 1),
        in_specs=[sm,
                  pl.BlockSpec((1, TOK_BLOCK, hw), blk),
                  pl.BlockSpec((1, TOK_BLOCK, hw), lambda bb, h, i: (bb, ctx_blk, ng + h)),
                  pl.BlockSpec((1, hw, TOK_BLOCK), lambda bb, h, i: (bb, h, ctx_blk)),
                  pl.BlockSpec((1, TOK_BLOCK, hw), blk),
                  sg_spec,
                  pl.BlockSpec(memory_space=pl.ANY)],
        out_specs=pl.BlockSpec((1, TOK_BLOCK, hw), blk),
        out_shape=jax.ShapeDtypeStruct((b, lt, di), BF16),
        input_output_aliases={6: 0},
        compiler_params=_cparams("arbitrary", "arbitrary", "arbitrary"),
        name="diff_attn_ctx",
    )(*args, y)


def _rope_tables(n_ctx, t):
    quarter = DIFF_HEAD // 4
    inv = ROPE_BASE ** (-jnp.arange(quarter, dtype=F32) / quarter)
    pos = jnp.arange(t)
    rows = (pos // GRID_W).astype(F32)[:, None] * inv
    cols = (pos % GRID_W).astype(F32)[:, None] * inv
    cos64 = jnp.concatenate([jnp.cos(rows), jnp.cos(rows), jnp.cos(cols), jnp.cos(cols)], -1)
    sin64 = jnp.concatenate([-jnp.sin(rows), jnp.sin(rows), -jnp.sin(cols), jnp.sin(cols)], -1)
    cos = jnp.concatenate([jnp.tile(cos64, (1, 2)), jnp.ones((n_ctx, LANES), F32)], 0)
    sin = jnp.concatenate([jnp.tile(sin64, (1, 2)), jnp.zeros((n_ctx, LANES), F32)], 0)
    return cos, sin


def _even_layer(alpha, xs, scale, shift, gmod, w_in, w_out, mu_prev, mu_next, w0, w2, a0, a2, k_k, k_a, r_k,
                lnx_g, lnx_b, lb, norm_g, ln_g, ln_b):
    w = k_k.shape[-1]
    sw = mu_prev.shape[-1]
    hw = 4 * lb.shape[-1]
    w_in = w_in.astype(BF16)
    p_rwkv = _project(xs, scale, shift, w_in[:, :sw], sw, F32)
    p_hgrn = _project(xs, scale, shift, w_in[:, sw:sw + hw], hw // 2, F32)
    gate = _project(xs, scale, shift, w_in[:, sw + hw:], w_in.shape[1] - sw - hw, BF16)
    v, bonus, ops_f, ops_r = _rwkv_prep(p_rwkv, mu_prev, mu_next, w0, w2, a0, a2, k_k, k_a, r_k)
    ya0, yb0 = _mix_scan(ops_f, v, p_hgrn, lb[0], 0, False)
    ya1, yb1 = _mix_scan(ops_r, v, p_hgrn, lb[1], 1, True)
    return _even_out(alpha, ya0, ya1, bonus, yb0, yb1, gate, xs, gmod, w_out.astype(BF16),
                     lnx_g, lnx_b, norm_g, ln_g, ln_b)


def _odd_layer(alpha, xs, scale, shift, gmod, w_in, w_out, lam_p, subln_g, lam_init, cos, sin, ln_g, ln_b,
               rows_out):
    di = w_out.shape[0]
    w_in = w_in.astype(BF16)
    p_qk = _project(xs, scale, shift, w_in[:, :2 * di], di, BF16, rope=(cos, sin))
    p_vt = _project_t(xs, scale, shift, jnp.transpose(w_in[:, 2 * di:3 * di]), BF16)
    p_g = _project(xs, scale, shift, w_in[:, 3 * di:], di, BF16)
    lam = jnp.exp(jnp.sum(lam_p[0] * lam_p[1])) - jnp.exp(jnp.sum(lam_p[2] * lam_p[3])) + lam_init
    y = _diff_attention(p_qk, p_vt, p_g, lam, subln_g, lam_init)
    return _odd_out(alpha, y, xs, gmod, w_out.astype(BF16), ln_g, ln_b, rows_out)


def kernel(x, c, ctx, c_ctx, ada_w, ada_b, ln_g, ln_b, even_w_in, even_w_out, rwkv_mu_prev, rwkv_mu_next, rwkv_w0, rwkv_w2, rwkv_a0, rwkv_a2, rwkv_k_k, rwkv_k_a, rwkv_r_k, rwkv_lnx_g, rwkv_lnx_b, hgrn_lb_logits, hgrn_norm_g, odd_w_in, odd_w_out, diff_lambda, diff_subln_g):
    b, t, d = x.shape
    n_ctx = ctx.shape[1]
    depth = ada_w.shape[0]
    assert n_ctx == TOK_BLOCK and t % TOK_BLOCK == 0 and b + 1 <= 8
    alpha = (2.0 * depth) ** 0.25

    xs = jnp.concatenate([x, ctx], axis=1)
    cvec = jnp.concatenate([c, c_ctx[None], jnp.zeros((8 - b - 1, d), F32)], axis=0)
    mods = _ada_mods(cvec, ada_w, ada_b)
    lb_all = jax.nn.softmax(hgrn_lb_logits.astype(F32), axis=0)
    lb_all = jnp.cumsum(lb_all, axis=0) - lb_all[0]
    cos, sin = _rope_tables(n_ctx, t)

    for layer in range(depth):
        m = mods[layer]
        per_seg = lambda u: jnp.stack([jnp.broadcast_to(u[b], (b, d)), u[:b]], axis=1)[:, :, None, :]
        shift, scale, gmod = (per_seg(m[:, j * d:(j + 1) * d]) for j in range(3))
        j = layer // 2
        if layer % 2 == 0:
            xs = _even_layer(alpha, xs, scale, shift, gmod, even_w_in[j], even_w_out[j], rwkv_mu_prev[j],
                             rwkv_mu_next[j], rwkv_w0[j], rwkv_w2[j], rwkv_a0[j], rwkv_a2[j], rwkv_k_k[j],
                             rwkv_k_a[j], rwkv_r_k[j], rwkv_lnx_g[j], rwkv_lnx_b[j], lb_all[j], hgrn_norm_g[j],
                             ln_g[layer], ln_b[layer])
        else:
            lam_init = 0.8 - 0.6 * float(np.exp(-0.3 * layer))
            xs = _odd_layer(alpha, xs, scale, shift, gmod, odd_w_in[j], odd_w_out[j], diff_lambda[j],
                            diff_subln_g[j], lam_init, cos, sin, ln_g[layer], ln_b[layer],
                            t if layer == depth - 1 else t + n_ctx)
    return xs[:, :t]
```

```python
import functools

import numpy as np
import jax
import jax.numpy as jnp
from jax import lax
from jax.experimental import pallas as pl
from jax.experimental.pallas import tpu as pltpu

F32 = jnp.float32
BF16 = jnp.bfloat16
HI = lax.Precision.HIGHEST

GRID_W = 64
RWKV_HEAD = 64
RWKV_LORA = 64
RWKV_GN_EPS = 64e-5
HGRN_HEAD = 128
DIFF_HEAD = 64
ROPE_BASE = 10000.0
LN_EPS = 1e-5
LOG2E = float(np.log2(np.e))

LANES = 128
MXU_DIM = 256
VMEM_LIMIT = 56 * 1024 * 1024

CHUNK = 64
SUB = 16
TOK_BLOCK = 256
PROJ_ROWS = 768
GROUP_HEADS = MXU_DIM // RWKV_HEAD
HGRN_PAR = 4
ATTN_UNROLL = 16
ATTN_HEADS = 4


def _cparams(*sem):
    return pltpu.CompilerParams(dimension_semantics=sem, vmem_limit_bytes=VMEM_LIMIT)


def _nt(a, b):
    return lax.dot_general(a, b, (((1,), (1,)), ((), ())), preferred_element_type=F32)


def _tn(a, b):
    return lax.dot_general(a, b, (((0,), (0,)), ((), ())), preferred_element_type=F32)


def _mm(a, b):
    return jnp.dot(a, b, preferred_element_type=F32)


def _mm_exact(a, b):
    return jnp.dot(a, b, preferred_element_type=F32, precision=HI)


def _sigmoid(x):
    return 1.0 / (1.0 + jnp.exp(-x))


def _silu(x):
    return x * _sigmoid(x)


def _lane_tiles(x):
    return [x[:, j * LANES:(j + 1) * LANES] for j in range(x.shape[-1] // LANES)]


def _group64_sum(x):
    lane = lax.broadcasted_iota(jnp.int32, (1, LANES), 1)
    low = lane < RWKV_HEAD
    out = []
    for xt in _lane_tiles(x):
        s_all = jnp.sum(xt, axis=-1, keepdims=True)
        s_lo = jnp.sum(jnp.where(low, xt, 0.0), axis=-1, keepdims=True)
        out.append(jnp.where(low, s_lo, s_all - s_lo))
    return jnp.concatenate(out, axis=-1)


def _group128_mean(x):
    out = []
    for xt in _lane_tiles(x):
        out.append(jnp.broadcast_to(jnp.mean(xt, axis=-1, keepdims=True), xt.shape))
    return jnp.concatenate(out, axis=-1)


def _ada_kernel(c_ref, w_ref, b_ref, o_ref):
    cond = _silu(c_ref[...])
    o_ref[0] = _mm_exact(cond, w_ref[0]) + b_ref[0]


def _ada_mods(cvec, ada_w, ada_b):
    depth, d, d3 = ada_w.shape
    tn = 1024
    return pl.pallas_call(
        _ada_kernel,
        grid=(depth, d3 // tn),
        in_specs=[
            pl.BlockSpec((8, d), lambda l, j: (0, 0)),
            pl.BlockSpec((1, d, tn), lambda l, j: (l, 0, j)),
            pl.BlockSpec((1, 1, tn), lambda l, j: (l, 0, j)),
        ],
        out_specs=pl.BlockSpec((1, 8, tn), lambda l, j: (l, 0, j)),
        out_shape=jax.ShapeDtypeStruct((depth, 8, d3), F32),
        compiler_params=_cparams("arbitrary", "arbitrary"),
        name="ada_mods",
    )(cvec, ada_w, ada_b.reshape(depth, 1, d3))


def _proj_rows(lt):
    return PROJ_ROWS if lt % PROJ_ROWS == 0 else TOK_BLOCK


def _modulate(n_lat, axis, x_ref, sc_ref, sh_ref):
    tm = x_ref.shape[1]
    rows = pl.program_id(axis) * tm + lax.broadcasted_iota(jnp.int32, (tm, 1), 0)
    is_ctx = rows >= n_lat
    sc = jnp.where(is_ctx, sc_ref[0, 0], sc_ref[0, 1])
    sh = jnp.where(is_ctx, sh_ref[0, 0], sh_ref[0, 1])
    return (x_ref[0] * (1.0 + sc) + sh).astype(BF16)


def _proj_kernel(n_lat, x_ref, sc_ref, sh_ref, w_ref, o_ref):
    o_ref[0] = _mm(_modulate(n_lat, 2, x_ref, sc_ref, sh_ref), w_ref[...]).astype(o_ref.dtype)


def _proj_rope_kernel(n_lat, x_ref, sc_ref, sh_ref, w_ref, cos_ref, sin_ref, o_ref):
    p = _mm(_modulate(n_lat, 2, x_ref, sc_ref, sh_ref), w_ref[...])
    reps = p.shape[-1] // LANES
    cos = jnp.concatenate([cos_ref[...]] * reps, axis=-1)
    sin = jnp.concatenate([sin_ref[...]] * reps, axis=-1)
    o_ref[0] = _rope(p, cos, sin).astype(o_ref.dtype)


def _proj_t_kernel(n_lat, x_ref, sc_ref, sh_ref, wt_ref, o_ref):
    o_ref[0] = _nt(wt_ref[...], _modulate(n_lat, 1, x_ref, sc_ref, sh_ref)).astype(o_ref.dtype)


def _project(xs, scale, shift, w, tn, out_dtype, rope=None):
    b, lt, d = xs.shape
    n = w.shape[1]
    tm = _proj_rows(lt)
    n_lat = lt - TOK_BLOCK
    in_specs = [
        pl.BlockSpec((1, tm, d), lambda j, bb, i: (bb, i, 0)),
        pl.BlockSpec((1, 2, 1, d), lambda j, bb, i: (bb, 0, 0, 0)),
        pl.BlockSpec((1, 2, 1, d), lambda j, bb, i: (bb, 0, 0, 0)),
        pl.BlockSpec((d, tn), lambda j, bb, i: (0, j)),
    ]
    args = (xs, scale, shift, w)
    if rope is not None:
        in_specs += [pl.BlockSpec((tm, LANES), lambda j, bb, i: (i, 0))] * 2
        args += tuple(rope)
    return pl.pallas_call(
        functools.partial(_proj_kernel if rope is None else _proj_rope_kernel, n_lat),
        grid=(n // tn, b, lt // tm),
        in_specs=in_specs,
        out_specs=pl.BlockSpec((1, tm, tn), lambda j, bb, i: (bb, i, j)),
        out_shape=jax.ShapeDtypeStruct((b, lt, n), out_dtype),
        compiler_params=_cparams("arbitrary", "arbitrary", "arbitrary"),
        name="mod_proj" if rope is None else "mod_proj_rope",
    )(*args)


def _project_t(xs, scale, shift, wt, out_dtype):
    b, lt, d = xs.shape
    n = wt.shape[0]
    tm = _proj_rows(lt)
    return pl.pallas_call(
        functools.partial(_proj_t_kernel, lt - TOK_BLOCK),
        grid=(b, lt // tm),
        in_specs=[
            pl.BlockSpec((1, tm, d), lambda bb, i: (bb, i, 0)),
            pl.BlockSpec((1, 2, 1, d), lambda bb, i: (bb, 0, 0, 0)),
            pl.BlockSpec((1, 2, 1, d), lambda bb, i: (bb, 0, 0, 0)),
            pl.BlockSpec((n, d), lambda bb, i: (0, 0)),
        ],
        out_specs=pl.BlockSpec((1, n, tm), lambda bb, i: (bb, 0, i)),
        out_shape=jax.ShapeDtypeStruct((b, n, lt), out_dtype),
        compiler_params=_cparams("arbitrary", "arbitrary"),
        name="mod_proj_t",
    )(xs, scale, shift, wt)


def _lora(x, w_ref):
    x1 = x.astype(BF16)
    x2 = (x - x1.astype(F32)).astype(BF16)
    return _mm(jnp.concatenate([x1, x1, x2], axis=-1), w_ref[...])


def _rwkv_prep_kernel(nblk, p_ref, prev_ref, next_ref, mup_ref, mun_ref, w0_ref, w2_ref, a0_ref, a2_ref,
                      kk_ref, ka_ref, rk_ref, cm_ref,
                      v_out, bonus_out, *dir_outs):
    i = pl.program_id(1)
    w = kk_ref.shape[-1]
    p = p_ref[0]
    row = lax.broadcasted_iota(jnp.int32, (TOK_BLOCK, 1), 0)
    prev_row = jnp.where(jnp.logical_and(i >= 1, i <= nblk - 2), prev_ref[0][7:8, :], 0.0)
    next_row = jnp.where(i <= nblk - 3, next_ref[0][0:1, :], 0.0)
    prev = jnp.where(row == 0, prev_row, pltpu.roll(p, 1, 0))
    nxt = jnp.where(row == TOK_BLOCK - 1, next_row, pltpu.roll(p, TOK_BLOCK - 1, 0))
    ps = p * (1.0 - mup_ref[...] - mun_ref[...]) + mup_ref[...] * prev + mun_ref[...] * nxt

    r = ps[:, 0:w]
    k = ps[:, w:2 * w]
    v = ps[:, 2 * w:3 * w]
    wlo = ps[:, 3 * w:3 * w + 2 * RWKV_LORA]
    alo = ps[:, 3 * w + 2 * RWKV_LORA:3 * w + 4 * RWKV_LORA]

    z = _lora(jnp.tanh(wlo), w2_ref) + w0_ref[...]
    lw = _sigmoid(z) * (-LOG2E * float(np.exp(-0.5)))
    a = _sigmoid(_lora(alo, a2_ref) + a0_ref[...])

    kk = k * kk_ref[...]
    kk = kk * lax.rsqrt(_group64_sum(kk * kk) + 1e-12)
    bonus_out[0] = (_group64_sum(r * k * rk_ref[...]) * v).astype(bonus_out.dtype)
    v_out[0] = v.astype(BF16)

    for d in range(2):
        rh_o, kh_o, bh_o, kap_o, kg_o, bg_o, gl_o = dir_outs[7 * d:7 * d + 7]
        lw_d = lw[:, d * w:(d + 1) * w]
        a_d = a[:, d * w:(d + 1) * w]
        kd = k * (1.0 + (a_d - 1.0) * ka_ref[...])
        b_d = kk * a_d
        sums = _mm(cm_ref[d], _split3(lw_d))
        g = sums[:TOK_BLOCK]
        g_all = sums[TOK_BLOCK:]
        e_neg = jnp.exp2(-g)
        e_rest = jnp.exp2(g_all - g)
        rh_o[0] = (r * jnp.exp2(g)).astype(BF16)
        kh_o[0] = (kd * e_neg).astype(BF16)
        bh_o[0] = (b_d * e_neg).astype(BF16)
        kap_o[0] = (kk * jnp.exp2(g - lw_d)).astype(BF16)
        kg_o[0] = (kd * e_rest).astype(BF16)
        bg_o[0] = (b_d * e_rest).astype(BF16)
        e_all = jnp.exp2(g_all)
        for c in range(TOK_BLOCK // CHUNK):
            gl_o[0, c] = e_all[c * CHUNK:c * CHUNK + 1, :]


def _rwkv_sum_consts(reverse):
    t = np.arange(TOK_BLOCK)[:, None]
    s = np.arange(TOK_BLOCK)[None, :]
    same = (t // CHUNK) == (s // CHUNK)
    order = (s >= t) if reverse else (s <= t)
    both = np.concatenate([same & order, same], axis=0).astype(np.float32)
    return np.concatenate([both] * 3, axis=1)


def _lora_pieces(m):
    hi = m.astype(BF16)
    lo = (m - hi.astype(F32)).astype(BF16)
    return jnp.concatenate([hi, lo, hi], axis=0)


def _rwkv_prep(p_rwkv, mu_prev, mu_next, w0, w2, a0, a2, k_k, k_a, r_k):
    b, lt, sw = p_rwkv.shape
    w = k_k.shape[-1]
    nblk = lt // TOK_BLOCK
    nch = lt // CHUNK
    halo = TOK_BLOCK // 8
    zl = jnp.zeros((RWKV_LORA, w), F32)
    w2cat = _lora_pieces(jnp.concatenate([jnp.concatenate([w2[0], zl], 1), jnp.concatenate([zl, w2[1]], 1)], 0))
    a2cat = _lora_pieces(jnp.concatenate([jnp.concatenate([a2[0], zl], 1), jnp.concatenate([zl, a2[1]], 1)], 0))
    cm = jnp.asarray(np.stack([_rwkv_sum_consts(False), _rwkv_sum_consts(True)]), dtype=BF16)
    row = lambda u: u.reshape(1, -1)
    full = lambda shape: pl.BlockSpec(shape, lambda bb, i: (0,) * len(shape))
    tok = lambda n, dt: (pl.BlockSpec((1, TOK_BLOCK, n), lambda bb, i: (bb, i, 0)), jax.ShapeDtypeStruct((b, lt, n), dt))
    gl = (pl.BlockSpec((1, TOK_BLOCK // CHUNK, 1, w), lambda bb, i: (bb, i, 0, 0)),
          jax.ShapeDtypeStruct((b, nch, 1, w), F32))
    outs = [tok(w, BF16), tok(w, BF16)] + 2 * ([tok(w, BF16)] * 6 + [gl])
    res = pl.pallas_call(
        functools.partial(_rwkv_prep_kernel, nblk),
        grid=(b, nblk),
        in_specs=[
            pl.BlockSpec((1, TOK_BLOCK, sw), lambda bb, i: (bb, i, 0)),
            pl.BlockSpec((1, 8, sw), lambda bb, i: (bb, jnp.maximum(i * halo - 1, 0), 0)),
            pl.BlockSpec((1, 8, sw), lambda bb, i: (bb, jnp.minimum((i + 1) * halo, lt // 8 - 1), 0)),
            full((1, sw)), full((1, sw)), full((1, 2 * w)), full(w2cat.shape),
            full((1, 2 * w)), full(a2cat.shape), full((1, w)), full((1, w)), full((1, w)),
            full(cm.shape),
        ],
        out_specs=[o[0] for o in outs],
        out_shape=[o[1] for o in outs],
        compiler_params=_cparams("arbitrary", "arbitrary"),
        name="rwkv_prep",
    )(p_rwkv, p_rwkv, p_rwkv, row(mu_prev), row(mu_next), row(w0), w2cat, row(a0), a2cat,
      row(k_k), row(k_a), row(r_k), cm)
    v, bonus = res[0], res[1]
    return v, bonus, res[2:9], res[9:16]


def _rwkv_masks(reverse):
    n = GROUP_HEADS * CHUNK
    i = np.arange(n)[:, None]
    j = np.arange(n)[None, :]
    same_head = (i // CHUNK) == (j // CHUNK)
    before = (j > i) if reverse else (j < i)
    strict = same_head & before
    incl = same_head & (before | (i == j))
    levels = []
    size = 1
    while size < CHUNK:
        levels.append(same_head & ((i // (2 * size)) == (j // (2 * size))) & ((i // size) != (j // size)))
        size *= 2
    head_lane = (i // CHUNK) == (j // RWKV_HEAD)
    tri = np.stack([strict, incl]).astype(np.float32)
    return tri, np.stack(levels).astype(np.float32), head_lane.astype(np.float32)


def _scan_chunk_index(c, nch, reverse):
    if reverse:
        return nch - 1 - c
    n_ctx = TOK_BLOCK // CHUNK
    return jnp.where(c < n_ctx, nch - n_ctx + c, c - n_ctx)


def _split3(x):
    x1 = x.astype(BF16)
    r1 = x - x1.astype(F32)
    x2 = r1.astype(BF16)
    x3 = (r1 - x2.astype(F32)).astype(BF16)
    return jnp.concatenate([x1, x2, x3], axis=0)


def _hgrn_chunk_consts(reverse):
    t = np.arange(CHUNK)[:, None]
    s = np.arange(CHUNK)[None, :]
    order = (s >= t) if reverse else (s <= t)
    earlier_sub = ((s // SUB) > (t // SUB)) if reverse else ((s // SUB) < (t // SUB))
    both = np.concatenate([order, earlier_sub], axis=0).astype(np.float32)
    return np.concatenate([both] * 3, axis=1)


def _rwkv_chunk_stages(rh_ref, kh_ref, bh_ref, kap_ref, kg_ref, bg_ref, v_ref, gl_ref, tri_ref, lvl_ref, hm_ref,
                       y_ref, s_ref):
    hm = hm_ref[...]
    strict = tri_ref[0]
    incl = tri_ref[1]
    eye = incl - strict
    n_levels = lvl_ref.shape[0]
    groups = range(y_ref.shape[-1] // MXU_DIM)
    lanes = lambda ref, g: ref[0, :, g * MXU_DIM:(g + 1) * MXU_DIM]
    stack = lambda ref, g: jnp.concatenate([lanes(ref, g)] * GROUP_HEADS, axis=0) * hm

    kap, bh, a_ab = [], [], []
    for g in groups:
        kap.append(stack(kap_ref, g))
        bh.append(stack(bh_ref, g))
        a_ab.append(_nt(kap[g], bh[g]) * strict)
    yield
    kh = [stack(kh_ref, g) for g in groups]
    a_ak = [(_nt(kap[g], kh[g]) * strict).astype(BF16) for g in groups]
    rh = [stack(rh_ref, g) for g in groups]
    p_rb = [(_nt(rh[g], bh[g]) * incl).astype(BF16) for g in groups]
    p_rk = [(_nt(rh[g], kh[g]) * incl).astype(BF16) for g in groups]
    vs = [stack(v_ref, g) for g in groups]
    t = [eye - a_ab[g] * lvl_ref[0] for g in groups]
    a_abb = [a_ab[g].astype(BF16) for g in groups]
    for lv in range(1, n_levels):
        yield
        tb = [t[g].astype(BF16) for g in groups]
        x = [_mm(tb[g], a_abb[g]).astype(BF16) for g in groups]
        yield
        t = [t[g] - _mm(x[g], tb[g]) * lvl_ref[lv] for g in groups]
    yield
    s0 = [s_ref[g] for g in groups]
    s0b = [s0[g].astype(BF16) for g in groups]
    hm32 = hm.astype(F32)
    ks0 = [jnp.concatenate([_nt(lanes(kap_ref, g), s0b[g])] * GROUP_HEADS, axis=0) * hm32 for g in groups]
    wmat = [(ks0[g] + _mm(a_ak[g], vs[g])).astype(BF16) for g in groups]
    yield
    ub = [(-_mm(t[g].astype(BF16), wmat[g])).astype(BF16) for g in groups]
    yield
    ys = [_mm(p_rb[g], ub[g]) + _mm(p_rk[g], vs[g]) for g in groups]
    for g in groups:
        y = _nt(lanes(rh_ref, g), s0b[g])
        for h in range(GROUP_HEADS):
            y = y + ys[g][h * CHUNK:(h + 1) * CHUNK]
        y_ref[0, :, g * MXU_DIM:(g + 1) * MXU_DIM] = y.astype(y_ref.dtype)
    yield
    for g in groups:
        gam = gl_ref[0, 0][:, g * MXU_DIM:(g + 1) * MXU_DIM]
        s_ref[g] = s0[g] * gam + _tn(ub[g], stack(bg_ref, g)) + _tn(vs[g], stack(kg_ref, g))


def _hgrn_chunk_stages(reverse, q_ref, f_ref, i_ref, lb_ref, cm_ref, o_ref, s_ref, g_scr, q_scr):
    n_sub = CHUNK // SUB
    n_heads = q_ref.shape[-1] // HGRN_HEAD
    sub_order = list(range(n_sub))[::-1] if reverse else list(range(n_sub))
    row_sub = lax.broadcasted_iota(jnp.int32, (CHUNK, 1), 0) // SUB
    srow = lax.broadcasted_iota(jnp.int32, (SUB, 1), 0)
    lane = lax.broadcasted_iota(jnp.int32, (1, LANES), 1)
    last_row = 0 if reverse else CHUNK - 1

    kgate = (1.0 - lb_ref[...]) * _sigmoid(-f_ref[0])
    sums = _mm(cm_ref[...], _split3(jnp.log1p(-kgate) * LOG2E))
    g_all = sums[:CHUNK]
    gb_all = sums[CHUNK:]
    q_all = _silu(q_ref[0])
    g_scr[...] = g_all
    q_scr[...] = q_all
    yield
    for hg in range(n_heads // HGRN_PAR):
        par = range(HGRN_PAR)
        hs = [hg * HGRN_PAR + j for j in par]
        cols = [slice(h * HGRN_HEAD, (h + 1) * HGRN_HEAD) for h in hs]
        g = [g_all[:, cs] for cs in cols]
        gb = [gb_all[:, cs] for cs in cols]
        q = [q_all[:, cs] for cs in cols]
        k = [kgate[:, cs] for cs in cols]
        vb = [i_ref[0, :, cs].astype(BF16) for cs in cols]
        s0 = [s_ref[h] for h in hs]
        q_in = [q[j] * jnp.exp2(g[j] - gb[j]) for j in par]
        at = []
        for j in par:
            kts, qts = [], []
            for si in sub_order[1:]:
                gb_i = gb[j][si * SUB:si * SUB + 1, :]
                earlier = (row_sub > si) if reverse else (row_sub < si)
                kts.append(jnp.where(earlier, k[j] * jnp.exp2(jnp.minimum(gb_i - g[j], 0.0)), 0.0).astype(BF16))
                qts.append(jnp.where(row_sub == si, q_in[j], 0.0).astype(BF16))
            at.append(_nt(jnp.concatenate(kts, axis=-1), jnp.concatenate(qts, axis=-1)))
        yield
        diag = [[] for _ in par]
        for si in range(n_sub):
            blk = slice(si * SUB, (si + 1) * SUB)
            acc = [jnp.zeros((SUB, LANES), F32) for _ in par]
            for tt in range(SUB):
                t = si * SUB + tt
                valid = jnp.logical_and(lane == t, (srow >= tt) if reverse else (srow <= tt))
                for j in par:
                    e = jnp.exp2(jnp.minimum(g_scr[t:t + 1, cols[j]] - g[j][blk], 0.0))
                    col = jnp.sum(k[j][blk] * e * q_scr[t:t + 1, cols[j]], axis=-1, keepdims=True)
                    acc[j] = jnp.where(valid, col, acc[j])
                if tt % 8 == 7:
                    yield
            for j in par:
                diag[j].append(acc[j])
        for j in par:
            a = (at[j] + jnp.concatenate(diag[j], axis=0)[:, :CHUNK]).astype(BF16)
            o = _tn(a, vb[j]) + _nt((q[j] * jnp.exp2(g[j])).astype(BF16), s0[j].astype(BF16))
            o_ref[0, :, cols[j]] = o.astype(o_ref.dtype)
            g_last = g[j][last_row:last_row + 1, :]
            k_out = (k[j] * jnp.exp2(g_last - g[j])).astype(BF16)
            s_ref[hs[j]] = s0[j] * jnp.exp2(g_last) + _tn(vb[j], k_out)
        yield


def _mix_scan_kernel(reverse, *refs):
    rwkv_in, hgrn_in = refs[:11], refs[11:16]
    y_ref, o_ref, s_ref, hs_ref, g_scr, q_scr = refs[16:]

    @pl.when(pl.program_id(1) == 0)
    def _():
        s_ref[...] = jnp.zeros_like(s_ref)
        hs_ref[...] = jnp.zeros_like(hs_ref)

    streams = [_rwkv_chunk_stages(*rwkv_in, y_ref, s_ref),
               _hgrn_chunk_stages(reverse, *hgrn_in, o_ref, hs_ref, g_scr, q_scr)]
    done = object()
    while streams:
        streams = [st for st in streams if next(st, done) is not done]


def _mix_scan(dir_ops, v, p_hgrn, lb_d, d, reverse):
    rh, kh, bh, kap, kg, bg, gl = dir_ops
    b, lt, w = v.shape
    nch = lt // CHUNK
    n = GROUP_HEADS * CHUNK
    tri, lvl, hm = _rwkv_masks(reverse)
    cm = jnp.asarray(_hgrn_chunk_consts(reverse), dtype=BF16)
    cidx = lambda c: _scan_chunk_index(c, nch, reverse)
    tok = pl.BlockSpec((1, CHUNK, w), lambda bb, c: (bb, cidx(c), 0))
    col = lambda j: pl.BlockSpec((1, CHUNK, w), lambda bb, c: (bb, cidx(c), j))
    full = lambda shape: pl.BlockSpec(shape, lambda bb, c: (0,) * len(shape))
    return pl.pallas_call(
        functools.partial(_mix_scan_kernel, reverse),
        grid=(b, nch),
        in_specs=[tok] * 7 + [
            pl.BlockSpec((1, 1, 1, w), lambda bb, c: (bb, cidx(c), 0, 0)),
            full(tri.shape), full(lvl.shape), full(hm.shape),
            col(0), col(1 + d), col(3), full((1, w)), full(cm.shape),
        ],
        out_specs=[tok, tok],
        out_shape=[jax.ShapeDtypeStruct((b, lt, w), BF16)] * 2,
        scratch_shapes=[pltpu.VMEM((w // MXU_DIM, n, n), F32),
                        pltpu.VMEM((w // HGRN_HEAD, HGRN_HEAD, HGRN_HEAD), F32),
                        pltpu.VMEM((CHUNK, w), F32), pltpu.VMEM((CHUNK, w), F32)],
        compiler_params=_cparams("arbitrary", "arbitrary"),
        name="mix_scan_rev" if reverse else "mix_scan_fwd",
    )(rh, kh, bh, kap, kg, bg, v, gl, jnp.asarray(tri), jnp.asarray(lvl), jnp.asarray(hm, dtype=BF16),
      p_hgrn, p_hgrn, p_hgrn, lb_d.reshape(1, w), cm)


def _residual_ln(alpha, n_lat, x, gm_ref, proj, lng, lnb):
    tm = x.shape[0]
    rows = pl.program_id(1) * tm + lax.broadcasted_iota(jnp.int32, (tm, 1), 0)
    gm = jnp.where(rows >= n_lat, gm_ref[0, 0], gm_ref[0, 1])
    z = alpha * x + gm * proj
    mu = jnp.mean(z, axis=-1, keepdims=True)
    zc = z - mu
    var = jnp.mean(zc * zc, axis=-1, keepdims=True)
    return zc * lax.rsqrt(var + LN_EPS) * lng + lnb


def _even_out_kernel(alpha, n_lat, ya0_ref, ya1_ref, bonus_ref, yb0_ref, yb1_ref, gate_ref, x_ref, gm_ref, w_ref,
                     lnxg_ref, lnxb_ref, ng_ref, lng_ref, lnb_ref, o_ref):
    y = ya0_ref[0].astype(F32) + ya1_ref[0].astype(F32)
    mu = _group64_sum(y) * (1.0 / RWKV_HEAD)
    yc = y - mu
    var = _group64_sum(yc * yc) * (1.0 / RWKV_HEAD)
    ya = yc * lax.rsqrt(var + RWKV_GN_EPS) * lnxg_ref[...] + lnxb_ref[...] + bonus_ref[0].astype(F32)
    o = yb0_ref[0].astype(F32) + yb1_ref[0].astype(F32)
    yb = o * lax.rsqrt(_group128_mean(o * o) + LN_EPS) * ng_ref[...]
    ycat = jnp.concatenate([ya, yb], axis=-1) * _silu(gate_ref[0].astype(F32))
    proj = _mm(ycat.astype(BF16), w_ref[...])
    o_ref[0] = _residual_ln(alpha, n_lat, x_ref[0], gm_ref, proj, lng_ref[...], lnb_ref[...])


def _odd_out_kernel(alpha, n_lat, y_ref, x_ref, gm_ref, w_ref, lng_ref, lnb_ref, o_ref):
    proj = _mm(y_ref[0].astype(BF16), w_ref[...])
    o_ref[0] = _residual_ln(alpha, n_lat, x_ref[0], gm_ref, proj, lng_ref[...], lnb_ref[...])


def _tok_spec(n, tm=TOK_BLOCK):
    return pl.BlockSpec((1, tm, n), lambda bb, i: (bb, i, 0))


def _full2(shape):
    return pl.BlockSpec(shape, lambda bb, i: (0,) * len(shape))


def _gate_spec(d):
    return pl.BlockSpec((1, 2, 1, d), lambda bb, i: (bb, 0, 0, 0))


def _even_out(alpha, ya0, ya1, bonus, yb0, yb1, gate, xs, gmod, w_out, lnx_g, lnx_b, norm_g, ln_g, ln_b):
    b, lt, d = xs.shape
    w = ya0.shape[-1]
    di = gate.shape[-1]
    row = lambda u: u.reshape(1, -1)
    tm = _proj_rows(lt)
    return pl.pallas_call(
        functools.partial(_even_out_kernel, alpha, lt - TOK_BLOCK),
        grid=(b, lt // tm),
        in_specs=[_tok_spec(w, tm)] * 5 + [_tok_spec(di, tm), _tok_spec(d, tm), _gate_spec(d), _full2((di, d)),
                                           _full2((1, w)), _full2((1, w)), _full2((1, w)), _full2((1, d)),
                                           _full2((1, d))],
        out_specs=_tok_spec(d, tm),
        out_shape=jax.ShapeDtypeStruct((b, lt, d), F32),
        compiler_params=_cparams("arbitrary", "arbitrary"),
        name="even_out",
    )(ya0, ya1, bonus, yb0, yb1, gate, xs, gmod, w_out, row(lnx_g), row(lnx_b),
      row(jnp.tile(norm_g, w // HGRN_HEAD)), row(ln_g), row(ln_b))


def _odd_out(alpha, y, xs, gmod, w_out, ln_g, ln_b, rows_out):
    b, lt, d = xs.shape
    di = y.shape[-1]
    row = lambda u: u.reshape(1, -1)
    tm = next(r for r in (PROJ_ROWS, 2 * TOK_BLOCK, TOK_BLOCK) if rows_out % r == 0)
    return pl.pallas_call(
        functools.partial(_odd_out_kernel, alpha, lt - TOK_BLOCK),
        grid=(b, rows_out // tm),
        in_specs=[_tok_spec(di, tm), _tok_spec(d, tm), _gate_spec(d), _full2((di, d)), _full2((1, d)),
                  _full2((1, d))],
        out_specs=_tok_spec(d, tm),
        out_shape=jax.ShapeDtypeStruct((b, rows_out, d), F32),
        compiler_params=_cparams("arbitrary", "arbitrary"),
        name="odd_out",
    )(y, xs, gmod, w_out, row(ln_g), row(ln_b))


def _rope(x, cos, sin):
    width = x.shape[-1]
    lane = lax.broadcasted_iota(jnp.int32, (1, width), 1)
    first = (lane % 32) < 16
    partner = jnp.where(first, pltpu.roll(x, width - 16, 1), pltpu.roll(x, 16, 1))
    return x * cos + partner * sin


def _attn_kernel(lam_init, tk, n_full, tail, lam_ref, q_ref, k_ref, vt_ref, g_ref, sg_ref, *rest):
    y_ref = rest[-1]
    tq = q_ref.shape[1]
    n_heads = q_ref.shape[2] // LANES
    ones_rows = 16
    assert n_full <= ATTN_UNROLL
    blocks = [(u * tk, tk) for u in range(n_full)] + [(n_full * tk, tail)]
    lane = lax.broadcasted_iota(jnp.int32, (1, LANES), 1)

    def head_stream(j):
        hl = slice(j * LANES, (j + 1) * LANES)
        q = q_ref[0, :, hl].astype(F32) * (DIFF_HEAD ** -0.5 * LOG2E)
        q_cat = jnp.concatenate([jnp.where(lane < DIFF_HEAD, q, 0.0), jnp.where(lane >= DIFF_HEAD, q, 0.0)],
                                axis=0).astype(BF16)
        scores = lambda blk: _nt(k_ref[0, pl.ds(blk[0], blk[1]), hl], q_cat)
        m_run = jnp.full((1, 2 * tq), -jnp.inf, F32)
        acc = jnp.zeros((LANES + ones_rows, 2 * tq), F32)
        pending = None
        st_next = scores(blocks[0])
        for i, (start, size) in enumerate(blocks):
            st = st_next
            if i + 1 < len(blocks):
                st_next = scores(blocks[i + 1])
            if pending is not None:
                acc = acc * pending[0] + _mm(pending[1], pending[2])
            vta = jnp.concatenate([vt_ref[0, hl, pl.ds(start, size)], jnp.ones((ones_rows, size), BF16)], axis=0)
            m_new = jnp.maximum(m_run, jnp.max(st, axis=0, keepdims=True))
            pending = (jnp.exp2(m_run - m_new), vta, jnp.exp2(st - m_new).astype(BF16))
            m_run = m_new
            yield
        acc = acc * pending[0] + _mm(pending[1], pending[2])
        a0 = acc[:, :tq]
        a1 = acc[:, tq:]
        o_t = a0[:LANES] / a0[LANES:LANES + 1] - lam_ref[0] * (a1[:LANES] / a1[LANES:LANES + 1])
        o = jnp.transpose(o_t)
        y = o * lax.rsqrt(jnp.mean(o * o, axis=-1, keepdims=True) + LN_EPS) * sg_ref[...] * (1.0 - lam_init)
        y_ref[0, :, hl] = (y * _silu(g_ref[0, :, hl].astype(F32))).astype(y_ref.dtype)

    streams = [head_stream(j) for j in range(n_heads)]
    done = object()
    while streams:
        streams = [st for st in streams if next(st, done) is not done]


def _diff_attention(p_qk, p_vt, p_g, lam, subln_g, lam_init):
    b, lt, di = p_g.shape
    nh = di // LANES
    t = lt - TOK_BLOCK
    tk = 512
    tq = 1024 // ATTN_HEADS
    assert t % tq == 0 and nh % ATTN_HEADS == 0
    hw = ATTN_HEADS * LANES
    ng = nh // ATTN_HEADS
    ctx_blk = t // TOK_BLOCK
    sm = pl.BlockSpec(memory_space=pltpu.SMEM)
    sg_spec = pl.BlockSpec((1, LANES), lambda bb, h, i: (0, 0))
    args = (lam.reshape(1), p_qk, p_qk, p_vt, p_g, subln_g.reshape(1, LANES))
    y = pl.pallas_call(
        functools.partial(_attn_kernel, lam_init, tk, t // tk, TOK_BLOCK),
        grid=(b, ng, t // tq),
        in_specs=[sm,
                  pl.BlockSpec((1, tq, hw), lambda bb, h, i: (bb, i, h)),
                  pl.BlockSpec((1, lt, hw), lambda bb, h, i: (bb, 0, ng + h)),
                  pl.BlockSpec((1, hw, lt), lambda bb, h, i: (bb, h, 0)),
                  pl.BlockSpec((1, tq, hw), lambda bb, h, i: (bb, i, h)),
                  sg_spec],
        out_specs=pl.BlockSpec((1, tq, hw), lambda bb, h, i: (bb, i, h)),
        out_shape=jax.ShapeDtypeStruct((b, lt, di), BF16),
        compiler_params=_cparams("arbitrary", "arbitrary", "arbitrary"),
        name="diff_attn",
    )(*args)
    blk = lambda bb, h, i: (bb, ctx_blk, h)
    return pl.pallas_call(
        functools.partial(_attn_kernel, lam_init, tk, 0, TOK_BLOCK),
        grid=(b, ng, 1),
        in_specs=[sm,
                  pl.BlockSpec((1, TOK_BLOCK, hw), blk),
                  pl.BlockSpec((1, TOK_BLOCK, hw), lambda bb, h, i: (bb, ctx_blk, ng + h)),
                  pl.BlockSpec((1, hw, TOK_BLOCK), lambda bb, h, i: (bb, h, ctx_blk)),
                  pl.BlockSpec((1, TOK_BLOCK, hw), blk),
                  sg_spec,
                  pl.BlockSpec(memory_space=pl.ANY)],
        out_specs=pl.BlockSpec((1, TOK_BLOCK, hw), blk),
        out_shape=jax.ShapeDtypeStruct((b, lt, di), BF16),
        input_output_aliases={6: 0},
        compiler_params=_cparams("arbitrary", "arbitrary", "arbitrary"),
        name="diff_attn_ctx",
    )(*args, y)


def _rope_tables(n_ctx, t):
    quarter = DIFF_HEAD // 4
    inv = ROPE_BASE ** (-jnp.arange(quarter, dtype=F32) / quarter)
    pos = jnp.arange(t)
    rows = (pos // GRID_W).astype(F32)[:, None] * inv
    cols = (pos % GRID_W).astype(F32)[:, None] * inv
    cos64 = jnp.concatenate([jnp.cos(rows), jnp.cos(rows), jnp.cos(cols), jnp.cos(cols)], -1)
    sin64 = jnp.concatenate([-jnp.sin(rows), jnp.sin(rows), -jnp.sin(cols), jnp.sin(cols)], -1)
    cos = jnp.concatenate([jnp.tile(cos64, (1, 2)), jnp.ones((n_ctx, LANES), F32)], 0)
    sin = jnp.concatenate([jnp.tile(sin64, (1, 2)), jnp.zeros((n_ctx, LANES), F32)], 0)
    return cos, sin


def _even_layer(alpha, xs, scale, shift, gmod, w_in, w_out, mu_prev, mu_next, w0, w2, a0, a2, k_k, k_a, r_k,
                lnx_g, lnx_b, lb, norm_g, ln_g, ln_b):
    w = k_k.shape[-1]
    sw = mu_prev.shape[-1]
    hw = 4 * lb.shape[-1]
    w_in = w_in.astype(BF16)
    p_rwkv = _project(xs, scale, shift, w_in[:, :sw], sw, F32)
    p_hgrn = _project(xs, scale, shift, w_in[:, sw:sw + hw], hw // 2, F32)
    gate = _project(xs, scale, shift, w_in[:, sw + hw:], w_in.shape[1] - sw - hw, BF16)
    v, bonus, ops_f, ops_r = _rwkv_prep(p_rwkv, mu_prev, mu_next, w0, w2, a0, a2, k_k, k_a, r_k)
    ya0, yb0 = _mix_scan(ops_f, v, p_hgrn, lb[0], 0, False)
    ya1, yb1 = _mix_scan(ops_r, v, p_hgrn, lb[1], 1, True)
    return _even_out(alpha, ya0, ya1, bonus, yb0, yb1, gate, xs, gmod, w_out.astype(BF16),
                     lnx_g, lnx_b, norm_g, ln_g, ln_b)


def _odd_layer(alpha, xs, scale, shift, gmod, w_in, w_out, lam_p, subln_g, lam_init, cos, sin, ln_g, ln_b,
               rows_out):
    di = w_out.shape[0]
    w_in = w_in.astype(BF16)
    p_qk = _project(xs, scale, shift, w_in[:, :2 * di], di, BF16, rope=(cos, sin))
    p_vt = _project_t(xs, scale, shift, jnp.transpose(w_in[:, 2 * di:3 * di]), BF16)
    p_g = _project(xs, scale, shift, w_in[:, 3 * di:], di, BF16)
    lam = jnp.exp(jnp.sum(lam_p[0] * lam_p[1])) - jnp.exp(jnp.sum(lam_p[2] * lam_p[3])) + lam_init
    y = _diff_attention(p_qk, p_vt, p_g, lam, subln_g, lam_init)
    return _odd_out(alpha, y, xs, gmod, w_out.astype(BF16), ln_g, ln_b, rows_out)


def kernel(x, c, ctx, c_ctx, ada_w, ada_b, ln_g, ln_b, even_w_in, even_w_out, rwkv_mu_prev, rwkv_mu_next, rwkv_w0, rwkv_w2, rwkv_a0, rwkv_a2, rwkv_k_k, rwkv_k_a, rwkv_r_k, rwkv_lnx_g, rwkv_lnx_b, hgrn_lb_logits, hgrn_norm_g, odd_w_in, odd_w_out, diff_lambda, diff_subln_g):
    b, t, d = x.shape
    n_ctx = ctx.shape[1]
    depth = ada_w.shape[0]
    assert n_ctx == TOK_BLOCK and t % TOK_BLOCK == 0 and b + 1 <= 8
    alpha = (2.0 * depth) ** 0.25

    xs = jnp.concatenate([x, ctx], axis=1)
    cvec = jnp.concatenate([c, c_ctx[None], jnp.zeros((8 - b - 1, d), F32)], axis=0)
    mods = _ada_mods(cvec, ada_w, ada_b)
    lb_all = jax.nn.softmax(hgrn_lb_logits.astype(F32), axis=0)
    lb_all = jnp.cumsum(lb_all, axis=0) - lb_all[0]
    cos, sin = _rope_tables(n_ctx, t)

    for layer in range(depth):
        m = mods[layer]
        per_seg = lambda u: jnp.stack([jnp.broadcast_to(u[b], (b, d)), u[:b]], axis=1)[:, :, None, :]
        shift, scale, gmod = (per_seg(m[:, j * d:(j + 1) * d]) for j in range(3))
        j = layer // 2
        if layer % 2 == 0:
            xs = _even_layer(alpha, xs, scale, shift, gmod, even_w_in[j], even_w_out[j], rwkv_mu_prev[j],
                             rwkv_mu_next[j], rwkv_w0[j], rwkv_w2[j], rwkv_a0[j], rwkv_a2[j], rwkv_k_k[j],
                             rwkv_k_a[j], rwkv_r_k[j], rwkv_lnx_g[j], rwkv_lnx_b[j], lb_all[j], hgrn_norm_g[j],
                             ln_g[layer], ln_b[layer])
        else:
            lam_init = 0.8 - 0.6 * float(np.exp(-0.3 * layer))
            xs = _odd_layer(alpha, xs, scale, shift, gmod, odd_w_in[j], odd_w_out[j], diff_lambda[j],
                            diff_subln_g[j], lam_init, cos, sin, ln_g[layer], ln_b[layer],
                            t if layer == depth - 1 else t + n_ctx)
    return xs[:, :t]
```
